```python
import functools
import jax, jax.numpy as jnp
from jax import lax
import numpy as np

D_MODEL = 1024
BATCH = 4
SEQ = 4096
DEPTH = 1
DEC_BATCH = 32
DEC_SEQ = 4
PAST_LEN = 16384
PAGE_SIZE = 128

D_CONV = D_MODEL // 2
CONV_WIDTH = 3
N_HEADS = 8
HEAD_DIM = 64
D_ATTN = N_HEADS * HEAD_DIM
Q_BLOCK = 128
FORGET_BIAS_INIT = 3.0
N_MEM = 256
N_XHEADS = 4
XHEAD_DIM = 128
D_XATTN = N_XHEADS * XHEAD_DIM
N_GROUPS = 4
EXPERTS_PER_GROUP = 8
N_EXPERTS = N_GROUPS * EXPERTS_PER_GROUP
TOP_K = 2
D_EXPERT = D_MODEL // 2
ROW_BLOCK = 128
D_IN = 3 * D_CONV + 3 * D_ATTN + N_HEADS + 2 * D_MODEL
RMS_EPS = 1e-6

kernel_name = 'hybrid_conv_fox_hmoe_step'


def rms_norm(x, g):
    xf = x.astype(jnp.float32)
    y = xf * lax.rsqrt(jnp.mean(xf * xf, axis=-1, keepdims=True) + RMS_EPS)
    return (y * g.astype(jnp.float32)).astype(x.dtype)


def split_combined(z):
    sizes = (D_CONV, D_CONV, D_CONV, D_ATTN, D_ATTN, D_ATTN, N_HEADS, D_MODEL, D_MODEL)
    offs = [int(o) for o in np.cumsum(sizes)[:-1]]
    return jnp.split(z, offs, axis=-1)


def short_conv(u, prev, conv_w, conv_b):
    s = u.shape[1]
    ext = jnp.concatenate([prev.astype(u.dtype), u], axis=1)
    y = conv_b.astype(u.dtype)
    for j in range(CONV_WIDTH):
        y = y + conv_w[j] * ext[:, j:j + s]
    return y, ext[:, -(CONV_WIDTH - 1):]


def fox_prompt(q, k, v, logf):
    b, s = q.shape[:2]
    nb = s // Q_BLOCK
    scale = HEAD_DIM ** -0.5
    dcum = lax.cumsum(logf, axis=1)
    dk = dcum.transpose(0, 2, 1)
    kpos = jnp.arange(s)
    qb = q.reshape(b, nb, Q_BLOCK, N_HEADS, HEAD_DIM).transpose(1, 0, 2, 3, 4)
    db = dcum.reshape(b, nb, Q_BLOCK, N_HEADS).transpose(1, 0, 3, 2)

    def block(args):
        i, qi, di = args
        qpos = i * Q_BLOCK + jnp.arange(Q_BLOCK)
        sc = jnp.einsum('bqhd,bkhd->bhqk', qi, k, preferred_element_type=jnp.float32) * scale
        sc = sc + di[:, :, :, None] - dk[:, :, None, :]
        sc = jnp.where(kpos[None, :] <= qpos[:, None], sc, -jnp.inf)
        p = jax.nn.softmax(sc, axis=-1)
        return jnp.einsum('bhqk,bkhd->bqhd', p.astype(v.dtype), v)

    out = lax.map(block, (jnp.arange(nb), qb, db))
    return out.transpose(1, 0, 2, 3, 4).reshape(b, s, D_ATTN)


def fox_sample(q, k, v, logf, k_past, v_past, logf_past):
    b, s = q.shape[:2]
    scale = HEAD_DIM ** -0.5
    d_new = lax.cumsum(logf, axis=1).transpose(0, 2, 1)
    rev = (lax.cumsum(logf_past, axis=1, reverse=True) - logf_past).transpose(0, 2, 1)
    sp = jnp.einsum('bqhd,bkhd->bhqk', q, k_past, preferred_element_type=jnp.float32) * scale
    sp = sp + d_new[:, :, :, None] + rev[:, :, None, :]
    sn = jnp.einsum('bqhd,bkhd->bhqk', q, k, preferred_element_type=jnp.float32) * scale
    sn = sn + d_new[:, :, :, None] - d_new[:, :, None, :]
    causal = jnp.arange(s)[None, :] <= jnp.arange(s)[:, None]
    sn = jnp.where(causal, sn, -jnp.inf)
    p = jax.nn.softmax(jnp.concatenate([sp, sn], axis=-1), axis=-1)
    n_past = k_past.shape[1]
    out = (jnp.einsum('bhqk,bkhd->bqhd', p[..., :n_past].astype(v_past.dtype), v_past)
           + jnp.einsum('bhqk,bkhd->bqhd', p[..., n_past:].astype(v.dtype), v))
    return out.reshape(b, s, D_ATTN)


def token_mixer(xn, conv_prev, attend, w_in, b_forget, conv_w, conv_b, w_up_conv, w_up_attn, w_mix_out):
    b, s = xn.shape[:2]
    z = xn @ w_in
    cb, cc, cx, q, k, v, fz, g_conv, g_attn = split_combined(z)
    cy, conv_new = short_conv(cc * cx, conv_prev, conv_w, conv_b)
    y_conv = (cb * cy) @ w_up_conv
    logf = jax.nn.log_sigmoid(fz.astype(jnp.float32) + b_forget.astype(jnp.float32))
    kh = k.reshape(b, s, N_HEADS, HEAD_DIM)
    vh = v.reshape(b, s, N_HEADS, HEAD_DIM)
    y_attn = attend(q.reshape(b, s, N_HEADS, HEAD_DIM), kh, vh, logf) @ w_up_attn
    mixed = jax.nn.sigmoid(g_conv) * y_conv + jax.nn.sigmoid(g_attn) * y_attn
    return mixed @ w_mix_out, conv_new, kh, vh, logf


def memory_kv(mem, norm_mem, w_xk, w_xv):
    b, m = mem.shape[:2]
    mn = rms_norm(mem, norm_mem)
    return ((mn @ w_xk).reshape(b, m, N_XHEADS, XHEAD_DIM),
            (mn @ w_xv).reshape(b, m, N_XHEADS, XHEAD_DIM))


def cross_attention(xn, mem_k, mem_v, w_xq, w_xo):
    b, s = xn.shape[:2]
    q = (xn @ w_xq).reshape(b, s, N_XHEADS, XHEAD_DIM)
    sc = jnp.einsum('bqhd,bmhd->bhqm', q, mem_k, preferred_element_type=jnp.float32) * (XHEAD_DIM ** -0.5)
    p = jax.nn.softmax(sc, axis=-1).astype(mem_v.dtype)
    o = jnp.einsum('bhqm,bmhd->bqhd', p, mem_v).reshape(b, s, D_XATTN)
    return o @ w_xo


def hier_moe(x, w_rg, w_re, w_eg, w_eu, w_ed):
    shp = x.shape
    xt = x.reshape(-1, D_MODEL)
    t = xt.shape[0]
    xf = xt.astype(jnp.float32)
    p_group = jax.nn.softmax(xf @ w_rg.astype(jnp.float32), axis=-1)
    g_idx = jnp.argmax(p_group, axis=-1)
    p_g = jnp.take_along_axis(p_group, g_idx[:, None], axis=-1)
    logits_all = jnp.einsum('td,gde->tge', xf, w_re.astype(jnp.float32))
    logits_e = jnp.take_along_axis(logits_all, g_idx[:, None, None], axis=1)[:, 0]
    top_v, top_i = lax.top_k(logits_e, TOP_K)
    gate = jax.nn.softmax(top_v, axis=-1) * p_g
    eid = (g_idx[:, None] * EXPERTS_PER_GROUP + top_i).reshape(-1).astype(jnp.int32)
    tok = jnp.repeat(jnp.arange(t, dtype=jnp.int32), TOP_K)
    wt = gate.reshape(-1)
    order = jnp.argsort(eid)
    eid_s, tok_s, wt_s = eid[order], tok[order], wt[order]
    counts = jnp.bincount(eid, length=N_EXPERTS)
    padded = (counts + ROW_BLOCK - 1) // ROW_BLOCK * ROW_BLOCK
    start = jnp.cumsum(counts) - counts
    pend = jnp.cumsum(padded)
    pstart = pend - padded
    dest = pstart[eid_s] + (jnp.arange(t * TOP_K) - start[eid_s])
    n_blocks = -(-(t * TOP_K + N_EXPERTS * (ROW_BLOCK - 1)) // ROW_BLOCK)
    n_rows = n_blocks * ROW_BLOCK
    row_tok = jnp.full((n_rows,), t, jnp.int32).at[dest].set(tok_s)
    row_w = jnp.zeros((n_rows,), jnp.float32).at[dest].set(wt_s)
    blk_e = jnp.minimum(jnp.searchsorted(pend, jnp.arange(n_blocks) * ROW_BLOCK, side='right'), N_EXPERTS - 1)
    x_pad = jnp.concatenate([xt, jnp.zeros((1, D_MODEL), xt.dtype)], axis=0)
    xb = x_pad[row_tok].reshape(n_blocks, ROW_BLOCK, D_MODEL)

    def expert_block(args):
        xi, e = args
        h = jax.nn.silu(xi @ w_eg[e]) * (xi @ w_eu[e])
        return h @ w_ed[e]

    yb = lax.map(expert_block, (xb, blk_e)).reshape(n_rows, D_MODEL)
    yb = yb * row_w[:, None].astype(yb.dtype)
    out = jnp.zeros((t + 1, D_MODEL), yb.dtype).at[row_tok].add(yb)[:t]
    return out.reshape(shp).astype(x.dtype)


def trunk_layer(h, conv_prev, attend, mem_k, mem_v, norm_mix, w_in, b_forget, conv_w, conv_b,
                w_up_conv, w_up_attn, w_mix_out, norm_xattn, w_xq, w_xo, norm_ffn,
                w_rg, w_re, w_eg, w_eu, w_ed):
    mix, conv_new, k, v, logf = token_mixer(rms_norm(h, norm_mix), conv_prev, attend, w_in, b_forget,
                                            conv_w, conv_b, w_up_conv, w_up_attn, w_mix_out)
    h = h + mix
    h = h + cross_attention(rms_norm(h, norm_xattn), mem_k, mem_v, w_xq, w_xo)
    h = h + hier_moe(rms_norm(h, norm_ffn), w_rg, w_re, w_eg, w_eu, w_ed)
    return h, conv_new, k, v, logf


def _dense(key, shape, fan_in):
    return jax.random.normal(key, shape, jnp.float32) * (fan_in ** -0.5)


def setup_inputs(seed: int = 0) -> dict:
    key = jax.random.key(seed)
    ks = jax.random.split(key, 32)
    n_pages = PAST_LEN // PAGE_SIZE
    n_used = DEC_BATCH * n_pages
    n_pool = n_used + (n_used + 3) // 4
    page_table = jax.random.permutation(ks[0], n_pool)[:n_used].reshape(DEC_BATCH, n_pages).astype(jnp.int32)
    nrm = lambda k, shp: jax.random.normal(k, shp, jnp.float32)
    gain = lambda k: 1.0 + 0.02 * nrm(k, (DEPTH, D_MODEL))
    return {
        'x_prompt': nrm(ks[1], (BATCH, SEQ, D_MODEL)),
        'x_sample': nrm(ks[2], (DEC_BATCH, DEC_SEQ, D_MODEL)),
        'cache_k': nrm(ks[3], (DEPTH, n_pool, PAGE_SIZE, N_HEADS, HEAD_DIM)),
        'cache_v': nrm(ks[4], (DEPTH, n_pool, PAGE_SIZE, N_HEADS, HEAD_DIM)),
        'cache_logf': jax.nn.log_sigmoid(FORGET_BIAS_INIT + 0.5 * nrm(ks[5], (DEPTH, n_pool, PAGE_SIZE, N_HEADS))),
        'cache_mem_k': nrm(ks[6], (DEPTH, DEC_BATCH, N_MEM, N_XHEADS, XHEAD_DIM)),
        'cache_mem_v': nrm(ks[7], (DEPTH, DEC_BATCH, N_MEM, N_XHEADS, XHEAD_DIM)),
        'state_conv': nrm(ks[8], (DEPTH, DEC_BATCH, CONV_WIDTH - 1, D_CONV)),
        'page_table': page_table,
        'mem_prompt': nrm(ks[9], (BATCH, N_MEM, D_MODEL)),
        'norm_mix': gain(ks[10]),
        'w_in': _dense(ks[11], (DEPTH, D_MODEL, D_IN), D_MODEL),
        'b_forget': FORGET_BIAS_INIT + 0.1 * nrm(ks[12], (DEPTH, N_HEADS)),
        'conv_w': _dense(ks[13], (DEPTH, CONV_WIDTH, D_CONV), CONV_WIDTH),
        'conv_b': 0.01 * nrm(ks[14], (DEPTH, D_CONV)),
        'w_up_conv': _dense(ks[15], (DEPTH, D_CONV, D_MODEL), D_CONV),
        'w_up_attn': _dense(ks[16], (DEPTH, D_ATTN, D_MODEL), D_ATTN),
        'w_mix_out': _dense(ks[17], (DEPTH, D_MODEL, D_MODEL), D_MODEL),
        'norm_xattn': gain(ks[18]),
        'norm_mem': gain(ks[19]),
        'w_xq': _dense(ks[20], (DEPTH, D_MODEL, D_XATTN), D_MODEL),
        'w_xk': _dense(ks[21], (DEPTH, D_MODEL, D_XATTN), D_MODEL),
        'w_xv': _dense(ks[22], (DEPTH, D_MODEL, D_XATTN), D_MODEL),
        'w_xo': _dense(ks[23], (DEPTH, D_XATTN, D_MODEL), D_XATTN),
        'norm_ffn': gain(ks[24]),
        'w_router_group': _dense(ks[25], (DEPTH, D_MODEL, N_GROUPS), D_MODEL),
        'w_router_expert': _dense(ks[26], (DEPTH, N_GROUPS, D_MODEL, EXPERTS_PER_GROUP), D_MODEL),
        'w_expert_gate': _dense(ks[27], (DEPTH, N_EXPERTS, D_MODEL, D_EXPERT), D_MODEL),
        'w_expert_up': _dense(ks[28], (DEPTH, N_EXPERTS, D_MODEL, D_EXPERT), D_MODEL),
        'w_expert_down': _dense(ks[29], (DEPTH, N_EXPERTS, D_EXPERT, D_MODEL), D_EXPERT),
        'norm_final': 1.0 + 0.02 * nrm(ks[30], (D_MODEL,)),
    }


def reference(x_prompt, x_sample, cache_k, cache_v, cache_logf, cache_mem_k, cache_mem_v, state_conv,
              page_table, mem_prompt, norm_mix, w_in, b_forget, conv_w, conv_b, w_up_conv, w_up_attn,
              w_mix_out, norm_xattn, norm_mem, w_xq, w_xk, w_xv, w_xo, norm_ffn, w_router_group,
              w_router_expert, w_expert_gate, w_expert_up, w_expert_down, norm_final):
    db = x_sample.shape[0]
    past_len = page_table.shape[1] * PAGE_SIZE
    hp, hs = x_prompt, x_sample
    conv_p, k_p, v_p, lf_p, mk_p, mv_p = [], [], [], [], [], []
    conv_s, k_s, v_s, lf_s = [], [], [], []
    for l in range(DEPTH):
        shared = (norm_mix[l], w_in[l], b_forget[l], conv_w[l], conv_b[l], w_up_conv[l], w_up_attn[l],
                  w_mix_out[l], norm_xattn[l], w_xq[l], w_xo[l], norm_ffn[l], w_router_group[l],
                  w_router_expert[l], w_expert_gate[l], w_expert_up[l], w_expert_down[l])
        mem_k, mem_v = memory_kv(mem_prompt, norm_mem[l], w_xk[l], w_xv[l])
        conv0 = jnp.zeros((hp.shape[0], CONV_WIDTH - 1, D_CONV), hp.dtype)
        hp, c, k, v, lf = trunk_layer(hp, conv0, fox_prompt, mem_k, mem_v, *shared)
        conv_p.append(c); k_p.append(k); v_p.append(v); lf_p.append(lf); mk_p.append(mem_k); mv_p.append(mem_v)
        k_past = cache_k[l][page_table].reshape(db, past_len, N_HEADS, HEAD_DIM)
        v_past = cache_v[l][page_table].reshape(db, past_len, N_HEADS, HEAD_DIM)
        lf_past = cache_logf[l][page_table].reshape(db, past_len, N_HEADS).astype(jnp.float32)
        attend_s = functools.partial(fox_sample, k_past=k_past, v_past=v_past, logf_past=lf_past)
        hs, c, k, v, lf = trunk_layer(hs, state_conv[l], attend_s, cache_mem_k[l], cache_mem_v[l], *shared)
        conv_s.append(c); k_s.append(k); v_s.append(v); lf_s.append(lf)
    y_prompt = rms_norm(hp, norm_final)
    y_sample = rms_norm(hs, norm_final)
    return (y_prompt, y_sample,
            jnp.stack(conv_p), jnp.stack(k_p), jnp.stack(v_p), jnp.stack(lf_p), jnp.stack(mk_p), jnp.stack(mv_p),
            jnp.stack(conv_s), jnp.stack(k_s), jnp.stack(v_s), jnp.stack(lf_s))
```

```python
import functools

import jax
import jax.numpy as jnp
from jax import lax
from jax.experimental import pallas as pl
from jax.experimental.pallas import tpu as pltpu

F32 = jnp.float32
BF16 = jnp.bfloat16

RMS_EPS = 1e-6
TOP_K = 2
LANES = 128
SUBLANES = 8
VMEM_LIMIT = 56 * 1024 * 1024
NEG_BIG = -1e30
EXPERT_ROWS = 256
ROUTE_ROWS = 8
EXPERT_ROW0 = 8


def _cparams(sem, vmem=VMEM_LIMIT):
    return pltpu.CompilerParams(dimension_semantics=sem, vmem_limit_bytes=vmem)


def _rms(x, g):
    ms = jnp.mean(x * x, axis=-1, keepdims=True)
    return x * lax.rsqrt(ms + RMS_EPS) * g


def _split3(x):
    hi = x.astype(BF16)
    r = x - hi.astype(F32)
    mid = r.astype(BF16)
    lo = (r - mid.astype(F32)).astype(BF16)
    return hi, mid, lo


def _dot(a, b):
    return jnp.dot(a, b, preferred_element_type=F32)


def _dot_nt(a, b):
    return lax.dot_general(a, b, (((1,), (1,)), ((), ())), preferred_element_type=F32)


def _const_spec(shape):
    nd = len(shape)
    return pl.BlockSpec(shape, lambda *_: (0,) * nd)


def _inproj_kernel(*refs, tm, seq_blocks, sample, seq_len):
    if sample:
        (x_ref, gain_ref, wc_ref, wqkv_ref, wg_ref, wfh_ref, wfl_ref, bf_ref, cw_ref, cb_ref, fix1_ref, fix2_ref,
         gc_ref, q_ref, kbf_ref, vbf_ref, kf_ref, vf_ref, g_ref, logf_ref, dq_ref, dt_ref, u_ref) = refs
    else:
        (x_ref, gain_ref, wc_ref, wqkv_ref, wg_ref, wfh_ref, wfl_ref, bf_ref, cw_ref, cb_ref,
         gc_ref, q_ref, kbf_ref, vbf_ref, kf_ref, vf_ref, g_ref, logf_ref, dq_ref, dt_ref, u_ref,
         carry_u, carry_d) = refs
    i = pl.program_id(0)
    dc = cw_ref.shape[1]
    da = kf_ref.shape[1]

    xn = _rms(x_ref[...], gain_ref[...])
    xb = xn.astype(BF16)

    cb = _dot(xb, wc_ref[:, 0:dc])
    cc = _dot(xb, wc_ref[:, dc:2 * dc])
    cx = _dot(xb, wc_ref[:, 2 * dc:3 * dc])
    u = cc * cx
    row = lax.broadcasted_iota(jnp.int32, (tm, 1), 0)
    r1 = pltpu.roll(u, 1, axis=0)
    r2 = pltpu.roll(u, 2, axis=0)
    if sample:
        pos = row % seq_len
        p1 = jnp.where(pos == 0, fix1_ref[...], r1)
        p2 = jnp.where(pos < 2, fix2_ref[...], r2)
        u_ref[...] = u
    else:
        @pl.when(i % seq_blocks == 0)
        def _():
            carry_u[...] = jnp.zeros_like(carry_u)
            carry_d[...] = jnp.zeros_like(carry_d)
        c0 = carry_u[0:1, :]
        c1 = carry_u[1:2, :]
        p1 = jnp.where(row == 0, c1, r1)
        p2 = jnp.where(row == 0, c0, jnp.where(row == 1, c1, r2))
        carry_u[0:2, :] = u[tm - 2:tm, :]
        u_ref[...] = u[tm - 2:tm, :]
    cw = cw_ref[...]
    cy = cb_ref[...] + cw[0:1, :] * p2 + cw[1:2, :] * p1 + cw[2:3, :] * u
    gc_ref[...] = (cb * cy).astype(BF16)

    q = _dot(xb, wqkv_ref[:, 0:da])
    q_ref[...] = (q * (1.0 / 8.0)).astype(q_ref.dtype)
    k = _dot(xb, wqkv_ref[:, da:2 * da])
    kf_ref[...] = k
    kbf_ref[...] = k.astype(BF16)
    v = _dot(xb, wqkv_ref[:, 2 * da:3 * da])
    vf_ref[...] = v
    vbf_ref[...] = v.astype(BF16)

    gw = g_ref.shape[1]
    for c in range(gw // 512):
        g_ref[:, c * 512:(c + 1) * 512] = _dot(xb, wg_ref[:, c * 512:(c + 1) * 512]).astype(BF16)

    xl = (xn - xb.astype(F32)).astype(BF16)
    fz = _dot(xb, wfh_ref[...]) + (_dot(xb, wfl_ref[...]) + _dot(xl, wfh_ref[...]))
    z = fz + bf_ref[...]
    logf = jnp.minimum(z, 0.0) - jnp.log1p(jnp.exp(-jnp.abs(z)))
    lane = lax.broadcasted_iota(jnp.int32, (1, LANES), 1)
    nh = da // 64
    logf = jnp.where(lane < nh, logf, 0.0)
    logf_ref[...] = logf

    rr = lax.broadcasted_iota(jnp.int32, (tm, tm), 0)
    cc_ = lax.broadcasted_iota(jnp.int32, (tm, tm), 1)
    if sample:
        tri = (cc_ <= rr) & ((rr // seq_len) == (cc_ // seq_len))
    else:
        tri = cc_ <= rr
    tri = jnp.where(tri, 1.0, 0.0).astype(BF16)
    hi, mid, lo = _split3(logf)
    d = _dot(tri, hi) + (_dot(tri, mid) + _dot(tri, lo))
    if not sample:
        d = d + carry_d[0:1, :]
        carry_d[0:1, :] = d[tm - 1:tm, :]
    dq_ref[...] = d
    dt_ref[...] = jnp.transpose(d)[0:SUBLANES, :]


def _inproj(x, gain, wc, wqkv, wg, wfh, wfl, bfp, cw, cbias, *, seq_len, sample, fix=None):
    t, dm = x.shape
    dc = cw.shape[1]
    da = wqkv.shape[1] // 3
    gw = wg.shape[1]
    if sample:
        tm = t
        seq_blocks = 1
    else:
        tm = min(512, seq_len)
        seq_blocks = seq_len // tm
    nblk = t // tm
    nseq = t // seq_len
    kern = functools.partial(_inproj_kernel, tm=tm, seq_blocks=seq_blocks, sample=sample, seq_len=seq_len)
    rows = lambda w: pl.BlockSpec((tm, w), lambda i: (i, 0))
    in_specs = [rows(dm), _const_spec((1, dm)), _const_spec(wc.shape), _const_spec(wqkv.shape), _const_spec(wg.shape),
                _const_spec(wfh.shape), _const_spec(wfl.shape), _const_spec((1, LANES)), _const_spec(cw.shape),
                _const_spec((1, dc))]
    args = [x, gain, wc, wqkv, wg, wfh, wfl, bfp, cw, cbias]
    if sample:
        in_specs += [rows(dc), rows(dc)]
        args += list(fix)
        u_shape = jax.ShapeDtypeStruct((t, dc), F32)
        u_spec = rows(dc)
        scratch = []
    else:
        u_shape = jax.ShapeDtypeStruct((nseq, 2, dc), F32)
        u_spec = pl.BlockSpec((None, 2, dc), lambda i: (i // seq_blocks, 0, 0))
        scratch = [pltpu.VMEM((SUBLANES, dc), F32), pltpu.VMEM((SUBLANES, LANES), F32)]
    out_shape = [jax.ShapeDtypeStruct((t, dc), BF16),
                 jax.ShapeDtypeStruct((t, da), BF16),
                 jax.ShapeDtypeStruct((t, da), BF16),
                 jax.ShapeDtypeStruct((t, da), BF16),
                 jax.ShapeDtypeStruct((t, da), F32),
                 jax.ShapeDtypeStruct((t, da), F32),
                 jax.ShapeDtypeStruct((t, gw), BF16),
                 jax.ShapeDtypeStruct((t, LANES), F32),
                 jax.ShapeDtypeStruct((t, LANES), F32),
                 jax.ShapeDtypeStruct((SUBLANES, t), F32),
                 u_shape]
    out_specs = [rows(dc), rows(da), rows(da), rows(da), rows(da), rows(da), rows(gw), rows(LANES), rows(LANES),
                 pl.BlockSpec((SUBLANES, tm), lambda i: (0, i)), u_spec]
    return pl.pallas_call(
        kern, grid=(nblk,), in_specs=in_specs, out_specs=out_specs, out_shape=out_shape,
        scratch_shapes=scratch, compiler_params=_cparams(("arbitrary",)),
        name="inproj_sample" if sample else "inproj_prompt")(*args)


def _fox_prompt_kernel(q_ref, k_ref, v_ref, dq_ref, dt_ref, o_ref, m_sc, l_sc, acc_sc, *, tq, dh):
    hp = pl.program_id(1)
    qi = pl.program_id(2)
    lane = lax.broadcasted_iota(jnp.int32, (1, LANES), 1)
    q2 = q_ref[...]
    dqb = dq_ref[...]
    heads = LANES // dh
    rloc = lax.broadcasted_iota(jnp.int32, (tq, tq), 0)
    cloc = lax.broadcasted_iota(jnp.int32, (tq, tq), 1)
    outs = []
    for h in range(heads):
        head_lanes = (lane // dh) == h
        qm = jnp.where(head_lanes, q2.astype(F32), 0.0).astype(BF16)
        hrow = heads * hp + h
        dqh = jnp.sum(jnp.where(lane == hrow, dqb, 0.0), axis=1, keepdims=True)
        m_sc[h] = jnp.full((tq, 1), NEG_BIG, F32)
        l_sc[h] = jnp.zeros((tq, 1), F32)
        acc_sc[h] = jnp.zeros((tq, LANES), F32)

        def step(j, masked, h=h, qm=qm, dqh=dqh, hrow=hrow):
            ks = pl.multiple_of(j * tq, tq)
            kb = k_ref[pl.ds(ks, tq), :]
            vb = v_ref[pl.ds(ks, tq), :]
            dk = dt_ref[pl.ds(hrow, 1), pl.ds(ks, tq)]
            s = _dot_nt(qm, kb) + (dqh - dk)
            if masked:
                s = jnp.where(cloc <= rloc, s, NEG_BIG)
            m_prev = m_sc[h]
            m_new = jnp.maximum(m_prev, jnp.max(s, axis=1, keepdims=True))
            p = jnp.exp(s - m_new)
            alpha = jnp.exp(m_prev - m_new)
            l_sc[h] = alpha * l_sc[h] + jnp.sum(p, axis=1, keepdims=True)
            acc_sc[h] = alpha * acc_sc[h] + _dot(p.astype(BF16), vb)
            m_sc[h] = m_new

        def body(j, c):
            step(j, False)
            return c

        lax.fori_loop(0, qi, body, 0)
        step(qi, True)
        outs.append(acc_sc[h] / l_sc[h])
    o = outs[0]
    for h in range(1, heads):
        o = jnp.where((lane // dh) == h, outs[h], o)
    o_ref[...] = o.astype(o_ref.dtype)


def _fox_prompt(q, k, v, dq, dt, *, nseq, seq_len, dh):
    t, da = q.shape
    tq = min(256, seq_len)
    nq = seq_len // tq
    npair = da // LANES
    kern = functools.partial(_fox_prompt_kernel, tq=tq, dh=dh)
    heads = LANES // dh
    return pl.pallas_call(
        kern, grid=(nseq, npair, nq),
        in_specs=[pl.BlockSpec((tq, LANES), lambda b, hp, qi: (b * nq + qi, hp)),
                  pl.BlockSpec((seq_len, LANES), lambda b, hp, qi: (b, hp)),
                  pl.BlockSpec((seq_len, LANES), lambda b, hp, qi: (b, hp)),
                  pl.BlockSpec((tq, LANES), lambda b, hp, qi: (b * nq + qi, 0)),
                  pl.BlockSpec((SUBLANES, seq_len), lambda b, hp, qi: (0, b))],
        out_specs=pl.BlockSpec((tq, LANES), lambda b, hp, qi: (b * nq + qi, hp)),
        out_shape=jax.ShapeDtypeStruct((t, da), BF16),
        scratch_shapes=[pltpu.VMEM((heads, tq, 1), F32), pltpu.VMEM((heads, tq, 1), F32),
                        pltpu.VMEM((heads, tq, LANES), F32)],
        compiler_params=_cparams(("parallel", "parallel", "arbitrary")),
        name="fox_prompt")(q, k, v, dq, dt)


def _rev_kernel(pt_ref, *refs, pp, page):
    lf_refs = refs[:pp]
    o_ref = refs[pp]
    run = refs[pp + 1]
    j = pl.program_id(1)

    @pl.when(j == 0)
    def _():
        run[...] = jnp.zeros_like(run)

    rr = lax.broadcasted_iota(jnp.int32, (page, page), 0)
    cc = lax.broadcasted_iota(jnp.int32, (page, page), 1)
    upper = jnp.where(rr > cc, 1.0, 0.0).astype(BF16)
    ones = jnp.ones((page, LANES), BF16)
    for i in range(pp):
        lf = lf_refs[i][...]
        hi, mid, lo = _split3(lf)
        inner = _dot(hi, upper) + (_dot(mid, upper) + _dot(lo, upper))
        tot = _dot(hi, ones) + (_dot(mid, ones) + _dot(lo, ones))
        r = run[...]
        o_ref[:, (pp - 1 - i) * page:(pp - i) * page] = inner + r
        run[...] = r + tot


def _rev_sums(page_table_flat, logf_pages, *, nb, npages, page):
    pp = 8
    nsteps = npages // pp
    heads = logf_pages.shape[1]
    kern = functools.partial(_rev_kernel, pp=pp, page=page)

    def lf_map(i):
        return lambda b, j, pt: (pt[b * npages + (npages - 1 - (j * pp + i))], 0, 0)

    grid_spec = pltpu.PrefetchScalarGridSpec(
        num_scalar_prefetch=1, grid=(nb, nsteps),
        in_specs=[pl.BlockSpec((None, heads, page), lf_map(i)) for i in range(pp)],
        out_specs=pl.BlockSpec((None, SUBLANES, pp * page), lambda b, j, pt: (b, 0, nsteps - 1 - j)),
        scratch_shapes=[pltpu.VMEM((SUBLANES, LANES), F32)])
    return pl.pallas_call(
        kern, grid_spec=grid_spec,
        out_shape=jax.ShapeDtypeStruct((nb, SUBLANES, npages * page), F32),
        compiler_params=_cparams(("parallel", "arbitrary")),
        name="rev_sums")(page_table_flat, *([logf_pages] * pp))


def _fox_sample_kernel(pt_ref, *refs, pp, page, s_new, heads, dh):
    k_refs = refs[:pp]
    v_refs = refs[pp:2 * pp]
    qbd_ref, rev_ref, kn_ref, vn_ref, dtn_ref, o_ref, m_sc, l_sc, acc_sc = refs[2 * pp:]
    b = pl.program_id(0)
    j = pl.program_id(1)
    nrow = s_new * heads
    da = heads * dh

    @pl.when(j == 0)
    def _():
        m_sc[...] = jnp.full_like(m_sc, NEG_BIG)
        l_sc[...] = jnp.zeros_like(l_sc)
        acc_sc[...] = jnp.zeros_like(acc_sc)

    qbd = qbd_ref[...]

    def online(s, vb):
        m_prev = m_sc[...]
        m_new = jnp.maximum(m_prev, jnp.max(s, axis=1, keepdims=True))
        p = jnp.exp(s - m_new)
        alpha = jnp.exp(m_prev - m_new)
        l_sc[...] = alpha * l_sc[...] + jnp.sum(p, axis=1, keepdims=True)
        acc_sc[...] = alpha * acc_sc[...] + _dot(p.astype(BF16), vb)
        m_sc[...] = m_new

    for i in range(pp):
        kb = k_refs[i][...].astype(BF16)
        vb = v_refs[i][...].astype(BF16)
        rev8 = rev_ref[:, i * page:(i + 1) * page]
        rev = jnp.concatenate([rev8] * s_new, axis=0)
        online(_dot_nt(qbd, kb) + rev, vb)

    @pl.when(j == pl.num_programs(1) - 1)
    def _():
        ntok = kn_ref.shape[0]
        dtn = dtn_ref[...]
        lane = lax.broadcasted_iota(jnp.int32, (1, ntok), 1)
        dq_rows = [jnp.sum(jnp.where(lane == b * s_new + t, dtn, 0.0), axis=1, keepdims=True) for t in range(s_new)]
        dq = jnp.concatenate(dq_rows, axis=0)
        dk = jnp.concatenate([dtn] * s_new, axis=0)
        s = _dot_nt(qbd, kn_ref[...]) + (dq - dk)
        rowt = lax.broadcasted_iota(jnp.int32, (nrow, ntok), 0) // heads
        col = lax.broadcasted_iota(jnp.int32, (nrow, ntok), 1)
        keep = ((col // s_new) == b) & ((col % s_new) <= rowt)
        s = jnp.where(keep, s, NEG_BIG)
        online(s, vn_ref[...])
        o = acc_sc[...] / l_sc[...]
        rowh = lax.broadcasted_iota(jnp.int32, (nrow, da), 0) % heads
        colh = lax.broadcasted_iota(jnp.int32, (nrow, da), 1) // dh
        o = jnp.where(rowh == colh, o, 0.0)
        o_ref[...] = jnp.sum(o.reshape(s_new, heads, da), axis=1)


def _fox_sample(page_table_flat, k_pages, v_pages, qbd, rev, kn, vn, dtn, *, nb, npages, page, s_new, heads, dh):
    pp = 8
    nsteps = npages // pp
    da = heads * dh
    nrow = s_new * heads
    ntok = kn.shape[0]
    kern = functools.partial(_fox_sample_kernel, pp=pp, page=page, s_new=s_new, heads=heads, dh=dh)

    def page_map(i):
        return lambda b, j, pt: (pt[b * npages + j * pp + i], 0, 0)

    page_specs = [pl.BlockSpec((None, page, da), page_map(i)) for i in range(pp)]
    grid_spec = pltpu.PrefetchScalarGridSpec(
        num_scalar_prefetch=1, grid=(nb, nsteps),
        in_specs=page_specs + page_specs + [
            pl.BlockSpec((None, nrow, da), lambda b, j, pt: (b, 0, 0)),
            pl.BlockSpec((None, SUBLANES, pp * page), lambda b, j, pt: (b, 0, j)),
            pl.BlockSpec((ntok, da), lambda b, j, pt: (0, 0)),
            pl.BlockSpec((ntok, da), lambda b, j, pt: (0, 0)),
            pl.BlockSpec((SUBLANES, ntok), lambda b, j, pt: (0, 0))],
        out_specs=pl.BlockSpec((None, s_new, da), lambda b, j, pt: (b, 0, 0)),
        scratch_shapes=[pltpu.VMEM((nrow, 1), F32), pltpu.VMEM((nrow, 1), F32), pltpu.VMEM((nrow, da), F32)])
    return pl.pallas_call(
        kern, grid_spec=grid_spec,
        out_shape=jax.ShapeDtypeStruct((nb, s_new, da), F32),
        compiler_params=_cparams(("parallel", "arbitrary")),
        name="fox_sample")(page_table_flat, *([k_pages] * pp), *([v_pages] * pp), qbd, rev, kn, vn, dtn)


def _post_attn_kernel(x_ref, gc_ref, at_ref, g_ref, wuc_ref, wua_ref, wmo_ref, gain_ref, wxq_ref, h_ref, qx_ref):
    dm = x_ref.shape[1]
    y_conv = _dot(gc_ref[...], wuc_ref[...])
    y_attn = _dot(at_ref[...].astype(BF16), wua_ref[...])
    g_conv = g_ref[:, 0:dm].astype(F32)
    g_attn = g_ref[:, dm:2 * dm].astype(F32)
    mixed = jax.nn.sigmoid(g_conv) * y_conv + jax.nn.sigmoid(g_attn) * y_attn
    h = x_ref[...] + _dot(mixed.astype(BF16), wmo_ref[...])
    h_ref[...] = h
    xn = _rms(h, gain_ref[...]).astype(BF16)
    qx_ref[...] = _dot(xn, wxq_ref[...]).astype(qx_ref.dtype)


def _post_attn(x, gc, at, g, wuc, wua, wmo, gain, wxq, *, tm, name):
    t, dm = x.shape
    dx = wxq.shape[1]
    rows = lambda w: pl.BlockSpec((tm, w), lambda i: (i, 0))
    return pl.pallas_call(
        _post_attn_kernel, grid=(t // tm,),
        in_specs=[rows(dm), rows(gc.shape[1]), rows(at.shape[1]), rows(g.shape[1]), _const_spec(wuc.shape),
                  _const_spec(wua.shape), _const_spec(wmo.shape), _const_spec((1, dm)), _const_spec(wxq.shape)],
        out_specs=[rows(dm), rows(dx)],
        out_shape=[jax.ShapeDtypeStruct((t, dm), F32), jax.ShapeDtypeStruct((t, dx), BF16)],
        compiler_params=_cparams(("parallel",)), name=name)(x, gc, at, g, wuc, wua, wmo, gain, wxq)


def _memkv_kernel(m_ref, gain_ref, wk_ref, wv_ref, k_ref, v_ref):
    mn = _rms(m_ref[...], gain_ref[...]).astype(BF16)
    k_ref[...] = _dot(mn, wk_ref[...])
    v_ref[...] = _dot(mn, wv_ref[...])


def _memkv(mem, gain, wk, wv):
    t, dm = mem.shape
    tm = min(512, t)
    dx = wk.shape[1]
    rows = lambda w: pl.BlockSpec((tm, w), lambda i: (i, 0))
    return pl.pallas_call(
        _memkv_kernel, grid=(t // tm,),
        in_specs=[rows(dm), _const_spec((1, dm)), _const_spec(wk.shape), _const_spec(wv.shape)],
        out_specs=[rows(dx), rows(dx)],
        out_shape=[jax.ShapeDtypeStruct((t, dx), F32)] * 2,
        compiler_params=_cparams(("parallel",)), name="memkv")(mem, gain, wk, wv)


def _xattn_kernel(q_ref, mk_ref, mv_ref, o_ref, *, xh, xd):
    q = q_ref[...].astype(BF16)
    mk = mk_ref[...].astype(BF16)
    mv = mv_ref[...].astype(BF16)
    scale = xd ** -0.5
    outs = []
    for h in range(xh):
        sl = slice(h * xd, (h + 1) * xd)
        s = _dot_nt(q[:, sl], mk[:, sl]) * scale
        m = jnp.max(s, axis=1, keepdims=True)
        p = jnp.exp(s - m)
        p = p / jnp.sum(p, axis=1, keepdims=True)
        outs.append(_dot(p.astype(BF16), mv[:, sl]))
    o_ref[...] = jnp.concatenate(outs, axis=1).astype(o_ref.dtype)


def _xattn(q, mk, mv, *, tq, xh, name):
    nb, s, dx = q.shape
    nm = mk.shape[1]
    kern = functools.partial(_xattn_kernel, xh=xh, xd=dx // xh)
    return pl.pallas_call(
        kern, grid=(nb, s // tq),
        in_specs=[pl.BlockSpec((None, tq, dx), lambda b, i: (b, i, 0)),
                  pl.BlockSpec((None, nm, dx), lambda b, i: (b, 0, 0)),
                  pl.BlockSpec((None, nm, dx), lambda b, i: (b, 0, 0))],
        out_specs=pl.BlockSpec((None, tq, dx), lambda b, i: (b, i, 0)),
        out_shape=jax.ShapeDtypeStruct((nb, s, dx), q.dtype),
        compiler_params=_cparams(("parallel", "parallel")), name=name)(q, mk, mv)


def _pre_moe_kernel(h_ref, o_ref, wxo_ref, gain_ref, wrh_ref, wrl_ref, cnt_in_ref,
                    h2_ref, xn_ref, rt_ref, rr_ref, cnt_out_ref, base, *, tm, ngroups, epg):
    i = pl.program_id(0)

    @pl.when(i == 0)
    def _():
        base[...] = cnt_in_ref[...]

    h2 = h_ref[...] + _dot(o_ref[...].astype(BF16), wxo_ref[...])
    h2_ref[...] = h2
    xn = _rms(h2, gain_ref[...])
    xn_ref[...] = xn

    xh = xn.astype(BF16)
    xl = (xn - xh.astype(F32)).astype(BF16)
    lt = _dot_nt(wrh_ref[...], xh) + (_dot_nt(wrh_ref[...], xl) + _dot_nt(wrl_ref[...], xh))

    sub = lax.broadcasted_iota(jnp.int32, (SUBLANES, tm), 0)
    gl = jnp.where(sub < ngroups, lt[0:SUBLANES, :], -jnp.inf)
    gmax = jnp.max(gl, axis=0, keepdims=True)
    gidx = jnp.min(jnp.where(gl == gmax, sub, SUBLANES), axis=0, keepdims=True)
    pg = 1.0 / jnp.sum(jnp.exp(gl - gmax), axis=0, keepdims=True)
    el = jnp.zeros((epg, tm), F32)
    for g in range(ngroups):
        el = jnp.where(gidx == g, lt[EXPERT_ROW0 + g * epg:EXPERT_ROW0 + (g + 1) * epg, :], el)
    v1 = jnp.max(el, axis=0, keepdims=True)
    i1 = jnp.min(jnp.where(el == v1, sub, epg), axis=0, keepdims=True)
    el2 = jnp.where(sub == i1, -jnp.inf, el)
    v2 = jnp.max(el2, axis=0, keepdims=True)
    i2 = jnp.min(jnp.where(el2 == v2, sub, epg), axis=0, keepdims=True)
    t2 = jnp.exp(v2 - v1)
    den = 1.0 + t2
    w0 = (1.0 / den) * pg
    w1 = (t2 / den) * pg
    e0 = gidx * epg + i1
    e1 = gidx * epg + i2

    erow = lax.broadcasted_iota(jnp.int32, (LANES, tm), 0)
    oh0 = erow == e0
    oh1 = erow == e1
    rr = lax.broadcasted_iota(jnp.int32, (tm, tm), 0)
    cc = lax.broadcasted_iota(jnp.int32, (tm, tm), 1)
    triu = jnp.where(rr <= cc, 1.0, 0.0).astype(BF16)
    pre0 = _dot(jnp.where(oh0, 1.0, 0.0).astype(BF16), triu)
    pre1 = _dot(jnp.where(oh1, 1.0, 0.0).astype(BF16), triu)
    b0 = base[:, 0:1]
    tot0 = pre0[:, tm - 1:tm]
    tot1 = pre1[:, tm - 1:tm]
    rank0 = jnp.sum(jnp.where(oh0, pre0 - 1.0 + b0, 0.0), axis=0, keepdims=True)
    rank1 = jnp.sum(jnp.where(oh1, pre1 - 1.0 + (b0 + tot0), 0.0), axis=0, keepdims=True)
    newb = b0 + tot0 + tot1
    base[...] = jnp.broadcast_to(newb, base.shape)
    cnt_out_ref[...] = jnp.broadcast_to(newb, cnt_out_ref.shape)

    zero = jnp.zeros((1, tm), F32)
    rt = jnp.concatenate([e0.astype(F32), e1.astype(F32), w0, w1, rank0, rank1, zero, zero], axis=0)
    rt_ref[...] = rt
    rt_pad = jnp.concatenate([rt, jnp.zeros((LANES - ROUTE_ROWS, tm), F32)], axis=0)
    rr_ref[...] = jnp.transpose(rt_pad)


def _pre_moe(h, o, wxo, gain, wrh, wrl, cnt_in, *, tm, ngroups, epg, name):
    t, dm = h.shape
    dx = o.shape[1]
    kern = functools.partial(_pre_moe_kernel, tm=tm, ngroups=ngroups, epg=epg)
    rows = lambda w: pl.BlockSpec((tm, w), lambda i: (i, 0))
    return pl.pallas_call(
        kern, grid=(t // tm,),
        in_specs=[rows(dm), rows(dx), _const_spec(wxo.shape), _const_spec((1, dm)), _const_spec(wrh.shape),
                  _const_spec(wrl.shape), _const_spec((LANES, LANES))],
        out_specs=[rows(dm), rows(dm), pl.BlockSpec((ROUTE_ROWS, tm), lambda i: (0, i)), rows(LANES),
                   _const_spec((LANES, LANES))],
        out_shape=[jax.ShapeDtypeStruct((t, dm), F32), jax.ShapeDtypeStruct((t, dm), F32),
                   jax.ShapeDtypeStruct((ROUTE_ROWS, t), F32), jax.ShapeDtypeStruct((t, LANES), F32),
                   jax.ShapeDtypeStruct((LANES, LANES), F32)],
        scratch_shapes=[pltpu.VMEM((LANES, LANES), F32)],
        compiler_params=_cparams(("arbitrary",)), name=name)(h, o, wxo, gain, wrh, wrl, cnt_in)


def _row_copy(src, r_src, dst, r_dst, sem):
    return pltpu.make_async_copy(src.at[pl.ds(r_src, 1)], dst.at[pl.ds(r_dst, 1)], sem)


def _scatter_kernel(e_ref, rank_ref, pstart_ref, pend_ref, xp_ref, xs_ref, out_ref, zeros, sem, zsem,
                    *, tm_p, nblk_p, nexp):
    i = pl.program_id(0)
    tall = rank_ref.shape[0] // TOP_K

    @pl.when(i == 0)
    def _():
        zeros[...] = jnp.zeros_like(zeros)

        def zero_block(blk):
            start = pl.multiple_of(blk * EXPERT_ROWS, EXPERT_ROWS)
            return pltpu.make_async_copy(zeros, out_ref.at[pl.ds(start, EXPERT_ROWS)], zsem)

        for e in range(nexp):
            @pl.when(pend_ref[e] > pstart_ref[e])
            def _():
                zero_block(pend_ref[e] // EXPERT_ROWS - 1).start()
        for e in range(nexp):
            @pl.when(pend_ref[e] > pstart_ref[e])
            def _():
                zero_block(pend_ref[e] // EXPERT_ROWS - 1).wait()
        nblk = out_ref.shape[0] // EXPERT_ROWS
        nused = pend_ref[nexp - 1] // EXPERT_ROWS

        def start_unused(blk, c):
            zero_block(blk).start()
            return c

        def wait_unused(blk, c):
            zero_block(blk).wait()
            return c

        lax.fori_loop(nused, nblk, start_unused, 0)
        lax.fori_loop(nused, nblk, wait_unused, 0)

    def copy_rows(x_ref, base):
        tm = x_ref.shape[0]

        def issue(r, c):
            for k in range(TOP_K):
                a = k * tall + base + r
                dest = pstart_ref[e_ref[a]] + rank_ref[a]
                _row_copy(x_ref, r, out_ref, dest, sem).start()
            return c

        lax.fori_loop(0, tm, issue, 0)
        for k in range(TOP_K):
            pltpu.make_async_copy(x_ref, x_ref, sem).wait()

    @pl.when(i < nblk_p)
    def _():
        copy_rows(xp_ref, i * tm_p)

    @pl.when(i == nblk_p)
    def _():
        copy_rows(xs_ref, nblk_p * tm_p)


def _scatter(e_flat, rank_flat, pstart, pend, x_p, x_s, *, n_rows, tm_p, nexp):
    tp, dm = x_p.shape
    ts = x_s.shape[0]
    nblk_p = tp // tm_p
    kern = functools.partial(_scatter_kernel, tm_p=tm_p, nblk_p=nblk_p, nexp=nexp)
    grid_spec = pltpu.PrefetchScalarGridSpec(
        num_scalar_prefetch=4, grid=(nblk_p + 1,),
        in_specs=[pl.BlockSpec((tm_p, dm), lambda i, *_: (jnp.minimum(i, nblk_p - 1), 0)),
                  pl.BlockSpec((ts, dm), lambda i, *_: (0, 0))],
        out_specs=pl.BlockSpec(memory_space=pl.ANY),
        scratch_shapes=[pltpu.VMEM((EXPERT_ROWS, dm), F32), pltpu.SemaphoreType.DMA(()), pltpu.SemaphoreType.DMA(())])
    return pl.pallas_call(
        kern, grid_spec=grid_spec, out_shape=jax.ShapeDtypeStruct((n_rows, dm), F32),
        compiler_params=_cparams(("arbitrary",)), name="scatter")(e_flat, rank_flat, pstart, pend, x_p, x_s)


def _experts_kernel(blk_e_ref, nused_ref, xs_ref, wg_ref, wu_ref, wd_ref, y_ref, wgb, wub, wdb):
    i = pl.program_id(0)
    prev = blk_e_ref[jnp.maximum(i - 1, 0)]
    fresh = (i == 0) | (blk_e_ref[i] != prev)

    @pl.when(i < nused_ref[0])
    def _():
        @pl.when(fresh)
        def _():
            wgb[...] = wg_ref[...].astype(BF16)
            wub[...] = wu_ref[...].astype(BF16)
            wdb[...] = wd_ref[...].astype(BF16)
        x = xs_ref[...].astype(BF16)
        a = _dot(x, wgb[...])
        u = _dot(x, wub[...])
        hmid = (a * jax.nn.sigmoid(a)) * u
        y_ref[...] = _dot(hmid.astype(BF16), wdb[...])

    @pl.when(i >= nused_ref[0])
    def _():
        y_ref[...] = jnp.zeros_like(y_ref)


def _experts(blk_e, nused, xs, wg, wu, wd):
    n_rows, dm = xs.shape
    de = wg.shape[2]
    nblk = n_rows // EXPERT_ROWS

    def row_map(i, be, nu):
        return (jnp.minimum(i, nu[0] - 1), 0)

    grid_spec = pltpu.PrefetchScalarGridSpec(
        num_scalar_prefetch=2, grid=(nblk,),
        in_specs=[pl.BlockSpec((EXPERT_ROWS, dm), row_map),
                  pl.BlockSpec((None, dm, de), lambda i, be, nu: (be[i], 0, 0)),
                  pl.BlockSpec((None, dm, de), lambda i, be, nu: (be[i], 0, 0)),
                  pl.BlockSpec((None, de, dm), lambda i, be, nu: (be[i], 0, 0))],
        out_specs=pl.BlockSpec((EXPERT_ROWS, dm), lambda i, be, nu: (i, 0)),
        scratch_shapes=[pltpu.VMEM((dm, de), BF16), pltpu.VMEM((dm, de), BF16), pltpu.VMEM((de, dm), BF16)])
    return pl.pallas_call(
        _experts_kernel, grid_spec=grid_spec, out_shape=jax.ShapeDtypeStruct((n_rows, dm), F32),
        compiler_params=_cparams(("arbitrary",)), name="experts")(blk_e, nused, xs, wg, wu, wd)


def _combine_kernel(e_ref, rank_ref, pstart_ref, h_ref, rr_ref, gain_ref, y_hbm, o_ref, buf, sem, *, tm, tok0):
    i = pl.program_id(0)
    base = tok0 + i * tm

    def issue(r, c):
        for k in range(TOP_K):
            a = k * rank_ref.shape[0] // TOP_K + base + r
            dest = pstart_ref[e_ref[a]] + rank_ref[a]
            _row_copy(y_hbm, dest, buf.at[k], r, sem).start()
        return c

    lax.fori_loop(0, tm, issue, 0)
    for k in range(TOP_K):
        pltpu.make_async_copy(buf.at[k], buf.at[k], sem).wait()
    rr = rr_ref[...]
    h = h_ref[...] + (rr[:, 2:3] * buf[0] + rr[:, 3:4] * buf[1])
    o_ref[...] = _rms(h, gain_ref[...])


def _combine(e_flat, rank_flat, pstart, h, rr, gain, y, *, tm, tok0, name):
    t, dm = h.shape
    kern = functools.partial(_combine_kernel, tm=tm, tok0=tok0)
    grid_spec = pltpu.PrefetchScalarGridSpec(
        num_scalar_prefetch=3, grid=(t // tm,),
        in_specs=[pl.BlockSpec((tm, dm), lambda i, *_: (i, 0)), pl.BlockSpec((tm, LANES), lambda i, *_: (i, 0)),
                  pl.BlockSpec((1, dm), lambda i, *_: (0, 0)), pl.BlockSpec(memory_space=pl.ANY)],
        out_specs=pl.BlockSpec((tm, dm), lambda i, *_: (i, 0)),
        scratch_shapes=[pltpu.VMEM((TOP_K, tm, dm), F32), pltpu.SemaphoreType.DMA(())])
    return pl.pallas_call(
        kern, grid_spec=grid_spec, out_shape=jax.ShapeDtypeStruct((t, dm), F32),
        compiler_params=_cparams(("arbitrary",)), name=name)(e_flat, rank_flat, pstart, h, rr, gain, y)


def kernel(x_prompt, x_sample, cache_k, cache_v, cache_logf, cache_mem_k, cache_mem_v, state_conv, page_table,
           mem_prompt, norm_mix, w_in, b_forget, conv_w, conv_b, w_up_conv, w_up_attn, w_mix_out, norm_xattn,
           norm_mem, w_xq, w_xk, w_xv, w_xo, norm_ffn, w_router_group, w_router_expert, w_expert_gate,
           w_expert_up, w_expert_down, norm_final):
    depth = w_in.shape[0]
    assert depth == 1, "single-layer trunk"
    nbp, seq, dm = x_prompt.shape
    nbs, s_new, _ = x_sample.shape
    _, n_pool, page, heads, dh = cache_k.shape
    npages = page_table.shape[1]
    nmem, xh, xd = cache_mem_k.shape[2:]
    dc = conv_w.shape[2]
    da = heads * dh
    dx = xh * xd
    ngroups, _, epg = w_router_expert.shape[1:]
    nexp = ngroups * epg
    tp = nbp * seq
    ts = nbs * s_new
    assert conv_w.shape[1] == 3 and s_new >= 2 and dh == 64 and heads == SUBLANES and epg == SUBLANES

    l = 0
    wi = w_in[l]
    wc = wi[:, 0:3 * dc].astype(BF16)
    wqkv = wi[:, 3 * dc:3 * dc + 3 * da].astype(BF16)
    o_f = 3 * dc + 3 * da
    wf = jnp.pad(wi[:, o_f:o_f + heads], ((0, 0), (0, LANES - heads)))
    wfh = wf.astype(BF16)
    wfl = (wf - wfh.astype(F32)).astype(BF16)
    wg = wi[:, o_f + heads:].astype(BF16)
    bfp = jnp.pad(b_forget[l][None, :], ((0, 0), (0, LANES - heads)))
    cw = conv_w[l]
    cbias = conv_b[l][None, :]
    g_mix = norm_mix[l][None, :]
    wuc = w_up_conv[l].astype(BF16)
    wua = w_up_attn[l].astype(BF16)
    wmo = w_mix_out[l].astype(BF16)
    g_x = norm_xattn[l][None, :]
    wxq = w_xq[l].astype(BF16)
    wxo = w_xo[l].astype(BF16)
    g_f = norm_ffn[l][None, :]
    wr = jnp.zeros((LANES, dm), F32)
    wr = wr.at[0:ngroups].set(w_router_group[l].T)
    wr = wr.at[EXPERT_ROW0:EXPERT_ROW0 + nexp].set(jnp.transpose(w_router_expert[l], (0, 2, 1)).reshape(nexp, dm))
    wrh = wr.astype(BF16)
    wrl = (wr - wrh.astype(F32)).astype(BF16)

    xp = x_prompt.reshape(tp, dm)
    (gc_p, q_p, kb_p, vb_p, kf_p, vf_p, g_p, lf_p, dq_p, dt_p, ulast_p) = _inproj(
        xp, g_mix, wc, wqkv, wg, wfh, wfl, bfp, cw, cbias, seq_len=seq, sample=False)
    at_p = _fox_prompt(q_p, kb_p, vb_p, dq_p, dt_p, nseq=nbp, seq_len=seq, dh=dh)
    tm_p = min(512, seq)
    h1_p, qx_p = _post_attn(xp, gc_p, at_p, g_p, wuc, wua, wmo, g_x, wxq, tm=tm_p, name="post_attn_prompt")
    mk_p, mv_p = _memkv(mem_prompt.reshape(nbp * nmem, dm), norm_mem[l][None, :], w_xk[l].astype(BF16),
                        w_xv[l].astype(BF16))
    o_p = _xattn(qx_p.reshape(nbp, seq, dx), mk_p.reshape(nbp, nmem, dx), mv_p.reshape(nbp, nmem, dx),
                 tq=tm_p, xh=xh, name="xattn_prompt").reshape(tp, dx)
    cnt0 = jnp.zeros((LANES, LANES), F32)
    h2_p, xn_p, rt_p, rr_p, cnt1 = _pre_moe(h1_p, o_p, wxo, g_f, wrh, wrl, cnt0, tm=tm_p, ngroups=ngroups, epg=epg,
                                            name="pre_moe_prompt")

    xs_ = x_sample.reshape(ts, dm)
    st = state_conv[l]
    zeros_row = jnp.zeros((nbs, 1, dc), F32)
    fix1 = jnp.concatenate([st[:, 1:2], jnp.tile(zeros_row, (1, s_new - 1, 1))], axis=1).reshape(ts, dc)
    fix2 = jnp.concatenate([st[:, 0:1], st[:, 1:2], jnp.tile(zeros_row, (1, s_new - 2, 1))], axis=1).reshape(ts, dc)
    (gc_s, q_s, kb_s, vb_s, kf_s, vf_s, g_s, lf_s, dq_s, dt_s, u_s) = _inproj(
        xs_, g_mix, wc, wqkv, wg, wfh, wfl, bfp, cw, cbias, seq_len=s_new, sample=True, fix=(fix1, fix2))
    pt_flat = page_table.reshape(-1).astype(jnp.int32)
    rev = _rev_sums(pt_flat, jnp.swapaxes(cache_logf[l], 1, 2), nb=nbs, npages=npages, page=page)
    head_of_col = jnp.arange(da) // dh
    qbd = jnp.where(head_of_col[None, None, None, :] == jnp.arange(heads)[None, None, :, None],
                    q_s.reshape(nbs, s_new, 1, da), jnp.zeros((), BF16)).reshape(nbs, s_new * heads, da)
    at_s = _fox_sample(pt_flat, cache_k[l].reshape(n_pool, page, da), cache_v[l].reshape(n_pool, page, da),
                       qbd, rev, kb_s, vb_s, dt_s, nb=nbs, npages=npages, page=page, s_new=s_new, heads=heads, dh=dh)
    h1_s, qx_s = _post_attn(xs_, gc_s, at_s.reshape(ts, da), g_s, wuc, wua, wmo, g_x, wxq, tm=ts,
                            name="post_attn_sample")
    qx_s8 = jnp.pad(qx_s.astype(F32).reshape(nbs, s_new, dx), ((0, 0), (0, SUBLANES - s_new), (0, 0)))
    o_s = _xattn(qx_s8, cache_mem_k[l].reshape(nbs, nmem, dx), cache_mem_v[l].reshape(nbs, nmem, dx),
                 tq=SUBLANES, xh=xh, name="xattn_sample")[:, :s_new].reshape(ts, dx)
    h2_s, xn_s, rt_s, rr_s, cnt2 = _pre_moe(h1_s, o_s, wxo, g_f, wrh, wrl, cnt1, tm=ts, ngroups=ngroups, epg=epg,
                                            name="pre_moe_sample")

    tall = tp + ts
    counts = cnt2[0:nexp, 0].astype(jnp.int32)
    padded = (counts + EXPERT_ROWS - 1) // EXPERT_ROWS * EXPERT_ROWS
    pend = jnp.cumsum(padded).astype(jnp.int32)
    pstart = pend - padded
    nblk = (tall * TOP_K + nexp * (EXPERT_ROWS - 1)) // EXPERT_ROWS
    n_rows = nblk * EXPERT_ROWS
    blk_e = jnp.minimum(jnp.searchsorted(pend, jnp.arange(nblk, dtype=jnp.int32) * EXPERT_ROWS, side='right'),
                        nexp - 1).astype(jnp.int32)
    nused = (pend[nexp - 1:nexp] // EXPERT_ROWS).astype(jnp.int32)
    rt_all = jnp.concatenate([rt_p, rt_s], axis=1)
    e_flat = rt_all[0:TOP_K].astype(jnp.int32).reshape(-1)
    rank_flat = rt_all[4:4 + TOP_K].astype(jnp.int32).reshape(-1)
    xsg = _scatter(e_flat, rank_flat, pstart, pend, xn_p, xn_s, n_rows=n_rows, tm_p=tm_p, nexp=nexp)
    y = _experts(blk_e, nused, xsg, w_expert_gate[l], w_expert_up[l], w_expert_down[l])
    g_fin = norm_final[None, :]
    y_p = _combine(e_flat, rank_flat, pstart, h2_p, rr_p, g_fin, y, tm=tm_p // 2, tok0=0, name="combine_prompt")
    y_s = _combine(e_flat, rank_flat, pstart, h2_s, rr_s, g_fin, y, tm=ts, tok0=tp, name="combine_sample")

    return (y_p.reshape(nbp, seq, dm), y_s.reshape(nbs, s_new, dm),
            ulast_p[None], kf_p.reshape(1, nbp, seq, heads, dh), vf_p.reshape(1, nbp, seq, heads, dh),
            lf_p[:, :heads].reshape(1, nbp, seq, heads),
            mk_p.reshape(1, nbp, nmem, xh, xd), mv_p.reshape(1, nbp, nmem, xh, xd),
            u_s.reshape(nbs, s_new, dc)[None, :, s_new - 2:], kf_s.reshape(1, nbs, s_new, heads, dh),
            vf_s.reshape(1, nbs, s_new, heads, dh), lf_s[:, :heads].reshape(1, nbs, s_new, heads))
```

```python
import functools

import numpy as np
import jax
import jax.numpy as jnp
from jax import lax
from jax.experimental import pallas as pl
from jax.experimental.pallas import tpu as pltpu

F32 = jnp.float32
BF16 = jnp.bfloat16

RMS_EPS = 1e-6
TOP_K = 2
LANES = 128
SUBLANES = 8
VMEM_LIMIT = 56 * 1024 * 1024
NEG_BIG = -1e30
EXPERT_ROWS = 256
SAMPLE_PAGES_PER_STEP = 16
ROUTE_ROWS = 8
EXPERT_ROW0 = 8


def _cparams(sem, vmem=VMEM_LIMIT):
    return pltpu.CompilerParams(dimension_semantics=sem, vmem_limit_bytes=vmem)


def _rms(x, g):
    ms = jnp.mean(x * x, axis=-1, keepdims=True)
    return x * lax.rsqrt(ms + RMS_EPS) * g


def _split3(x):
    hi = x.astype(BF16)
    r = x - hi.astype(F32)
    mid = r.astype(BF16)
    lo = (r - mid.astype(F32)).astype(BF16)
    return hi, mid, lo


def _dot(a, b):
    return jnp.dot(a, b, preferred_element_type=F32)


def _dot_nt(a, b):
    return lax.dot_general(a, b, (((1,), (1,)), ((), ())), preferred_element_type=F32)


def _lane_tile(x, width):
    if width % LANES == 0:
        return jnp.concatenate([x] * (width // LANES), axis=1)
    return jnp.broadcast_to(x[:, 0:1], (x.shape[0], width))


def _const_spec(shape):
    nd = len(shape)
    return pl.BlockSpec(shape, lambda *_: (0,) * nd)


def _inproj_kernel(*refs, tm, seq_blocks, sample, seq_len):
    if sample:
        (x_ref, gain_ref, wc_ref, wqkv_ref, wg_ref, wfh_ref, wfl_ref, bf_ref, cw_ref, cb_ref, fix1_ref, fix2_ref,
         gc_ref, q_ref, kbf_ref, vbf_ref, kf_ref, vf_ref, g_ref, logf_ref, dt_ref, u_ref) = refs
    else:
        (x_ref, gain_ref, wc_ref, wqkv_ref, wg_ref, wfh_ref, wfl_ref, bf_ref, cw_ref, cb_ref,
         selq_ref, selk_ref, oneq_ref, onek_ref,
         gc_ref, qa_ref, ka_ref, vbf_ref, kf_ref, vf_ref, g_ref, logf_ref, u_ref,
         carry_u, carry_d) = refs
    i = pl.program_id(0)
    dc = cw_ref.shape[1]
    da = kf_ref.shape[1]

    xn = _rms(x_ref[...], gain_ref[...])
    xb = xn.astype(BF16)

    cb = _dot(xb, wc_ref[:, 0:dc])
    cc = _dot(xb, wc_ref[:, dc:2 * dc])
    cx = _dot(xb, wc_ref[:, 2 * dc:3 * dc])
    u = cc * cx
    row = lax.broadcasted_iota(jnp.int32, (tm, 1), 0)
    r1 = pltpu.roll(u, 1, axis=0)
    r2 = pltpu.roll(u, 2, axis=0)
    if sample:
        pos = row % seq_len
        p1 = jnp.where(pos == 0, fix1_ref[...], r1)
        p2 = jnp.where(pos < 2, fix2_ref[...], r2)
        u_ref[...] = u
    else:
        @pl.when(i % seq_blocks == 0)
        def _():
            carry_u[...] = jnp.zeros_like(carry_u)
            carry_d[...] = jnp.zeros_like(carry_d)
        c0 = carry_u[0:1, :]
        c1 = carry_u[1:2, :]
        p1 = jnp.where(row == 0, c1, r1)
        p2 = jnp.where(row == 0, c0, jnp.where(row == 1, c1, r2))
        carry_u[0:2, :] = u[tm - 2:tm, :]
        u_ref[...] = u[tm - 2:tm, :]
    cw = cw_ref[...]
    cy = cb_ref[...] + cw[0:1, :] * p2 + cw[1:2, :] * p1 + cw[2:3, :] * u
    gc_ref[...] = (cb * cy).astype(BF16)

    qs = (_dot(xb, wqkv_ref[:, 0:da]) * (1.0 / 8.0)).astype(BF16)
    k = _dot(xb, wqkv_ref[:, da:2 * da])
    kf_ref[...] = k
    kb = k.astype(BF16)
    if sample:
        q_ref[...] = qs
        kbf_ref[...] = kb
    v = _dot(xb, wqkv_ref[:, 2 * da:3 * da])
    vf_ref[...] = v
    vbf_ref[...] = v.astype(BF16)

    gw = g_ref.shape[1]
    for c in range(gw // 512):
        g_ref[:, c * 512:(c + 1) * 512] = _dot(xb, wg_ref[:, c * 512:(c + 1) * 512]).astype(BF16)

    xl = (xn - xb.astype(F32)).astype(BF16)
    fz = _dot(xb, wfh_ref[...]) + (_dot(xb, wfl_ref[...]) + _dot(xl, wfh_ref[...]))
    z = fz + bf_ref[...]
    logf = jnp.minimum(z, 0.0) - jnp.log1p(jnp.exp(-jnp.abs(z)))
    lane = lax.broadcasted_iota(jnp.int32, (1, LANES), 1)
    nh = da // 64
    logf = jnp.where(lane < nh, logf, 0.0)
    logf_ref[...] = logf

    rr = lax.broadcasted_iota(jnp.int32, (tm, tm), 0)
    cc_ = lax.broadcasted_iota(jnp.int32, (tm, tm), 1)
    if sample:
        tri = (cc_ <= rr) & ((rr // seq_len) == (cc_ // seq_len))
    else:
        tri = cc_ <= rr
    tri = jnp.where(tri, 1.0, 0.0).astype(BF16)
    hi, mid, lo = _split3(logf)
    d = _dot(tri, hi) + (_dot(tri, mid) + _dot(tri, lo))
    if sample:
        dt_ref[...] = jnp.transpose(d)[0:SUBLANES, :]
    else:
        d = d + carry_d[0:1, :]
        carry_d[0:1, :] = d[tm - 1:tm, :]
        dcat = jnp.concatenate(_split3(d), axis=1)
        aq = (_dot(dcat, selq_ref[...]) + oneq_ref[...]).astype(BF16)
        ak = (_dot(dcat, selk_ref[...]) + onek_ref[...]).astype(BF16)
        for p in range(da // LANES):
            lo_, hi_ = p * LANES, (p + 1) * LANES
            qa_ref[:, 2 * lo_:2 * lo_ + LANES] = qs[:, lo_:hi_]
            qa_ref[:, 2 * lo_ + LANES:2 * hi_] = aq[:, lo_:hi_]
            ka_ref[:, 2 * lo_:2 * lo_ + LANES] = kb[:, lo_:hi_]
            ka_ref[:, 2 * lo_ + LANES:2 * hi_] = ak[:, lo_:hi_]


def _decay_columns(heads):
    npair = heads // 2
    selq = np.zeros((3 * LANES, npair * LANES), np.float32)
    selk = np.zeros((3 * LANES, npair * LANES), np.float32)
    oneq = np.zeros((1, npair * LANES), np.float32)
    onek = np.zeros((1, npair * LANES), np.float32)
    for p in range(npair):
        for hh in range(2):
            for term in range(3):
                selq[term * LANES + 2 * p + hh, p * LANES + 3 * hh + term] = 1.0
                selk[term * LANES + 2 * p + hh, p * LANES + 6 + 3 * hh + term] = -1.0
                oneq[0, p * LANES + 6 + 3 * hh + term] = 1.0
                onek[0, p * LANES + 3 * hh + term] = 1.0
    return jnp.asarray(selq, BF16), jnp.asarray(selk, BF16), jnp.asarray(oneq), jnp.asarray(onek)


def _inproj(x, gain, wc, wqkv, wg, wfh, wfl, bfp, cw, cbias, *, seq_len, sample, fix=None, aug=None):
    t, dm = x.shape
    dc = cw.shape[1]
    da = wqkv.shape[1] // 3
    gw = wg.shape[1]
    if sample:
        tm = t
        seq_blocks = 1
    else:
        tm = min(512, seq_len)
        seq_blocks = seq_len // tm
    nblk = t // tm
    nseq = t // seq_len
    kern = functools.partial(_inproj_kernel, tm=tm, seq_blocks=seq_blocks, sample=sample, seq_len=seq_len)
    rows = lambda w: pl.BlockSpec((tm, w), lambda i: (i, 0))
    in_specs = [rows(dm), _const_spec((1, dm)), _const_spec(wc.shape), _const_spec(wqkv.shape), _const_spec(wg.shape),
                _const_spec(wfh.shape), _const_spec(wfl.shape), _const_spec((1, LANES)), _const_spec(cw.shape),
                _const_spec((1, dc))]
    args = [x, gain, wc, wqkv, wg, wfh, wfl, bfp, cw, cbias]
    sds = jax.ShapeDtypeStruct
    if sample:
        in_specs += [rows(dc), rows(dc)]
        args += list(fix)
        qk_shapes = [sds((t, da), BF16), sds((t, da), BF16)]
        qk_specs = [rows(da), rows(da)]
        tail_shapes = [sds((SUBLANES, t), F32), sds((t, dc), F32)]
        tail_specs = [pl.BlockSpec((SUBLANES, tm), lambda i: (0, i)), rows(dc)]
        scratch = []
    else:
        in_specs += [_const_spec(a.shape) for a in aug]
        args += list(aug)
        qk_shapes = [sds((t, 2 * da), BF16), sds((t, 2 * da), BF16)]
        qk_specs = [rows(2 * da), rows(2 * da)]
        tail_shapes = [sds((nseq, 2, dc), F32)]
        tail_specs = [pl.BlockSpec((None, 2, dc), lambda i: (i // seq_blocks, 0, 0))]
        scratch = [pltpu.VMEM((SUBLANES, dc), F32), pltpu.VMEM((SUBLANES, LANES), F32)]
    out_shape = ([sds((t, dc), BF16)] + qk_shapes +
                 [sds((t, da), BF16),
                  sds((t, da), F32), sds((t, da), F32),
                  sds((t, gw), BF16),
                  sds((t, LANES), F32)] + tail_shapes)
    out_specs = [rows(dc)] + qk_specs + [rows(da), rows(da), rows(da), rows(gw), rows(LANES)] + tail_specs
    return pl.pallas_call(
        kern, grid=(nblk,), in_specs=in_specs, out_specs=out_specs, out_shape=out_shape,
        scratch_shapes=scratch, compiler_params=_cparams(("arbitrary",)),
        name="inproj_sample" if sample else "inproj_prompt")(*args)


def _fox_prompt_kernel(qa_ref, ka_ref, v_ref, o_ref, m_sc, l_sc, acc_sc, *, tq, dh):
    qi = pl.program_id(2)
    lane = lax.broadcasted_iota(jnp.int32, (1, 2 * LANES), 1)
    ext = lane - LANES
    qf = qa_ref[...].astype(F32)
    halves = []
    for h in range(2):
        keep = (((lane >= h * dh) & (lane < (h + 1) * dh))
                | ((ext >= 3 * h) & (ext < 3 * h + 3)) | ((ext >= 6 + 3 * h) & (ext < 9 + 3 * h)))
        halves.append(jnp.where(keep, qf, 0.0))
    qs = jnp.concatenate(halves, axis=0).astype(BF16)
    m_sc[...] = jnp.full_like(m_sc, NEG_BIG)
    l_sc[...] = jnp.zeros_like(l_sc)
    acc_sc[...] = jnp.zeros_like(acc_sc)
    reps = tq // LANES

    def step(j, masked):
        ks = pl.multiple_of(j * tq, tq)
        kb = ka_ref[pl.ds(ks, tq), :]
        vb = v_ref[pl.ds(ks, tq), :]
        s = _dot_nt(qs, kb)
        if masked:
            rloc = lax.broadcasted_iota(jnp.int32, (2 * tq, tq), 0)
            rloc = jnp.where(rloc >= tq, rloc - tq, rloc)
            cloc = lax.broadcasted_iota(jnp.int32, (2 * tq, tq), 1)
            s = jnp.where(cloc <= rloc, s, NEG_BIG)
        m_prev = m_sc[...]
        m_new = jnp.maximum(m_prev, jnp.max(s, axis=1, keepdims=True))
        p = jnp.exp(s - jnp.concatenate([m_new] * reps, axis=1))
        alpha = jnp.exp(m_prev - m_new)
        l_sc[...] = alpha * l_sc[...] + jnp.sum(p, axis=1, keepdims=True)
        acc_sc[...] = alpha * acc_sc[...] + _dot(p.astype(BF16), vb)
        m_sc[...] = m_new

    def body(j, c):
        step(j, False)
        return c

    lax.fori_loop(0, qi, body, 0)
    step(qi, True)
    o = acc_sc[...] / l_sc[...]
    lane_o = lax.broadcasted_iota(jnp.int32, (1, LANES), 1)
    o_ref[...] = jnp.where(lane_o < dh, o[0:tq], o[tq:2 * tq]).astype(o_ref.dtype)


def _fox_prompt(qa, ka, v, *, nseq, seq_len, dh):
    t, da = v.shape
    tq = min(512, seq_len)
    nq = seq_len // tq
    npair = da // LANES
    kern = functools.partial(_fox_prompt_kernel, tq=tq, dh=dh)
    return pl.pallas_call(
        kern, grid=(nseq, npair, nq),
        in_specs=[pl.BlockSpec((tq, 2 * LANES), lambda b, hp, qi: (b * nq + qi, hp)),
                  pl.BlockSpec((seq_len, 2 * LANES), lambda b, hp, qi: (b, hp)),
                  pl.BlockSpec((seq_len, LANES), lambda b, hp, qi: (b, hp))],
        out_specs=pl.BlockSpec((tq, LANES), lambda b, hp, qi: (b * nq + qi, hp)),
        out_shape=jax.ShapeDtypeStruct((t, da), BF16),
        scratch_shapes=[pltpu.VMEM((2 * tq, LANES), F32), pltpu.VMEM((2 * tq, LANES), F32),
                        pltpu.VMEM((2 * tq, LANES), F32)],
        compiler_params=_cparams(("parallel", "parallel", "arbitrary")),
        name="fox_prompt")(qa, ka, v)


def _fox_sample_kernel(pt_ref, *refs, pp, page, s_new, heads, dh):
    k_refs = refs[:pp]
    v_refs = refs[pp:2 * pp]
    lf_refs = refs[2 * pp:3 * pp]
    qbd_ref, kn_ref, vn_ref, dtn_ref, o_ref, m_sc, l_sc, acc_sc, run_sc = refs[3 * pp:]
    b = pl.program_id(0)
    j = pl.program_id(1)
    nrow = s_new * heads
    da = heads * dh

    @pl.when(j == 0)
    def _():
        m_sc[...] = jnp.full_like(m_sc, NEG_BIG)
        l_sc[...] = jnp.zeros_like(l_sc)
        acc_sc[...] = jnp.zeros_like(acc_sc)
        run_sc[...] = jnp.zeros_like(run_sc)

    qbd = qbd_ref[...]

    def online(s, pv_fn):
        m_prev = m_sc[...]
        m_new = jnp.maximum(m_prev, jnp.max(s, axis=1, keepdims=True))
        p = jnp.exp(s - _lane_tile(m_new, s.shape[1]))
        alpha = jnp.exp(m_prev - m_new)
        l_sc[...] = alpha * l_sc[...] + jnp.sum(p, axis=1, keepdims=True)
        acc_sc[...] = _lane_tile(alpha, da) * acc_sc[...] + pv_fn(p.astype(BF16))
        m_sc[...] = m_new

    rr = lax.broadcasted_iota(jnp.int32, (page, 2 * page), 0)
    cc = lax.broadcasted_iota(jnp.int32, (page, 2 * page), 1)
    after = jnp.where((rr > cc) | (cc >= page), 1.0, 0.0).astype(BF16)
    lf = jnp.concatenate([lf_refs[i][...] for i in range(pp)], axis=0)
    n8 = pp * heads
    r3 = _dot(jnp.concatenate(_split3(lf), axis=0), after)
    both = r3[0:n8] + (r3[n8:2 * n8] + r3[2 * n8:3 * n8])
    run = run_sc[...]
    scores = []
    for i in range(pp):
        inner = both[i * heads:(i + 1) * heads, 0:page]
        rev = inner + run
        run = run + both[i * heads:(i + 1) * heads, page:2 * page]
        kb = k_refs[i][...].astype(BF16)
        scores.append(_dot(qbd, kb) + jnp.concatenate([rev] * s_new, axis=0))
    run_sc[...] = run

    def pv_pages(p):
        acc = None
        for i in range(pp):
            term = _dot_nt(p[:, i * page:(i + 1) * page], v_refs[i][...].astype(BF16))
            acc = term if acc is None else acc + term
        return acc

    online(jnp.concatenate(scores, axis=1), pv_pages)

    @pl.when(j == pl.num_programs(1) - 1)
    def _():
        ntok = kn_ref.shape[0]
        dtn = dtn_ref[...]
        lane = lax.broadcasted_iota(jnp.int32, (1, ntok), 1)
        dq_rows = [jnp.sum(jnp.where(lane == b * s_new + t, dtn, 0.0), axis=1, keepdims=True) for t in range(s_new)]
        dq = jnp.concatenate(dq_rows, axis=0)
        dk = jnp.concatenate([dtn] * s_new, axis=0)
        s = _dot_nt(qbd, kn_ref[...]) + (dq - dk)
        rowt = lax.broadcasted_iota(jnp.int32, (nrow, ntok), 0) // heads
        col = lax.broadcasted_iota(jnp.int32, (nrow, ntok), 1)
        keep = ((col // s_new) == b) & ((col % s_new) <= rowt)
        s = jnp.where(keep, s, NEG_BIG)
        online(s, lambda p: _dot(p, vn_ref[...]))
        o = acc_sc[...] / _lane_tile(l_sc[...], da)
        rowh = lax.broadcasted_iota(jnp.int32, (nrow, da), 0) % heads
        colh = lax.broadcasted_iota(jnp.int32, (nrow, da), 1) // dh
        o = jnp.where(rowh == colh, o, 0.0)
        o_ref[...] = jnp.sum(o.reshape(s_new, heads, da), axis=1)


def _fox_sample(page_table_flat, kt_pages, vt_pages, lf_pages, qbd, kn, vn, dtn, *, nb, npages, s_new, heads, dh):
    pp = min(SAMPLE_PAGES_PER_STEP, npages)
    nsteps = npages // pp
    page = kt_pages.shape[2]
    da = heads * dh
    nrow = s_new * heads
    ntok = kn.shape[0]
    kern = functools.partial(_fox_sample_kernel, pp=pp, page=page, s_new=s_new, heads=heads, dh=dh)

    def page_map(i):
        return lambda b, j, pt: (pt[b * npages + (npages - 1 - (j * pp + i))], 0, 0)

    page_specs = [pl.BlockSpec((None, da, page), page_map(i)) for i in range(pp)]
    lf_specs = [pl.BlockSpec((None, heads, page), page_map(i)) for i in range(pp)]
    grid_spec = pltpu.PrefetchScalarGridSpec(
        num_scalar_prefetch=1, grid=(nb, nsteps),
        in_specs=page_specs + page_specs + lf_specs + [
            pl.BlockSpec((None, nrow, da), lambda b, j, pt: (b, 0, 0)),
            pl.BlockSpec((ntok, da), lambda b, j, pt: (0, 0)),
            pl.BlockSpec((ntok, da), lambda b, j, pt: (0, 0)),
            pl.BlockSpec((SUBLANES, ntok), lambda b, j, pt: (0, 0))],
        out_specs=pl.BlockSpec((None, s_new, da), lambda b, j, pt: (b, 0, 0)),
        scratch_shapes=[pltpu.VMEM((nrow, LANES), F32), pltpu.VMEM((nrow, LANES), F32), pltpu.VMEM((nrow, da), F32),
                        pltpu.VMEM((heads, LANES), F32)])
    return pl.pallas_call(
        kern, grid_spec=grid_spec,
        out_shape=jax.ShapeDtypeStruct((nb, s_new, da), F32),
        compiler_params=_cparams(("parallel", "arbitrary")),
        name="fox_sample")(page_table_flat, *([kt_pages] * pp), *([vt_pages] * pp), *([lf_pages] * pp),
                           qbd, kn, vn, dtn)


def _post_attn_kernel(x_ref, gc_ref, at_ref, g_ref, wuc_ref, wua_ref, wmo_ref, gain_ref, wxq_ref, h_ref, qx_ref):
    dm = x_ref.shape[1]
    y_conv = _dot(gc_ref[...], wuc_ref[...])
    y_attn = _dot(at_ref[...].astype(BF16), wua_ref[...])
    g_conv = g_ref[:, 0:dm].astype(F32)
    g_attn = g_ref[:, dm:2 * dm].astype(F32)
    mixed = jax.nn.sigmoid(g_conv) * y_conv + jax.nn.sigmoid(g_attn) * y_attn
    h = x_ref[...] + _dot(mixed.astype(BF16), wmo_ref[...])
    h_ref[...] = h
    xn = _rms(h, gain_ref[...]).astype(BF16)
    qx_ref[...] = _dot(xn, wxq_ref[...]).astype(qx_ref.dtype)


def _post_attn(x, gc, at, g, wuc, wua, wmo, gain, wxq, *, tm, name):
    t, dm = x.shape
    dx = wxq.shape[1]
    rows = lambda w: pl.BlockSpec((tm, w), lambda i: (i, 0))
    return pl.pallas_call(
        _post_attn_kernel, grid=(t // tm,),
        in_specs=[rows(dm), rows(gc.shape[1]), rows(at.shape[1]), rows(g.shape[1]), _const_spec(wuc.shape),
                  _const_spec(wua.shape), _const_spec(wmo.shape), _const_spec((1, dm)), _const_spec(wxq.shape)],
        out_specs=[rows(dm), rows(dx)],
        out_shape=[jax.ShapeDtypeStruct((t, dm), F32), jax.ShapeDtypeStruct((t, dx), BF16)],
        compiler_params=_cparams(("parallel",)), name=name)(x, gc, at, g, wuc, wua, wmo, gain, wxq)


def _memkv_kernel(m_ref, gain_ref, wk_ref, wv_ref, k_ref, v_ref):
    mn = _rms(m_ref[...], gain_ref[...]).astype(BF16)
    k_ref[...] = _dot(mn, wk_ref[...])
    v_ref[...] = _dot(mn, wv_ref[...])


def _memkv(mem, gain, wk, wv):
    t, dm = mem.shape
    tm = min(512, t)
    dx = wk.shape[1]
    rows = lambda w: pl.BlockSpec((tm, w), lambda i: (i, 0))
    return pl.pallas_call(
        _memkv_kernel, grid=(t // tm,),
        in_specs=[rows(dm), _const_spec((1, dm)), _const_spec(wk.shape), _const_spec(wv.shape)],
        out_specs=[rows(dx), rows(dx)],
        out_shape=[jax.ShapeDtypeStruct((t, dx), F32)] * 2,
        compiler_params=_cparams(("parallel",)), name="memkv")(mem, gain, wk, wv)


def _xattn_kernel(q_ref, mk_ref, mv_ref, o_ref, *, xh, xd):
    q = q_ref[...].astype(BF16)
    mk = mk_ref[...].astype(BF16)
    mv = mv_ref[...].astype(BF16)
    scale = xd ** -0.5
    outs = []
    for h in range(xh):
        sl = slice(h * xd, (h + 1) * xd)
        s = _dot_nt(q[:, sl], mk[:, sl]) * scale
        m = jnp.max(s, axis=1, keepdims=True)
        p = jnp.exp(s - m)
        p = p / jnp.sum(p, axis=1, keepdims=True)
        outs.append(_dot(p.astype(BF16), mv[:, sl]))
    o_ref[...] = jnp.concatenate(outs, axis=1).astype(o_ref.dtype)


def _xattn(q, mk, mv, *, tq, xh, name):
    nb, s, dx = q.shape
    nm = mk.shape[1]
    kern = functools.partial(_xattn_kernel, xh=xh, xd=dx // xh)
    return pl.pallas_call(
        kern, grid=(nb, s // tq),
        in_specs=[pl.BlockSpec((None, tq, dx), lambda b, i: (b, i, 0)),
                  pl.BlockSpec((None, nm, dx), lambda b, i: (b, 0, 0)),
                  pl.BlockSpec((None, nm, dx), lambda b, i: (b, 0, 0))],
        out_specs=pl.BlockSpec((None, tq, dx), lambda b, i: (b, i, 0)),
        out_shape=jax.ShapeDtypeStruct((nb, s, dx), q.dtype),
        compiler_params=_cparams(("parallel", "parallel")), name=name)(q, mk, mv)


def _pre_moe_kernel(h_ref, o_ref, wxo_ref, gain_ref, wrh_ref, wrl_ref, cnt_in_ref,
                    h2_ref, xn_ref, rt_ref, rr_ref, cnt_out_ref, base, *, tm, ngroups, epg):
    i = pl.program_id(0)

    @pl.when(i == 0)
    def _():
        base[...] = cnt_in_ref[...]

    h2 = h_ref[...] + _dot(o_ref[...].astype(BF16), wxo_ref[...])
    h2_ref[...] = h2
    xn = _rms(h2, gain_ref[...])
    xn_ref[...] = xn

    xh = xn.astype(BF16)
    xl = (xn - xh.astype(F32)).astype(BF16)
    lt = _dot_nt(wrh_ref[...], xh) + (_dot_nt(wrh_ref[...], xl) + _dot_nt(wrl_ref[...], xh))

    sub = lax.broadcasted_iota(jnp.int32, (SUBLANES, tm), 0)
    gl = jnp.where(sub < ngroups, lt[0:SUBLANES, :], -jnp.inf)
    gmax = jnp.max(gl, axis=0, keepdims=True)
    gidx = jnp.min(jnp.where(gl == gmax, sub, SUBLANES), axis=0, keepdims=True)
    pg = 1.0 / jnp.sum(jnp.exp(gl - gmax), axis=0, keepdims=True)
    el = jnp.zeros((epg, tm), F32)
    for g in range(ngroups):
        el = jnp.where(gidx == g, lt[EXPERT_ROW0 + g * epg:EXPERT_ROW0 + (g + 1) * epg, :], el)
    v1 = jnp.max(el, axis=0, keepdims=True)
    i1 = jnp.min(jnp.where(el == v1, sub, epg), axis=0, keepdims=True)
    el2 = jnp.where(sub == i1, -jnp.inf, el)
    v2 = jnp.max(el2, axis=0, keepdims=True)
    i2 = jnp.min(jnp.where(el2 == v2, sub, epg), axis=0, keepdims=True)
    t2 = jnp.exp(v2 - v1)
    den = 1.0 + t2
    w0 = (1.0 / den) * pg
    w1 = (t2 / den) * pg
    e0 = gidx * epg + i1
    e1 = gidx * epg + i2

    erow = lax.broadcasted_iota(jnp.int32, (LANES, tm), 0)
    oh0 = erow == e0
    oh1 = erow == e1
    rr = lax.broadcasted_iota(jnp.int32, (tm, tm), 0)
    cc = lax.broadcasted_iota(jnp.int32, (tm, tm), 1)
    triu = jnp.where(rr <= cc, 1.0, 0.0).astype(BF16)
    pre0 = _dot(jnp.where(oh0, 1.0, 0.0).astype(BF16), triu)
    pre1 = _dot(jnp.where(oh1, 1.0, 0.0).astype(BF16), triu)
    b0 = base[:, 0:1]
    tot0 = pre0[:, tm - 1:tm]
    tot1 = pre1[:, tm - 1:tm]
    rank0 = jnp.sum(jnp.where(oh0, pre0 - 1.0 + b0, 0.0), axis=0, keepdims=True)
    rank1 = jnp.sum(jnp.where(oh1, pre1 - 1.0 + (b0 + tot0), 0.0), axis=0, keepdims=True)
    newb = b0 + tot0 + tot1
    base[...] = jnp.broadcast_to(newb, base.shape)
    cnt_out_ref[...] = jnp.broadcast_to(newb, cnt_out_ref.shape)

    zero = jnp.zeros((1, tm), F32)
    rt = jnp.concatenate([e0.astype(F32), e1.astype(F32), w0, w1, rank0, rank1, zero, zero], axis=0)
    rt_ref[...] = rt
    rt_pad = jnp.concatenate([rt, jnp.zeros((LANES - ROUTE_ROWS, tm), F32)], axis=0)
    rr_ref[...] = jnp.transpose(rt_pad)


def _pre_moe(h, o, wxo, gain, wrh, wrl, cnt_in, *, tm, ngroups, epg, name):
    t, dm = h.shape
    dx = o.shape[1]
    kern = functools.partial(_pre_moe_kernel, tm=tm, ngroups=ngroups, epg=epg)
    rows = lambda w: pl.BlockSpec((tm, w), lambda i: (i, 0))
    return pl.pallas_call(
        kern, grid=(t // tm,),
        in_specs=[rows(dm), rows(dx), _const_spec(wxo.shape), _const_spec((1, dm)), _const_spec(wrh.shape),
                  _const_spec(wrl.shape), _const_spec((LANES, LANES))],
        out_specs=[rows(dm), rows(dm), pl.BlockSpec((ROUTE_ROWS, tm), lambda i: (0, i)), rows(LANES),
                   _const_spec((LANES, LANES))],
        out_shape=[jax.ShapeDtypeStruct((t, dm), F32), jax.ShapeDtypeStruct((t, dm), F32),
                   jax.ShapeDtypeStruct((ROUTE_ROWS, t), F32), jax.ShapeDtypeStruct((t, LANES), F32),
                   jax.ShapeDtypeStruct((LANES, LANES), F32)],
        scratch_shapes=[pltpu.VMEM((LANES, LANES), F32)],
        compiler_params=_cparams(("arbitrary",)), name=name)(h, o, wxo, gain, wrh, wrl, cnt_in)


def _row_copy(src, r_src, dst, r_dst, sem):
    return pltpu.make_async_copy(src.at[pl.ds(r_src, 1)], dst.at[pl.ds(r_dst, 1)], sem)


def _scatter_kernel(e_ref, rank_ref, pstart_ref, pend_ref, xp_ref, xs_ref, out_ref, zeros, sem, zsem,
                    *, tm_p, nblk_p, nexp):
    i = pl.program_id(0)
    tall = rank_ref.shape[0] // TOP_K

    @pl.when(i == 0)
    def _():
        zeros[...] = jnp.zeros_like(zeros)

        def zero_block(blk):
            start = pl.multiple_of(blk * EXPERT_ROWS, EXPERT_ROWS)
            return pltpu.make_async_copy(zeros, out_ref.at[pl.ds(start, EXPERT_ROWS)], zsem)

        for e in range(nexp):
            @pl.when(pend_ref[e] > pstart_ref[e])
            def _():
                zero_block(pend_ref[e] // EXPERT_ROWS - 1).start()
        for e in range(nexp):
            @pl.when(pend_ref[e] > pstart_ref[e])
            def _():
                zero_block(pend_ref[e] // EXPERT_ROWS - 1).wait()
        nblk = out_ref.shape[0] // EXPERT_ROWS
        nused = pend_ref[nexp - 1] // EXPERT_ROWS

        def start_unused(blk, c):
            zero_block(blk).start()
            return c

        def wait_unused(blk, c):
            zero_block(blk).wait()
            return c

        lax.fori_loop(nused, nblk, start_unused, 0)
        lax.fori_loop(nused, nblk, wait_unused, 0)

    def copy_rows(x_ref, base):
        tm = x_ref.shape[0]

        def issue(r, c):
            for k in range(TOP_K):
                a = k * tall + base + r
                dest = pstart_ref[e_ref[a]] + rank_ref[a]
                _row_copy(x_ref, r, out_ref, dest, sem).start()
            return c

        lax.fori_loop(0, tm, issue, 0, unroll=8)
        for k in range(TOP_K):
            pltpu.make_async_copy(x_ref, x_ref, sem).wait()

    @pl.when(i < nblk_p)
    def _():
        copy_rows(xp_ref, i * tm_p)

    @pl.when(i == nblk_p)
    def _():
        copy_rows(xs_ref, nblk_p * tm_p)


def _scatter(e_flat, rank_flat, pstart, pend, x_p, x_s, *, n_rows, tm_p, nexp):
    tp, dm = x_p.shape
    ts = x_s.shape[0]
    nblk_p = tp // tm_p
    kern = functools.partial(_scatter_kernel, tm_p=tm_p, nblk_p=nblk_p, nexp=nexp)
    grid_spec = pltpu.PrefetchScalarGridSpec(
        num_scalar_prefetch=4, grid=(nblk_p + 1,),
        in_specs=[pl.BlockSpec((tm_p, dm), lambda i, *_: (jnp.minimum(i, nblk_p - 1), 0)),
                  pl.BlockSpec((ts, dm), lambda i, *_: (0, 0))],
        out_specs=pl.BlockSpec(memory_space=pl.ANY),
        scratch_shapes=[pltpu.VMEM((EXPERT_ROWS, dm), F32), pltpu.SemaphoreType.DMA(()), pltpu.SemaphoreType.DMA(())])
    return pl.pallas_call(
        kern, grid_spec=grid_spec, out_shape=jax.ShapeDtypeStruct((n_rows, dm), F32),
        compiler_params=_cparams(("arbitrary",)), name="scatter")(e_flat, rank_flat, pstart, pend, x_p, x_s)


def _experts_kernel(blk_e_ref, nused_ref, xs_ref, wg_ref, wu_ref, wd_ref, y_ref, wgb, wub, wdb):
    i = pl.program_id(0)
    prev = blk_e_ref[jnp.maximum(i - 1, 0)]
    fresh = (i == 0) | (blk_e_ref[i] != prev)

    @pl.when(i < nused_ref[0])
    def _():
        @pl.when(fresh)
        def _():
            wgb[...] = wg_ref[...].astype(BF16)
            wub[...] = wu_ref[...].astype(BF16)
            wdb[...] = wd_ref[...].astype(BF16)
        x = xs_ref[...].astype(BF16)
        a = _dot(x, wgb[...])
        u = _dot(x, wub[...])
        hmid = (a * jax.nn.sigmoid(a)) * u
        y_ref[...] = _dot(hmid.astype(BF16), wdb[...])

    @pl.when(i >= nused_ref[0])
    def _():
        y_ref[...] = jnp.zeros_like(y_ref)


def _experts(blk_e, nused, xs, wg, wu, wd):
    n_rows, dm = xs.shape
    de = wg.shape[2]
    nblk = n_rows // EXPERT_ROWS

    def row_map(i, be, nu):
        return (jnp.minimum(i, nu[0] - 1), 0)

    grid_spec = pltpu.PrefetchScalarGridSpec(
        num_scalar_prefetch=2, grid=(nblk,),
        in_specs=[pl.BlockSpec((EXPERT_ROWS, dm), row_map),
                  pl.BlockSpec((None, dm, de), lambda i, be, nu: (be[i], 0, 0)),
                  pl.BlockSpec((None, dm, de), lambda i, be, nu: (be[i], 0, 0)),
                  pl.BlockSpec((None, de, dm), lambda i, be, nu: (be[i], 0, 0))],
        out_specs=pl.BlockSpec((EXPERT_ROWS, dm), lambda i, be, nu: (i, 0)),
        scratch_shapes=[pltpu.VMEM((dm, de), BF16), pltpu.VMEM((dm, de), BF16), pltpu.VMEM((de, dm), BF16)])
    return pl.pallas_call(
        _experts_kernel, grid_spec=grid_spec, out_shape=jax.ShapeDtypeStruct((n_rows, dm), F32),
        compiler_params=_cparams(("arbitrary",)), name="experts")(blk_e, nused, xs, wg, wu, wd)


def _combine_kernel(e_ref, rank_ref, pstart_ref, h_ref, rr_ref, gain_ref, y_hbm, o_ref, buf, sem, *, tm, tok0):
    i = pl.program_id(0)
    base = tok0 + i * tm

    def issue(r, c):
        for k in range(TOP_K):
            a = k * rank_ref.shape[0] // TOP_K + base + r
            dest = pstart_ref[e_ref[a]] + rank_ref[a]
            _row_copy(y_hbm, dest, buf.at[k], r, sem).start()
        return c

    lax.fori_loop(0, tm, issue, 0, unroll=8)
    for k in range(TOP_K):
        pltpu.make_async_copy(buf.at[k], buf.at[k], sem).wait()
    rr = rr_ref[...]
    h = h_ref[...] + (rr[:, 2:3] * buf[0] + rr[:, 3:4] * buf[1])
    o_ref[...] = _rms(h, gain_ref[...])


def _combine(e_flat, rank_flat, pstart, h, rr, gain, y, *, tm, tok0, name):
    t, dm = h.shape
    kern = functools.partial(_combine_kernel, tm=tm, tok0=tok0)
    grid_spec = pltpu.PrefetchScalarGridSpec(
        num_scalar_prefetch=3, grid=(t // tm,),
        in_specs=[pl.BlockSpec((tm, dm), lambda i, *_: (i, 0)), pl.BlockSpec((tm, LANES), lambda i, *_: (i, 0)),
                  pl.BlockSpec((1, dm), lambda i, *_: (0, 0)), pl.BlockSpec(memory_space=pl.ANY)],
        out_specs=pl.BlockSpec((tm, dm), lambda i, *_: (i, 0)),
        scratch_shapes=[pltpu.VMEM((TOP_K, tm, dm), F32), pltpu.SemaphoreType.DMA(())])
    return pl.pallas_call(
        kern, grid_spec=grid_spec, out_shape=jax.ShapeDtypeStruct((t, dm), F32),
        compiler_params=_cparams(("arbitrary",)), name=name)(e_flat, rank_flat, pstart, h, rr, gain, y)


def kernel(x_prompt, x_sample, cache_k, cache_v, cache_logf, cache_mem_k, cache_mem_v, state_conv, page_table,
           mem_prompt, norm_mix, w_in, b_forget, conv_w, conv_b, w_up_conv, w_up_attn, w_mix_out, norm_xattn,
           norm_mem, w_xq, w_xk, w_xv, w_xo, norm_ffn, w_router_group, w_router_expert, w_expert_gate,
           w_expert_up, w_expert_down, norm_final):
    depth = w_in.shape[0]
    assert depth == 1, "single-layer trunk"
    nbp, seq, dm = x_prompt.shape
    nbs, s_new, _ = x_sample.shape
    _, n_pool, page, heads, dh = cache_k.shape
    npages = page_table.shape[1]
    nmem, xh, xd = cache_mem_k.shape[2:]
    dc = conv_w.shape[2]
    da = heads * dh
    dx = xh * xd
    ngroups, _, epg = w_router_expert.shape[1:]
    nexp = ngroups * epg
    tp = nbp * seq
    ts = nbs * s_new
    assert conv_w.shape[1] == 3 and s_new >= 2 and dh == 64 and heads == SUBLANES and epg == SUBLANES
    assert page == LANES

    l = 0
    wi = w_in[l]
    wc = wi[:, 0:3 * dc].astype(BF16)
    wqkv = wi[:, 3 * dc:3 * dc + 3 * da].astype(BF16)
    o_f = 3 * dc + 3 * da
    wf = jnp.pad(wi[:, o_f:o_f + heads], ((0, 0), (0, LANES - heads)))
    wfh = wf.astype(BF16)
    wfl = (wf - wfh.astype(F32)).astype(BF16)
    wg = wi[:, o_f + heads:].astype(BF16)
    bfp = jnp.pad(b_forget[l][None, :], ((0, 0), (0, LANES - heads)))
    cw = conv_w[l]
    cbias = conv_b[l][None, :]
    g_mix = norm_mix[l][None, :]
    wuc = w_up_conv[l].astype(BF16)
    wua = w_up_attn[l].astype(BF16)
    wmo = w_mix_out[l].astype(BF16)
    g_x = norm_xattn[l][None, :]
    wxq = w_xq[l].astype(BF16)
    wxo = w_xo[l].astype(BF16)
    g_f = norm_ffn[l][None, :]
    wr = jnp.zeros((LANES, dm), F32)
    wr = wr.at[0:ngroups].set(w_router_group[l].T)
    wr = wr.at[EXPERT_ROW0:EXPERT_ROW0 + nexp].set(jnp.transpose(w_router_expert[l], (0, 2, 1)).reshape(nexp, dm))
    wrh = wr.astype(BF16)
    wrl = (wr - wrh.astype(F32)).astype(BF16)

    xp = x_prompt.reshape(tp, dm)
    (gc_p, qa_p, ka_p, vb_p, kf_p, vf_p, g_p, lf_p, ulast_p) = _inproj(
        xp, g_mix, wc, wqkv, wg, wfh, wfl, bfp, cw, cbias, seq_len=seq, sample=False, aug=_decay_columns(heads))
    at_p = _fox_prompt(qa_p, ka_p, vb_p, nseq=nbp, seq_len=seq, dh=dh)
    tm_p = min(512, seq)
    h1_p, qx_p = _post_attn(xp, gc_p, at_p, g_p, wuc, wua, wmo, g_x, wxq, tm=tm_p, name="post_attn_prompt")
    mk_p, mv_p = _memkv(mem_prompt.reshape(nbp * nmem, dm), norm_mem[l][None, :], w_xk[l].astype(BF16),
                        w_xv[l].astype(BF16))
    o_p = _xattn(qx_p.reshape(nbp, seq, dx), mk_p.reshape(nbp, nmem, dx), mv_p.reshape(nbp, nmem, dx),
                 tq=tm_p, xh=xh, name="xattn_prompt").reshape(tp, dx)
    cnt0 = jnp.zeros((LANES, LANES), F32)
    h2_p, xn_p, rt_p, rr_p, cnt1 = _pre_moe(h1_p, o_p, wxo, g_f, wrh, wrl, cnt0, tm=tm_p, ngroups=ngroups, epg=epg,
                                            name="pre_moe_prompt")

    xs_ = x_sample.reshape(ts, dm)
    st = state_conv[l]
    zeros_row = jnp.zeros((nbs, 1, dc), F32)
    fix1 = jnp.concatenate([st[:, 1:2], jnp.tile(zeros_row, (1, s_new - 1, 1))], axis=1).reshape(ts, dc)
    fix2 = jnp.concatenate([st[:, 0:1], st[:, 1:2], jnp.tile(zeros_row, (1, s_new - 2, 1))], axis=1).reshape(ts, dc)
    (gc_s, q_s, kb_s, vb_s, kf_s, vf_s, g_s, lf_s, dt_s, u_s) = _inproj(
        xs_, g_mix, wc, wqkv, wg, wfh, wfl, bfp, cw, cbias, seq_len=s_new, sample=True, fix=(fix1, fix2))
    pt_flat = page_table.reshape(-1).astype(jnp.int32)
    head_of_col = jnp.arange(da) // dh
    qbd = jnp.where(head_of_col[None, None, None, :] == jnp.arange(heads)[None, None, :, None],
                    q_s.reshape(nbs, s_new, 1, da), jnp.zeros((), BF16)).reshape(nbs, s_new * heads, da)
    kt_pages = jnp.transpose(cache_k[l], (0, 2, 3, 1)).reshape(n_pool, da, page)
    vt_pages = jnp.transpose(cache_v[l], (0, 2, 3, 1)).reshape(n_pool, da, page)
    lf_pages = jnp.swapaxes(cache_logf[l], 1, 2)
    at_s = _fox_sample(pt_flat, kt_pages, vt_pages, lf_pages, qbd, kb_s, vb_s, dt_s,
                       nb=nbs, npages=npages, s_new=s_new, heads=heads, dh=dh)
    h1_s, qx_s = _post_attn(xs_, gc_s, at_s.reshape(ts, da), g_s, wuc, wua, wmo, g_x, wxq, tm=ts,
                            name="post_attn_sample")
    qx_s8 = jnp.pad(qx_s.astype(F32).reshape(nbs, s_new, dx), ((0, 0), (0, SUBLANES - s_new), (0, 0)))
    o_s = _xattn(qx_s8, cache_mem_k[l].reshape(nbs, nmem, dx), cache_mem_v[l].reshape(nbs, nmem, dx),
                 tq=SUBLANES, xh=xh, name="xattn_sample")[:, :s_new].reshape(ts, dx)
    h2_s, xn_s, rt_s, rr_s, cnt2 = _pre_moe(h1_s, o_s, wxo, g_f, wrh, wrl, cnt1, tm=ts, ngroups=ngroups, epg=epg,
                                            name="pre_moe_sample")

    tall = tp + ts
    counts = cnt2[0:nexp, 0].astype(jnp.int32)
    padded = (counts + EXPERT_ROWS - 1) // EXPERT_ROWS * EXPERT_ROWS
    pend = jnp.cumsum(padded).astype(jnp.int32)
    pstart = pend - padded
    nblk = (tall * TOP_K + nexp * (EXPERT_ROWS - 1)) // EXPERT_ROWS
    n_rows = nblk * EXPERT_ROWS
    blk_e = jnp.minimum(jnp.searchsorted(pend, jnp.arange(nblk, dtype=jnp.int32) * EXPERT_ROWS, side='right'),
                        nexp - 1).astype(jnp.int32)
    nused = (pend[nexp - 1:nexp] // EXPERT_ROWS).astype(jnp.int32)
    rt_all = jnp.concatenate([rt_p, rt_s], axis=1)
    e_flat = rt_all[0:TOP_K].astype(jnp.int32).reshape(-1)
    rank_flat = rt_all[4:4 + TOP_K].astype(jnp.int32).reshape(-1)
    xsg = _scatter(e_flat, rank_flat, pstart, pend, xn_p, xn_s, n_rows=n_rows, tm_p=tm_p, nexp=nexp)
    y = _experts(blk_e, nused, xsg, w_expert_gate[l], w_expert_up[l], w_expert_down[l])
    g_fin = norm_final[None, :]
    y_p = _combine(e_flat, rank_flat, pstart, h2_p, rr_p, g_fin, y, tm=tm_p // 2, tok0=0, name="combine_prompt")
    y_s = _combine(e_flat, rank_flat, pstart, h2_s, rr_s, g_fin, y, tm=ts, tok0=tp, name="combine_sample")

    return (y_p.reshape(nbp, seq, dm), y_s.reshape(nbs, s_new, dm),
            ulast_p[None], kf_p.reshape(1, nbp, seq, heads, dh), vf_p.reshape(1, nbp, seq, heads, dh),
            lf_p[:, :heads].reshape(1, nbp, seq, heads),
            mk_p.reshape(1, nbp, nmem, xh, xd), mv_p.reshape(1, nbp, nmem, xh, xd),
            u_s.reshape(nbs, s_new, dc)[None, :, s_new - 2:], kf_s.reshape(1, nbs, s_new, heads, dh),
            vf_s.reshape(1, nbs, s_new, heads, dh), lf_s[:, :heads].reshape(1, nbs, s_new, heads))
```

```python
import functools

import numpy as np
import jax
import jax.numpy as jnp
from jax import lax
from jax.experimental import pallas as pl
from jax.experimental.pallas import tpu as pltpu

F32 = jnp.float32
BF16 = jnp.bfloat16

RMS_EPS = 1e-6
TOP_K = 2
LANES = 128
SUBLANES = 8
VMEM_LIMIT = 56 * 1024 * 1024
NEG_BIG = -1e30
EXPERT_ROWS = 256
SAMPLE_PAGES_PER_STEP = 16
ROUTE_ROWS = 8
EXPERT_ROW0 = 8


def _cparams(sem, vmem=VMEM_LIMIT):
    return pltpu.CompilerParams(dimension_semantics=sem, vmem_limit_bytes=vmem)


def _rms(x, g):
    ms = jnp.mean(x * x, axis=-1, keepdims=True)
    return x * lax.rsqrt(ms + RMS_EPS) * g


def _split3(x):
    hi = x.astype(BF16)
    r = x - hi.astype(F32)
    mid = r.astype(BF16)
    lo = (r - mid.astype(F32)).astype(BF16)
    return hi, mid, lo


def _dot(a, b):
    return jnp.dot(a, b, preferred_element_type=F32)


def _dot_nt(a, b):
    return lax.dot_general(a, b, (((1,), (1,)), ((), ())), preferred_element_type=F32)


def _lane_tile(x, width):
    if width % LANES == 0:
        return jnp.concatenate([x] * (width // LANES), axis=1)
    return jnp.broadcast_to(x[:, 0:1], (x.shape[0], width))


def _const_spec(shape):
    nd = len(shape)
    return pl.BlockSpec(shape, lambda *_: (0,) * nd)


def _inproj_kernel(*refs, tm, seq_blocks, sample, seq_len):
    if sample:
        (x_ref, gain_ref, wc_ref, wqkv_ref, wg_ref, wfh_ref, wfl_ref, bf_ref, cw_ref, cb_ref, fix1_ref, fix2_ref,
         gc_ref, q_ref, kbf_ref, vbf_ref, kf_ref, vf_ref, g_ref, logf_ref, dt_ref, u_ref) = refs
    else:
        (x_ref, gain_ref, wc_ref, wqkv_ref, wg_ref, wfh_ref, wfl_ref, bf_ref, cw_ref, cb_ref,
         selq_ref, selk_ref, oneq_ref, onek_ref,
         gc_ref, qa_ref, ka_ref, vbf_ref, kf_ref, vf_ref, g_ref, logf_ref, u_ref,
         carry_u, carry_d) = refs
    i = pl.program_id(0)
    dc = cw_ref.shape[1]
    da = kf_ref.shape[1]

    xn = _rms(x_ref[...], gain_ref[...])
    xb = xn.astype(BF16)

    cb = _dot(xb, wc_ref[:, 0:dc])
    cc = _dot(xb, wc_ref[:, dc:2 * dc])
    cx = _dot(xb, wc_ref[:, 2 * dc:3 * dc])
    u = cc * cx
    row = lax.broadcasted_iota(jnp.int32, (tm, 1), 0)
    r1 = pltpu.roll(u, 1, axis=0)
    r2 = pltpu.roll(u, 2, axis=0)
    if sample:
        pos = row % seq_len
        p1 = jnp.where(pos == 0, fix1_ref[...], r1)
        p2 = jnp.where(pos < 2, fix2_ref[...], r2)
        u_ref[...] = u
    else:
        @pl.when(i % seq_blocks == 0)
        def _():
            carry_u[...] = jnp.zeros_like(carry_u)
            carry_d[...] = jnp.zeros_like(carry_d)
        c0 = carry_u[0:1, :]
        c1 = carry_u[1:2, :]
        p1 = jnp.where(row == 0, c1, r1)
        p2 = jnp.where(row == 0, c0, jnp.where(row == 1, c1, r2))
        carry_u[0:2, :] = u[tm - 2:tm, :]
        u_ref[...] = u[tm - 2:tm, :]
    cw = cw_ref[...]
    cy = cb_ref[...] + cw[0:1, :] * p2 + cw[1:2, :] * p1 + cw[2:3, :] * u
    gc_ref[...] = (cb * cy).astype(BF16)

    qs = (_dot(xb, wqkv_ref[:, 0:da]) * (1.0 / 8.0)).astype(BF16)
    k = _dot(xb, wqkv_ref[:, da:2 * da])
    kf_ref[...] = k
    kb = k.astype(BF16)
    if sample:
        q_ref[...] = qs
        kbf_ref[...] = kb
    v = _dot(xb, wqkv_ref[:, 2 * da:3 * da])
    vf_ref[...] = v
    vbf_ref[...] = v.astype(BF16)

    gw = g_ref.shape[1]
    for c in range(gw // 512):
        g_ref[:, c * 512:(c + 1) * 512] = _dot(xb, wg_ref[:, c * 512:(c + 1) * 512]).astype(BF16)

    xl = (xn - xb.astype(F32)).astype(BF16)
    fz = _dot(xb, wfh_ref[...]) + (_dot(xb, wfl_ref[...]) + _dot(xl, wfh_ref[...]))
    z = fz + bf_ref[...]
    logf = jnp.minimum(z, 0.0) - jnp.log1p(jnp.exp(-jnp.abs(z)))
    lane = lax.broadcasted_iota(jnp.int32, (1, LANES), 1)
    nh = da // 64
    logf = jnp.where(lane < nh, logf, 0.0)
    logf_ref[...] = logf

    rr = lax.broadcasted_iota(jnp.int32, (tm, tm), 0)
    cc_ = lax.broadcasted_iota(jnp.int32, (tm, tm), 1)
    if sample:
        tri = (cc_ <= rr) & ((rr // seq_len) == (cc_ // seq_len))
    else:
        tri = cc_ <= rr
    tri = jnp.where(tri, 1.0, 0.0).astype(BF16)
    hi, mid, lo = _split3(logf)
    d = _dot(tri, hi) + (_dot(tri, mid) + _dot(tri, lo))
    if sample:
        dt_ref[...] = jnp.transpose(d)[0:SUBLANES, :]
    else:
        d = d + carry_d[0:1, :]
        carry_d[0:1, :] = d[tm - 1:tm, :]
        dcat = jnp.concatenate(_split3(d), axis=1)
        aq = (_dot(dcat, selq_ref[...]) + oneq_ref[...]).astype(BF16)
        ak = (_dot(dcat, selk_ref[...]) + onek_ref[...]).astype(BF16)
        for p in range(da // LANES):
            lo_, hi_ = p * LANES, (p + 1) * LANES
            qa_ref[:, 2 * lo_:2 * lo_ + LANES] = qs[:, lo_:hi_]
            qa_ref[:, 2 * lo_ + LANES:2 * hi_] = aq[:, lo_:hi_]
            ka_ref[:, 2 * lo_:2 * lo_ + LANES] = kb[:, lo_:hi_]
            ka_ref[:, 2 * lo_ + LANES:2 * hi_] = ak[:, lo_:hi_]


def _decay_columns(heads):
    npair = heads // 2
    selq = np.zeros((3 * LANES, npair * LANES), np.float32)
    selk = np.zeros((3 * LANES, npair * LANES), np.float32)
    oneq = np.zeros((1, npair * LANES), np.float32)
    onek = np.zeros((1, npair * LANES), np.float32)
    for p in range(npair):
        for hh in range(2):
            for term in range(3):
                selq[term * LANES + 2 * p + hh, p * LANES + 3 * hh + term] = 1.0
                selk[term * LANES + 2 * p + hh, p * LANES + 6 + 3 * hh + term] = -1.0
                oneq[0, p * LANES + 6 + 3 * hh + term] = 1.0
                onek[0, p * LANES + 3 * hh + term] = 1.0
    return jnp.asarray(selq, BF16), jnp.asarray(selk, BF16), jnp.asarray(oneq), jnp.asarray(onek)


def _inproj(x, gain, wc, wqkv, wg, wfh, wfl, bfp, cw, cbias, *, seq_len, sample, fix=None, aug=None):
    t, dm = x.shape
    dc = cw.shape[1]
    da = wqkv.shape[1] // 3
    gw = wg.shape[1]
    if sample:
        tm = t
        seq_blocks = 1
    else:
        tm = min(512, seq_len)
        seq_blocks = seq_len // tm
    nblk = t // tm
    nseq = t // seq_len
    kern = functools.partial(_inproj_kernel, tm=tm, seq_blocks=seq_blocks, sample=sample, seq_len=seq_len)
    rows = lambda w: pl.BlockSpec((tm, w), lambda i: (i, 0))
    in_specs = [rows(dm), _const_spec((1, dm)), _const_spec(wc.shape), _const_spec(wqkv.shape), _const_spec(wg.shape),
                _const_spec(wfh.shape), _const_spec(wfl.shape), _const_spec((1, LANES)), _const_spec(cw.shape),
                _const_spec((1, dc))]
    args = [x, gain, wc, wqkv, wg, wfh, wfl, bfp, cw, cbias]
    sds = jax.ShapeDtypeStruct
    if sample:
        in_specs += [rows(dc), rows(dc)]
        args += list(fix)
        qk_shapes = [sds((t, da), BF16), sds((t, da), BF16)]
        qk_specs = [rows(da), rows(da)]
        tail_shapes = [sds((SUBLANES, t), F32), sds((t, dc), F32)]
        tail_specs = [pl.BlockSpec((SUBLANES, tm), lambda i: (0, i)), rows(dc)]
        scratch = []
    else:
        in_specs += [_const_spec(a.shape) for a in aug]
        args += list(aug)
        qk_shapes = [sds((t, 2 * da), BF16), sds((t, 2 * da), BF16)]
        qk_specs = [rows(2 * da), rows(2 * da)]
        tail_shapes = [sds((nseq, 2, dc), F32)]
        tail_specs = [pl.BlockSpec((None, 2, dc), lambda i: (i // seq_blocks, 0, 0))]
        scratch = [pltpu.VMEM((SUBLANES, dc), F32), pltpu.VMEM((SUBLANES, LANES), F32)]
    out_shape = ([sds((t, dc), BF16)] + qk_shapes +
                 [sds((t, da), BF16),
                  sds((t, da), F32), sds((t, da), F32),
                  sds((t, gw), BF16),
                  sds((t, LANES), F32)] + tail_shapes)
    out_specs = [rows(dc)] + qk_specs + [rows(da), rows(da), rows(da), rows(gw), rows(LANES)] + tail_specs
    return pl.pallas_call(
        kern, grid=(nblk,), in_specs=in_specs, out_specs=out_specs, out_shape=out_shape,
        scratch_shapes=scratch, compiler_params=_cparams(("arbitrary",)),
        name="inproj_sample" if sample else "inproj_prompt")(*args)


def _fox_prompt_kernel(qa_ref, ka_ref, v_ref, o_ref, m_sc, l_sc, acc_sc, *, tq, dh):
    qi = pl.program_id(2)
    lane = lax.broadcasted_iota(jnp.int32, (1, 2 * LANES), 1)
    ext = lane - LANES
    qf = qa_ref[...].astype(F32)
    halves = []
    for h in range(2):
        keep = (((lane >= h * dh) & (lane < (h + 1) * dh))
                | ((ext >= 3 * h) & (ext < 3 * h + 3)) | ((ext >= 6 + 3 * h) & (ext < 9 + 3 * h)))
        halves.append(jnp.where(keep, qf, 0.0))
    qs = jnp.concatenate(halves, axis=0).astype(BF16)
    m_sc[...] = jnp.full_like(m_sc, NEG_BIG)
    l_sc[...] = jnp.zeros_like(l_sc)
    acc_sc[...] = jnp.zeros_like(acc_sc)
    reps = tq // LANES

    def step(j, masked):
        ks = pl.multiple_of(j * tq, tq)
        kb = ka_ref[pl.ds(ks, tq), :]
        vb = v_ref[pl.ds(ks, tq), :]
        s = _dot_nt(qs, kb)
        if masked:
            rloc = lax.broadcasted_iota(jnp.int32, (2 * tq, tq), 0)
            rloc = jnp.where(rloc >= tq, rloc - tq, rloc)
            cloc = lax.broadcasted_iota(jnp.int32, (2 * tq, tq), 1)
            s = jnp.where(cloc <= rloc, s, NEG_BIG)
        m_prev = m_sc[...]
        m_new = jnp.maximum(m_prev, jnp.max(s, axis=1, keepdims=True))
        p = jnp.exp(s - jnp.concatenate([m_new] * reps, axis=1))
        alpha = jnp.exp(m_prev - m_new)
        l_sc[...] = alpha * l_sc[...] + jnp.sum(p, axis=1, keepdims=True)
        acc_sc[...] = alpha * acc_sc[...] + _dot(p.astype(BF16), vb)
        m_sc[...] = m_new

    def body(j, c):
        step(j, False)
        return c

    lax.fori_loop(0, qi, body, 0)
    step(qi, True)
    o = acc_sc[...] / l_sc[...]
    lane_o = lax.broadcasted_iota(jnp.int32, (1, LANES), 1)
    o_ref[...] = jnp.where(lane_o < dh, o[0:tq], o[tq:2 * tq]).astype(o_ref.dtype)


def _fox_prompt(qa, ka, v, *, nseq, seq_len, dh):
    t, da = v.shape
    tq = min(512, seq_len)
    nq = seq_len // tq
    npair = da // LANES
    kern = functools.partial(_fox_prompt_kernel, tq=tq, dh=dh)
    return pl.pallas_call(
        kern, grid=(nseq, npair, nq),
        in_specs=[pl.BlockSpec((tq, 2 * LANES), lambda b, hp, qi: (b * nq + qi, hp)),
                  pl.BlockSpec((seq_len, 2 * LANES), lambda b, hp, qi: (b, hp)),
                  pl.BlockSpec((seq_len, LANES), lambda b, hp, qi: (b, hp))],
        out_specs=pl.BlockSpec((tq, LANES), lambda b, hp, qi: (b * nq + qi, hp)),
        out_shape=jax.ShapeDtypeStruct((t, da), BF16),
        scratch_shapes=[pltpu.VMEM((2 * tq, LANES), F32), pltpu.VMEM((2 * tq, LANES), F32),
                        pltpu.VMEM((2 * tq, LANES), F32)],
        compiler_params=_cparams(("parallel", "parallel", "arbitrary")),
        name="fox_prompt")(qa, ka, v)


def _fox_sample_kernel(pt_ref, *refs, pp, page, s_new, heads, dh):
    k_refs = refs[:pp]
    v_refs = refs[pp:2 * pp]
    lf_refs = refs[2 * pp:3 * pp]
    qbd_ref, kn_ref, vn_ref, dtn_ref, o_ref, m_sc, l_sc, acc_sc, run_sc = refs[3 * pp:]
    b = pl.program_id(0)
    j = pl.program_id(1)
    nrow = s_new * heads
    da = heads * dh

    @pl.when(j == 0)
    def _():
        m_sc[...] = jnp.full_like(m_sc, NEG_BIG)
        l_sc[...] = jnp.zeros_like(l_sc)
        acc_sc[...] = jnp.zeros_like(acc_sc)
        run_sc[...] = jnp.zeros_like(run_sc)

    qbd = qbd_ref[...]

    def online(s, pv_fn):
        m_prev = m_sc[...]
        m_new = jnp.maximum(m_prev, jnp.max(s, axis=1, keepdims=True))
        p = jnp.exp(s - _lane_tile(m_new, s.shape[1]))
        alpha = jnp.exp(m_prev - m_new)
        l_sc[...] = alpha * l_sc[...] + jnp.sum(p, axis=1, keepdims=True)
        acc_sc[...] = _lane_tile(alpha, da) * acc_sc[...] + pv_fn(p.astype(BF16))
        m_sc[...] = m_new

    rr = lax.broadcasted_iota(jnp.int32, (page, 2 * page), 0)
    cc = lax.broadcasted_iota(jnp.int32, (page, 2 * page), 1)
    after = jnp.where((rr > cc) | (cc >= page), 1.0, 0.0).astype(BF16)
    lf = jnp.concatenate([lf_refs[i][...] for i in range(pp)], axis=0)
    n8 = pp * heads
    r3 = _dot(jnp.concatenate(_split3(lf), axis=0), after)
    both = r3[0:n8] + (r3[n8:2 * n8] + r3[2 * n8:3 * n8])
    run = run_sc[...]
    scores = []
    for i in range(pp):
        inner = both[i * heads:(i + 1) * heads, 0:page]
        rev = inner + run
        run = run + both[i * heads:(i + 1) * heads, page:2 * page]
        kb = k_refs[i][...].astype(BF16)
        scores.append(_dot(qbd, kb) + jnp.concatenate([rev] * s_new, axis=0))
    run_sc[...] = run

    def pv_pages(p):
        acc = None
        for i in range(pp):
            term = _dot_nt(p[:, i * page:(i + 1) * page], v_refs[i][...].astype(BF16))
            acc = term if acc is None else acc + term
        return acc

    online(jnp.concatenate(scores, axis=1), pv_pages)

    @pl.when(j == pl.num_programs(1) - 1)
    def _():
        ntok = kn_ref.shape[0]
        dtn = dtn_ref[...]
        lane = lax.broadcasted_iota(jnp.int32, (1, ntok), 1)
        dq_rows = [jnp.sum(jnp.where(lane == b * s_new + t, dtn, 0.0), axis=1, keepdims=True) for t in range(s_new)]
        dq = jnp.concatenate(dq_rows, axis=0)
        dk = jnp.concatenate([dtn] * s_new, axis=0)
        s = _dot_nt(qbd, kn_ref[...]) + (dq - dk)
        rowt = lax.broadcasted_iota(jnp.int32, (nrow, ntok), 0) // heads
        col = lax.broadcasted_iota(jnp.int32, (nrow, ntok), 1)
        keep = ((col // s_new) == b) & ((col % s_new) <= rowt)
        s = jnp.where(keep, s, NEG_BIG)
        online(s, lambda p: _dot(p, vn_ref[...]))
        o = acc_sc[...] / _lane_tile(l_sc[...], da)
        rowh = lax.broadcasted_iota(jnp.int32, (nrow, da), 0) % heads
        colh = lax.broadcasted_iota(jnp.int32, (nrow, da), 1) // dh
        o = jnp.where(rowh == colh, o, 0.0)
        o_ref[...] = jnp.sum(o.reshape(s_new, heads, da), axis=1)


def _fox_sample(page_table_flat, kt_pages, vt_pages, lf_pages, qbd, kn, vn, dtn, *, nb, npages, s_new, heads, dh):
    pp = min(SAMPLE_PAGES_PER_STEP, npages)
    nsteps = npages // pp
    page = kt_pages.shape[2]
    da = heads * dh
    nrow = s_new * heads
    ntok = kn.shape[0]
    kern = functools.partial(_fox_sample_kernel, pp=pp, page=page, s_new=s_new, heads=heads, dh=dh)

    def page_map(i):
        return lambda b, j, pt: (pt[b * npages + (npages - 1 - (j * pp + i))], 0, 0)

    page_specs = [pl.BlockSpec((None, da, page), page_map(i)) for i in range(pp)]
    lf_specs = [pl.BlockSpec((None, heads, page), page_map(i)) for i in range(pp)]
    grid_spec = pltpu.PrefetchScalarGridSpec(
        num_scalar_prefetch=1, grid=(nb, nsteps),
        in_specs=page_specs + page_specs + lf_specs + [
            pl.BlockSpec((None, nrow, da), lambda b, j, pt: (b, 0, 0)),
            pl.BlockSpec((ntok, da), lambda b, j, pt: (0, 0)),
            pl.BlockSpec((ntok, da), lambda b, j, pt: (0, 0)),
            pl.BlockSpec((SUBLANES, ntok), lambda b, j, pt: (0, 0))],
        out_specs=pl.BlockSpec((None, s_new, da), lambda b, j, pt: (b, 0, 0)),
        scratch_shapes=[pltpu.VMEM((nrow, LANES), F32), pltpu.VMEM((nrow, LANES), F32), pltpu.VMEM((nrow, da), F32),
                        pltpu.VMEM((heads, LANES), F32)])
    return pl.pallas_call(
        kern, grid_spec=grid_spec,
        out_shape=jax.ShapeDtypeStruct((nb, s_new, da), F32),
        compiler_params=_cparams(("parallel", "arbitrary")),
        name="fox_sample")(page_table_flat, *([kt_pages] * pp), *([vt_pages] * pp), *([lf_pages] * pp),
                           qbd, kn, vn, dtn)


def _post_attn_kernel(x_ref, gc_ref, at_ref, g_ref, wuc_ref, wua_ref, wmo_ref, gain_ref, wxq_ref, h_ref, qx_ref):
    dm = x_ref.shape[1]
    y_conv = _dot(gc_ref[...], wuc_ref[...])
    y_attn = _dot(at_ref[...].astype(BF16), wua_ref[...])
    g_conv = g_ref[:, 0:dm].astype(F32)
    g_attn = g_ref[:, dm:2 * dm].astype(F32)
    mixed = jax.nn.sigmoid(g_conv) * y_conv + jax.nn.sigmoid(g_attn) * y_attn
    h = x_ref[...] + _dot(mixed.astype(BF16), wmo_ref[...])
    h_ref[...] = h
    xn = _rms(h, gain_ref[...]).astype(BF16)
    qx_ref[...] = _dot(xn, wxq_ref[...]).astype(qx_ref.dtype)


def _post_attn(x, gc, at, g, wuc, wua, wmo, gain, wxq, *, tm, name):
    t, dm = x.shape
    dx = wxq.shape[1]
    rows = lambda w: pl.BlockSpec((tm, w), lambda i: (i, 0))
    return pl.pallas_call(
        _post_attn_kernel, grid=(t // tm,),
        in_specs=[rows(dm), rows(gc.shape[1]), rows(at.shape[1]), rows(g.shape[1]), _const_spec(wuc.shape),
                  _const_spec(wua.shape), _const_spec(wmo.shape), _const_spec((1, dm)), _const_spec(wxq.shape)],
        out_specs=[rows(dm), rows(dx)],
        out_shape=[jax.ShapeDtypeStruct((t, dm), F32), jax.ShapeDtypeStruct((t, dx), BF16)],
        compiler_params=_cparams(("parallel",)), name=name)(x, gc, at, g, wuc, wua, wmo, gain, wxq)


def _memkv_kernel(m_ref, gain_ref, wk_ref, wv_ref, k_ref, v_ref):
    mn = _rms(m_ref[...], gain_ref[...]).astype(BF16)
    k_ref[...] = _dot(mn, wk_ref[...])
    v_ref[...] = _dot(mn, wv_ref[...])


def _memkv(mem, gain, wk, wv):
    t, dm = mem.shape
    tm = min(512, t)
    dx = wk.shape[1]
    rows = lambda w: pl.BlockSpec((tm, w), lambda i: (i, 0))
    return pl.pallas_call(
        _memkv_kernel, grid=(t // tm,),
        in_specs=[rows(dm), _const_spec((1, dm)), _const_spec(wk.shape), _const_spec(wv.shape)],
        out_specs=[rows(dx), rows(dx)],
        out_shape=[jax.ShapeDtypeStruct((t, dx), F32)] * 2,
        compiler_params=_cparams(("parallel",)), name="memkv")(mem, gain, wk, wv)


def _xattn_kernel(q_ref, mk_ref, mv_ref, o_ref, *, xh, xd):
    q = q_ref[...].astype(BF16)
    mk = mk_ref[...].astype(BF16)
    mv = mv_ref[...].astype(BF16)
    scale = xd ** -0.5
    outs = []
    for h in range(xh):
        sl = slice(h * xd, (h + 1) * xd)
        s = _dot_nt(q[:, sl], mk[:, sl]) * scale
        m = jnp.max(s, axis=1, keepdims=True)
        p = jnp.exp(s - m)
        p = p / jnp.sum(p, axis=1, keepdims=True)
        outs.append(_dot(p.astype(BF16), mv[:, sl]))
    o_ref[...] = jnp.concatenate(outs, axis=1).astype(o_ref.dtype)


def _xattn(q, mk, mv, *, tq, xh, name):
    nb, s, dx = q.shape
    nm = mk.shape[1]
    kern = functools.partial(_xattn_kernel, xh=xh, xd=dx // xh)
    return pl.pallas_call(
        kern, grid=(nb, s // tq),
        in_specs=[pl.BlockSpec((None, tq, dx), lambda b, i: (b, i, 0)),
                  pl.BlockSpec((None, nm, dx), lambda b, i: (b, 0, 0)),
                  pl.BlockSpec((None, nm, dx), lambda b, i: (b, 0, 0))],
        out_specs=pl.BlockSpec((None, tq, dx), lambda b, i: (b, i, 0)),
        out_shape=jax.ShapeDtypeStruct((nb, s, dx), q.dtype),
        compiler_params=_cparams(("parallel", "parallel")), name=name)(q, mk, mv)


def _pre_moe_kernel(h_ref, o_ref, wxo_ref, gain_ref, wrh_ref, wrl_ref, cnt_in_ref,
                    h2_ref, xn_ref, rt_ref, rr_ref, cnt_out_ref, base, *, tm, ngroups, epg):
    i = pl.program_id(0)

    @pl.when(i == 0)
    def _():
        base[...] = cnt_in_ref[...]

    h2 = h_ref[...] + _dot(o_ref[...].astype(BF16), wxo_ref[...])
    h2_ref[...] = h2
    xn = _rms(h2, gain_ref[...])
    nc = xn.shape[1] // LANES
    for c in range(nc):
        xn_ref[pl.ds(c, tm, stride=nc), :] = xn[:, c * LANES:(c + 1) * LANES]

    xh = xn.astype(BF16)
    xl = (xn - xh.astype(F32)).astype(BF16)
    lt = _dot_nt(wrh_ref[...], xh) + (_dot_nt(wrh_ref[...], xl) + _dot_nt(wrl_ref[...], xh))

    sub = lax.broadcasted_iota(jnp.int32, (SUBLANES, tm), 0)
    gl = jnp.where(sub < ngroups, lt[0:SUBLANES, :], -jnp.inf)
    gmax = jnp.max(gl, axis=0, keepdims=True)
    gidx = jnp.min(jnp.where(gl == gmax, sub, SUBLANES), axis=0, keepdims=True)
    pg = 1.0 / jnp.sum(jnp.exp(gl - gmax), axis=0, keepdims=True)
    el = jnp.zeros((epg, tm), F32)
    for g in range(ngroups):
        el = jnp.where(gidx == g, lt[EXPERT_ROW0 + g * epg:EXPERT_ROW0 + (g + 1) * epg, :], el)
    v1 = jnp.max(el, axis=0, keepdims=True)
    i1 = jnp.min(jnp.where(el == v1, sub, epg), axis=0, keepdims=True)
    el2 = jnp.where(sub == i1, -jnp.inf, el)
    v2 = jnp.max(el2, axis=0, keepdims=True)
    i2 = jnp.min(jnp.where(el2 == v2, sub, epg), axis=0, keepdims=True)
    t2 = jnp.exp(v2 - v1)
    den = 1.0 + t2
    w0 = (1.0 / den) * pg
    w1 = (t2 / den) * pg
    e0 = gidx * epg + i1
    e1 = gidx * epg + i2

    erow = lax.broadcasted_iota(jnp.int32, (LANES, tm), 0)
    oh0 = erow == e0
    oh1 = erow == e1
    rr = lax.broadcasted_iota(jnp.int32, (tm, tm), 0)
    cc = lax.broadcasted_iota(jnp.int32, (tm, tm), 1)
    triu = jnp.where(rr <= cc, 1.0, 0.0).astype(BF16)
    pre0 = _dot(jnp.where(oh0, 1.0, 0.0).astype(BF16), triu)
    pre1 = _dot(jnp.where(oh1, 1.0, 0.0).astype(BF16), triu)
    b0 = base[:, 0:1]
    tot0 = pre0[:, tm - 1:tm]
    tot1 = pre1[:, tm - 1:tm]
    rank0 = jnp.sum(jnp.where(oh0, pre0 - 1.0 + b0, 0.0), axis=0, keepdims=True)
    rank1 = jnp.sum(jnp.where(oh1, pre1 - 1.0 + (b0 + tot0), 0.0), axis=0, keepdims=True)
    newb = b0 + tot0 + tot1
    base[...] = jnp.broadcast_to(newb, base.shape)
    cnt_out_ref[...] = jnp.broadcast_to(newb, cnt_out_ref.shape)

    zero = jnp.zeros((1, tm), F32)
    rt = jnp.concatenate([e0.astype(F32), e1.astype(F32), w0, w1, rank0, rank1, zero, zero], axis=0)
    rt_ref[...] = rt
    rt_pad = jnp.concatenate([rt, jnp.zeros((LANES - ROUTE_ROWS, tm), F32)], axis=0)
    rr_ref[...] = jnp.transpose(rt_pad)


def _pre_moe(h, o, wxo, gain, wrh, wrl, cnt_in, *, tm, ngroups, epg, name):
    t, dm = h.shape
    dx = o.shape[1]
    kern = functools.partial(_pre_moe_kernel, tm=tm, ngroups=ngroups, epg=epg)
    rows = lambda w: pl.BlockSpec((tm, w), lambda i: (i, 0))
    return pl.pallas_call(
        kern, grid=(t // tm,),
        in_specs=[rows(dm), rows(dx), _const_spec(wxo.shape), _const_spec((1, dm)), _const_spec(wrh.shape),
                  _const_spec(wrl.shape), _const_spec((LANES, LANES))],
        out_specs=[rows(dm), pl.BlockSpec((tm * (dm // LANES), LANES), lambda i: (i, 0)),
                   pl.BlockSpec((ROUTE_ROWS, tm), lambda i: (0, i)), rows(LANES), _const_spec((LANES, LANES))],
        out_shape=[jax.ShapeDtypeStruct((t, dm), F32), jax.ShapeDtypeStruct((t * (dm // LANES), LANES), F32),
                   jax.ShapeDtypeStruct((ROUTE_ROWS, t), F32), jax.ShapeDtypeStruct((t, LANES), F32),
                   jax.ShapeDtypeStruct((LANES, LANES), F32)],
        scratch_shapes=[pltpu.VMEM((LANES, LANES), F32)],
        compiler_params=_cparams(("arbitrary",)), name=name)(h, o, wxo, gain, wrh, wrl, cnt_in)


def _row_copy(src, r_src, dst, r_dst, sem, nc):
    return pltpu.make_async_copy(src.at[pl.ds(pl.multiple_of(r_src * nc, nc), nc)],
                                 dst.at[pl.ds(pl.multiple_of(r_dst * nc, nc), nc)], sem)


def _slab_rows(ref, nc):
    rows = ref.shape[0] // nc
    return jnp.concatenate([ref[pl.ds(c, rows, stride=nc), :] for c in range(nc)], axis=1)


def _scatter_kernel(e_ref, rank_ref, pstart_ref, pend_ref, xp_ref, xs_ref, out_ref, zeros, sem, zsem,
                    *, tm_p, nblk_p, nexp, nc):
    i = pl.program_id(0)
    tall = rank_ref.shape[0] // TOP_K

    @pl.when(i == 0)
    def _():
        zeros[...] = jnp.zeros_like(zeros)

        def zero_block(blk):
            start = pl.multiple_of(blk * (EXPERT_ROWS * nc), EXPERT_ROWS * nc)
            return pltpu.make_async_copy(zeros, out_ref.at[pl.ds(start, EXPERT_ROWS * nc)], zsem)

        for e in range(nexp):
            @pl.when(pend_ref[e] > pstart_ref[e])
            def _():
                zero_block(pend_ref[e] // EXPERT_ROWS - 1).start()
        for e in range(nexp):
            @pl.when(pend_ref[e] > pstart_ref[e])
            def _():
                zero_block(pend_ref[e] // EXPERT_ROWS - 1).wait()
        nblk = out_ref.shape[0] // (EXPERT_ROWS * nc)
        nused = pend_ref[nexp - 1] // EXPERT_ROWS

        def start_unused(blk, c):
            zero_block(blk).start()
            return c

        def wait_unused(blk, c):
            zero_block(blk).wait()
            return c

        lax.fori_loop(nused, nblk, start_unused, 0)
        lax.fori_loop(nused, nblk, wait_unused, 0)

    def copy_rows(x_ref, base):
        tm = x_ref.shape[0] // nc

        def issue(r, c):
            for k in range(TOP_K):
                a = k * tall + base + r
                dest = pstart_ref[e_ref[a]] + rank_ref[a]
                _row_copy(x_ref, r, out_ref, dest, sem, nc).start()
            return c

        lax.fori_loop(0, tm, issue, 0, unroll=8)
        for k in range(TOP_K):
            pltpu.make_async_copy(x_ref, x_ref, sem).wait()

    @pl.when(i < nblk_p)
    def _():
        copy_rows(xp_ref, i * tm_p)

    @pl.when(i == nblk_p)
    def _():
        copy_rows(xs_ref, nblk_p * tm_p)


def _scatter(e_flat, rank_flat, pstart, pend, x_p, x_s, *, n_rows, tm_p, nexp, dm):
    nc = dm // LANES
    tp = x_p.shape[0] // nc
    ts = x_s.shape[0] // nc
    nblk_p = tp // tm_p
    kern = functools.partial(_scatter_kernel, tm_p=tm_p, nblk_p=nblk_p, nexp=nexp, nc=nc)
    grid_spec = pltpu.PrefetchScalarGridSpec(
        num_scalar_prefetch=4, grid=(nblk_p + 1,),
        in_specs=[pl.BlockSpec((tm_p * nc, LANES), lambda i, *_: (jnp.minimum(i, nblk_p - 1), 0)),
                  pl.BlockSpec((ts * nc, LANES), lambda i, *_: (0, 0))],
        out_specs=pl.BlockSpec(memory_space=pl.ANY),
        scratch_shapes=[pltpu.VMEM((EXPERT_ROWS * nc, LANES), F32), pltpu.SemaphoreType.DMA(()),
                        pltpu.SemaphoreType.DMA(())])
    return pl.pallas_call(
        kern, grid_spec=grid_spec, out_shape=jax.ShapeDtypeStruct((n_rows * nc, LANES), F32),
        compiler_params=_cparams(("arbitrary",)), name="scatter")(e_flat, rank_flat, pstart, pend, x_p, x_s)


def _experts_kernel(blk_e_ref, nused_ref, xs_ref, wg_ref, wu_ref, wd_ref, y_ref, wgb, wub, wdb, *, nc):
    i = pl.program_id(0)
    prev = blk_e_ref[jnp.maximum(i - 1, 0)]
    fresh = (i == 0) | (blk_e_ref[i] != prev)

    @pl.when(i < nused_ref[0])
    def _():
        @pl.when(fresh)
        def _():
            wgb[...] = wg_ref[...].astype(BF16)
            wub[...] = wu_ref[...].astype(BF16)
            wdb[...] = wd_ref[...].astype(BF16)
        x = _slab_rows(xs_ref, nc).astype(BF16)
        a = _dot(x, wgb[...])
        u = _dot(x, wub[...])
        hmid = (a * jax.nn.sigmoid(a)) * u
        y = _dot(hmid.astype(BF16), wdb[...])
        for c in range(nc):
            y_ref[pl.ds(c, EXPERT_ROWS, stride=nc), :] = y[:, c * LANES:(c + 1) * LANES]

    @pl.when(i >= nused_ref[0])
    def _():
        y_ref[...] = jnp.zeros_like(y_ref)


def _experts(blk_e, nused, xs, wg, wu, wd):
    dm, de = wg.shape[1:]
    nc = dm // LANES
    nblk = xs.shape[0] // (EXPERT_ROWS * nc)

    def row_map(i, be, nu):
        return (jnp.minimum(i, nu[0] - 1), 0)

    grid_spec = pltpu.PrefetchScalarGridSpec(
        num_scalar_prefetch=2, grid=(nblk,),
        in_specs=[pl.BlockSpec((EXPERT_ROWS * nc, LANES), row_map),
                  pl.BlockSpec((None, dm, de), lambda i, be, nu: (be[i], 0, 0)),
                  pl.BlockSpec((None, dm, de), lambda i, be, nu: (be[i], 0, 0)),
                  pl.BlockSpec((None, de, dm), lambda i, be, nu: (be[i], 0, 0))],
        out_specs=pl.BlockSpec((EXPERT_ROWS * nc, LANES), lambda i, be, nu: (i, 0)),
        scratch_shapes=[pltpu.VMEM((dm, de), BF16), pltpu.VMEM((dm, de), BF16), pltpu.VMEM((de, dm), BF16)])
    return pl.pallas_call(
        functools.partial(_experts_kernel, nc=nc), grid_spec=grid_spec, out_shape=jax.ShapeDtypeStruct(xs.shape, F32),
        compiler_params=_cparams(("arbitrary",)), name="experts")(blk_e, nused, xs, wg, wu, wd)


def _combine_kernel(e_ref, rank_ref, pstart_ref, h_ref, rr_ref, gain_ref, y_hbm, o_ref, buf, sem, *, tm, tok0):
    i = pl.program_id(0)
    nc = h_ref.shape[1] // LANES
    base = tok0 + i * tm

    def issue(r, c):
        for k in range(TOP_K):
            a = k * rank_ref.shape[0] // TOP_K + base + r
            dest = pstart_ref[e_ref[a]] + rank_ref[a]
            _row_copy(y_hbm, dest, buf.at[k], r, sem, nc).start()
        return c

    lax.fori_loop(0, tm, issue, 0, unroll=8)
    for k in range(TOP_K):
        pltpu.make_async_copy(buf.at[k], buf.at[k], sem).wait()
    rr = rr_ref[...]
    h = h_ref[...] + (rr[:, 2:3] * _slab_rows(buf.at[0], nc) + rr[:, 3:4] * _slab_rows(buf.at[1], nc))
    o_ref[...] = _rms(h, gain_ref[...])


def _combine(e_flat, rank_flat, pstart, h, rr, gain, y, *, tm, tok0, name):
    t, dm = h.shape
    kern = functools.partial(_combine_kernel, tm=tm, tok0=tok0)
    grid_spec = pltpu.PrefetchScalarGridSpec(
        num_scalar_prefetch=3, grid=(t // tm,),
        in_specs=[pl.BlockSpec((tm, dm), lambda i, *_: (i, 0)), pl.BlockSpec((tm, LANES), lambda i, *_: (i, 0)),
                  pl.BlockSpec((1, dm), lambda i, *_: (0, 0)), pl.BlockSpec(memory_space=pl.ANY)],
        out_specs=pl.BlockSpec((tm, dm), lambda i, *_: (i, 0)),
        scratch_shapes=[pltpu.VMEM((TOP_K, tm * (dm // LANES), LANES), F32), pltpu.SemaphoreType.DMA(())])
    return pl.pallas_call(
        kern, grid_spec=grid_spec, out_shape=jax.ShapeDtypeStruct((t, dm), F32),
        compiler_params=_cparams(("arbitrary",)), name=name)(e_flat, rank_flat, pstart, h, rr, gain, y)


def kernel(x_prompt, x_sample, cache_k, cache_v, cache_logf, cache_mem_k, cache_mem_v, state_conv, page_table,
           mem_prompt, norm_mix, w_in, b_forget, conv_w, conv_b, w_up_conv, w_up_attn, w_mix_out, norm_xattn,
           norm_mem, w_xq, w_xk, w_xv, w_xo, norm_ffn, w_router_group, w_router_expert, w_expert_gate,
           w_expert_up, w_expert_down, norm_final):
    depth = w_in.shape[0]
    assert depth == 1, "single-layer trunk"
    nbp, seq, dm = x_prompt.shape
    nbs, s_new, _ = x_sample.shape
    _, n_pool, page, heads, dh = cache_k.shape
    npages = page_table.shape[1]
    nmem, xh, xd = cache_mem_k.shape[2:]
    dc = conv_w.shape[2]
    da = heads * dh
    dx = xh * xd
    ngroups, _, epg = w_router_expert.shape[1:]
    nexp = ngroups * epg
    tp = nbp * seq
    ts = nbs * s_new
    assert conv_w.shape[1] == 3 and s_new >= 2 and dh == 64 and heads == SUBLANES and epg == SUBLANES
    assert page == LANES

    l = 0
    wi = w_in[l]
    wc = wi[:, 0:3 * dc].astype(BF16)
    wqkv = wi[:, 3 * dc:3 * dc + 3 * da].astype(BF16)
    o_f = 3 * dc + 3 * da
    wf = jnp.pad(wi[:, o_f:o_f + heads], ((0, 0), (0, LANES - heads)))
    wfh = wf.astype(BF16)
    wfl = (wf - wfh.astype(F32)).astype(BF16)
    wg = wi[:, o_f + heads:].astype(BF16)
    bfp = jnp.pad(b_forget[l][None, :], ((0, 0), (0, LANES - heads)))
    cw = conv_w[l]
    cbias = conv_b[l][None, :]
    g_mix = norm_mix[l][None, :]
    wuc = w_up_conv[l].astype(BF16)
    wua = w_up_attn[l].astype(BF16)
    wmo = w_mix_out[l].astype(BF16)
    g_x = norm_xattn[l][None, :]
    wxq = w_xq[l].astype(BF16)
    wxo = w_xo[l].astype(BF16)
    g_f = norm_ffn[l][None, :]
    wr = jnp.zeros((LANES, dm), F32)
    wr = wr.at[0:ngroups].set(w_router_group[l].T)
    wr = wr.at[EXPERT_ROW0:EXPERT_ROW0 + nexp].set(jnp.transpose(w_router_expert[l], (0, 2, 1)).reshape(nexp, dm))
    wrh = wr.astype(BF16)
    wrl = (wr - wrh.astype(F32)).astype(BF16)

    xp = x_prompt.reshape(tp, dm)
    (gc_p, qa_p, ka_p, vb_p, kf_p, vf_p, g_p, lf_p, ulast_p) = _inproj(
        xp, g_mix, wc, wqkv, wg, wfh, wfl, bfp, cw, cbias, seq_len=seq, sample=False, aug=_decay_columns(heads))
    at_p = _fox_prompt(qa_p, ka_p, vb_p, nseq=nbp, seq_len=seq, dh=dh)
    tm_p = min(512, seq)
    h1_p, qx_p = _post_attn(xp, gc_p, at_p, g_p, wuc, wua, wmo, g_x, wxq, tm=tm_p, name="post_attn_prompt")
    mk_p, mv_p = _memkv(mem_prompt.reshape(nbp * nmem, dm), norm_mem[l][None, :], w_xk[l].astype(BF16),
                        w_xv[l].astype(BF16))
    o_p = _xattn(qx_p.reshape(nbp, seq, dx), mk_p.reshape(nbp, nmem, dx), mv_p.reshape(nbp, nmem, dx),
                 tq=tm_p, xh=xh, name="xattn_prompt").reshape(tp, dx)
    cnt0 = jnp.zeros((LANES, LANES), F32)
    h2_p, xn_p, rt_p, rr_p, cnt1 = _pre_moe(h1_p, o_p, wxo, g_f, wrh, wrl, cnt0, tm=tm_p, ngroups=ngroups, epg=epg,
                                            name="pre_moe_prompt")

    xs_ = x_sample.reshape(ts, dm)
    st = state_conv[l]
    zeros_row = jnp.zeros((nbs, 1, dc), F32)
    fix1 = jnp.concatenate([st[:, 1:2], jnp.tile(zeros_row, (1, s_new - 1, 1))], axis=1).reshape(ts, dc)
    fix2 = jnp.concatenate([st[:, 0:1], st[:, 1:2], jnp.tile(zeros_row, (1, s_new - 2, 1))], axis=1).reshape(ts, dc)
    (gc_s, q_s, kb_s, vb_s, kf_s, vf_s, g_s, lf_s, dt_s, u_s) = _inproj(
        xs_, g_mix, wc, wqkv, wg, wfh, wfl, bfp, cw, cbias, seq_len=s_new, sample=True, fix=(fix1, fix2))
    pt_flat = page_table.reshape(-1).astype(jnp.int32)
    head_of_col = jnp.arange(da) // dh
    qbd = jnp.where(head_of_col[None, None, None, :] == jnp.arange(heads)[None, None, :, None],
                    q_s.reshape(nbs, s_new, 1, da), jnp.zeros((), BF16)).reshape(nbs, s_new * heads, da)
    kt_pages = jnp.transpose(cache_k[l], (0, 2, 3, 1)).reshape(n_pool, da, page)
    vt_pages = jnp.transpose(cache_v[l], (0, 2, 3, 1)).reshape(n_pool, da, page)
    lf_pages = jnp.swapaxes(cache_logf[l], 1, 2)
    at_s = _fox_sample(pt_flat, kt_pages, vt_pages, lf_pages, qbd, kb_s, vb_s, dt_s,
                       nb=nbs, npages=npages, s_new=s_new, heads=heads, dh=dh)
    h1_s, qx_s = _post_attn(xs_, gc_s, at_s.reshape(ts, da), g_s, wuc, wua, wmo, g_x, wxq, tm=ts,
                            name="post_attn_sample")
    qx_s8 = jnp.pad(qx_s.astype(F32).reshape(nbs, s_new, dx), ((0, 0), (0, SUBLANES - s_new), (0, 0)))
    o_s = _xattn(qx_s8, cache_mem_k[l].reshape(nbs, nmem, dx), cache_mem_v[l].reshape(nbs, nmem, dx),
                 tq=SUBLANES, xh=xh, name="xattn_sample")[:, :s_new].reshape(ts, dx)
    h2_s, xn_s, rt_s, rr_s, cnt2 = _pre_moe(h1_s, o_s, wxo, g_f, wrh, wrl, cnt1, tm=ts, ngroups=ngroups, epg=epg,
                                            name="pre_moe_sample")

    tall = tp + ts
    counts = cnt2[0:nexp, 0].astype(jnp.int32)
    padded = (counts + EXPERT_ROWS - 1) // EXPERT_ROWS * EXPERT_ROWS
    pend = jnp.cumsum(padded).astype(jnp.int32)
    pstart = pend - padded
    nblk = (tall * TOP_K + nexp * (EXPERT_ROWS - 1)) // EXPERT_ROWS
    n_rows = nblk * EXPERT_ROWS
    blk_row0 = jnp.arange(nblk, dtype=jnp.int32) * EXPERT_ROWS
    blk_e = jnp.minimum(jnp.sum((pend[None, :] <= blk_row0[:, None]).astype(jnp.int32), axis=1), nexp - 1)
    nused = (pend[nexp - 1:nexp] // EXPERT_ROWS).astype(jnp.int32)
    rt_all = jnp.concatenate([rt_p, rt_s], axis=1)
    e_flat = rt_all[0:TOP_K].astype(jnp.int32).reshape(-1)
    rank_flat = rt_all[4:4 + TOP_K].astype(jnp.int32).reshape(-1)
    xsg = _scatter(e_flat, rank_flat, pstart, pend, xn_p, xn_s, n_rows=n_rows, tm_p=tm_p, nexp=nexp, dm=dm)
    y = _experts(blk_e, nused, xsg, w_expert_gate[l], w_expert_up[l], w_expert_down[l])
    g_fin = norm_final[None, :]
    y_p = _combine(e_flat, rank_flat, pstart, h2_p, rr_p, g_fin, y, tm=tm_p // 2, tok0=0, name="combine_prompt")
    y_s = _combine(e_flat, rank_flat, pstart, h2_s, rr_s, g_fin, y, tm=ts, tok0=tp, name="combine_sample")

    return (y_p.reshape(nbp, seq, dm), y_s.reshape(nbs, s_new, dm),
            ulast_p[None], kf_p.reshape(1, nbp, seq, heads, dh), vf_p.reshape(1, nbp, seq, heads, dh),
            lf_p[:, :heads].reshape(1, nbp, seq, heads),
            mk_p.reshape(1, nbp, nmem, xh, xd), mv_p.reshape(1, nbp, nmem, xh, xd),
            u_s.reshape(nbs, s_new, dc)[None, :, s_new - 2:], kf_s.reshape(1, nbs, s_new, heads, dh),
            vf_s.reshape(1, nbs, s_new, heads, dh), lf_s[:, :heads].reshape(1, nbs, s_new, heads))
```

```python
import functools

import numpy as np
import jax
import jax.numpy as jnp
from jax import lax
from jax.experimental import pallas as pl
from jax.experimental.pallas import tpu as pltpu

F32 = jnp.float32
BF16 = jnp.bfloat16

RMS_EPS = 1e-6
TOP_K = 2
LANES = 128
SUBLANES = 8
VMEM_LIMIT = 56 * 1024 * 1024
NEG_BIG = -1e30
EXPERT_ROWS = 256
SAMPLE_PAGES_PER_STEP = 32
FOX_QUERY_BLOCK = 512
FOX_KEY_BLOCK = 512
ROUTE_ROWS = 8
EXPERT_ROW0 = 8


def _cparams(sem, vmem=VMEM_LIMIT):
    return pltpu.CompilerParams(dimension_semantics=sem, vmem_limit_bytes=vmem)


def _rms(x, g):
    ms = jnp.mean(x * x, axis=-1, keepdims=True)
    return x * lax.rsqrt(ms + RMS_EPS) * g


def _split3(x):
    hi = x.astype(BF16)
    r = x - hi.astype(F32)
    mid = r.astype(BF16)
    lo = (r - mid.astype(F32)).astype(BF16)
    return hi, mid, lo


def _dot(a, b):
    return jnp.dot(a, b, preferred_element_type=F32)


def _dot_nt(a, b):
    return lax.dot_general(a, b, (((1,), (1,)), ((), ())), preferred_element_type=F32)


def _lane_tile(x, width):
    if width % LANES == 0:
        return jnp.concatenate([x] * (width // LANES), axis=1)
    return jnp.broadcast_to(x[:, 0:1], (x.shape[0], width))


def _const_spec(shape):
    nd = len(shape)
    return pl.BlockSpec(shape, lambda *_: (0,) * nd)


def _inproj_kernel(*refs, tm, seq_blocks, sample, seq_len):
    if sample:
        (x_ref, gain_ref, wc_ref, wqkv_ref, wg_ref, wf2_ref, bf_ref, cw_ref, cb_ref, fix1_ref, fix2_ref,
         gc_ref, q_ref, kbf_ref, vbf_ref, kf_ref, vf_ref, g_ref, logf_ref, dt_ref, u_ref) = refs
    else:
        (x_ref, gain_ref, wc_ref, wqkv_ref, wg_ref, wf2_ref, bf_ref, cw_ref, cb_ref,
         selq_ref, selk_ref, oneq_ref, onek_ref,
         gc_ref, qa_ref, ka_ref, vbf_ref, kf_ref, vf_ref, g_ref, logf_ref, u_ref,
         carry_u, carry_d) = refs
    i = pl.program_id(0)
    dc = cw_ref.shape[1]
    da = vbf_ref.shape[1]

    xn = _rms(x_ref[...], gain_ref[...])
    xb = xn.astype(BF16)

    cb = _dot(xb, wc_ref[:, 0:dc])
    cc = _dot(xb, wc_ref[:, dc:2 * dc])
    cx = _dot(xb, wc_ref[:, 2 * dc:3 * dc])
    u = cc * cx
    row = lax.broadcasted_iota(jnp.int32, (tm, 1), 0)
    r1 = pltpu.roll(u, 1, axis=0)
    r2 = pltpu.roll(u, 2, axis=0)
    if sample:
        pos = row % seq_len
        p1 = jnp.where(pos == 0, fix1_ref[...], r1)
        p2 = jnp.where(pos < 2, fix2_ref[...], r2)
        u_ref[...] = u
    else:
        @pl.when(i % seq_blocks == 0)
        def _():
            carry_u[...] = jnp.zeros_like(carry_u)
            carry_d[...] = jnp.zeros_like(carry_d)
        c0 = carry_u[0:1, :]
        c1 = carry_u[1:2, :]
        p1 = jnp.where(row == 0, c1, r1)
        p2 = jnp.where(row == 0, c0, jnp.where(row == 1, c1, r2))
        carry_u[0:2, :] = u[tm - 2:tm, :]
        u_ref[...] = u[tm - 2:tm, :]
    cw = cw_ref[...]
    cy = cb_ref[...] + cw[0:1, :] * p2 + cw[1:2, :] * p1 + cw[2:3, :] * u
    gc_ref[...] = (cb * cy).astype(BF16)

    qs = (_dot(xb, wqkv_ref[:, 0:da]) * (1.0 / 8.0)).astype(BF16)
    k = _dot(xb, wqkv_ref[:, da:2 * da])
    kb = k.astype(BF16)
    v = _dot(xb, wqkv_ref[:, 2 * da:3 * da])
    vbf_ref[...] = v.astype(BF16)
    if sample:
        q_ref[...] = qs
        kbf_ref[...] = kb
        kf_ref[...] = k
        vf_ref[...] = v
    else:
        kf_ref[...] = jnp.transpose(k)
        vf_ref[...] = jnp.transpose(v)

    gw = g_ref.shape[1]
    for c in range(gw // 512):
        g_ref[:, c * 512:(c + 1) * 512] = _dot(xb, wg_ref[:, c * 512:(c + 1) * 512]).astype(BF16)

    xl = (xn - xb.astype(F32)).astype(BF16)
    hh_hl = _dot(xb, wf2_ref[...])
    fz = hh_hl[:, 0:LANES] + (hh_hl[:, LANES:2 * LANES] + _dot(xl, wf2_ref[:, 0:LANES]))
    z = fz + bf_ref[...]
    logf = jnp.minimum(z, 0.0) - jnp.log1p(jnp.exp(-jnp.abs(z)))
    lane = lax.broadcasted_iota(jnp.int32, (1, LANES), 1)
    nh = da // 64
    logf = jnp.where(lane < nh, logf, 0.0)
    logf_ref[...] = logf

    rr = lax.broadcasted_iota(jnp.int32, (tm, tm), 0)
    cc_ = lax.broadcasted_iota(jnp.int32, (tm, tm), 1)
    if sample:
        tri = (cc_ <= rr) & ((rr // seq_len) == (cc_ // seq_len))
    else:
        tri = cc_ <= rr
    tri = jnp.where(tri, 1.0, 0.0).astype(BF16)
    d3 = _dot(tri, jnp.concatenate(_split3(logf), axis=1))
    d = d3[:, 0:LANES] + (d3[:, LANES:2 * LANES] + d3[:, 2 * LANES:3 * LANES])
    if sample:
        dt_ref[...] = jnp.transpose(d)[0:SUBLANES, :]
    else:
        d = d + carry_d[0:1, :]
        carry_d[0:1, :] = d[tm - 1:tm, :]
        dcat = jnp.concatenate(_split3(d), axis=1)
        aq = (_dot(dcat, selq_ref[...]) + oneq_ref[...]).astype(BF16)
        ak = (_dot(dcat, selk_ref[...]) + onek_ref[...]).astype(BF16)
        for p in range(da // LANES):
            lo_, hi_ = p * LANES, (p + 1) * LANES
            qa_ref[:, 2 * lo_:2 * lo_ + LANES] = qs[:, lo_:hi_]
            qa_ref[:, 2 * lo_ + LANES:2 * hi_] = aq[:, lo_:hi_]
            ka_ref[:, 2 * lo_:2 * lo_ + LANES] = kb[:, lo_:hi_]
            ka_ref[:, 2 * lo_ + LANES:2 * hi_] = ak[:, lo_:hi_]


def _decay_columns(heads):
    npair = heads // 2
    selq = np.zeros((3 * LANES, npair * LANES), np.float32)
    selk = np.zeros((3 * LANES, npair * LANES), np.float32)
    oneq = np.zeros((1, npair * LANES), np.float32)
    onek = np.zeros((1, npair * LANES), np.float32)
    for p in range(npair):
        for hh in range(2):
            for term in range(3):
                selq[term * LANES + 2 * p + hh, p * LANES + 3 * hh + term] = 1.0
                selk[term * LANES + 2 * p + hh, p * LANES + 6 + 3 * hh + term] = -1.0
                oneq[0, p * LANES + 6 + 3 * hh + term] = 1.0
                onek[0, p * LANES + 3 * hh + term] = 1.0
    return jnp.asarray(selq, BF16), jnp.asarray(selk, BF16), jnp.asarray(oneq), jnp.asarray(onek)


def _inproj(x, gain, wc, wqkv, wg, wf2, bfp, cw, cbias, *, seq_len, sample, fix=None, aug=None):
    t, dm = x.shape
    dc = cw.shape[1]
    da = wqkv.shape[1] // 3
    gw = wg.shape[1]
    if sample:
        tm = t
        seq_blocks = 1
    else:
        tm = min(512, seq_len)
        seq_blocks = seq_len // tm
    nblk = t // tm
    nseq = t // seq_len
    kern = functools.partial(_inproj_kernel, tm=tm, seq_blocks=seq_blocks, sample=sample, seq_len=seq_len)
    rows = lambda w: pl.BlockSpec((tm, w), lambda i: (i, 0))
    in_specs = [rows(dm), _const_spec((1, dm)), _const_spec(wc.shape), _const_spec(wqkv.shape), _const_spec(wg.shape),
                _const_spec(wf2.shape), _const_spec((1, LANES)), _const_spec(cw.shape), _const_spec((1, dc))]
    args = [x, gain, wc, wqkv, wg, wf2, bfp, cw, cbias]
    sds = jax.ShapeDtypeStruct
    if sample:
        in_specs += [rows(dc), rows(dc)]
        args += list(fix)
        qk_shapes = [sds((t, da), BF16), sds((t, da), BF16)]
        qk_specs = [rows(da), rows(da)]
        kv_shapes = [sds((t, da), F32), sds((t, da), F32)]
        kv_specs = [rows(da), rows(da)]
        tail_shapes = [sds((SUBLANES, t), F32), sds((t, dc), F32)]
        tail_specs = [pl.BlockSpec((SUBLANES, tm), lambda i: (0, i)), rows(dc)]
        scratch = []
    else:
        in_specs += [_const_spec(a.shape) for a in aug]
        args += list(aug)
        qk_shapes = [sds((t, 2 * da), BF16), sds((t, 2 * da), BF16)]
        qk_specs = [rows(2 * da), rows(2 * da)]
        kv_shapes = [sds((nseq, da, seq_len), F32), sds((nseq, da, seq_len), F32)]
        kv_specs = [pl.BlockSpec((None, da, tm), lambda i: (i // seq_blocks, 0, i % seq_blocks))] * 2
        tail_shapes = [sds((nseq, 2, dc), F32)]
        tail_specs = [pl.BlockSpec((None, 2, dc), lambda i: (i // seq_blocks, 0, 0))]
        scratch = [pltpu.VMEM((SUBLANES, dc), F32), pltpu.VMEM((SUBLANES, LANES), F32)]
    out_shape = ([sds((t, dc), BF16)] + qk_shapes +
                 [sds((t, da), BF16)] + kv_shapes +
                 [sds((t, gw), BF16),
                  sds((t, LANES), F32)] + tail_shapes)
    out_specs = [rows(dc)] + qk_specs + [rows(da)] + kv_specs + [rows(gw), rows(LANES)] + tail_specs
    return pl.pallas_call(
        kern, grid=(nblk,), in_specs=in_specs, out_specs=out_specs, out_shape=out_shape,
        scratch_shapes=scratch, compiler_params=_cparams(("arbitrary",)),
        name="inproj_sample" if sample else "inproj_prompt")(*args)


def _fox_prompt_kernel(qa_ref, ka_ref, v_ref, o_ref, m_sc, l_sc, acc_sc, *, tq, tk, dh):
    qi = pl.program_id(2)
    lane = lax.broadcasted_iota(jnp.int32, (1, 2 * LANES), 1)
    ext = lane - LANES
    qf = qa_ref[...].astype(F32)
    halves = []
    for h in range(2):
        keep = (((lane >= h * dh) & (lane < (h + 1) * dh))
                | ((ext >= 3 * h) & (ext < 3 * h + 3)) | ((ext >= 6 + 3 * h) & (ext < 9 + 3 * h)))
        halves.append(jnp.where(keep, qf, 0.0))
    qs = jnp.concatenate(halves, axis=0).astype(BF16)
    m_sc[...] = jnp.full_like(m_sc, NEG_BIG)
    l_sc[...] = jnp.zeros_like(l_sc)
    acc_sc[...] = jnp.zeros_like(acc_sc)
    nfull = (qi * tq) // tk

    def step(j, masked):
        ks = pl.multiple_of(j * tk, tk)
        kb = ka_ref[pl.ds(ks, tk), :]
        vb = v_ref[pl.ds(ks, tk), :]
        s = _dot_nt(qs, kb)
        if masked:
            rloc = lax.broadcasted_iota(jnp.int32, (2 * tq, tk), 0)
            rloc = jnp.where(rloc >= tq, rloc - tq, rloc) + qi * tq
            cloc = lax.broadcasted_iota(jnp.int32, (2 * tq, tk), 1) + ks
            s = jnp.where(cloc <= rloc, s, NEG_BIG)
        m_prev = m_sc[...]
        m_new = jnp.maximum(m_prev, jnp.max(s, axis=1, keepdims=True))
        p = jnp.exp(s - _lane_tile(m_new, tk))
        alpha = jnp.exp(m_prev - m_new)
        l_sc[...] = alpha * l_sc[...] + jnp.sum(p, axis=1, keepdims=True)
        acc_sc[...] = alpha * acc_sc[...] + _dot(p.astype(BF16), vb)
        m_sc[...] = m_new

    def body(j, c):
        step(j, False)
        return c

    lax.fori_loop(0, nfull, body, 0)
    step(nfull, True)
    o = acc_sc[...] / l_sc[...]
    lane_o = lax.broadcasted_iota(jnp.int32, (1, LANES), 1)
    o_ref[...] = jnp.where(lane_o < dh, o[0:tq], o[tq:2 * tq]).astype(o_ref.dtype)


def _fox_prompt(qa, ka, v, *, nseq, seq_len, dh):
    t, da = v.shape
    tk = min(FOX_KEY_BLOCK, seq_len)
    tq = min(FOX_QUERY_BLOCK, tk)
    nq = seq_len // tq
    npair = da // LANES
    kern = functools.partial(_fox_prompt_kernel, tq=tq, tk=tk, dh=dh)
    return pl.pallas_call(
        kern, grid=(nseq, npair, nq),
        in_specs=[pl.BlockSpec((tq, 2 * LANES), lambda b, hp, qi: (b * nq + qi, hp)),
                  pl.BlockSpec((seq_len, 2 * LANES), lambda b, hp, qi: (b, hp)),
                  pl.BlockSpec((seq_len, LANES), lambda b, hp, qi: (b, hp))],
        out_specs=pl.BlockSpec((tq, LANES), lambda b, hp, qi: (b * nq + qi, hp)),
        out_shape=jax.ShapeDtypeStruct((t, da), BF16),
        scratch_shapes=[pltpu.VMEM((2 * tq, LANES), F32), pltpu.VMEM((2 * tq, LANES), F32),
                        pltpu.VMEM((2 * tq, LANES), F32)],
        compiler_params=_cparams(("parallel", "parallel", "arbitrary")),
        name="fox_prompt")(qa, ka, v)


def _fox_sample_kernel(pt_ref, *refs, pp, page, s_new, heads, dh):
    k_refs = refs[:pp]
    v_refs = refs[pp:2 * pp]
    lf_refs = refs[2 * pp:3 * pp]
    qbd_ref, kn_ref, vn_ref, dtn_ref, o_ref, m_sc, l_sc, acc_sc, run_sc = refs[3 * pp:]
    b = pl.program_id(0)
    j = pl.program_id(1)
    nrow = s_new * heads
    da = heads * dh

    @pl.when(j == 0)
    def _():
        m_sc[...] = jnp.full_like(m_sc, NEG_BIG)
        l_sc[...] = jnp.zeros_like(l_sc)
        acc_sc[...] = jnp.zeros_like(acc_sc)
        run_sc[...] = jnp.zeros_like(run_sc)

    qbd = qbd_ref[...]

    def online(s, pv_fn):
        m_prev = m_sc[...]
        m_new = jnp.maximum(m_prev, jnp.max(s, axis=1, keepdims=True))
        p = jnp.exp(s - _lane_tile(m_new, s.shape[1]))
        alpha = jnp.exp(m_prev - m_new)
        l_sc[...] = alpha * l_sc[...] + jnp.sum(p, axis=1, keepdims=True)
        acc_sc[...] = _lane_tile(alpha, da) * acc_sc[...] + pv_fn(p.astype(BF16))
        m_sc[...] = m_new

    rr = lax.broadcasted_iota(jnp.int32, (page, 2 * page), 0)
    cc = lax.broadcasted_iota(jnp.int32, (page, 2 * page), 1)
    after = jnp.where((rr > cc) | (cc >= page), 1.0, 0.0).astype(BF16)
    lf = jnp.concatenate([lf_refs[i][...] for i in range(pp)], axis=0)
    n8 = pp * heads
    r3 = _dot(jnp.concatenate(_split3(lf), axis=0), after)
    both = r3[0:n8] + (r3[n8:2 * n8] + r3[2 * n8:3 * n8])
    run = run_sc[...]
    scores = []
    for i in range(pp):
        inner = both[i * heads:(i + 1) * heads, 0:page]
        rev = inner + run
        run = run + both[i * heads:(i + 1) * heads, page:2 * page]
        kb = k_refs[i][...].astype(BF16)
        scores.append(_dot(qbd, kb) + jnp.concatenate([rev] * s_new, axis=0))
    run_sc[...] = run

    def pv_pages(p):
        acc = None
        for i in range(pp):
            term = _dot_nt(p[:, i * page:(i + 1) * page], v_refs[i][...].astype(BF16))
            acc = term if acc is None else acc + term
        return acc

    online(jnp.concatenate(scores, axis=1), pv_pages)

    @pl.when(j == pl.num_programs(1) - 1)
    def _():
        ntok = kn_ref.shape[0]
        dtn = dtn_ref[...]
        lane = lax.broadcasted_iota(jnp.int32, (1, ntok), 1)
        dq_rows = [jnp.sum(jnp.where(lane == b * s_new + t, dtn, 0.0), axis=1, keepdims=True) for t in range(s_new)]
        dq = jnp.concatenate(dq_rows, axis=0)
        dk = jnp.concatenate([dtn] * s_new, axis=0)
        s = _dot_nt(qbd, kn_ref[...]) + (dq - dk)
        rowt = lax.broadcasted_iota(jnp.int32, (nrow, ntok), 0) // heads
        col = lax.broadcasted_iota(jnp.int32, (nrow, ntok), 1)
        keep = ((col // s_new) == b) & ((col % s_new) <= rowt)
        s = jnp.where(keep, s, NEG_BIG)
        online(s, lambda p: _dot(p, vn_ref[...]))
        o = acc_sc[...] / _lane_tile(l_sc[...], da)
        rowh = lax.broadcasted_iota(jnp.int32, (nrow, da), 0) % heads
        colh = lax.broadcasted_iota(jnp.int32, (nrow, da), 1) // dh
        o = jnp.where(rowh == colh, o, 0.0)
        o_ref[...] = jnp.sum(o.reshape(s_new, heads, da), axis=1)


def _fox_sample(page_table_flat, kt_pages, vt_pages, lf_pages, qbd, kn, vn, dtn, *, nb, npages, s_new, heads, dh):
    pp = min(SAMPLE_PAGES_PER_STEP, npages)
    nsteps = npages // pp
    page = kt_pages.shape[2]
    da = heads * dh
    nrow = s_new * heads
    ntok = kn.shape[0]
    kern = functools.partial(_fox_sample_kernel, pp=pp, page=page, s_new=s_new, heads=heads, dh=dh)

    def page_map(i):
        return lambda b, j, pt: (pt[b * npages + (npages - 1 - (j * pp + i))], 0, 0)

    page_specs = [pl.BlockSpec((None, da, page), page_map(i)) for i in range(pp)]
    lf_specs = [pl.BlockSpec((None, heads, page), page_map(i)) for i in range(pp)]
    grid_spec = pltpu.PrefetchScalarGridSpec(
        num_scalar_prefetch=1, grid=(nb, nsteps),
        in_specs=page_specs + page_specs + lf_specs + [
            pl.BlockSpec((None, nrow, da), lambda b, j, pt: (b, 0, 0)),
            pl.BlockSpec((ntok, da), lambda b, j, pt: (0, 0)),
            pl.BlockSpec((ntok, da), lambda b, j, pt: (0, 0)),
            pl.BlockSpec((SUBLANES, ntok), lambda b, j, pt: (0, 0))],
        out_specs=pl.BlockSpec((None, s_new, da), lambda b, j, pt: (b, 0, 0)),
        scratch_shapes=[pltpu.VMEM((nrow, LANES), F32), pltpu.VMEM((nrow, LANES), F32), pltpu.VMEM((nrow, da), F32),
                        pltpu.VMEM((heads, LANES), F32)])
    return pl.pallas_call(
        kern, grid_spec=grid_spec,
        out_shape=jax.ShapeDtypeStruct((nb, s_new, da), F32),
        compiler_params=_cparams(("parallel", "arbitrary")),
        name="fox_sample")(page_table_flat, *([kt_pages] * pp), *([vt_pages] * pp), *([lf_pages] * pp),
                           qbd, kn, vn, dtn)


def _post_attn_kernel(x_ref, gc_ref, at_ref, g_ref, wuc_ref, wua_ref, wmo_ref, gain_ref, wxq_ref, h_ref, qx_ref):
    dm = x_ref.shape[1]
    y_conv = _dot(gc_ref[...], wuc_ref[...])
    y_attn = _dot(at_ref[...].astype(BF16), wua_ref[...])
    g_conv = g_ref[:, 0:dm].astype(F32)
    g_attn = g_ref[:, dm:2 * dm].astype(F32)
    mixed = jax.nn.sigmoid(g_conv) * y_conv + jax.nn.sigmoid(g_attn) * y_attn
    h = x_ref[...] + _dot(mixed.astype(BF16), wmo_ref[...])
    h_ref[...] = h
    xn = _rms(h, gain_ref[...]).astype(BF16)
    qx_ref[...] = _dot(xn, wxq_ref[...]).astype(qx_ref.dtype)


def _post_attn(x, gc, at, g, wuc, wua, wmo, gain, wxq, *, tm, name):
    t, dm = x.shape
    dx = wxq.shape[1]
    rows = lambda w: pl.BlockSpec((tm, w), lambda i: (i, 0))
    return pl.pallas_call(
        _post_attn_kernel, grid=(t // tm,),
        in_specs=[rows(dm), rows(gc.shape[1]), rows(at.shape[1]), rows(g.shape[1]), _const_spec(wuc.shape),
                  _const_spec(wua.shape), _const_spec(wmo.shape), _const_spec((1, dm)), _const_spec(wxq.shape)],
        out_specs=[rows(dm), rows(dx)],
        out_shape=[jax.ShapeDtypeStruct((t, dm), F32), jax.ShapeDtypeStruct((t, dx), BF16)],
        compiler_params=_cparams(("parallel",)), name=name)(x, gc, at, g, wuc, wua, wmo, gain, wxq)


def _memkv_kernel(m_ref, gain_ref, wk_ref, wv_ref, k_ref, v_ref, *, xh):
    mn = _rms(m_ref[...], gain_ref[...]).astype(BF16)
    tm = m_ref.shape[0]
    xd = wk_ref.shape[1] // xh
    k = _dot(mn, wk_ref[...])
    v = _dot(mn, wv_ref[...])
    for h in range(xh):
        k_ref[pl.ds(h, tm, stride=xh), :] = k[:, h * xd:(h + 1) * xd]
        v_ref[pl.ds(h, tm, stride=xh), :] = v[:, h * xd:(h + 1) * xd]


def _memkv(mem, gain, wk, wv, *, xh):
    t, dm = mem.shape
    tm = min(512, t)
    xd = wk.shape[1] // xh
    rows = lambda w: pl.BlockSpec((tm, w), lambda i: (i, 0))
    return pl.pallas_call(
        functools.partial(_memkv_kernel, xh=xh), grid=(t // tm,),
        in_specs=[rows(dm), _const_spec((1, dm)), _const_spec(wk.shape), _const_spec(wv.shape)],
        out_specs=[pl.BlockSpec((tm * xh, xd), lambda i: (i, 0))] * 2,
        out_shape=[jax.ShapeDtypeStruct((t * xh, xd), F32)] * 2,
        compiler_params=_cparams(("parallel",)), name="memkv")(mem, gain, wk, wv)


def _xattn_kernel(q_ref, mk_ref, mv_ref, o_ref, *, xh, xd):
    q = q_ref[...].astype(BF16)
    nm = mk_ref.shape[0] // xh
    scale = xd ** -0.5
    outs = []
    for h in range(xh):
        mk = mk_ref[pl.ds(h, nm, stride=xh), :].astype(BF16)
        mv = mv_ref[pl.ds(h, nm, stride=xh), :].astype(BF16)
        s = _dot_nt(q[:, h * xd:(h + 1) * xd], mk) * scale
        m = jnp.max(s, axis=1, keepdims=True)
        p = jnp.exp(s - m)
        p = p / jnp.sum(p, axis=1, keepdims=True)
        outs.append(_dot(p.astype(BF16), mv))
    o_ref[...] = jnp.concatenate(outs, axis=1).astype(o_ref.dtype)


def _xattn(q, mk, mv, *, tq, xh, name):
    nb, s, dx = q.shape
    nm, xd = mk.shape[1:]
    kern = functools.partial(_xattn_kernel, xh=xh, xd=xd)
    return pl.pallas_call(
        kern, grid=(nb, s // tq),
        in_specs=[pl.BlockSpec((None, tq, dx), lambda b, i: (b, i, 0)),
                  pl.BlockSpec((None, nm, xd), lambda b, i: (b, 0, 0)),
                  pl.BlockSpec((None, nm, xd), lambda b, i: (b, 0, 0))],
        out_specs=pl.BlockSpec((None, tq, dx), lambda b, i: (b, i, 0)),
        out_shape=jax.ShapeDtypeStruct((nb, s, dx), q.dtype),
        compiler_params=_cparams(("parallel", "parallel")), name=name)(q, mk, mv)


def _pre_moe_kernel(h_ref, o_ref, wxo_ref, gain_ref, wrh_ref, wrl_ref, cnt_in_ref,
                    h2_ref, xn_ref, rt_ref, rr_ref, cnt_out_ref, base, *, tm, ngroups, epg):
    i = pl.program_id(0)

    @pl.when(i == 0)
    def _():
        base[...] = cnt_in_ref[...]

    h2 = h_ref[...] + _dot(o_ref[...].astype(BF16), wxo_ref[...])
    h2_ref[...] = h2
    xn = _rms(h2, gain_ref[...])
    nc = xn.shape[1] // LANES
    for c in range(nc):
        xn_ref[pl.ds(c, tm, stride=nc), :] = xn[:, c * LANES:(c + 1) * LANES]

    xh = xn.astype(BF16)
    xl = (xn - xh.astype(F32)).astype(BF16)
    lt = _dot_nt(wrh_ref[...], xh) + (_dot_nt(wrh_ref[...], xl) + _dot_nt(wrl_ref[...], xh))

    sub = lax.broadcasted_iota(jnp.int32, (SUBLANES, tm), 0)
    gl = jnp.where(sub < ngroups, lt[0:SUBLANES, :], -jnp.inf)
    gmax = jnp.max(gl, axis=0, keepdims=True)
    gidx = jnp.min(jnp.where(gl == gmax, sub, SUBLANES), axis=0, keepdims=True)
    pg = 1.0 / jnp.sum(jnp.exp(gl - gmax), axis=0, keepdims=True)
    el = jnp.zeros((epg, tm), F32)
    for g in range(ngroups):
        el = jnp.where(gidx == g, lt[EXPERT_ROW0 + g * epg:EXPERT_ROW0 + (g + 1) * epg, :], el)
    v1 = jnp.max(el, axis=0, keepdims=True)
    i1 = jnp.min(jnp.where(el == v1, sub, epg), axis=0, keepdims=True)
    el2 = jnp.where(sub == i1, -jnp.inf, el)
    v2 = jnp.max(el2, axis=0, keepdims=True)
    i2 = jnp.min(jnp.where(el2 == v2, sub, epg), axis=0, keepdims=True)
    t2 = jnp.exp(v2 - v1)
    den = 1.0 + t2
    w0 = (1.0 / den) * pg
    w1 = (t2 / den) * pg
    e0 = gidx * epg + i1
    e1 = gidx * epg + i2

    erow = lax.broadcasted_iota(jnp.int32, (LANES, tm), 0)
    oh0 = erow == e0
    oh1 = erow == e1
    rr = lax.broadcasted_iota(jnp.int32, (tm, tm), 0)
    cc = lax.broadcasted_iota(jnp.int32, (tm, tm), 1)
    triu = jnp.where(rr <= cc, 1.0, 0.0).astype(BF16)
    pre0 = _dot(jnp.where(oh0, 1.0, 0.0).astype(BF16), triu)
    pre1 = _dot(jnp.where(oh1, 1.0, 0.0).astype(BF16), triu)
    b0 = base[:, 0:1]
    tot0 = pre0[:, tm - 1:tm]
    tot1 = pre1[:, tm - 1:tm]
    rank0 = jnp.sum(jnp.where(oh0, pre0 - 1.0 + b0, 0.0), axis=0, keepdims=True)
    rank1 = jnp.sum(jnp.where(oh1, pre1 - 1.0 + (b0 + tot0), 0.0), axis=0, keepdims=True)
    newb = b0 + tot0 + tot1
    base[...] = jnp.broadcast_to(newb, base.shape)
    cnt_out_ref[...] = jnp.broadcast_to(newb, cnt_out_ref.shape)

    zero = jnp.zeros((1, tm), F32)
    rt = jnp.concatenate([e0.astype(F32), e1.astype(F32), w0, w1, rank0, rank1, zero, zero], axis=0)
    rt_ref[...] = rt
    rt_pad = jnp.concatenate([rt, jnp.zeros((LANES - ROUTE_ROWS, tm), F32)], axis=0)
    rr_ref[...] = jnp.transpose(rt_pad)


def _pre_moe(h, o, wxo, gain, wrh, wrl, cnt_in, *, tm, ngroups, epg, name):
    t, dm = h.shape
    dx = o.shape[1]
    kern = functools.partial(_pre_moe_kernel, tm=tm, ngroups=ngroups, epg=epg)
    rows = lambda w: pl.BlockSpec((tm, w), lambda i: (i, 0))
    return pl.pallas_call(
        kern, grid=(t // tm,),
        in_specs=[rows(dm), rows(dx), _const_spec(wxo.shape), _const_spec((1, dm)), _const_spec(wrh.shape),
                  _const_spec(wrl.shape), _const_spec((LANES, LANES))],
        out_specs=[rows(dm), pl.BlockSpec((tm * (dm // LANES), LANES), lambda i: (i, 0)),
                   pl.BlockSpec((ROUTE_ROWS, tm), lambda i: (0, i)), rows(LANES), _const_spec((LANES, LANES))],
        out_shape=[jax.ShapeDtypeStruct((t, dm), F32), jax.ShapeDtypeStruct((t * (dm // LANES), LANES), F32),
                   jax.ShapeDtypeStruct((ROUTE_ROWS, t), F32), jax.ShapeDtypeStruct((t, LANES), F32),
                   jax.ShapeDtypeStruct((LANES, LANES), F32)],
        scratch_shapes=[pltpu.VMEM((LANES, LANES), F32)],
        compiler_params=_cparams(("arbitrary",)), name=name)(h, o, wxo, gain, wrh, wrl, cnt_in)


def _row_copy(src, r_src, dst, r_dst, sem, nc):
    return pltpu.make_async_copy(src.at[pl.ds(pl.multiple_of(r_src * nc, nc), nc)],
                                 dst.at[pl.ds(pl.multiple_of(r_dst * nc, nc), nc)], sem)


def _slab_rows(ref, nc):
    rows = ref.shape[0] // nc
    return jnp.concatenate([ref[pl.ds(c, rows, stride=nc), :] for c in range(nc)], axis=1)


def _scatter_kernel(e_ref, rank_ref, pstart_ref, pend_ref, xp_ref, xs_ref, out_ref, zeros, sem, zsem,
                    *, tm_p, nblk_p, nexp, nc):
    i = pl.program_id(0)
    tall = rank_ref.shape[0] // TOP_K

    @pl.when(i == 0)
    def _():
        zeros[...] = jnp.zeros_like(zeros)

        def zero_block(blk):
            start = pl.multiple_of(blk * (EXPERT_ROWS * nc), EXPERT_ROWS * nc)
            return pltpu.make_async_copy(zeros, out_ref.at[pl.ds(start, EXPERT_ROWS * nc)], zsem)

        for e in range(nexp):
            @pl.when(pend_ref[e] > pstart_ref[e])
            def _():
                zero_block(pend_ref[e] // EXPERT_ROWS - 1).start()
        for e in range(nexp):
            @pl.when(pend_ref[e] > pstart_ref[e])
            def _():
                zero_block(pend_ref[e] // EXPERT_ROWS - 1).wait()
        nblk = out_ref.shape[0] // (EXPERT_ROWS * nc)
        nused = pend_ref[nexp - 1] // EXPERT_ROWS

        def start_unused(blk, c):
            zero_block(blk).start()
            return c

        def wait_unused(blk, c):
            zero_block(blk).wait()
            return c

        lax.fori_loop(nused, nblk, start_unused, 0)
        lax.fori_loop(nused, nblk, wait_unused, 0)

    def copy_rows(x_ref, base):
        tm = x_ref.shape[0] // nc

        def issue(r, c):
            for k in range(TOP_K):
                a = k * tall + base + r
                dest = pstart_ref[e_ref[a]] + rank_ref[a]
                _row_copy(x_ref, r, out_ref, dest, sem, nc).start(priority=k % 2)
            return c

        lax.fori_loop(0, tm, issue, 0, unroll=8)
        for k in range(TOP_K):
            pltpu.make_async_copy(x_ref, x_ref, sem).wait()

    @pl.when(i < nblk_p)
    def _():
        copy_rows(xp_ref, i * tm_p)

    @pl.when(i == nblk_p)
    def _():
        copy_rows(xs_ref, nblk_p * tm_p)


def _scatter(e_flat, rank_flat, pstart, pend, x_p, x_s, *, n_rows, tm_p, nexp, dm):
    nc = dm // LANES
    tp = x_p.shape[0] // nc
    ts = x_s.shape[0] // nc
    nblk_p = tp // tm_p
    kern = functools.partial(_scatter_kernel, tm_p=tm_p, nblk_p=nblk_p, nexp=nexp, nc=nc)
    grid_spec = pltpu.PrefetchScalarGridSpec(
        num_scalar_prefetch=4, grid=(nblk_p + 1,),
        in_specs=[pl.BlockSpec((tm_p * nc, LANES), lambda i, *_: (jnp.minimum(i, nblk_p - 1), 0)),
                  pl.BlockSpec((ts * nc, LANES), lambda i, *_: (0, 0))],
        out_specs=pl.BlockSpec(memory_space=pl.ANY),
        scratch_shapes=[pltpu.VMEM((EXPERT_ROWS * nc, LANES), F32), pltpu.SemaphoreType.DMA(()),
                        pltpu.SemaphoreType.DMA(())])
    return pl.pallas_call(
        kern, grid_spec=grid_spec, out_shape=jax.ShapeDtypeStruct((n_rows * nc, LANES), F32),
        compiler_params=_cparams(("arbitrary",)), name="scatter")(e_flat, rank_flat, pstart, pend, x_p, x_s)


def _experts_kernel(blk_e_ref, nused_ref, xs_ref, wg_ref, wu_ref, wd_ref, y_ref, wgb, wub, wdb, *, nc):
    i = pl.program_id(0)
    prev = blk_e_ref[jnp.maximum(i - 1, 0)]
    fresh = (i == 0) | (blk_e_ref[i] != prev)

    @pl.when(i < nused_ref[0])
    def _():
        @pl.when(fresh)
        def _():
            wgb[...] = wg_ref[...].astype(BF16)
            wub[...] = wu_ref[...].astype(BF16)
            wdb[...] = wd_ref[...].astype(BF16)
        x = _slab_rows(xs_ref, nc).astype(BF16)
        a = _dot(x, wgb[...])
        u = _dot(x, wub[...])
        hmid = (a * jax.nn.sigmoid(a)) * u
        y = _dot(hmid.astype(BF16), wdb[...])
        for c in range(nc):
            y_ref[pl.ds(c, EXPERT_ROWS, stride=nc), :] = y[:, c * LANES:(c + 1) * LANES]

    @pl.when(i >= nused_ref[0])
    def _():
        y_ref[...] = jnp.zeros_like(y_ref)


def _experts(blk_e, nused, xs, wg, wu, wd):
    dm, de = wg.shape[1:]
    nc = dm // LANES
    nblk = xs.shape[0] // (EXPERT_ROWS * nc)

    def row_map(i, be, nu):
        return (jnp.minimum(i, nu[0] - 1), 0)

    grid_spec = pltpu.PrefetchScalarGridSpec(
        num_scalar_prefetch=2, grid=(nblk,),
        in_specs=[pl.BlockSpec((EXPERT_ROWS * nc, LANES), row_map),
                  pl.BlockSpec((None, dm, de), lambda i, be, nu: (be[i], 0, 0)),
                  pl.BlockSpec((None, dm, de), lambda i, be, nu: (be[i], 0, 0)),
                  pl.BlockSpec((None, de, dm), lambda i, be, nu: (be[i], 0, 0))],
        out_specs=pl.BlockSpec((EXPERT_ROWS * nc, LANES), lambda i, be, nu: (i, 0)),
        scratch_shapes=[pltpu.VMEM((dm, de), BF16), pltpu.VMEM((dm, de), BF16), pltpu.VMEM((de, dm), BF16)])
    return pl.pallas_call(
        functools.partial(_experts_kernel, nc=nc), grid_spec=grid_spec, out_shape=jax.ShapeDtypeStruct(xs.shape, F32),
        compiler_params=_cparams(("arbitrary",)), name="experts")(blk_e, nused, xs, wg, wu, wd)


def _combine_kernel(e_ref, rank_ref, pstart_ref, h_ref, rr_ref, gain_ref, y_hbm, o_ref, buf, sem, *, tm, tok0):
    i = pl.program_id(0)
    nc = h_ref.shape[1] // LANES
    base = tok0 + i * tm

    def issue(r, c):
        for k in range(TOP_K):
            a = k * rank_ref.shape[0] // TOP_K + base + r
            dest = pstart_ref[e_ref[a]] + rank_ref[a]
            _row_copy(y_hbm, dest, buf.at[k], r, sem, nc).start(priority=k % 2)
        return c

    lax.fori_loop(0, tm, issue, 0, unroll=8)
    for k in range(TOP_K):
        pltpu.make_async_copy(buf.at[k], buf.at[k], sem).wait()
    rr = rr_ref[...]
    h = h_ref[...] + (rr[:, 2:3] * _slab_rows(buf.at[0], nc) + rr[:, 3:4] * _slab_rows(buf.at[1], nc))
    o_ref[...] = _rms(h, gain_ref[...])


def _combine(e_flat, rank_flat, pstart, h, rr, gain, y, *, tm, tok0, name):
    t, dm = h.shape
    kern = functools.partial(_combine_kernel, tm=tm, tok0=tok0)
    grid_spec = pltpu.PrefetchScalarGridSpec(
        num_scalar_prefetch=3, grid=(t // tm,),
        in_specs=[pl.BlockSpec((tm, dm), lambda i, *_: (i, 0)), pl.BlockSpec((tm, LANES), lambda i, *_: (i, 0)),
                  pl.BlockSpec((1, dm), lambda i, *_: (0, 0)), pl.BlockSpec(memory_space=pl.ANY)],
        out_specs=pl.BlockSpec((tm, dm), lambda i, *_: (i, 0)),
        scratch_shapes=[pltpu.VMEM((TOP_K, tm * (dm // LANES), LANES), F32), pltpu.SemaphoreType.DMA(())])
    return pl.pallas_call(
        kern, grid_spec=grid_spec, out_shape=jax.ShapeDtypeStruct((t, dm), F32),
        compiler_params=_cparams(("arbitrary",)), name=name)(e_flat, rank_flat, pstart, h, rr, gain, y)


def kernel(x_prompt, x_sample, cache_k, cache_v, cache_logf, cache_mem_k, cache_mem_v, state_conv, page_table,
           mem_prompt, norm_mix, w_in, b_forget, conv_w, conv_b, w_up_conv, w_up_attn, w_mix_out, norm_xattn,
           norm_mem, w_xq, w_xk, w_xv, w_xo, norm_ffn, w_router_group, w_router_expert, w_expert_gate,
           w_expert_up, w_expert_down, norm_final):
    depth = w_in.shape[0]
    assert depth == 1, "single-layer trunk"
    nbp, seq, dm = x_prompt.shape
    nbs, s_new, _ = x_sample.shape
    _, n_pool, page, heads, dh = cache_k.shape
    npages = page_table.shape[1]
    nmem, xh, xd = cache_mem_k.shape[2:]
    dc = conv_w.shape[2]
    da = heads * dh
    dx = xh * xd
    ngroups, _, epg = w_router_expert.shape[1:]
    nexp = ngroups * epg
    tp = nbp * seq
    ts = nbs * s_new
    assert conv_w.shape[1] == 3 and s_new >= 2 and dh == 64 and heads == SUBLANES and epg == SUBLANES
    assert page == LANES

    l = 0
    wi = w_in[l]
    wc = wi[:, 0:3 * dc].astype(BF16)
    wqkv = wi[:, 3 * dc:3 * dc + 3 * da].astype(BF16)
    o_f = 3 * dc + 3 * da
    wf = jnp.pad(wi[:, o_f:o_f + heads], ((0, 0), (0, LANES - heads)))
    wfh = wf.astype(BF16)
    wf2 = jnp.concatenate([wfh, (wf - wfh.astype(F32)).astype(BF16)], axis=1)
    wg = wi[:, o_f + heads:].astype(BF16)
    bfp = jnp.pad(b_forget[l][None, :], ((0, 0), (0, LANES - heads)))
    cw = conv_w[l]
    cbias = conv_b[l][None, :]
    g_mix = norm_mix[l][None, :]
    wuc = w_up_conv[l].astype(BF16)
    wua = w_up_attn[l].astype(BF16)
    wmo = w_mix_out[l].astype(BF16)
    g_x = norm_xattn[l][None, :]
    wxq = w_xq[l].astype(BF16)
    wxo = w_xo[l].astype(BF16)
    g_f = norm_ffn[l][None, :]
    wr = jnp.zeros((LANES, dm), F32)
    wr = wr.at[0:ngroups].set(w_router_group[l].T)
    wr = wr.at[EXPERT_ROW0:EXPERT_ROW0 + nexp].set(jnp.transpose(w_router_expert[l], (0, 2, 1)).reshape(nexp, dm))
    wrh = wr.astype(BF16)
    wrl = (wr - wrh.astype(F32)).astype(BF16)

    xp = x_prompt.reshape(tp, dm)
    (gc_p, qa_p, ka_p, vb_p, kt_p, vt_p, g_p, lf_p, ulast_p) = _inproj(
        xp, g_mix, wc, wqkv, wg, wf2, bfp, cw, cbias, seq_len=seq, sample=False, aug=_decay_columns(heads))
    at_p = _fox_prompt(qa_p, ka_p, vb_p, nseq=nbp, seq_len=seq, dh=dh)
    tm_p = min(512, seq)
    h1_p, qx_p = _post_attn(xp, gc_p, at_p, g_p, wuc, wua, wmo, g_x, wxq, tm=tm_p, name="post_attn_prompt")
    mk_p, mv_p = _memkv(mem_prompt.reshape(nbp * nmem, dm), norm_mem[l][None, :], w_xk[l].astype(BF16),
                        w_xv[l].astype(BF16), xh=xh)
    o_p = _xattn(qx_p.reshape(nbp, seq, dx), mk_p.reshape(nbp, nmem * xh, xd), mv_p.reshape(nbp, nmem * xh, xd),
                 tq=tm_p, xh=xh, name="xattn_prompt").reshape(tp, dx)
    cnt0 = jnp.zeros((LANES, LANES), F32)
    h2_p, xn_p, rt_p, rr_p, cnt1 = _pre_moe(h1_p, o_p, wxo, g_f, wrh, wrl, cnt0, tm=tm_p, ngroups=ngroups, epg=epg,
                                            name="pre_moe_prompt")

    xs_ = x_sample.reshape(ts, dm)
    st = state_conv[l]
    zeros_row = jnp.zeros((nbs, 1, dc), F32)
    fix1 = jnp.concatenate([st[:, 1:2], jnp.tile(zeros_row, (1, s_new - 1, 1))], axis=1).reshape(ts, dc)
    fix2 = jnp.concatenate([st[:, 0:1], st[:, 1:2], jnp.tile(zeros_row, (1, s_new - 2, 1))], axis=1).reshape(ts, dc)
    (gc_s, q_s, kb_s, vb_s, kf_s, vf_s, g_s, lf_s, dt_s, u_s) = _inproj(
        xs_, g_mix, wc, wqkv, wg, wf2, bfp, cw, cbias, seq_len=s_new, sample=True, fix=(fix1, fix2))
    pt_flat = page_table.reshape(-1).astype(jnp.int32)
    head_of_col = jnp.arange(da) // dh
    qbd = jnp.where(head_of_col[None, None, None, :] == jnp.arange(heads)[None, None, :, None],
                    q_s.reshape(nbs, s_new, 1, da), jnp.zeros((), BF16)).reshape(nbs, s_new * heads, da)
    kt_pages = jnp.transpose(cache_k[l], (0, 2, 3, 1)).reshape(n_pool, da, page)
    vt_pages = jnp.transpose(cache_v[l], (0, 2, 3, 1)).reshape(n_pool, da, page)
    lf_pages = jnp.swapaxes(cache_logf[l], 1, 2)
    at_s = _fox_sample(pt_flat, kt_pages, vt_pages, lf_pages, qbd, kb_s, vb_s, dt_s,
                       nb=nbs, npages=npages, s_new=s_new, heads=heads, dh=dh)
    h1_s, qx_s = _post_attn(xs_, gc_s, at_s.reshape(ts, da), g_s, wuc, wua, wmo, g_x, wxq, tm=ts,
                            name="post_attn_sample")
    qx_s8 = jnp.pad(qx_s.astype(F32).reshape(nbs, s_new, dx), ((0, 0), (0, SUBLANES - s_new), (0, 0)))
    o_s = _xattn(qx_s8, cache_mem_k[l].reshape(nbs, nmem * xh, xd), cache_mem_v[l].reshape(nbs, nmem * xh, xd),
                 tq=SUBLANES, xh=xh, name="xattn_sample")[:, :s_new].reshape(ts, dx)
    h2_s, xn_s, rt_s, rr_s, cnt2 = _pre_moe(h1_s, o_s, wxo, g_f, wrh, wrl, cnt1, tm=ts, ngroups=ngroups, epg=epg,
                                            name="pre_moe_sample")

    tall = tp + ts
    counts = cnt2[0:nexp, 0].astype(jnp.int32)
    padded = (counts + EXPERT_ROWS - 1) // EXPERT_ROWS * EXPERT_ROWS
    pend = jnp.cumsum(padded).astype(jnp.int32)
    pstart = pend - padded
    nblk = (tall * TOP_K + nexp * (EXPERT_ROWS - 1)) // EXPERT_ROWS
    n_rows = nblk * EXPERT_ROWS
    blk_row0 = jnp.arange(nblk, dtype=jnp.int32) * EXPERT_ROWS
    blk_e = jnp.minimum(jnp.sum((pend[None, :] <= blk_row0[:, None]).astype(jnp.int32), axis=1), nexp - 1)
    nused = (pend[nexp - 1:nexp] // EXPERT_ROWS).astype(jnp.int32)
    rt_all = jnp.concatenate([rt_p, rt_s], axis=1)
    e_flat = rt_all[0:TOP_K].astype(jnp.int32).reshape(-1)
    rank_flat = rt_all[4:4 + TOP_K].astype(jnp.int32).reshape(-1)
    xsg = _scatter(e_flat, rank_flat, pstart, pend, xn_p, xn_s, n_rows=n_rows, tm_p=tm_p, nexp=nexp, dm=dm)
    y = _experts(blk_e, nused, xsg, w_expert_gate[l], w_expert_up[l], w_expert_down[l])
    g_fin = norm_final[None, :]
    y_p = _combine(e_flat, rank_flat, pstart, h2_p, rr_p, g_fin, y, tm=tm_p // 2, tok0=0, name="combine_prompt")
    y_s = _combine(e_flat, rank_flat, pstart, h2_s, rr_s, g_fin, y, tm=ts, tok0=tp, name="combine_sample")

    return (y_p.reshape(nbp, seq, dm), y_s.reshape(nbs, s_new, dm),
            ulast_p[None],
            jnp.transpose(kt_p.reshape(nbp, heads, dh, seq), (0, 3, 1, 2))[None],
            jnp.transpose(vt_p.reshape(nbp, heads, dh, seq), (0, 3, 1, 2))[None],
            lf_p[:, :heads].reshape(1, nbp, seq, heads),
            mk_p.reshape(1, nbp, nmem, xh, xd), mv_p.reshape(1, nbp, nmem, xh, xd),
            u_s.reshape(nbs, s_new, dc)[None, :, s_new - 2:], kf_s.reshape(1, nbs, s_new, heads, dh),
            vf_s.reshape(1, nbs, s_new, heads, dh), lf_s[:, :heads].reshape(1, nbs, s_new, heads))
```

```python
import functools

import numpy as np
import jax
import jax.numpy as jnp
from jax import lax
from jax.experimental import pallas as pl
from jax.experimental.pallas import tpu as pltpu

F32 = jnp.float32
BF16 = jnp.bfloat16

RMS_EPS = 1e-6
TOP_K = 2
LANES = 128
SUBLANES = 8
VMEM_LIMIT = 56 * 1024 * 1024
NEG_BIG = -1e30
EXPERT_ROWS = 256
SAMPLE_PAGES_PER_STEP = 32
FUSED_PAGES_PER_ITERATION = 8
FUSED_RING = 3
FOX_QUERY_BLOCK = 512
FOX_KEY_BLOCK = 512
ROUTE_ROWS = 8
EXPERT_ROW0 = 8


def _cparams(sem, vmem=VMEM_LIMIT):
    return pltpu.CompilerParams(dimension_semantics=sem, vmem_limit_bytes=vmem)


def _rms(x, g):
    ms = jnp.mean(x * x, axis=-1, keepdims=True)
    return x * lax.rsqrt(ms + RMS_EPS) * g


def _split3(x):
    hi = x.astype(BF16)
    r = x - hi.astype(F32)
    mid = r.astype(BF16)
    lo = (r - mid.astype(F32)).astype(BF16)
    return hi, mid, lo


def _dot(a, b):
    return jnp.dot(a, b, preferred_element_type=F32)


def _dot_nt(a, b):
    return lax.dot_general(a, b, (((1,), (1,)), ((), ())), preferred_element_type=F32)


def _lane_tile(x, width):
    if width % LANES == 0:
        return jnp.concatenate([x] * (width // LANES), axis=1)
    return jnp.broadcast_to(x[:, 0:1], (x.shape[0], width))


def _const_spec(shape):
    nd = len(shape)
    return pl.BlockSpec(shape, lambda *_: (0,) * nd)


def _inproj_kernel(*refs, tm, seq_blocks, sample, seq_len):
    if sample:
        (x_ref, gain_ref, wc_ref, wqkv_ref, wg_ref, wf2_ref, bf_ref, cw_ref, cb_ref, fix1_ref, fix2_ref,
         gc_ref, q_ref, kbf_ref, vbf_ref, kf_ref, vf_ref, g_ref, logf_ref, dt_ref, u_ref) = refs
    else:
        (x_ref, gain_ref, wc_ref, wqkv_ref, wg_ref, wf2_ref, bf_ref, cw_ref, cb_ref,
         selq_ref, selk_ref, oneq_ref, onek_ref,
         gc_ref, qa_ref, ka_ref, vbf_ref, kf_ref, vf_ref, g_ref, logf_ref, u_ref,
         carry_u, carry_d) = refs
    i = pl.program_id(0)
    dc = cw_ref.shape[1]
    da = wqkv_ref.shape[1] // 3

    xn = _rms(x_ref[...], gain_ref[...])
    xb = xn.astype(BF16)

    cb = _dot(xb, wc_ref[:, 0:dc])
    cc = _dot(xb, wc_ref[:, dc:2 * dc])
    cx = _dot(xb, wc_ref[:, 2 * dc:3 * dc])
    u = cc * cx
    row = lax.broadcasted_iota(jnp.int32, (tm, 1), 0)
    r1 = pltpu.roll(u, 1, axis=0)
    r2 = pltpu.roll(u, 2, axis=0)
    if sample:
        pos = row % seq_len
        p1 = jnp.where(pos == 0, fix1_ref[...], r1)
        p2 = jnp.where(pos < 2, fix2_ref[...], r2)
        u_ref[...] = u
    else:
        @pl.when(i % seq_blocks == 0)
        def _():
            carry_u[...] = jnp.zeros_like(carry_u)
            carry_d[...] = jnp.zeros_like(carry_d)
        c0 = carry_u[0:1, :]
        c1 = carry_u[1:2, :]
        p1 = jnp.where(row == 0, c1, r1)
        p2 = jnp.where(row == 0, c0, jnp.where(row == 1, c1, r2))
        carry_u[0:2, :] = u[tm - 2:tm, :]
        u_ref[...] = u[tm - 2:tm, :]
    cw = cw_ref[...]
    cy = cb_ref[...] + cw[0:1, :] * p2 + cw[1:2, :] * p1 + cw[2:3, :] * u
    gc_ref[...] = (cb * cy).astype(BF16)

    qs = (_dot(xb, wqkv_ref[:, 0:da]) * (1.0 / 8.0)).astype(BF16)
    k = _dot(xb, wqkv_ref[:, da:2 * da])
    kb = k.astype(BF16)
    v = _dot(xb, wqkv_ref[:, 2 * da:3 * da])
    vbf_ref[...] = v.astype(BF16)
    if sample:
        q_ref[...] = qs
        kbf_ref[...] = kb
        kf_ref[...] = k
        vf_ref[...] = v
    else:
        kf_ref[...] = jnp.transpose(k)
        vf_ref[...] = jnp.transpose(v)

    gw = g_ref.shape[1]
    for c in range(gw // 512):
        g_ref[:, c * 512:(c + 1) * 512] = _dot(xb, wg_ref[:, c * 512:(c + 1) * 512]).astype(BF16)

    xl = (xn - xb.astype(F32)).astype(BF16)
    hh_hl = _dot(xb, wf2_ref[...])
    fz = hh_hl[:, 0:LANES] + (hh_hl[:, LANES:2 * LANES] + _dot(xl, wf2_ref[:, 0:LANES]))
    z = fz + bf_ref[...]
    logf = jnp.minimum(z, 0.0) - jnp.log1p(jnp.exp(-jnp.abs(z)))
    lane = lax.broadcasted_iota(jnp.int32, (1, LANES), 1)
    nh = da // 64
    logf = jnp.where(lane < nh, logf, 0.0)
    logf_ref[...] = logf

    rr = lax.broadcasted_iota(jnp.int32, (tm, tm), 0)
    cc_ = lax.broadcasted_iota(jnp.int32, (tm, tm), 1)
    if sample:
        tri = (cc_ <= rr) & ((rr // seq_len) == (cc_ // seq_len))
    else:
        tri = cc_ <= rr
    tri = jnp.where(tri, 1.0, 0.0).astype(BF16)
    d3 = _dot(tri, jnp.concatenate(_split3(logf), axis=1))
    d = d3[:, 0:LANES] + (d3[:, LANES:2 * LANES] + d3[:, 2 * LANES:3 * LANES])
    if sample:
        dt_ref[...] = jnp.transpose(d)[0:SUBLANES, :]
    else:
        d = d + carry_d[0:1, :]
        carry_d[0:1, :] = d[tm - 1:tm, :]
        dcat = jnp.concatenate(_split3(d), axis=1)
        aq = (_dot(dcat, selq_ref[...]) + oneq_ref[...]).astype(BF16)
        ak = (_dot(dcat, selk_ref[...]) + onek_ref[...]).astype(BF16)
        for p in range(da // LANES):
            lo_, hi_ = p * LANES, (p + 1) * LANES
            qa_ref[:, 2 * lo_:2 * lo_ + LANES] = qs[:, lo_:hi_]
            qa_ref[:, 2 * lo_ + LANES:2 * hi_] = aq[:, lo_:hi_]
            ka_ref[:, 2 * lo_:2 * lo_ + LANES] = kb[:, lo_:hi_]
            ka_ref[:, 2 * lo_ + LANES:2 * hi_] = ak[:, lo_:hi_]


def _decay_columns(heads):
    npair = heads // 2
    selq = np.zeros((3 * LANES, npair * LANES), np.float32)
    selk = np.zeros((3 * LANES, npair * LANES), np.float32)
    oneq = np.zeros((1, npair * LANES), np.float32)
    onek = np.zeros((1, npair * LANES), np.float32)
    for p in range(npair):
        for hh in range(2):
            for term in range(3):
                selq[term * LANES + 2 * p + hh, p * LANES + 3 * hh + term] = 1.0
                selk[term * LANES + 2 * p + hh, p * LANES + 6 + 3 * hh + term] = -1.0
                oneq[0, p * LANES + 6 + 3 * hh + term] = 1.0
                onek[0, p * LANES + 3 * hh + term] = 1.0
    return jnp.asarray(selq, BF16), jnp.asarray(selk, BF16), jnp.asarray(oneq), jnp.asarray(onek)


def _inproj(x, gain, wc, wqkv, wg, wf2, bfp, cw, cbias, *, seq_len, sample, fix=None, aug=None):
    t, dm = x.shape
    dc = cw.shape[1]
    da = wqkv.shape[1] // 3
    gw = wg.shape[1]
    if sample:
        tm = t
        seq_blocks = 1
    else:
        tm = min(512, seq_len)
        seq_blocks = seq_len // tm
    nblk = t // tm
    nseq = t // seq_len
    kern = functools.partial(_inproj_kernel, tm=tm, seq_blocks=seq_blocks, sample=sample, seq_len=seq_len)
    rows = lambda w: pl.BlockSpec((tm, w), lambda i: (i, 0))
    in_specs = [rows(dm), _const_spec((1, dm)), _const_spec(wc.shape), _const_spec(wqkv.shape), _const_spec(wg.shape),
                _const_spec(wf2.shape), _const_spec((1, LANES)), _const_spec(cw.shape), _const_spec((1, dc))]
    args = [x, gain, wc, wqkv, wg, wf2, bfp, cw, cbias]
    sds = jax.ShapeDtypeStruct
    if sample:
        in_specs += [rows(dc), rows(dc)]
        args += list(fix)
        qk_shapes = [sds((t, da), BF16), sds((t, da), BF16)]
        qk_specs = [rows(da), rows(da)]
        kv_shapes = [sds((t, da), F32), sds((t, da), F32)]
        kv_specs = [rows(da), rows(da)]
        tail_shapes = [sds((SUBLANES, t), F32), sds((t, dc), F32)]
        tail_specs = [pl.BlockSpec((SUBLANES, tm), lambda i: (0, i)), rows(dc)]
        scratch = []
    else:
        in_specs += [_const_spec(a.shape) for a in aug]
        args += list(aug)
        qk_shapes = [sds((t, 2 * da), BF16), sds((t, 2 * da), BF16)]
        qk_specs = [rows(2 * da), rows(2 * da)]
        kv_shapes = [sds((nseq, da, seq_len), F32), sds((nseq, da, seq_len), F32)]
        kv_specs = [pl.BlockSpec((None, da, tm), lambda i: (i // seq_blocks, 0, i % seq_blocks))] * 2
        tail_shapes = [sds((nseq, 2, dc), F32)]
        tail_specs = [pl.BlockSpec((None, 2, dc), lambda i: (i // seq_blocks, 0, 0))]
        scratch = [pltpu.VMEM((SUBLANES, dc), F32), pltpu.VMEM((SUBLANES, LANES), F32)]
    out_shape = ([sds((t, dc), BF16)] + qk_shapes +
                 [sds((t, da), BF16)] + kv_shapes +
                 [sds((t, gw), BF16),
                  sds((t, LANES), F32)] + tail_shapes)
    out_specs = [rows(dc)] + qk_specs + [rows(da)] + kv_specs + [rows(gw), rows(LANES)] + tail_specs
    return pl.pallas_call(
        kern, grid=(nblk,), in_specs=in_specs, out_specs=out_specs, out_shape=out_shape,
        scratch_shapes=scratch, compiler_params=_cparams(("arbitrary",)),
        name="inproj_sample" if sample else "inproj_prompt")(*args)


def _fox_prompt_kernel(qa_ref, ka_ref, v_ref, o_ref, m_sc, l_sc, acc_sc, *, tq, tk, dh):
    qi = pl.program_id(2)
    lane = lax.broadcasted_iota(jnp.int32, (1, 2 * LANES), 1)
    ext = lane - LANES
    qf = qa_ref[...].astype(F32)
    halves = []
    for h in range(2):
        keep = (((lane >= h * dh) & (lane < (h + 1) * dh))
                | ((ext >= 3 * h) & (ext < 3 * h + 3)) | ((ext >= 6 + 3 * h) & (ext < 9 + 3 * h)))
        halves.append(jnp.where(keep, qf, 0.0))
    qs = jnp.concatenate(halves, axis=0).astype(BF16)
    m_sc[...] = jnp.full_like(m_sc, NEG_BIG)
    l_sc[...] = jnp.zeros_like(l_sc)
    acc_sc[...] = jnp.zeros_like(acc_sc)
    nfull = (qi * tq) // tk

    def step(j, masked):
        ks = pl.multiple_of(j * tk, tk)
        kb = ka_ref[pl.ds(ks, tk), :]
        vb = v_ref[pl.ds(ks, tk), :]
        s = _dot_nt(qs, kb)
        if masked:
            rloc = lax.broadcasted_iota(jnp.int32, (2 * tq, tk), 0)
            rloc = jnp.where(rloc >= tq, rloc - tq, rloc) + qi * tq
            cloc = lax.broadcasted_iota(jnp.int32, (2 * tq, tk), 1) + ks
            s = jnp.where(cloc <= rloc, s, NEG_BIG)
        m_prev = m_sc[...]
        m_new = jnp.maximum(m_prev, jnp.max(s, axis=1, keepdims=True))
        p = jnp.exp(s - _lane_tile(m_new, tk))
        alpha = jnp.exp(m_prev - m_new)
        l_sc[...] = alpha * l_sc[...] + jnp.sum(p, axis=1, keepdims=True)
        acc_sc[...] = alpha * acc_sc[...] + _dot(p.astype(BF16), vb)
        m_sc[...] = m_new

    def body(j, c):
        step(j, False)
        return c

    lax.fori_loop(0, nfull, body, 0)
    step(nfull, True)
    o = acc_sc[...] / l_sc[...]
    lane_o = lax.broadcasted_iota(jnp.int32, (1, LANES), 1)
    o_ref[...] = jnp.where(lane_o < dh, o[0:tq], o[tq:2 * tq]).astype(o_ref.dtype)


def _fox_prompt(qa, ka, v, *, nseq, seq_len, dh):
    t, da = v.shape
    tk = min(FOX_KEY_BLOCK, seq_len)
    tq = min(FOX_QUERY_BLOCK, tk)
    nq = seq_len // tq
    npair = da // LANES
    kern = functools.partial(_fox_prompt_kernel, tq=tq, tk=tk, dh=dh)
    return pl.pallas_call(
        kern, grid=(nseq, npair, nq),
        in_specs=[pl.BlockSpec((tq, 2 * LANES), lambda b, hp, qi: (b * nq + qi, hp)),
                  pl.BlockSpec((seq_len, 2 * LANES), lambda b, hp, qi: (b, hp)),
                  pl.BlockSpec((seq_len, LANES), lambda b, hp, qi: (b, hp))],
        out_specs=pl.BlockSpec((tq, LANES), lambda b, hp, qi: (b * nq + qi, hp)),
        out_shape=jax.ShapeDtypeStruct((t, da), BF16),
        scratch_shapes=[pltpu.VMEM((2 * tq, LANES), F32), pltpu.VMEM((2 * tq, LANES), F32),
                        pltpu.VMEM((2 * tq, LANES), F32)],
        compiler_params=_cparams(("parallel", "parallel", "arbitrary")),
        name="fox_prompt")(qa, ka, v)


def _fox_sample_kernel(pt_ref, *refs, pp, page, s_new, heads, dh):
    k_refs = refs[:pp]
    v_refs = refs[pp:2 * pp]
    lf_refs = refs[2 * pp:3 * pp]
    qbd_ref, kn_ref, vn_ref, dtn_ref, o_ref, m_sc, l_sc, acc_sc, run_sc = refs[3 * pp:]
    b = pl.program_id(0)
    j = pl.program_id(1)
    nrow = s_new * heads
    da = heads * dh

    @pl.when(j == 0)
    def _():
        m_sc[...] = jnp.full_like(m_sc, NEG_BIG)
        l_sc[...] = jnp.zeros_like(l_sc)
        acc_sc[...] = jnp.zeros_like(acc_sc)
        run_sc[...] = jnp.zeros_like(run_sc)

    qbd = qbd_ref[...]

    def online(s, pv_fn):
        m_prev = m_sc[...]
        m_new = jnp.maximum(m_prev, jnp.max(s, axis=1, keepdims=True))
        p = jnp.exp(s - _lane_tile(m_new, s.shape[1]))
        alpha = jnp.exp(m_prev - m_new)
        l_sc[...] = alpha * l_sc[...] + jnp.sum(p, axis=1, keepdims=True)
        acc_sc[...] = _lane_tile(alpha, da) * acc_sc[...] + pv_fn(p.astype(BF16))
        m_sc[...] = m_new

    rr = lax.broadcasted_iota(jnp.int32, (page, 2 * page), 0)
    cc = lax.broadcasted_iota(jnp.int32, (page, 2 * page), 1)
    after = jnp.where((rr > cc) | (cc >= page), 1.0, 0.0).astype(BF16)
    lf = jnp.concatenate([lf_refs[i][...] for i in range(pp)], axis=0)
    n8 = pp * heads
    r3 = _dot(jnp.concatenate(_split3(lf), axis=0), after)
    both = r3[0:n8] + (r3[n8:2 * n8] + r3[2 * n8:3 * n8])
    run = run_sc[...]
    scores = []
    for i in range(pp):
        inner = both[i * heads:(i + 1) * heads, 0:page]
        rev = inner + run
        run = run + both[i * heads:(i + 1) * heads, page:2 * page]
        kb = k_refs[i][...].astype(BF16)
        scores.append(_dot(qbd, kb) + jnp.concatenate([rev] * s_new, axis=0))
    run_sc[...] = run

    def pv_pages(p):
        acc = None
        for i in range(pp):
            term = _dot_nt(p[:, i * page:(i + 1) * page], v_refs[i][...].astype(BF16))
            acc = term if acc is None else acc + term
        return acc

    online(jnp.concatenate(scores, axis=1), pv_pages)

    @pl.when(j == pl.num_programs(1) - 1)
    def _():
        ntok = kn_ref.shape[0]
        dtn = dtn_ref[...]
        lane = lax.broadcasted_iota(jnp.int32, (1, ntok), 1)
        dq_rows = [jnp.sum(jnp.where(lane == b * s_new + t, dtn, 0.0), axis=1, keepdims=True) for t in range(s_new)]
        dq = jnp.concatenate(dq_rows, axis=0)
        dk = jnp.concatenate([dtn] * s_new, axis=0)
        s = _dot_nt(qbd, kn_ref[...]) + (dq - dk)
        rowt = lax.broadcasted_iota(jnp.int32, (nrow, ntok), 0) // heads
        col = lax.broadcasted_iota(jnp.int32, (nrow, ntok), 1)
        keep = ((col // s_new) == b) & ((col % s_new) <= rowt)
        s = jnp.where(keep, s, NEG_BIG)
        online(s, lambda p: _dot(p, vn_ref[...]))
        o = acc_sc[...] / _lane_tile(l_sc[...], da)
        rowh = lax.broadcasted_iota(jnp.int32, (nrow, da), 0) % heads
        colh = lax.broadcasted_iota(jnp.int32, (nrow, da), 1) // dh
        o = jnp.where(rowh == colh, o, 0.0)
        o_ref[...] = jnp.sum(o.reshape(s_new, heads, da), axis=1)


def _fox_sample(page_table_flat, kt_pages, vt_pages, lf_pages, qbd, kn, vn, dtn, *, nb, npages, s_new, heads, dh):
    pp = min(SAMPLE_PAGES_PER_STEP, npages)
    nsteps = npages // pp
    page = kt_pages.shape[2]
    da = heads * dh
    nrow = s_new * heads
    ntok = kn.shape[0]
    kern = functools.partial(_fox_sample_kernel, pp=pp, page=page, s_new=s_new, heads=heads, dh=dh)

    def page_map(i):
        return lambda b, j, pt: (pt[b * npages + (npages - 1 - (j * pp + i))], 0, 0)

    page_specs = [pl.BlockSpec((None, da, page), page_map(i)) for i in range(pp)]
    lf_specs = [pl.BlockSpec((None, heads, page), page_map(i)) for i in range(pp)]
    grid_spec = pltpu.PrefetchScalarGridSpec(
        num_scalar_prefetch=1, grid=(nb, nsteps),
        in_specs=page_specs + page_specs + lf_specs + [
            pl.BlockSpec((None, nrow, da), lambda b, j, pt: (b, 0, 0)),
            pl.BlockSpec((ntok, da), lambda b, j, pt: (0, 0)),
            pl.BlockSpec((ntok, da), lambda b, j, pt: (0, 0)),
            pl.BlockSpec((SUBLANES, ntok), lambda b, j, pt: (0, 0))],
        out_specs=pl.BlockSpec((None, s_new, da), lambda b, j, pt: (b, 0, 0)),
        scratch_shapes=[pltpu.VMEM((nrow, LANES), F32), pltpu.VMEM((nrow, LANES), F32), pltpu.VMEM((nrow, da), F32),
                        pltpu.VMEM((heads, LANES), F32)])
    return pl.pallas_call(
        kern, grid_spec=grid_spec,
        out_shape=jax.ShapeDtypeStruct((nb, s_new, da), F32),
        compiler_params=_cparams(("parallel", "arbitrary")),
        name="fox_sample")(page_table_flat, *([kt_pages] * pp), *([vt_pages] * pp), *([lf_pages] * pp),
                           qbd, kn, vn, dtn)


def _fox_fused_kernel(pt_ref, qa_ref, ka_ref, v_ref, kt_hbm, vt_hbm, lf_hbm, qbd_ref, kn_ref, vn_ref, dtn_ref,
                      o_ref, os_ref,
                      m_sc, l_sc, acc_sc, ms_sc, ls_sc, as_sc, run_sc, kbuf, vbuf, lbuf, sem, g_ref,
                      *, tq, tk, dh, pp, ring, npages, nchunks, s_new, heads):
    first_step = (pl.program_id(0) == 0) & (pl.program_id(1) == 0) & (pl.program_id(2) == 0)
    qi = pl.program_id(2)
    page = kbuf.shape[3]
    cpb = npages // pp
    nrow = s_new * heads
    da = heads * dh
    nbs = qbd_ref.shape[0]

    def chunk_copies(c):
        slot = c % ring
        bs = c // cpb
        jc = c % cpb
        copies = []
        for i in range(pp):
            pid = pt_ref[bs * npages + (npages - 1 - (jc * pp + i))]
            copies.append(pltpu.make_async_copy(kt_hbm.at[pid], kbuf.at[slot, i], sem.at[slot]))
            copies.append(pltpu.make_async_copy(vt_hbm.at[pid], vbuf.at[slot, i], sem.at[slot]))
            copies.append(pltpu.make_async_copy(lf_hbm.at[pid], lbuf.at[slot, i], sem.at[slot]))
        return copies

    @pl.when(first_step)
    def _():
        g_ref[0] = 0
        for c in range(min(ring - 1, nchunks)):
            for cp in chunk_copies(c):
                cp.start()

    lane = lax.broadcasted_iota(jnp.int32, (1, 2 * LANES), 1)
    ext = lane - LANES
    qf = qa_ref[...].astype(F32)
    halves = []
    for h in range(2):
        keep = (((lane >= h * dh) & (lane < (h + 1) * dh))
                | ((ext >= 3 * h) & (ext < 3 * h + 3)) | ((ext >= 6 + 3 * h) & (ext < 9 + 3 * h)))
        halves.append(jnp.where(keep, qf, 0.0))
    qs = jnp.concatenate(halves, axis=0).astype(BF16)
    m_sc[...] = jnp.full_like(m_sc, NEG_BIG)
    l_sc[...] = jnp.zeros_like(l_sc)
    acc_sc[...] = jnp.zeros_like(acc_sc)
    nfull = (qi * tq) // tk

    def prompt_step(j, masked):
        ks = pl.multiple_of(j * tk, tk)
        kb = ka_ref[pl.ds(ks, tk), :]
        vb = v_ref[pl.ds(ks, tk), :]
        s = _dot_nt(qs, kb)
        if masked:
            rloc = lax.broadcasted_iota(jnp.int32, (2 * tq, tk), 0)
            rloc = jnp.where(rloc >= tq, rloc - tq, rloc) + qi * tq
            cloc = lax.broadcasted_iota(jnp.int32, (2 * tq, tk), 1) + ks
            s = jnp.where(cloc <= rloc, s, NEG_BIG)
        m_prev = m_sc[...]
        m_new = jnp.maximum(m_prev, jnp.max(s, axis=1, keepdims=True))
        p = jnp.exp(s - _lane_tile(m_new, tk))
        alpha = jnp.exp(m_prev - m_new)
        l_sc[...] = alpha * l_sc[...] + jnp.sum(p, axis=1, keepdims=True)
        acc_sc[...] = alpha * acc_sc[...] + _dot(p.astype(BF16), vb)
        m_sc[...] = m_new

    def sample_update(s, pv_fn):
        m_prev = ms_sc[...]
        m_new = jnp.maximum(m_prev, jnp.max(s, axis=1, keepdims=True))
        p = jnp.exp(s - _lane_tile(m_new, s.shape[1]))
        alpha = jnp.exp(m_prev - m_new)
        ls_sc[...] = alpha * ls_sc[...] + jnp.sum(p, axis=1, keepdims=True)
        as_sc[...] = _lane_tile(alpha, da) * as_sc[...] + pv_fn(p.astype(BF16))
        ms_sc[...] = m_new

    def sample_chunk(g, valid):
        slot = g % ring
        bs = jnp.minimum(g // cpb, nbs - 1)
        qbd = qbd_ref[bs]
        rr = lax.broadcasted_iota(jnp.int32, (page, 2 * page), 0)
        cc = lax.broadcasted_iota(jnp.int32, (page, 2 * page), 1)
        after = jnp.where((rr > cc) | (cc >= page), 1.0, 0.0).astype(BF16)
        lf = jnp.concatenate([lbuf[slot, i] for i in range(pp)], axis=0)
        n8 = pp * heads
        r3 = _dot(jnp.concatenate(_split3(lf), axis=0), after)
        both = r3[0:n8] + (r3[n8:2 * n8] + r3[2 * n8:3 * n8])
        run = run_sc[...]
        scores = []
        for i in range(pp):
            rev = both[i * heads:(i + 1) * heads, 0:page] + run
            run = run + both[i * heads:(i + 1) * heads, page:2 * page]
            scores.append(_dot(qbd, kbuf[slot, i].astype(BF16)) + jnp.concatenate([rev] * s_new, axis=0))
        run_sc[...] = run
        s = jnp.where(valid, jnp.concatenate(scores, axis=1), NEG_BIG)

        def pv_pages(p):
            acc = None
            for i in range(pp):
                term = _dot_nt(p[:, i * page:(i + 1) * page], vbuf[slot, i].astype(BF16))
                acc = term if acc is None else acc + term
            return acc

        sample_update(s, pv_pages)

    def sample_finish(bs):
        ntok = kn_ref.shape[0]
        qbd = qbd_ref[bs]
        dtn = dtn_ref[...]
        lane_t = lax.broadcasted_iota(jnp.int32, (1, ntok), 1)
        dq_rows = [jnp.sum(jnp.where(lane_t == bs * s_new + t, dtn, 0.0), axis=1, keepdims=True)
                   for t in range(s_new)]
        dq = jnp.concatenate(dq_rows, axis=0)
        dk = jnp.concatenate([dtn] * s_new, axis=0)
        s = _dot_nt(qbd, kn_ref[...]) + (dq - dk)
        rowt = lax.broadcasted_iota(jnp.int32, (nrow, ntok), 0) // heads
        col = lax.broadcasted_iota(jnp.int32, (nrow, ntok), 1)
        keep = ((col // s_new) == bs) & ((col % s_new) <= rowt)
        sample_update(jnp.where(keep, s, NEG_BIG), lambda p: _dot(p, vn_ref[...]))
        o = as_sc[...] / _lane_tile(ls_sc[...], da)
        rowh = lax.broadcasted_iota(jnp.int32, (nrow, da), 0) % heads
        colh = lax.broadcasted_iota(jnp.int32, (nrow, da), 1) // dh
        o = jnp.where(rowh == colh, o, 0.0)
        os_ref[bs] = jnp.sum(o.reshape(s_new, heads, da), axis=1)

    def iteration(j, masked):
        g = g_ref[0]
        valid = g < nchunks

        @pl.when(g + (ring - 1) < nchunks)
        def _():
            for cp in chunk_copies(g + (ring - 1)):
                cp.start()

        @pl.when(valid)
        def _():
            for cp in chunk_copies(g):
                cp.wait()

        @pl.when(valid & (g % cpb == 0))
        def _():
            ms_sc[...] = jnp.full_like(ms_sc, NEG_BIG)
            ls_sc[...] = jnp.zeros_like(ls_sc)
            as_sc[...] = jnp.zeros_like(as_sc)
            run_sc[...] = jnp.zeros_like(run_sc)

        prompt_step(j, masked)
        sample_chunk(g, valid)

        @pl.when(valid & (g % cpb == cpb - 1))
        def _():
            sample_finish(g // cpb)

        g_ref[0] = g + 1

    def body(j, c):
        iteration(j, False)
        return c

    lax.fori_loop(0, nfull, body, 0)
    iteration(nfull, True)
    o = acc_sc[...] / l_sc[...]
    lane_o = lax.broadcasted_iota(jnp.int32, (1, LANES), 1)
    o_ref[...] = jnp.where(lane_o < dh, o[0:tq], o[tq:2 * tq]).astype(o_ref.dtype)


def _fox_fused(page_table_flat, qa, ka, v, kt_pages, vt_pages, lf_pages, qbd, kn, vn, dtn,
               *, nseq, seq_len, dh, npages, s_new, heads):
    t, da = v.shape
    nbs, nrow, _ = qbd.shape
    page = kt_pages.shape[2]
    ntok = kn.shape[0]
    tk = min(FOX_KEY_BLOCK, seq_len)
    tq = min(FOX_QUERY_BLOCK, tk)
    nq = seq_len // tq
    npair = da // LANES
    pp = min(FUSED_PAGES_PER_ITERATION, npages)
    nchunks = nbs * (npages // pp)
    iters = nseq * npair * sum((qi * tq) // tk + 1 for qi in range(nq))
    assert npages % pp == 0 and nchunks <= iters, "the page stream must fit in the prompt attention's iterations"
    kern = functools.partial(_fox_fused_kernel, tq=tq, tk=tk, dh=dh, pp=pp, ring=FUSED_RING, npages=npages,
                             nchunks=nchunks, s_new=s_new, heads=heads)
    const = lambda shape: pl.BlockSpec(shape, lambda b, hp, qi, pt: (0,) * len(shape))
    grid_spec = pltpu.PrefetchScalarGridSpec(
        num_scalar_prefetch=1, grid=(nseq, npair, nq),
        in_specs=[pl.BlockSpec((tq, 2 * LANES), lambda b, hp, qi, pt: (b * nq + qi, hp)),
                  pl.BlockSpec((seq_len, 2 * LANES), lambda b, hp, qi, pt: (b, hp)),
                  pl.BlockSpec((seq_len, LANES), lambda b, hp, qi, pt: (b, hp)),
                  pl.BlockSpec(memory_space=pl.ANY), pl.BlockSpec(memory_space=pl.ANY),
                  pl.BlockSpec(memory_space=pl.ANY),
                  const((nbs, nrow, da)), const((ntok, da)), const((ntok, da)), const((SUBLANES, ntok))],
        out_specs=[pl.BlockSpec((tq, LANES), lambda b, hp, qi, pt: (b * nq + qi, hp)),
                   const((nbs, s_new, da))],
        scratch_shapes=[pltpu.VMEM((2 * tq, LANES), F32), pltpu.VMEM((2 * tq, LANES), F32),
                        pltpu.VMEM((2 * tq, LANES), F32),
                        pltpu.VMEM((nrow, LANES), F32), pltpu.VMEM((nrow, LANES), F32), pltpu.VMEM((nrow, da), F32),
                        pltpu.VMEM((heads, LANES), F32),
                        pltpu.VMEM((FUSED_RING, pp, da, page), F32), pltpu.VMEM((FUSED_RING, pp, da, page), F32),
                        pltpu.VMEM((FUSED_RING, pp, heads, page), F32),
                        pltpu.SemaphoreType.DMA((FUSED_RING,)), pltpu.SMEM((1,), jnp.int32)])
    return pl.pallas_call(
        kern, grid_spec=grid_spec,
        out_shape=[jax.ShapeDtypeStruct((t, da), BF16), jax.ShapeDtypeStruct((nbs, s_new, da), F32)],
        compiler_params=_cparams(("arbitrary", "arbitrary", "arbitrary")),
        name="fox_fused")(page_table_flat, qa, ka, v, kt_pages, vt_pages, lf_pages, qbd, kn, vn, dtn)


def _post_attn_kernel(x_ref, gc_ref, at_ref, g_ref, wuc_ref, wua_ref, wmo_ref, gain_ref, wxq_ref, h_ref, qx_ref):
    dm = x_ref.shape[1]
    y_conv = _dot(gc_ref[...], wuc_ref[...])
    y_attn = _dot(at_ref[...].astype(BF16), wua_ref[...])
    g_conv = g_ref[:, 0:dm].astype(F32)
    g_attn = g_ref[:, dm:2 * dm].astype(F32)
    mixed = jax.nn.sigmoid(g_conv) * y_conv + jax.nn.sigmoid(g_attn) * y_attn
    h = x_ref[...] + _dot(mixed.astype(BF16), wmo_ref[...])
    h_ref[...] = h
    xn = _rms(h, gain_ref[...]).astype(BF16)
    qx_ref[...] = _dot(xn, wxq_ref[...]).astype(qx_ref.dtype)


def _post_attn(x, gc, at, g, wuc, wua, wmo, gain, wxq, *, tm, name):
    t, dm = x.shape
    dx = wxq.shape[1]
    rows = lambda w: pl.BlockSpec((tm, w), lambda i: (i, 0))
    return pl.pallas_call(
        _post_attn_kernel, grid=(t // tm,),
        in_specs=[rows(dm), rows(gc.shape[1]), rows(at.shape[1]), rows(g.shape[1]), _const_spec(wuc.shape),
                  _const_spec(wua.shape), _const_spec(wmo.shape), _const_spec((1, dm)), _const_spec(wxq.shape)],
        out_specs=[rows(dm), rows(dx)],
        out_shape=[jax.ShapeDtypeStruct((t, dm), F32), jax.ShapeDtypeStruct((t, dx), BF16)],
        compiler_params=_cparams(("parallel",)), name=name)(x, gc, at, g, wuc, wua, wmo, gain, wxq)


def _memkv_kernel(m_ref, gain_ref, wk_ref, wv_ref, k_ref, v_ref, *, xh):
    mn = _rms(m_ref[...], gain_ref[...]).astype(BF16)
    tm = m_ref.shape[0]
    xd = wk_ref.shape[1] // xh
    k = _dot(mn, wk_ref[...])
    v = _dot(mn, wv_ref[...])
    for h in range(xh):
        k_ref[pl.ds(h, tm, stride=xh), :] = k[:, h * xd:(h + 1) * xd]
        v_ref[pl.ds(h, tm, stride=xh), :] = v[:, h * xd:(h + 1) * xd]


def _memkv(mem, gain, wk, wv, *, xh):
    t, dm = mem.shape
    tm = min(512, t)
    xd = wk.shape[1] // xh
    rows = lambda w: pl.BlockSpec((tm, w), lambda i: (i, 0))
    return pl.pallas_call(
        functools.partial(_memkv_kernel, xh=xh), grid=(t // tm,),
        in_specs=[rows(dm), _const_spec((1, dm)), _const_spec(wk.shape), _const_spec(wv.shape)],
        out_specs=[pl.BlockSpec((tm * xh, xd), lambda i: (i, 0))] * 2,
        out_shape=[jax.ShapeDtypeStruct((t * xh, xd), F32)] * 2,
        compiler_params=_cparams(("parallel",)), name="memkv")(mem, gain, wk, wv)


def _xattn_kernel(q_ref, mk_ref, mv_ref, o_ref, *, xh, xd):
    q = q_ref[...].astype(BF16)
    nm = mk_ref.shape[0] // xh
    scale = xd ** -0.5
    outs = []
    for h in range(xh):
        mk = mk_ref[pl.ds(h, nm, stride=xh), :].astype(BF16)
        mv = mv_ref[pl.ds(h, nm, stride=xh), :].astype(BF16)
        s = _dot_nt(q[:, h * xd:(h + 1) * xd], mk) * scale
        m = jnp.max(s, axis=1, keepdims=True)
        p = jnp.exp(s - m)
        p = p / jnp.sum(p, axis=1, keepdims=True)
        outs.append(_dot(p.astype(BF16), mv))
    o_ref[...] = jnp.concatenate(outs, axis=1).astype(o_ref.dtype)


def _xattn(q, mk, mv, *, tq, xh, name):
    nb, s, dx = q.shape
    nm, xd = mk.shape[1:]
    kern = functools.partial(_xattn_kernel, xh=xh, xd=xd)
    return pl.pallas_call(
        kern, grid=(nb, s // tq),
        in_specs=[pl.BlockSpec((None, tq, dx), lambda b, i: (b, i, 0)),
                  pl.BlockSpec((None, nm, xd), lambda b, i: (b, 0, 0)),
                  pl.BlockSpec((None, nm, xd), lambda b, i: (b, 0, 0))],
        out_specs=pl.BlockSpec((None, tq, dx), lambda b, i: (b, i, 0)),
        out_shape=jax.ShapeDtypeStruct((nb, s, dx), q.dtype),
        compiler_params=_cparams(("parallel", "parallel")), name=name)(q, mk, mv)


def _pre_moe_kernel(h_ref, o_ref, wxo_ref, gain_ref, wrh_ref, wrl_ref, cnt_in_ref,
                    h2_ref, xn_ref, rt_ref, rr_ref, cnt_out_ref, base, *, tm, ngroups, epg):
    i = pl.program_id(0)

    @pl.when(i == 0)
    def _():
        base[...] = cnt_in_ref[...]

    h2 = h_ref[...] + _dot(o_ref[...].astype(BF16), wxo_ref[...])
    h2_ref[...] = h2
    xn = _rms(h2, gain_ref[...])
    nc = xn.shape[1] // LANES
    for c in range(nc):
        xn_ref[pl.ds(c, tm, stride=nc), :] = xn[:, c * LANES:(c + 1) * LANES]

    xh = xn.astype(BF16)
    xl = (xn - xh.astype(F32)).astype(BF16)
    lt = _dot_nt(wrh_ref[...], xh) + (_dot_nt(wrh_ref[...], xl) + _dot_nt(wrl_ref[...], xh))

    sub = lax.broadcasted_iota(jnp.int32, (SUBLANES, tm), 0)
    gl = jnp.where(sub < ngroups, lt[0:SUBLANES, :], -jnp.inf)
    gmax = jnp.max(gl, axis=0, keepdims=True)
    gidx = jnp.min(jnp.where(gl == gmax, sub, SUBLANES), axis=0, keepdims=True)
    pg = 1.0 / jnp.sum(jnp.exp(gl - gmax), axis=0, keepdims=True)
    el = jnp.zeros((epg, tm), F32)
    for g in range(ngroups):
        el = jnp.where(gidx == g, lt[EXPERT_ROW0 + g * epg:EXPERT_ROW0 + (g + 1) * epg, :], el)
    v1 = jnp.max(el, axis=0, keepdims=True)
    i1 = jnp.min(jnp.where(el == v1, sub, epg), axis=0, keepdims=True)
    el2 = jnp.where(sub == i1, -jnp.inf, el)
    v2 = jnp.max(el2, axis=0, keepdims=True)
    i2 = jnp.min(jnp.where(el2 == v2, sub, epg), axis=0, keepdims=True)
    t2 = jnp.exp(v2 - v1)
    den = 1.0 + t2
    w0 = (1.0 / den) * pg
    w1 = (t2 / den) * pg
    e0 = gidx * epg + i1
    e1 = gidx * epg + i2

    erow = lax.broadcasted_iota(jnp.int32, (LANES, tm), 0)
    oh0 = erow == e0
    oh1 = erow == e1
    rr = lax.broadcasted_iota(jnp.int32, (tm, tm), 0)
    cc = lax.broadcasted_iota(jnp.int32, (tm, tm), 1)
    triu = jnp.where(rr <= cc, 1.0, 0.0).astype(BF16)
    pre0 = _dot(jnp.where(oh0, 1.0, 0.0).astype(BF16), triu)
    pre1 = _dot(jnp.where(oh1, 1.0, 0.0).astype(BF16), triu)
    b0 = base[:, 0:1]
    tot0 = pre0[:, tm - 1:tm]
    tot1 = pre1[:, tm - 1:tm]
    rank0 = jnp.sum(jnp.where(oh0, pre0 - 1.0 + b0, 0.0), axis=0, keepdims=True)
    rank1 = jnp.sum(jnp.where(oh1, pre1 - 1.0 + (b0 + tot0), 0.0), axis=0, keepdims=True)
    newb = b0 + tot0 + tot1
    base[...] = jnp.broadcast_to(newb, base.shape)
    cnt_out_ref[...] = jnp.broadcast_to(newb, cnt_out_ref.shape)

    zero = jnp.zeros((1, tm), F32)
    rt = jnp.concatenate([e0.astype(F32), e1.astype(F32), w0, w1, rank0, rank1, zero, zero], axis=0)
    rt_ref[...] = rt
    rt_pad = jnp.concatenate([rt, jnp.zeros((LANES - ROUTE_ROWS, tm), F32)], axis=0)
    rr_ref[...] = jnp.transpose(rt_pad)


def _pre_moe(h, o, wxo, gain, wrh, wrl, cnt_in, *, tm, ngroups, epg, name):
    t, dm = h.shape
    dx = o.shape[1]
    kern = functools.partial(_pre_moe_kernel, tm=tm, ngroups=ngroups, epg=epg)
    rows = lambda w: pl.BlockSpec((tm, w), lambda i: (i, 0))
    return pl.pallas_call(
        kern, grid=(t // tm,),
        in_specs=[rows(dm), rows(dx), _const_spec(wxo.shape), _const_spec((1, dm)), _const_spec(wrh.shape),
                  _const_spec(wrl.shape), _const_spec((LANES, LANES))],
        out_specs=[rows(dm), pl.BlockSpec((tm * (dm // LANES), LANES), lambda i: (i, 0)),
                   pl.BlockSpec((ROUTE_ROWS, tm), lambda i: (0, i)), rows(LANES), _const_spec((LANES, LANES))],
        out_shape=[jax.ShapeDtypeStruct((t, dm), F32), jax.ShapeDtypeStruct((t * (dm // LANES), LANES), F32),
                   jax.ShapeDtypeStruct((ROUTE_ROWS, t), F32), jax.ShapeDtypeStruct((t, LANES), F32),
                   jax.ShapeDtypeStruct((LANES, LANES), F32)],
        scratch_shapes=[pltpu.VMEM((LANES, LANES), F32)],
        compiler_params=_cparams(("arbitrary",)), name=name)(h, o, wxo, gain, wrh, wrl, cnt_in)


def _row_copy(src, r_src, dst, r_dst, sem, nc):
    return pltpu.make_async_copy(src.at[pl.ds(pl.multiple_of(r_src * nc, nc), nc)],
                                 dst.at[pl.ds(pl.multiple_of(r_dst * nc, nc), nc)], sem)


def _slab_rows(ref, nc):
    rows = ref.shape[0] // nc
    return jnp.concatenate([ref[pl.ds(c, rows, stride=nc), :] for c in range(nc)], axis=1)


def _scatter_kernel(e_ref, rank_ref, pstart_ref, pend_ref, xp_ref, xs_ref, out_ref, zeros, sem, zsem,
                    *, tm_p, nblk_p, nexp, nc):
    i = pl.program_id(0)
    tall = rank_ref.shape[0] // TOP_K

    @pl.when(i == 0)
    def _():
        zeros[...] = jnp.zeros_like(zeros)

        def zero_block(blk):
            start = pl.multiple_of(blk * (EXPERT_ROWS * nc), EXPERT_ROWS * nc)
            return pltpu.make_async_copy(zeros, out_ref.at[pl.ds(start, EXPERT_ROWS * nc)], zsem)

        for e in range(nexp):
            @pl.when(pend_ref[e] > pstart_ref[e])
            def _():
                zero_block(pend_ref[e] // EXPERT_ROWS - 1).start()
        for e in range(nexp):
            @pl.when(pend_ref[e] > pstart_ref[e])
            def _():
                zero_block(pend_ref[e] // EXPERT_ROWS - 1).wait()
        nblk = out_ref.shape[0] // (EXPERT_ROWS * nc)
        nused = pend_ref[nexp - 1] // EXPERT_ROWS

        def start_unused(blk, c):
            zero_block(blk).start()
            return c

        def wait_unused(blk, c):
            zero_block(blk).wait()
            return c

        lax.fori_loop(nused, nblk, start_unused, 0)
        lax.fori_loop(nused, nblk, wait_unused, 0)

    def copy_rows(x_ref, base):
        tm = x_ref.shape[0] // nc

        def issue(r, c):
            for k in range(TOP_K):
                a = k * tall + base + r
                dest = pstart_ref[e_ref[a]] + rank_ref[a]
                _row_copy(x_ref, r, out_ref, dest, sem, nc).start(priority=k % 2)
            return c

        lax.fori_loop(0, tm, issue, 0, unroll=8)
        for k in range(TOP_K):
            pltpu.make_async_copy(x_ref, x_ref, sem).wait()

    @pl.when(i < nblk_p)
    def _():
        copy_rows(xp_ref, i * tm_p)

    @pl.when(i == nblk_p)
    def _():
        copy_rows(xs_ref, nblk_p * tm_p)


def _scatter(e_flat, rank_flat, pstart, pend, x_p, x_s, *, n_rows, tm_p, nexp, dm):
    nc = dm // LANES
    tp = x_p.shape[0] // nc
    ts = x_s.shape[0] // nc
    nblk_p = tp // tm_p
    kern = functools.partial(_scatter_kernel, tm_p=tm_p, nblk_p=nblk_p, nexp=nexp, nc=nc)
    grid_spec = pltpu.PrefetchScalarGridSpec(
        num_scalar_prefetch=4, grid=(nblk_p + 1,),
        in_specs=[pl.BlockSpec((tm_p * nc, LANES), lambda i, *_: (jnp.minimum(i, nblk_p - 1), 0)),
                  pl.BlockSpec((ts * nc, LANES), lambda i, *_: (0, 0))],
        out_specs=pl.BlockSpec(memory_space=pl.ANY),
        scratch_shapes=[pltpu.VMEM((EXPERT_ROWS * nc, LANES), F32), pltpu.SemaphoreType.DMA(()),
                        pltpu.SemaphoreType.DMA(())])
    return pl.pallas_call(
        kern, grid_spec=grid_spec, out_shape=jax.ShapeDtypeStruct((n_rows * nc, LANES), F32),
        compiler_params=_cparams(("arbitrary",)), name="scatter")(e_flat, rank_flat, pstart, pend, x_p, x_s)


def _experts_kernel(blk_e_ref, nused_ref, xs_ref, wg_ref, wu_ref, wd_ref, y_ref, wgb, wub, wdb, *, nc):
    i = pl.program_id(0)
    prev = blk_e_ref[jnp.maximum(i - 1, 0)]
    fresh = (i == 0) | (blk_e_ref[i] != prev)

    @pl.when(i < nused_ref[0])
    def _():
        @pl.when(fresh)
        def _():
            wgb[...] = wg_ref[...].astype(BF16)
            wub[...] = wu_ref[...].astype(BF16)
            wdb[...] = wd_ref[...].astype(BF16)
        x = _slab_rows(xs_ref, nc).astype(BF16)
        a = _dot(x, wgb[...])
        u = _dot(x, wub[...])
        hmid = (a * jax.nn.sigmoid(a)) * u
        y = _dot(hmid.astype(BF16), wdb[...])
        for c in range(nc):
            y_ref[pl.ds(c, EXPERT_ROWS, stride=nc), :] = y[:, c * LANES:(c + 1) * LANES]

    @pl.when(i >= nused_ref[0])
    def _():
        y_ref[...] = jnp.zeros_like(y_ref)


def _experts(blk_e, nused, xs, wg, wu, wd):
    dm, de = wg.shape[1:]
    nc = dm // LANES
    nblk = xs.shape[0] // (EXPERT_ROWS * nc)

    def row_map(i, be, nu):
        return (jnp.minimum(i, nu[0] - 1), 0)

    grid_spec = pltpu.PrefetchScalarGridSpec(
        num_scalar_prefetch=2, grid=(nblk,),
        in_specs=[pl.BlockSpec((EXPERT_ROWS * nc, LANES), row_map),
                  pl.BlockSpec((None, dm, de), lambda i, be, nu: (be[i], 0, 0)),
                  pl.BlockSpec((None, dm, de), lambda i, be, nu: (be[i], 0, 0)),
                  pl.BlockSpec((None, de, dm), lambda i, be, nu: (be[i], 0, 0))],
        out_specs=pl.BlockSpec((EXPERT_ROWS * nc, LANES), lambda i, be, nu: (i, 0)),
        scratch_shapes=[pltpu.VMEM((dm, de), BF16), pltpu.VMEM((dm, de), BF16), pltpu.VMEM((de, dm), BF16)])
    return pl.pallas_call(
        functools.partial(_experts_kernel, nc=nc), grid_spec=grid_spec, out_shape=jax.ShapeDtypeStruct(xs.shape, F32),
        compiler_params=_cparams(("arbitrary",)), name="experts")(blk_e, nused, xs, wg, wu, wd)


def _combine_kernel(e_ref, rank_ref, pstart_ref, h_ref, rr_ref, gain_ref, y_hbm, o_ref, buf, sem, *, tm, tok0):
    i = pl.program_id(0)
    nc = h_ref.shape[1] // LANES
    base = tok0 + i * tm

    def issue(r, c):
        for k in range(TOP_K):
            a = k * rank_ref.shape[0] // TOP_K + base + r
            dest = pstart_ref[e_ref[a]] + rank_ref[a]
            _row_copy(y_hbm, dest, buf.at[k], r, sem, nc).start(priority=k % 2)
        return c

    lax.fori_loop(0, tm, issue, 0, unroll=8)
    for k in range(TOP_K):
        pltpu.make_async_copy(buf.at[k], buf.at[k], sem).wait()
    rr = rr_ref[...]
    h = h_ref[...] + (rr[:, 2:3] * _slab_rows(buf.at[0], nc) + rr[:, 3:4] * _slab_rows(buf.at[1], nc))
    o_ref[...] = _rms(h, gain_ref[...])


def _combine(e_flat, rank_flat, pstart, h, rr, gain, y, *, tm, tok0, name):
    t, dm = h.shape
    kern = functools.partial(_combine_kernel, tm=tm, tok0=tok0)
    grid_spec = pltpu.PrefetchScalarGridSpec(
        num_scalar_prefetch=3, grid=(t // tm,),
        in_specs=[pl.BlockSpec((tm, dm), lambda i, *_: (i, 0)), pl.BlockSpec((tm, LANES), lambda i, *_: (i, 0)),
                  pl.BlockSpec((1, dm), lambda i, *_: (0, 0)), pl.BlockSpec(memory_space=pl.ANY)],
        out_specs=pl.BlockSpec((tm, dm), lambda i, *_: (i, 0)),
        scratch_shapes=[pltpu.VMEM((TOP_K, tm * (dm // LANES), LANES), F32), pltpu.SemaphoreType.DMA(())])
    return pl.pallas_call(
        kern, grid_spec=grid_spec, out_shape=jax.ShapeDtypeStruct((t, dm), F32),
        compiler_params=_cparams(("arbitrary",)), name=name)(e_flat, rank_flat, pstart, h, rr, gain, y)


def kernel(x_prompt, x_sample, cache_k, cache_v, cache_logf, cache_mem_k, cache_mem_v, state_conv, page_table,
           mem_prompt, norm_mix, w_in, b_forget, conv_w, conv_b, w_up_conv, w_up_attn, w_mix_out, norm_xattn,
           norm_mem, w_xq, w_xk, w_xv, w_xo, norm_ffn, w_router_group, w_router_expert, w_expert_gate,
           w_expert_up, w_expert_down, norm_final):
    depth = w_in.shape[0]
    assert depth == 1, "single-layer trunk"
    nbp, seq, dm = x_prompt.shape
    nbs, s_new, _ = x_sample.shape
    _, n_pool, page, heads, dh = cache_k.shape
    npages = page_table.shape[1]
    nmem, xh, xd = cache_mem_k.shape[2:]
    dc = conv_w.shape[2]
    da = heads * dh
    dx = xh * xd
    ngroups, _, epg = w_router_expert.shape[1:]
    nexp = ngroups * epg
    tp = nbp * seq
    ts = nbs * s_new
    assert conv_w.shape[1] == 3 and s_new >= 2 and dh == 64 and heads == SUBLANES and epg == SUBLANES
    assert page == LANES

    l = 0
    wi = w_in[l]
    wc = wi[:, 0:3 * dc].astype(BF16)
    wqkv = wi[:, 3 * dc:3 * dc + 3 * da].astype(BF16)
    o_f = 3 * dc + 3 * da
    wf = jnp.pad(wi[:, o_f:o_f + heads], ((0, 0), (0, LANES - heads)))
    wfh = wf.astype(BF16)
    wf2 = jnp.concatenate([wfh, (wf - wfh.astype(F32)).astype(BF16)], axis=1)
    wg = wi[:, o_f + heads:].astype(BF16)
    bfp = jnp.pad(b_forget[l][None, :], ((0, 0), (0, LANES - heads)))
    cw = conv_w[l]
    cbias = conv_b[l][None, :]
    g_mix = norm_mix[l][None, :]
    wuc = w_up_conv[l].astype(BF16)
    wua = w_up_attn[l].astype(BF16)
    wmo = w_mix_out[l].astype(BF16)
    g_x = norm_xattn[l][None, :]
    wxq = w_xq[l].astype(BF16)
    wxo = w_xo[l].astype(BF16)
    g_f = norm_ffn[l][None, :]
    wr = jnp.zeros((LANES, dm), F32)
    wr = wr.at[0:ngroups].set(w_router_group[l].T)
    wr = wr.at[EXPERT_ROW0:EXPERT_ROW0 + nexp].set(jnp.transpose(w_router_expert[l], (0, 2, 1)).reshape(nexp, dm))
    wrh = wr.astype(BF16)
    wrl = (wr - wrh.astype(F32)).astype(BF16)

    xp = x_prompt.reshape(tp, dm)
    (gc_p, qa_p, ka_p, vb_p, kt_p, vt_p, g_p, lf_p, ulast_p) = _inproj(
        xp, g_mix, wc, wqkv, wg, wf2, bfp, cw, cbias, seq_len=seq, sample=False, aug=_decay_columns(heads))
    xs_ = x_sample.reshape(ts, dm)
    st = state_conv[l]
    zeros_row = jnp.zeros((nbs, 1, dc), F32)
    fix1 = jnp.concatenate([st[:, 1:2], jnp.tile(zeros_row, (1, s_new - 1, 1))], axis=1).reshape(ts, dc)
    fix2 = jnp.concatenate([st[:, 0:1], st[:, 1:2], jnp.tile(zeros_row, (1, s_new - 2, 1))], axis=1).reshape(ts, dc)
    (gc_s, q_s, kb_s, vb_s, kf_s, vf_s, g_s, lf_s, dt_s, u_s) = _inproj(
        xs_, g_mix, wc, wqkv, wg, wf2, bfp, cw, cbias, seq_len=s_new, sample=True, fix=(fix1, fix2))
    pt_flat = page_table.reshape(-1).astype(jnp.int32)
    head_of_col = jnp.arange(da) // dh
    qbd = jnp.where(head_of_col[None, None, None, :] == jnp.arange(heads)[None, None, :, None],
                    q_s.reshape(nbs, s_new, 1, da), jnp.zeros((), BF16)).reshape(nbs, s_new * heads, da)
    kt_pages = jnp.transpose(cache_k[l], (0, 2, 3, 1)).reshape(n_pool, da, page)
    vt_pages = jnp.transpose(cache_v[l], (0, 2, 3, 1)).reshape(n_pool, da, page)
    lf_pages = jnp.swapaxes(cache_logf[l], 1, 2)
    at_p, at_s = _fox_fused(pt_flat, qa_p, ka_p, vb_p, kt_pages, vt_pages, lf_pages, qbd, kb_s, vb_s, dt_s,
                            nseq=nbp, seq_len=seq, dh=dh, npages=npages, s_new=s_new, heads=heads)

    tm_p = min(512, seq)
    h1_p, qx_p = _post_attn(xp, gc_p, at_p, g_p, wuc, wua, wmo, g_x, wxq, tm=tm_p, name="post_attn_prompt")
    mk_p, mv_p = _memkv(mem_prompt.reshape(nbp * nmem, dm), norm_mem[l][None, :], w_xk[l].astype(BF16),
                        w_xv[l].astype(BF16), xh=xh)
    o_p = _xattn(qx_p.reshape(nbp, seq, dx), mk_p.reshape(nbp, nmem * xh, xd), mv_p.reshape(nbp, nmem * xh, xd),
                 tq=tm_p, xh=xh, name="xattn_prompt").reshape(tp, dx)
    cnt0 = jnp.zeros((LANES, LANES), F32)
    h2_p, xn_p, rt_p, rr_p, cnt1 = _pre_moe(h1_p, o_p, wxo, g_f, wrh, wrl, cnt0, tm=tm_p, ngroups=ngroups, epg=epg,
                                            name="pre_moe_prompt")

    h1_s, qx_s = _post_attn(xs_, gc_s, at_s.reshape(ts, da), g_s, wuc, wua, wmo, g_x, wxq, tm=ts,
                            name="post_attn_sample")
    qx_s8 = jnp.pad(qx_s.astype(F32).reshape(nbs, s_new, dx), ((0, 0), (0, SUBLANES - s_new), (0, 0)))
    o_s = _xattn(qx_s8, cache_mem_k[l].reshape(nbs, nmem * xh, xd), cache_mem_v[l].reshape(nbs, nmem * xh, xd),
                 tq=SUBLANES, xh=xh, name="xattn_sample")[:, :s_new].reshape(ts, dx)
    h2_s, xn_s, rt_s, rr_s, cnt2 = _pre_moe(h1_s, o_s, wxo, g_f, wrh, wrl, cnt1, tm=ts, ngroups=ngroups, epg=epg,
                                            name="pre_moe_sample")

    tall = tp + ts
    counts = cnt2[0:nexp, 0].astype(jnp.int32)
    padded = (counts + EXPERT_ROWS - 1) // EXPERT_ROWS * EXPERT_ROWS
    pend = jnp.cumsum(padded).astype(jnp.int32)
    pstart = pend - padded
    nblk = (tall * TOP_K + nexp * (EXPERT_ROWS - 1)) // EXPERT_ROWS
    n_rows = nblk * EXPERT_ROWS
    blk_row0 = jnp.arange(nblk, dtype=jnp.int32) * EXPERT_ROWS
    blk_e = jnp.minimum(jnp.sum((pend[None, :] <= blk_row0[:, None]).astype(jnp.int32), axis=1), nexp - 1)
    nused = (pend[nexp - 1:nexp] // EXPERT_ROWS).astype(jnp.int32)
    rt_all = jnp.concatenate([rt_p, rt_s], axis=1)
    e_flat = rt_all[0:TOP_K].astype(jnp.int32).reshape(-1)
    rank_flat = rt_all[4:4 + TOP_K].astype(jnp.int32).reshape(-1)
    xsg = _scatter(e_flat, rank_flat, pstart, pend, xn_p, xn_s, n_rows=n_rows, tm_p=tm_p, nexp=nexp, dm=dm)
    y = _experts(blk_e, nused, xsg, w_expert_gate[l], w_expert_up[l], w_expert_down[l])
    g_fin = norm_final[None, :]
    y_p = _combine(e_flat, rank_flat, pstart, h2_p, rr_p, g_fin, y, tm=tm_p // 2, tok0=0, name="combine_prompt")
    y_s = _combine(e_flat, rank_flat, pstart, h2_s, rr_s, g_fin, y, tm=ts, tok0=tp, name="combine_sample")

    return (y_p.reshape(nbp, seq, dm), y_s.reshape(nbs, s_new, dm),
            ulast_p[None],
            jnp.transpose(kt_p.reshape(nbp, heads, dh, seq), (0, 3, 1, 2))[None],
            jnp.transpose(vt_p.reshape(nbp, heads, dh, seq), (0, 3, 1, 2))[None],
            lf_p[:, :heads].reshape(1, nbp, seq, heads),
            mk_p.reshape(1, nbp, nmem, xh, xd), mv_p.reshape(1, nbp, nmem, xh, xd),
            u_s.reshape(nbs, s_new, dc)[None, :, s_new - 2:], kf_s.reshape(1, nbs, s_new, heads, dh),
            vf_s.reshape(1, nbs, s_new, heads, dh), lf_s[:, :heads].reshape(1, nbs, s_new, heads))
```

```python
import functools

import numpy as np
import jax
import jax.numpy as jnp
from jax import lax
from jax.experimental import pallas as pl
from jax.experimental.pallas import tpu as pltpu

F32 = jnp.float32
BF16 = jnp.bfloat16

RMS_EPS = 1e-6
TOP_K = 2
LANES = 128
SUBLANES = 8
VMEM_LIMIT = 56 * 1024 * 1024
NEG_BIG = -1e30
EXPERT_ROWS = 256
SAMPLE_PAGES_PER_STEP = 32
FUSED_PAGES_PER_ITERATION = 8
FUSED_RING = 3
FOX_QUERY_BLOCK = 512
FOX_KEY_BLOCK = 512
ROUTE_ROWS = 8
EXPERT_ROW0 = 8


def _cparams(sem, vmem=VMEM_LIMIT):
    return pltpu.CompilerParams(dimension_semantics=sem, vmem_limit_bytes=vmem)


def _rms(x, g):
    ms = jnp.mean(x * x, axis=-1, keepdims=True)
    return x * lax.rsqrt(ms + RMS_EPS) * g


def _split3(x):
    hi = x.astype(BF16)
    r = x - hi.astype(F32)
    mid = r.astype(BF16)
    lo = (r - mid.astype(F32)).astype(BF16)
    return hi, mid, lo


def _dot(a, b):
    return jnp.dot(a, b, preferred_element_type=F32)


def _dot_nt(a, b):
    return lax.dot_general(a, b, (((1,), (1,)), ((), ())), preferred_element_type=F32)


def _lane_tile(x, width):
    if width % LANES == 0:
        return jnp.concatenate([x] * (width // LANES), axis=1)
    return jnp.broadcast_to(x[:, 0:1], (x.shape[0], width))


def _const_spec(shape):
    nd = len(shape)
    return pl.BlockSpec(shape, lambda *_: (0,) * nd)


def _inproj_kernel(*refs, tm, seq_blocks, sample, seq_len):
    if sample:
        (x_ref, gain_ref, wc_ref, wqkv_ref, wg_ref, wf2_ref, bf_ref, cw_ref, cb_ref, fix1_ref, fix2_ref,
         gc_ref, q_ref, kbf_ref, vbf_ref, kf_ref, vf_ref, g_ref, logf_ref, dt_ref, u_ref) = refs
    else:
        (x_ref, gain_ref, wc_ref, wqkv_ref, wg_ref, wf2_ref, bf_ref, cw_ref, cb_ref,
         selq_ref, selk_ref, oneq_ref, onek_ref,
         gc_ref, qa_ref, ka_ref, vbf_ref, kf_ref, vf_ref, g_ref, logf_ref, u_ref,
         carry_u, carry_d) = refs
    i = pl.program_id(0)
    dc = cw_ref.shape[1]
    da = wqkv_ref.shape[1] // 3

    xn = _rms(x_ref[...], gain_ref[...])
    xb = xn.astype(BF16)

    cb = _dot(xb, wc_ref[:, 0:dc])
    cc = _dot(xb, wc_ref[:, dc:2 * dc])
    cx = _dot(xb, wc_ref[:, 2 * dc:3 * dc])
    u = cc * cx
    row = lax.broadcasted_iota(jnp.int32, (tm, 1), 0)
    r1 = pltpu.roll(u, 1, axis=0)
    r2 = pltpu.roll(u, 2, axis=0)
    if sample:
        pos = row % seq_len
        p1 = jnp.where(pos == 0, fix1_ref[...], r1)
        p2 = jnp.where(pos < 2, fix2_ref[...], r2)
        u_ref[...] = u
    else:
        @pl.when(i % seq_blocks == 0)
        def _():
            carry_u[...] = jnp.zeros_like(carry_u)
            carry_d[...] = jnp.zeros_like(carry_d)
        c0 = carry_u[0:1, :]
        c1 = carry_u[1:2, :]
        p1 = jnp.where(row == 0, c1, r1)
        p2 = jnp.where(row == 0, c0, jnp.where(row == 1, c1, r2))
        carry_u[0:2, :] = u[tm - 2:tm, :]
        u_ref[...] = u[tm - 2:tm, :]
    cw = cw_ref[...]
    cy = cb_ref[...] + cw[0:1, :] * p2 + cw[1:2, :] * p1 + cw[2:3, :] * u
    gc_ref[...] = (cb * cy).astype(BF16)

    qs = (_dot(xb, wqkv_ref[:, 0:da]) * (1.0 / 8.0)).astype(BF16)
    k = _dot(xb, wqkv_ref[:, da:2 * da])
    kb = k.astype(BF16)
    v = _dot(xb, wqkv_ref[:, 2 * da:3 * da])
    vbf_ref[...] = v.astype(BF16)
    if sample:
        q_ref[...] = qs
        kbf_ref[...] = kb
        kf_ref[...] = k
        vf_ref[...] = v
    else:
        kf_ref[...] = jnp.transpose(k)
        vf_ref[...] = jnp.transpose(v)

    gw = g_ref.shape[1]
    for c in range(gw // 512):
        g_ref[:, c * 512:(c + 1) * 512] = _dot(xb, wg_ref[:, c * 512:(c + 1) * 512]).astype(BF16)

    xl = (xn - xb.astype(F32)).astype(BF16)
    hh_hl = _dot(xb, wf2_ref[...])
    fz = hh_hl[:, 0:LANES] + (hh_hl[:, LANES:2 * LANES] + _dot(xl, wf2_ref[:, 0:LANES]))
    z = fz + bf_ref[...]
    logf = jnp.minimum(z, 0.0) - jnp.log1p(jnp.exp(-jnp.abs(z)))
    lane = lax.broadcasted_iota(jnp.int32, (1, LANES), 1)
    nh = da // 64
    logf = jnp.where(lane < nh, logf, 0.0)
    if sample:
        logf_ref[...] = logf
    else:
        logf_ref[...] = jnp.transpose(logf)[0:SUBLANES, :]

    rr = lax.broadcasted_iota(jnp.int32, (tm, tm), 0)
    cc_ = lax.broadcasted_iota(jnp.int32, (tm, tm), 1)
    if sample:
        tri = (cc_ <= rr) & ((rr // seq_len) == (cc_ // seq_len))
    else:
        tri = cc_ <= rr
    tri = jnp.where(tri, 1.0, 0.0).astype(BF16)
    d3 = _dot(tri, jnp.concatenate(_split3(logf), axis=1))
    d = d3[:, 0:LANES] + (d3[:, LANES:2 * LANES] + d3[:, 2 * LANES:3 * LANES])
    if sample:
        dt_ref[...] = jnp.transpose(d)[0:SUBLANES, :]
    else:
        d = d + carry_d[0:1, :]
        carry_d[0:1, :] = d[tm - 1:tm, :]
        dcat = jnp.concatenate(_split3(d), axis=1)
        aq = (_dot(dcat, selq_ref[...]) + oneq_ref[...]).astype(BF16)
        ak = (_dot(dcat, selk_ref[...]) + onek_ref[...]).astype(BF16)
        for p in range(da // LANES):
            lo_, hi_ = p * LANES, (p + 1) * LANES
            qa_ref[:, 2 * lo_:2 * lo_ + LANES] = qs[:, lo_:hi_]
            qa_ref[:, 2 * lo_ + LANES:2 * hi_] = aq[:, lo_:hi_]
            ka_ref[:, 2 * lo_:2 * lo_ + LANES] = kb[:, lo_:hi_]
            ka_ref[:, 2 * lo_ + LANES:2 * hi_] = ak[:, lo_:hi_]


def _decay_columns(heads):
    npair = heads // 2
    selq = np.zeros((3 * LANES, npair * LANES), np.float32)
    selk = np.zeros((3 * LANES, npair * LANES), np.float32)
    oneq = np.zeros((1, npair * LANES), np.float32)
    onek = np.zeros((1, npair * LANES), np.float32)
    for p in range(npair):
        for hh in range(2):
            for term in range(3):
                selq[term * LANES + 2 * p + hh, p * LANES + 3 * hh + term] = 1.0
                selk[term * LANES + 2 * p + hh, p * LANES + 6 + 3 * hh + term] = -1.0
                oneq[0, p * LANES + 6 + 3 * hh + term] = 1.0
                onek[0, p * LANES + 3 * hh + term] = 1.0
    return jnp.asarray(selq, BF16), jnp.asarray(selk, BF16), jnp.asarray(oneq), jnp.asarray(onek)


def _inproj(x, gain, wc, wqkv, wg, wf2, bfp, cw, cbias, *, seq_len, sample, fix=None, aug=None):
    t, dm = x.shape
    dc = cw.shape[1]
    da = wqkv.shape[1] // 3
    gw = wg.shape[1]
    if sample:
        tm = t
        seq_blocks = 1
    else:
        tm = min(512, seq_len)
        seq_blocks = seq_len // tm
    nblk = t // tm
    nseq = t // seq_len
    kern = functools.partial(_inproj_kernel, tm=tm, seq_blocks=seq_blocks, sample=sample, seq_len=seq_len)
    rows = lambda w: pl.BlockSpec((tm, w), lambda i: (i, 0))
    in_specs = [rows(dm), _const_spec((1, dm)), _const_spec(wc.shape), _const_spec(wqkv.shape), _const_spec(wg.shape),
                _const_spec(wf2.shape), _const_spec((1, LANES)), _const_spec(cw.shape), _const_spec((1, dc))]
    args = [x, gain, wc, wqkv, wg, wf2, bfp, cw, cbias]
    sds = jax.ShapeDtypeStruct
    if sample:
        in_specs += [rows(dc), rows(dc)]
        args += list(fix)
        qk_shapes = [sds((t, da), BF16), sds((t, da), BF16)]
        qk_specs = [rows(da), rows(da)]
        kv_shapes = [sds((t, da), F32), sds((t, da), F32)]
        kv_specs = [rows(da), rows(da)]
        tail_shapes = [sds((t, LANES), F32), sds((SUBLANES, t), F32), sds((t, dc), F32)]
        tail_specs = [rows(LANES), pl.BlockSpec((SUBLANES, tm), lambda i: (0, i)), rows(dc)]
        scratch = []
    else:
        in_specs += [_const_spec(a.shape) for a in aug]
        args += list(aug)
        qk_shapes = [sds((t, 2 * da), BF16), sds((t, 2 * da), BF16)]
        qk_specs = [rows(2 * da), rows(2 * da)]
        kv_shapes = [sds((nseq, da, seq_len), F32), sds((nseq, da, seq_len), F32)]
        kv_specs = [pl.BlockSpec((None, da, tm), lambda i: (i // seq_blocks, 0, i % seq_blocks))] * 2
        tail_shapes = [sds((nseq, SUBLANES, seq_len), F32), sds((nseq, 2, dc), F32)]
        tail_specs = [pl.BlockSpec((None, SUBLANES, tm), lambda i: (i // seq_blocks, 0, i % seq_blocks)),
                      pl.BlockSpec((None, 2, dc), lambda i: (i // seq_blocks, 0, 0))]
        scratch = [pltpu.VMEM((SUBLANES, dc), F32), pltpu.VMEM((SUBLANES, LANES), F32)]
    out_shape = ([sds((t, dc), BF16)] + qk_shapes +
                 [sds((t, da), BF16)] + kv_shapes +
                 [sds((t, gw), BF16)] + tail_shapes)
    out_specs = [rows(dc)] + qk_specs + [rows(da)] + kv_specs + [rows(gw)] + tail_specs
    return pl.pallas_call(
        kern, grid=(nblk,), in_specs=in_specs, out_specs=out_specs, out_shape=out_shape,
        scratch_shapes=scratch, compiler_params=_cparams(("arbitrary",)),
        name="inproj_sample" if sample else "inproj_prompt")(*args)


def _fox_prompt_kernel(qa_ref, ka_ref, v_ref, o_ref, m_sc, l_sc, acc_sc, *, tq, tk, dh):
    qi = pl.program_id(2)
    lane = lax.broadcasted_iota(jnp.int32, (1, 2 * LANES), 1)
    ext = lane - LANES
    qf = qa_ref[...].astype(F32)
    halves = []
    for h in range(2):
        keep = (((lane >= h * dh) & (lane < (h + 1) * dh))
                | ((ext >= 3 * h) & (ext < 3 * h + 3)) | ((ext >= 6 + 3 * h) & (ext < 9 + 3 * h)))
        halves.append(jnp.where(keep, qf, 0.0))
    qs = jnp.concatenate(halves, axis=0).astype(BF16)
    m_sc[...] = jnp.full_like(m_sc, NEG_BIG)
    l_sc[...] = jnp.zeros_like(l_sc)
    acc_sc[...] = jnp.zeros_like(acc_sc)
    nfull = (qi * tq) // tk

    def step(j, masked):
        ks = pl.multiple_of(j * tk, tk)
        kb = ka_ref[pl.ds(ks, tk), :]
        vb = v_ref[pl.ds(ks, tk), :]
        s = _dot_nt(qs, kb)
        if masked:
            rloc = lax.broadcasted_iota(jnp.int32, (2 * tq, tk), 0)
            rloc = jnp.where(rloc >= tq, rloc - tq, rloc) + qi * tq
            cloc = lax.broadcasted_iota(jnp.int32, (2 * tq, tk), 1) + ks
            s = jnp.where(cloc <= rloc, s, NEG_BIG)
        m_prev = m_sc[...]
        m_new = jnp.maximum(m_prev, jnp.max(s, axis=1, keepdims=True))
        p = jnp.exp(s - _lane_tile(m_new, tk))
        alpha = jnp.exp(m_prev - m_new)
        l_sc[...] = alpha * l_sc[...] + jnp.sum(p, axis=1, keepdims=True)
        acc_sc[...] = alpha * acc_sc[...] + _dot(p.astype(BF16), vb)
        m_sc[...] = m_new

    def body(j, c):
        step(j, False)
        return c

    lax.fori_loop(0, nfull, body, 0)
    step(nfull, True)
    o = acc_sc[...] / l_sc[...]
    lane_o = lax.broadcasted_iota(jnp.int32, (1, LANES), 1)
    o_ref[...] = jnp.where(lane_o < dh, o[0:tq], o[tq:2 * tq]).astype(o_ref.dtype)


def _fox_prompt(qa, ka, v, *, nseq, seq_len, dh):
    t, da = v.shape
    tk = min(FOX_KEY_BLOCK, seq_len)
    tq = min(FOX_QUERY_BLOCK, tk)
    nq = seq_len // tq
    npair = da // LANES
    kern = functools.partial(_fox_prompt_kernel, tq=tq, tk=tk, dh=dh)
    return pl.pallas_call(
        kern, grid=(nseq, npair, nq),
        in_specs=[pl.BlockSpec((tq, 2 * LANES), lambda b, hp, qi: (b * nq + qi, hp)),
                  pl.BlockSpec((seq_len, 2 * LANES), lambda b, hp, qi: (b, hp)),
                  pl.BlockSpec((seq_len, LANES), lambda b, hp, qi: (b, hp))],
        out_specs=pl.BlockSpec((tq, LANES), lambda b, hp, qi: (b * nq + qi, hp)),
        out_shape=jax.ShapeDtypeStruct((t, da), BF16),
        scratch_shapes=[pltpu.VMEM((2 * tq, LANES), F32), pltpu.VMEM((2 * tq, LANES), F32),
                        pltpu.VMEM((2 * tq, LANES), F32)],
        compiler_params=_cparams(("parallel", "parallel", "arbitrary")),
        name="fox_prompt")(qa, ka, v)


def _fox_sample_kernel(pt_ref, *refs, pp, page, s_new, heads, dh):
    k_refs = refs[:pp]
    v_refs = refs[pp:2 * pp]
    lf_refs = refs[2 * pp:3 * pp]
    qbd_ref, kn_ref, vn_ref, dtn_ref, o_ref, m_sc, l_sc, acc_sc, run_sc = refs[3 * pp:]
    b = pl.program_id(0)
    j = pl.program_id(1)
    nrow = s_new * heads
    da = heads * dh

    @pl.when(j == 0)
    def _():
        m_sc[...] = jnp.full_like(m_sc, NEG_BIG)
        l_sc[...] = jnp.zeros_like(l_sc)
        acc_sc[...] = jnp.zeros_like(acc_sc)
        run_sc[...] = jnp.zeros_like(run_sc)

    qbd = qbd_ref[...]

    def online(s, pv_fn):
        m_prev = m_sc[...]
        m_new = jnp.maximum(m_prev, jnp.max(s, axis=1, keepdims=True))
        p = jnp.exp(s - _lane_tile(m_new, s.shape[1]))
        alpha = jnp.exp(m_prev - m_new)
        l_sc[...] = alpha * l_sc[...] + jnp.sum(p, axis=1, keepdims=True)
        acc_sc[...] = _lane_tile(alpha, da) * acc_sc[...] + pv_fn(p.astype(BF16))
        m_sc[...] = m_new

    rr = lax.broadcasted_iota(jnp.int32, (page, 2 * page), 0)
    cc = lax.broadcasted_iota(jnp.int32, (page, 2 * page), 1)
    after = jnp.where((rr > cc) | (cc >= page), 1.0, 0.0).astype(BF16)
    lf = jnp.concatenate([lf_refs[i][...] for i in range(pp)], axis=0)
    n8 = pp * heads
    r3 = _dot(jnp.concatenate(_split3(lf), axis=0), after)
    both = r3[0:n8] + (r3[n8:2 * n8] + r3[2 * n8:3 * n8])
    run = run_sc[...]
    scores = []
    for i in range(pp):
        inner = both[i * heads:(i + 1) * heads, 0:page]
        rev = inner + run
        run = run + both[i * heads:(i + 1) * heads, page:2 * page]
        kb = k_refs[i][...].astype(BF16)
        scores.append(_dot(qbd, kb) + jnp.concatenate([rev] * s_new, axis=0))
    run_sc[...] = run

    def pv_pages(p):
        acc = None
        for i in range(pp):
            term = _dot_nt(p[:, i * page:(i + 1) * page], v_refs[i][...].astype(BF16))
            acc = term if acc is None else acc + term
        return acc

    online(jnp.concatenate(scores, axis=1), pv_pages)

    @pl.when(j == pl.num_programs(1) - 1)
    def _():
        ntok = kn_ref.shape[0]
        dtn = dtn_ref[...]
        lane = lax.broadcasted_iota(jnp.int32, (1, ntok), 1)
        dq_rows = [jnp.sum(jnp.where(lane == b * s_new + t, dtn, 0.0), axis=1, keepdims=True) for t in range(s_new)]
        dq = jnp.concatenate(dq_rows, axis=0)
        dk = jnp.concatenate([dtn] * s_new, axis=0)
        s = _dot_nt(qbd, kn_ref[...]) + (dq - dk)
        rowt = lax.broadcasted_iota(jnp.int32, (nrow, ntok), 0) // heads
        col = lax.broadcasted_iota(jnp.int32, (nrow, ntok), 1)
        keep = ((col // s_new) == b) & ((col % s_new) <= rowt)
        s = jnp.where(keep, s, NEG_BIG)
        online(s, lambda p: _dot(p, vn_ref[...]))
        o = acc_sc[...] / _lane_tile(l_sc[...], da)
        rowh = lax.broadcasted_iota(jnp.int32, (nrow, da), 0) % heads
        colh = lax.broadcasted_iota(jnp.int32, (nrow, da), 1) // dh
        o = jnp.where(rowh == colh, o, 0.0)
        o_ref[...] = jnp.sum(o.reshape(s_new, heads, da), axis=1)


def _fox_sample(page_table_flat, kt_pages, vt_pages, lf_pages, qbd, kn, vn, dtn, *, nb, npages, s_new, heads, dh):
    pp = min(SAMPLE_PAGES_PER_STEP, npages)
    nsteps = npages // pp
    page = kt_pages.shape[2]
    da = heads * dh
    nrow = s_new * heads
    ntok = kn.shape[0]
    kern = functools.partial(_fox_sample_kernel, pp=pp, page=page, s_new=s_new, heads=heads, dh=dh)

    def page_map(i):
        return lambda b, j, pt: (pt[b * npages + (npages - 1 - (j * pp + i))], 0, 0)

    page_specs = [pl.BlockSpec((None, da, page), page_map(i)) for i in range(pp)]
    lf_specs = [pl.BlockSpec((None, heads, page), page_map(i)) for i in range(pp)]
    grid_spec = pltpu.PrefetchScalarGridSpec(
        num_scalar_prefetch=1, grid=(nb, nsteps),
        in_specs=page_specs + page_specs + lf_specs + [
            pl.BlockSpec((None, nrow, da), lambda b, j, pt: (b, 0, 0)),
            pl.BlockSpec((ntok, da), lambda b, j, pt: (0, 0)),
            pl.BlockSpec((ntok, da), lambda b, j, pt: (0, 0)),
            pl.BlockSpec((SUBLANES, ntok), lambda b, j, pt: (0, 0))],
        out_specs=pl.BlockSpec((None, s_new, da), lambda b, j, pt: (b, 0, 0)),
        scratch_shapes=[pltpu.VMEM((nrow, LANES), F32), pltpu.VMEM((nrow, LANES), F32), pltpu.VMEM((nrow, da), F32),
                        pltpu.VMEM((heads, LANES), F32)])
    return pl.pallas_call(
        kern, grid_spec=grid_spec,
        out_shape=jax.ShapeDtypeStruct((nb, s_new, da), F32),
        compiler_params=_cparams(("parallel", "arbitrary")),
        name="fox_sample")(page_table_flat, *([kt_pages] * pp), *([vt_pages] * pp), *([lf_pages] * pp),
                           qbd, kn, vn, dtn)


def _fox_fused_kernel(pt_ref, qa_ref, ka_ref, v_ref, kt_hbm, vt_hbm, lf_hbm, qbd_ref, kn_ref, vn_ref, dtn_ref,
                      o_ref, os_ref,
                      m_sc, l_sc, acc_sc, ms_sc, ls_sc, as_sc, run_sc, kbuf, vbuf, lbuf, sem, g_ref,
                      *, tq, tk, dh, pp, ring, npages, nchunks, s_new, heads):
    first_step = (pl.program_id(0) == 0) & (pl.program_id(1) == 0) & (pl.program_id(2) == 0)
    qi = pl.program_id(2)
    page = kbuf.shape[3]
    cpb = npages // pp
    nrow = s_new * heads
    da = heads * dh
    nbs = qbd_ref.shape[0]

    def chunk_copies(c):
        slot = c % ring
        bs = c // cpb
        jc = c % cpb
        copies = []
        for i in range(pp):
            pid = pt_ref[bs * npages + (npages - 1 - (jc * pp + i))]
            copies.append(pltpu.make_async_copy(kt_hbm.at[pid], kbuf.at[slot, i], sem.at[slot]))
            copies.append(pltpu.make_async_copy(vt_hbm.at[pid], vbuf.at[slot, i], sem.at[slot]))
            copies.append(pltpu.make_async_copy(lf_hbm.at[pid], lbuf.at[slot, i], sem.at[slot]))
        return copies

    @pl.when(first_step)
    def _():
        g_ref[0] = 0
        for c in range(min(ring - 1, nchunks)):
            for cp in chunk_copies(c):
                cp.start()

    lane = lax.broadcasted_iota(jnp.int32, (1, 2 * LANES), 1)
    ext = lane - LANES
    qf = qa_ref[...].astype(F32)
    halves = []
    for h in range(2):
        keep = (((lane >= h * dh) & (lane < (h + 1) * dh))
                | ((ext >= 3 * h) & (ext < 3 * h + 3)) | ((ext >= 6 + 3 * h) & (ext < 9 + 3 * h)))
        halves.append(jnp.where(keep, qf, 0.0))
    qs = jnp.concatenate(halves, axis=0).astype(BF16)
    m_sc[...] = jnp.full_like(m_sc, NEG_BIG)
    l_sc[...] = jnp.zeros_like(l_sc)
    acc_sc[...] = jnp.zeros_like(acc_sc)
    nfull = (qi * tq) // tk

    def prompt_step(j, masked):
        ks = pl.multiple_of(j * tk, tk)
        kb = ka_ref[pl.ds(ks, tk), :]
        vb = v_ref[pl.ds(ks, tk), :]
        s = _dot_nt(qs, kb)
        if masked:
            rloc = lax.broadcasted_iota(jnp.int32, (2 * tq, tk), 0)
            rloc = jnp.where(rloc >= tq, rloc - tq, rloc) + qi * tq
            cloc = lax.broadcasted_iota(jnp.int32, (2 * tq, tk), 1) + ks
            s = jnp.where(cloc <= rloc, s, NEG_BIG)
        m_prev = m_sc[...]
        m_new = jnp.maximum(m_prev, jnp.max(s, axis=1, keepdims=True))
        p = jnp.exp(s - _lane_tile(m_new, tk))
        alpha = jnp.exp(m_prev - m_new)
        l_sc[...] = alpha * l_sc[...] + jnp.sum(p, axis=1, keepdims=True)
        acc_sc[...] = alpha * acc_sc[...] + _dot(p.astype(BF16), vb)
        m_sc[...] = m_new

    def sample_update(s, pv_fn):
        m_prev = ms_sc[...]
        m_new = jnp.maximum(m_prev, jnp.max(s, axis=1, keepdims=True))
        p = jnp.exp(s - _lane_tile(m_new, s.shape[1]))
        alpha = jnp.exp(m_prev - m_new)
        ls_sc[...] = alpha * ls_sc[...] + jnp.sum(p, axis=1, keepdims=True)
        as_sc[...] = _lane_tile(alpha, da) * as_sc[...] + pv_fn(p.astype(BF16))
        ms_sc[...] = m_new

    def sample_chunk(g, valid):
        slot = g % ring
        bs = jnp.minimum(g // cpb, nbs - 1)
        qbd = qbd_ref[bs]
        rr = lax.broadcasted_iota(jnp.int32, (page, 2 * page), 0)
        cc = lax.broadcasted_iota(jnp.int32, (page, 2 * page), 1)
        after = jnp.where((rr > cc) | (cc >= page), 1.0, 0.0).astype(BF16)
        lf = jnp.concatenate([lbuf[slot, i] for i in range(pp)], axis=0)
        n8 = pp * heads
        r3 = _dot(jnp.concatenate(_split3(lf), axis=0), after)
        both = r3[0:n8] + (r3[n8:2 * n8] + r3[2 * n8:3 * n8])
        run = run_sc[...]
        scores = []
        for i in range(pp):
            rev = both[i * heads:(i + 1) * heads, 0:page] + run
            run = run + both[i * heads:(i + 1) * heads, page:2 * page]
            scores.append(_dot(qbd, kbuf[slot, i].astype(BF16)) + jnp.concatenate([rev] * s_new, axis=0))
        run_sc[...] = run
        s = jnp.where(valid, jnp.concatenate(scores, axis=1), NEG_BIG)

        def pv_pages(p):
            acc = None
            for i in range(pp):
                term = _dot_nt(p[:, i * page:(i + 1) * page], vbuf[slot, i].astype(BF16))
                acc = term if acc is None else acc + term
            return acc

        sample_update(s, pv_pages)

    def sample_finish(bs):
        ntok = kn_ref.shape[0]
        qbd = qbd_ref[bs]
        dtn = dtn_ref[...]
        lane_t = lax.broadcasted_iota(jnp.int32, (1, ntok), 1)
        dq_rows = [jnp.sum(jnp.where(lane_t == bs * s_new + t, dtn, 0.0), axis=1, keepdims=True)
                   for t in range(s_new)]
        dq = jnp.concatenate(dq_rows, axis=0)
        dk = jnp.concatenate([dtn] * s_new, axis=0)
        s = _dot_nt(qbd, kn_ref[...]) + (dq - dk)
        rowt = lax.broadcasted_iota(jnp.int32, (nrow, ntok), 0) // heads
        col = lax.broadcasted_iota(jnp.int32, (nrow, ntok), 1)
        keep = ((col // s_new) == bs) & ((col % s_new) <= rowt)
        sample_update(jnp.where(keep, s, NEG_BIG), lambda p: _dot(p, vn_ref[...]))
        o = as_sc[...] / _lane_tile(ls_sc[...], da)
        rowh = lax.broadcasted_iota(jnp.int32, (nrow, da), 0) % heads
        colh = lax.broadcasted_iota(jnp.int32, (nrow, da), 1) // dh
        o = jnp.where(rowh == colh, o, 0.0)
        os_ref[bs] = jnp.sum(o.reshape(s_new, heads, da), axis=1)

    def iteration(j, masked):
        g = g_ref[0]
        valid = g < nchunks

        @pl.when(g + (ring - 1) < nchunks)
        def _():
            for cp in chunk_copies(g + (ring - 1)):
                cp.start()

        @pl.when(valid)
        def _():
            for cp in chunk_copies(g):
                cp.wait()

        @pl.when(valid & (g % cpb == 0))
        def _():
            ms_sc[...] = jnp.full_like(ms_sc, NEG_BIG)
            ls_sc[...] = jnp.zeros_like(ls_sc)
            as_sc[...] = jnp.zeros_like(as_sc)
            run_sc[...] = jnp.zeros_like(run_sc)

        prompt_step(j, masked)
        sample_chunk(g, valid)

        @pl.when(valid & (g % cpb == cpb - 1))
        def _():
            sample_finish(g // cpb)

        g_ref[0] = g + 1

    def body(j, c):
        iteration(j, False)
        return c

    lax.fori_loop(0, nfull, body, 0)
    iteration(nfull, True)
    o = acc_sc[...] / l_sc[...]
    lane_o = lax.broadcasted_iota(jnp.int32, (1, LANES), 1)
    o_ref[...] = jnp.where(lane_o < dh, o[0:tq], o[tq:2 * tq]).astype(o_ref.dtype)


def _fox_fused(page_table_flat, qa, ka, v, kt_pages, vt_pages, lf_pages, qbd, kn, vn, dtn,
               *, nseq, seq_len, dh, npages, s_new, heads):
    t, da = v.shape
    nbs, nrow, _ = qbd.shape
    page = kt_pages.shape[2]
    ntok = kn.shape[0]
    tk = min(FOX_KEY_BLOCK, seq_len)
    tq = min(FOX_QUERY_BLOCK, tk)
    nq = seq_len // tq
    npair = da // LANES
    pp = min(FUSED_PAGES_PER_ITERATION, npages)
    nchunks = nbs * (npages // pp)
    iters = nseq * npair * sum((qi * tq) // tk + 1 for qi in range(nq))
    assert npages % pp == 0 and nchunks <= iters, "the page stream must fit in the prompt attention's iterations"
    kern = functools.partial(_fox_fused_kernel, tq=tq, tk=tk, dh=dh, pp=pp, ring=FUSED_RING, npages=npages,
                             nchunks=nchunks, s_new=s_new, heads=heads)
    const = lambda shape: pl.BlockSpec(shape, lambda b, hp, qi, pt: (0,) * len(shape))
    grid_spec = pltpu.PrefetchScalarGridSpec(
        num_scalar_prefetch=1, grid=(nseq, npair, nq),
        in_specs=[pl.BlockSpec((tq, 2 * LANES), lambda b, hp, qi, pt: (b * nq + qi, hp)),
                  pl.BlockSpec((seq_len, 2 * LANES), lambda b, hp, qi, pt: (b, hp)),
                  pl.BlockSpec((seq_len, LANES), lambda b, hp, qi, pt: (b, hp)),
                  pl.BlockSpec(memory_space=pl.ANY), pl.BlockSpec(memory_space=pl.ANY),
                  pl.BlockSpec(memory_space=pl.ANY),
                  const((nbs, nrow, da)), const((ntok, da)), const((ntok, da)), const((SUBLANES, ntok))],
        out_specs=[pl.BlockSpec((tq, LANES), lambda b, hp, qi, pt: (b * nq + qi, hp)),
                   const((nbs, s_new, da))],
        scratch_shapes=[pltpu.VMEM((2 * tq, LANES), F32), pltpu.VMEM((2 * tq, LANES), F32),
                        pltpu.VMEM((2 * tq, LANES), F32),
                        pltpu.VMEM((nrow, LANES), F32), pltpu.VMEM((nrow, LANES), F32), pltpu.VMEM((nrow, da), F32),
                        pltpu.VMEM((heads, LANES), F32),
                        pltpu.VMEM((FUSED_RING, pp, da, page), F32), pltpu.VMEM((FUSED_RING, pp, da, page), F32),
                        pltpu.VMEM((FUSED_RING, pp, heads, page), F32),
                        pltpu.SemaphoreType.DMA((FUSED_RING,)), pltpu.SMEM((1,), jnp.int32)])
    return pl.pallas_call(
        kern, grid_spec=grid_spec,
        out_shape=[jax.ShapeDtypeStruct((t, da), BF16), jax.ShapeDtypeStruct((nbs, s_new, da), F32)],
        compiler_params=_cparams(("arbitrary", "arbitrary", "arbitrary")),
        name="fox_fused")(page_table_flat, qa, ka, v, kt_pages, vt_pages, lf_pages, qbd, kn, vn, dtn)


def _post_attn_kernel(x_ref, gc_ref, at_ref, g_ref, wuc_ref, wua_ref, wmo_ref, gain_ref, wxq_ref, h_ref, qx_ref):
    dm = x_ref.shape[1]
    y_conv = _dot(gc_ref[...], wuc_ref[...])
    y_attn = _dot(at_ref[...].astype(BF16), wua_ref[...])
    g_conv = g_ref[:, 0:dm].astype(F32)
    g_attn = g_ref[:, dm:2 * dm].astype(F32)
    mixed = jax.nn.sigmoid(g_conv) * y_conv + jax.nn.sigmoid(g_attn) * y_attn
    h = x_ref[...] + _dot(mixed.astype(BF16), wmo_ref[...])
    h_ref[...] = h
    xn = _rms(h, gain_ref[...]).astype(BF16)
    qx_ref[...] = _dot(xn, wxq_ref[...]).astype(qx_ref.dtype)


def _post_attn(x, gc, at, g, wuc, wua, wmo, gain, wxq, *, tm, name):
    t, dm = x.shape
    dx = wxq.shape[1]
    rows = lambda w: pl.BlockSpec((tm, w), lambda i: (i, 0))
    return pl.pallas_call(
        _post_attn_kernel, grid=(t // tm,),
        in_specs=[rows(dm), rows(gc.shape[1]), rows(at.shape[1]), rows(g.shape[1]), _const_spec(wuc.shape),
                  _const_spec(wua.shape), _const_spec(wmo.shape), _const_spec((1, dm)), _const_spec(wxq.shape)],
        out_specs=[rows(dm), rows(dx)],
        out_shape=[jax.ShapeDtypeStruct((t, dm), F32), jax.ShapeDtypeStruct((t, dx), BF16)],
        compiler_params=_cparams(("parallel",)), name=name)(x, gc, at, g, wuc, wua, wmo, gain, wxq)


def _memkv_kernel(m_ref, gain_ref, wk_ref, wv_ref, k_ref, v_ref, *, xh):
    mn = _rms(m_ref[...], gain_ref[...]).astype(BF16)
    tm = m_ref.shape[0]
    xd = wk_ref.shape[1] // xh
    k = _dot(mn, wk_ref[...])
    v = _dot(mn, wv_ref[...])
    for h in range(xh):
        k_ref[pl.ds(h, tm, stride=xh), :] = k[:, h * xd:(h + 1) * xd]
        v_ref[pl.ds(h, tm, stride=xh), :] = v[:, h * xd:(h + 1) * xd]


def _memkv(mem, gain, wk, wv, *, xh):
    t, dm = mem.shape
    tm = min(512, t)
    xd = wk.shape[1] // xh
    rows = lambda w: pl.BlockSpec((tm, w), lambda i: (i, 0))
    return pl.pallas_call(
        functools.partial(_memkv_kernel, xh=xh), grid=(t // tm,),
        in_specs=[rows(dm), _const_spec((1, dm)), _const_spec(wk.shape), _const_spec(wv.shape)],
        out_specs=[pl.BlockSpec((tm * xh, xd), lambda i: (i, 0))] * 2,
        out_shape=[jax.ShapeDtypeStruct((t * xh, xd), F32)] * 2,
        compiler_params=_cparams(("parallel",)), name="memkv")(mem, gain, wk, wv)


def _xattn_kernel(q_ref, mk_ref, mv_ref, o_ref, *, xh, xd):
    nm = mk_ref.shape[1] // xh
    scale = xd ** -0.5
    for b in range(q_ref.shape[0]):
        q = q_ref[b].astype(BF16)
        outs = []
        for h in range(xh):
            mk = mk_ref[b, pl.ds(h, nm, stride=xh), :].astype(BF16)
            mv = mv_ref[b, pl.ds(h, nm, stride=xh), :].astype(BF16)
            s = _dot_nt(q[:, h * xd:(h + 1) * xd], mk) * scale
            m = jnp.max(s, axis=1, keepdims=True)
            p = jnp.exp(s - m)
            p = p / jnp.sum(p, axis=1, keepdims=True)
            outs.append(_dot(p.astype(BF16), mv))
        o_ref[b] = jnp.concatenate(outs, axis=1).astype(o_ref.dtype)


def _xattn(q, mk, mv, *, tq, xh, nbb, name):
    nb, s, dx = q.shape
    nm, xd = mk.shape[1:]
    kern = functools.partial(_xattn_kernel, xh=xh, xd=xd)
    return pl.pallas_call(
        kern, grid=(nb // nbb, s // tq),
        in_specs=[pl.BlockSpec((nbb, tq, dx), lambda b, i: (b, i, 0)),
                  pl.BlockSpec((nbb, nm, xd), lambda b, i: (b, 0, 0)),
                  pl.BlockSpec((nbb, nm, xd), lambda b, i: (b, 0, 0))],
        out_specs=pl.BlockSpec((nbb, tq, dx), lambda b, i: (b, i, 0)),
        out_shape=jax.ShapeDtypeStruct((nb, s, dx), q.dtype),
        compiler_params=_cparams(("parallel", "parallel")), name=name)(q, mk, mv)


def _pre_moe_kernel(h_ref, o_ref, wxo_ref, gain_ref, wrh_ref, wrl_ref, cnt_in_ref,
                    h2_ref, xn_ref, rt_ref, rr_ref, cnt_out_ref, base, *, tm, ngroups, epg):
    i = pl.program_id(0)

    @pl.when(i == 0)
    def _():
        base[...] = cnt_in_ref[...]

    h2 = h_ref[...] + _dot(o_ref[...].astype(BF16), wxo_ref[...])
    h2_ref[...] = h2
    xn = _rms(h2, gain_ref[...])
    nc = xn.shape[1] // LANES
    for c in range(nc):
        xn_ref[pl.ds(c, tm, stride=nc), :] = xn[:, c * LANES:(c + 1) * LANES]

    xh = xn.astype(BF16)
    xl = (xn - xh.astype(F32)).astype(BF16)
    lt = _dot_nt(wrh_ref[...], xh) + (_dot_nt(wrh_ref[...], xl) + _dot_nt(wrl_ref[...], xh))

    sub = lax.broadcasted_iota(jnp.int32, (SUBLANES, tm), 0)
    gl = jnp.where(sub < ngroups, lt[0:SUBLANES, :], -jnp.inf)
    gmax = jnp.max(gl, axis=0, keepdims=True)
    gidx = jnp.min(jnp.where(gl == gmax, sub, SUBLANES), axis=0, keepdims=True)
    pg = 1.0 / jnp.sum(jnp.exp(gl - gmax), axis=0, keepdims=True)
    el = jnp.zeros((epg, tm), F32)
    for g in range(ngroups):
        el = jnp.where(gidx == g, lt[EXPERT_ROW0 + g * epg:EXPERT_ROW0 + (g + 1) * epg, :], el)
    v1 = jnp.max(el, axis=0, keepdims=True)
    i1 = jnp.min(jnp.where(el == v1, sub, epg), axis=0, keepdims=True)
    el2 = jnp.where(sub == i1, -jnp.inf, el)
    v2 = jnp.max(el2, axis=0, keepdims=True)
    i2 = jnp.min(jnp.where(el2 == v2, sub, epg), axis=0, keepdims=True)
    t2 = jnp.exp(v2 - v1)
    den = 1.0 + t2
    w0 = (1.0 / den) * pg
    w1 = (t2 / den) * pg
    e0 = gidx * epg + i1
    e1 = gidx * epg + i2

    erow = lax.broadcasted_iota(jnp.int32, (LANES, tm), 0)
    oh0 = erow == e0
    oh1 = erow == e1
    rr = lax.broadcasted_iota(jnp.int32, (tm, tm), 0)
    cc = lax.broadcasted_iota(jnp.int32, (tm, tm), 1)
    triu = jnp.where(rr <= cc, 1.0, 0.0).astype(BF16)
    pre0 = _dot(jnp.where(oh0, 1.0, 0.0).astype(BF16), triu)
    pre1 = _dot(jnp.where(oh1, 1.0, 0.0).astype(BF16), triu)
    b0 = base[:, 0:1]
    tot0 = pre0[:, tm - 1:tm]
    tot1 = pre1[:, tm - 1:tm]
    rank0 = jnp.sum(jnp.where(oh0, pre0 - 1.0 + b0, 0.0), axis=0, keepdims=True)
    rank1 = jnp.sum(jnp.where(oh1, pre1 - 1.0 + (b0 + tot0), 0.0), axis=0, keepdims=True)
    newb = b0 + tot0 + tot1
    base[...] = jnp.broadcast_to(newb, base.shape)
    cnt_out_ref[...] = jnp.broadcast_to(newb, cnt_out_ref.shape)

    zero = jnp.zeros((1, tm), F32)
    rt = jnp.concatenate([e0.astype(F32), e1.astype(F32), w0, w1, rank0, rank1, zero, zero], axis=0)
    rt_ref[...] = rt
    rt_pad = jnp.concatenate([rt, jnp.zeros((LANES - ROUTE_ROWS, tm), F32)], axis=0)
    rr_ref[...] = jnp.transpose(rt_pad)


def _pre_moe(h, o, wxo, gain, wrh, wrl, cnt_in, *, tm, ngroups, epg, name):
    t, dm = h.shape
    dx = o.shape[1]
    kern = functools.partial(_pre_moe_kernel, tm=tm, ngroups=ngroups, epg=epg)
    rows = lambda w: pl.BlockSpec((tm, w), lambda i: (i, 0))
    return pl.pallas_call(
        kern, grid=(t // tm,),
        in_specs=[rows(dm), rows(dx), _const_spec(wxo.shape), _const_spec((1, dm)), _const_spec(wrh.shape),
                  _const_spec(wrl.shape), _const_spec((LANES, LANES))],
        out_specs=[rows(dm), pl.BlockSpec((tm * (dm // LANES), LANES), lambda i: (i, 0)),
                   pl.BlockSpec((ROUTE_ROWS, tm), lambda i: (0, i)), rows(LANES), _const_spec((LANES, LANES))],
        out_shape=[jax.ShapeDtypeStruct((t, dm), F32), jax.ShapeDtypeStruct((t * (dm // LANES), LANES), F32),
                   jax.ShapeDtypeStruct((ROUTE_ROWS, t), F32), jax.ShapeDtypeStruct((t, LANES), F32),
                   jax.ShapeDtypeStruct((LANES, LANES), F32)],
        scratch_shapes=[pltpu.VMEM((LANES, LANES), F32)],
        compiler_params=_cparams(("arbitrary",)), name=name)(h, o, wxo, gain, wrh, wrl, cnt_in)


def _row_copy(src, r_src, dst, r_dst, sem, nc):
    return pltpu.make_async_copy(src.at[pl.ds(pl.multiple_of(r_src * nc, nc), nc)],
                                 dst.at[pl.ds(pl.multiple_of(r_dst * nc, nc), nc)], sem)


def _slab_rows(ref, nc):
    rows = ref.shape[0] // nc
    return jnp.concatenate([ref[pl.ds(c, rows, stride=nc), :] for c in range(nc)], axis=1)


def _scatter_kernel(e_ref, rank_ref, pstart_ref, pend_ref, xp_ref, xs_ref, out_ref, zeros, sem, zsem,
                    *, tm_p, nblk_p, nexp, nc):
    i = pl.program_id(0)
    tall = rank_ref.shape[0] // TOP_K

    @pl.when(i == 0)
    def _():
        zeros[...] = jnp.zeros_like(zeros)

        def zero_block(blk):
            start = pl.multiple_of(blk * (EXPERT_ROWS * nc), EXPERT_ROWS * nc)
            return pltpu.make_async_copy(zeros, out_ref.at[pl.ds(start, EXPERT_ROWS * nc)], zsem)

        for e in range(nexp):
            @pl.when(pend_ref[e] > pstart_ref[e])
            def _():
                zero_block(pend_ref[e] // EXPERT_ROWS - 1).start()
        for e in range(nexp):
            @pl.when(pend_ref[e] > pstart_ref[e])
            def _():
                zero_block(pend_ref[e] // EXPERT_ROWS - 1).wait()
        nblk = out_ref.shape[0] // (EXPERT_ROWS * nc)
        nused = pend_ref[nexp - 1] // EXPERT_ROWS

        def start_unused(blk, c):
            zero_block(blk).start()
            return c

        def wait_unused(blk, c):
            zero_block(blk).wait()
            return c

        lax.fori_loop(nused, nblk, start_unused, 0)
        lax.fori_loop(nused, nblk, wait_unused, 0)

    def copy_rows(x_ref, base):
        tm = x_ref.shape[0] // nc

        def issue(r, c):
            for k in range(TOP_K):
                a = k * tall + base + r
                dest = pstart_ref[e_ref[a]] + rank_ref[a]
                _row_copy(x_ref, r, out_ref, dest, sem, nc).start(priority=k % 2)
            return c

        lax.fori_loop(0, tm, issue, 0, unroll=8)
        for k in range(TOP_K):
            pltpu.make_async_copy(x_ref, x_ref, sem).wait()

    @pl.when(i < nblk_p)
    def _():
        copy_rows(xp_ref, i * tm_p)

    @pl.when(i == nblk_p)
    def _():
        copy_rows(xs_ref, nblk_p * tm_p)


def _scatter(e_flat, rank_flat, pstart, pend, x_p, x_s, *, n_rows, tm_p, nexp, dm):
    nc = dm // LANES
    tp = x_p.shape[0] // nc
    ts = x_s.shape[0] // nc
    nblk_p = tp // tm_p
    kern = functools.partial(_scatter_kernel, tm_p=tm_p, nblk_p=nblk_p, nexp=nexp, nc=nc)
    grid_spec = pltpu.PrefetchScalarGridSpec(
        num_scalar_prefetch=4, grid=(nblk_p + 1,),
        in_specs=[pl.BlockSpec((tm_p * nc, LANES), lambda i, *_: (jnp.minimum(i, nblk_p - 1), 0)),
                  pl.BlockSpec((ts * nc, LANES), lambda i, *_: (0, 0))],
        out_specs=pl.BlockSpec(memory_space=pl.ANY),
        scratch_shapes=[pltpu.VMEM((EXPERT_ROWS * nc, LANES), F32), pltpu.SemaphoreType.DMA(()),
                        pltpu.SemaphoreType.DMA(())])
    return pl.pallas_call(
        kern, grid_spec=grid_spec, out_shape=jax.ShapeDtypeStruct((n_rows * nc, LANES), F32),
        compiler_params=_cparams(("arbitrary",)), name="scatter")(e_flat, rank_flat, pstart, pend, x_p, x_s)


def _experts_kernel(blk_e_ref, nused_ref, xs_ref, wg_ref, wu_ref, wd_ref, y_ref, wgb, wub, wdb, *, nc):
    i = pl.program_id(0)
    prev = blk_e_ref[jnp.maximum(i - 1, 0)]
    fresh = (i == 0) | (blk_e_ref[i] != prev)

    @pl.when(i < nused_ref[0])
    def _():
        @pl.when(fresh)
        def _():
            wgb[...] = wg_ref[...].astype(BF16)
            wub[...] = wu_ref[...].astype(BF16)
            wdb[...] = wd_ref[...].astype(BF16)
        x = _slab_rows(xs_ref, nc).astype(BF16)
        a = _dot(x, wgb[...])
        u = _dot(x, wub[...])
        hmid = (a * jax.nn.sigmoid(a)) * u
        y = _dot(hmid.astype(BF16), wdb[...])
        for c in range(nc):
            y_ref[pl.ds(c, EXPERT_ROWS, stride=nc), :] = y[:, c * LANES:(c + 1) * LANES]

    @pl.when(i >= nused_ref[0])
    def _():
        y_ref[...] = jnp.zeros_like(y_ref)


def _experts(blk_e, nused, xs, wg, wu, wd):
    dm, de = wg.shape[1:]
    nc = dm // LANES
    nblk = xs.shape[0] // (EXPERT_ROWS * nc)

    def row_map(i, be, nu):
        return (jnp.minimum(i, nu[0] - 1), 0)

    grid_spec = pltpu.PrefetchScalarGridSpec(
        num_scalar_prefetch=2, grid=(nblk,),
        in_specs=[pl.BlockSpec((EXPERT_ROWS * nc, LANES), row_map),
                  pl.BlockSpec((None, dm, de), lambda i, be, nu: (be[i], 0, 0)),
                  pl.BlockSpec((None, dm, de), lambda i, be, nu: (be[i], 0, 0)),
                  pl.BlockSpec((None, de, dm), lambda i, be, nu: (be[i], 0, 0))],
        out_specs=pl.BlockSpec((EXPERT_ROWS * nc, LANES), lambda i, be, nu: (i, 0)),
        scratch_shapes=[pltpu.VMEM((dm, de), BF16), pltpu.VMEM((dm, de), BF16), pltpu.VMEM((de, dm), BF16)])
    return pl.pallas_call(
        functools.partial(_experts_kernel, nc=nc), grid_spec=grid_spec, out_shape=jax.ShapeDtypeStruct(xs.shape, F32),
        compiler_params=_cparams(("arbitrary",)), name="experts")(blk_e, nused, xs, wg, wu, wd)


def _combine_kernel(e_ref, rank_ref, pstart_ref, h_ref, rr_ref, gain_ref, y_hbm, o_ref, buf, sem, *, tm, tok0):
    i = pl.program_id(0)
    nc = h_ref.shape[1] // LANES
    base = tok0 + i * tm

    def issue(r, c):
        for k in range(TOP_K):
            a = k * rank_ref.shape[0] // TOP_K + base + r
            dest = pstart_ref[e_ref[a]] + rank_ref[a]
            _row_copy(y_hbm, dest, buf.at[k], r, sem, nc).start(priority=k % 2)
        return c

    lax.fori_loop(0, tm, issue, 0, unroll=8)
    for k in range(TOP_K):
        pltpu.make_async_copy(buf.at[k], buf.at[k], sem).wait()
    rr = rr_ref[...]
    h = h_ref[...] + (rr[:, 2:3] * _slab_rows(buf.at[0], nc) + rr[:, 3:4] * _slab_rows(buf.at[1], nc))
    o_ref[...] = _rms(h, gain_ref[...])


def _combine(e_flat, rank_flat, pstart, h, rr, gain, y, *, tm, tok0, name):
    t, dm = h.shape
    kern = functools.partial(_combine_kernel, tm=tm, tok0=tok0)
    grid_spec = pltpu.PrefetchScalarGridSpec(
        num_scalar_prefetch=3, grid=(t // tm,),
        in_specs=[pl.BlockSpec((tm, dm), lambda i, *_: (i, 0)), pl.BlockSpec((tm, LANES), lambda i, *_: (i, 0)),
                  pl.BlockSpec((1, dm), lambda i, *_: (0, 0)), pl.BlockSpec(memory_space=pl.ANY)],
        out_specs=pl.BlockSpec((tm, dm), lambda i, *_: (i, 0)),
        scratch_shapes=[pltpu.VMEM((TOP_K, tm * (dm // LANES), LANES), F32), pltpu.SemaphoreType.DMA(())])
    return pl.pallas_call(
        kern, grid_spec=grid_spec, out_shape=jax.ShapeDtypeStruct((t, dm), F32),
        compiler_params=_cparams(("arbitrary",)), name=name)(e_flat, rank_flat, pstart, h, rr, gain, y)


def kernel(x_prompt, x_sample, cache_k, cache_v, cache_logf, cache_mem_k, cache_mem_v, state_conv, page_table,
           mem_prompt, norm_mix, w_in, b_forget, conv_w, conv_b, w_up_conv, w_up_attn, w_mix_out, norm_xattn,
           norm_mem, w_xq, w_xk, w_xv, w_xo, norm_ffn, w_router_group, w_router_expert, w_expert_gate,
           w_expert_up, w_expert_down, norm_final):
    depth = w_in.shape[0]
    assert depth == 1, "single-layer trunk"
    nbp, seq, dm = x_prompt.shape
    nbs, s_new, _ = x_sample.shape
    _, n_pool, page, heads, dh = cache_k.shape
    npages = page_table.shape[1]
    nmem, xh, xd = cache_mem_k.shape[2:]
    dc = conv_w.shape[2]
    da = heads * dh
    dx = xh * xd
    ngroups, _, epg = w_router_expert.shape[1:]
    nexp = ngroups * epg
    tp = nbp * seq
    ts = nbs * s_new
    assert conv_w.shape[1] == 3 and s_new >= 2 and dh == 64 and heads == SUBLANES and epg == SUBLANES
    assert page == LANES

    l = 0
    wi = w_in[l]
    wc = wi[:, 0:3 * dc].astype(BF16)
    wqkv = wi[:, 3 * dc:3 * dc + 3 * da].astype(BF16)
    o_f = 3 * dc + 3 * da
    wf = jnp.pad(wi[:, o_f:o_f + heads], ((0, 0), (0, LANES - heads)))
    wfh = wf.astype(BF16)
    wf2 = jnp.concatenate([wfh, (wf - wfh.astype(F32)).astype(BF16)], axis=1)
    wg = wi[:, o_f + heads:].astype(BF16)
    bfp = jnp.pad(b_forget[l][None, :], ((0, 0), (0, LANES - heads)))
    cw = conv_w[l]
    cbias = conv_b[l][None, :]
    g_mix = norm_mix[l][None, :]
    wuc = w_up_conv[l].astype(BF16)
    wua = w_up_attn[l].astype(BF16)
    wmo = w_mix_out[l].astype(BF16)
    g_x = norm_xattn[l][None, :]
    wxq = w_xq[l].astype(BF16)
    wxo = w_xo[l].astype(BF16)
    g_f = norm_ffn[l][None, :]
    wr = jnp.zeros((LANES, dm), F32)
    wr = wr.at[0:ngroups].set(w_router_group[l].T)
    wr = wr.at[EXPERT_ROW0:EXPERT_ROW0 + nexp].set(jnp.transpose(w_router_expert[l], (0, 2, 1)).reshape(nexp, dm))
    wrh = wr.astype(BF16)
    wrl = (wr - wrh.astype(F32)).astype(BF16)

    xp = x_prompt.reshape(tp, dm)
    (gc_p, qa_p, ka_p, vb_p, kt_p, vt_p, g_p, lf_p, ulast_p) = _inproj(
        xp, g_mix, wc, wqkv, wg, wf2, bfp, cw, cbias, seq_len=seq, sample=False, aug=_decay_columns(heads))
    xs_ = x_sample.reshape(ts, dm)
    st = state_conv[l]
    zeros_row = jnp.zeros((nbs, 1, dc), F32)
    fix1 = jnp.concatenate([st[:, 1:2], jnp.tile(zeros_row, (1, s_new - 1, 1))], axis=1).reshape(ts, dc)
    fix2 = jnp.concatenate([st[:, 0:1], st[:, 1:2], jnp.tile(zeros_row, (1, s_new - 2, 1))], axis=1).reshape(ts, dc)
    (gc_s, q_s, kb_s, vb_s, kf_s, vf_s, g_s, lf_s, dt_s, u_s) = _inproj(
        xs_, g_mix, wc, wqkv, wg, wf2, bfp, cw, cbias, seq_len=s_new, sample=True, fix=(fix1, fix2))
    pt_flat = page_table.reshape(-1).astype(jnp.int32)
    head_of_col = jnp.arange(da) // dh
    qbd = jnp.where(head_of_col[None, None, None, :] == jnp.arange(heads)[None, None, :, None],
                    q_s.reshape(nbs, s_new, 1, da), jnp.zeros((), BF16)).reshape(nbs, s_new * heads, da)
    kt_pages = jnp.transpose(cache_k[l], (0, 2, 3, 1)).reshape(n_pool, da, page)
    vt_pages = jnp.transpose(cache_v[l], (0, 2, 3, 1)).reshape(n_pool, da, page)
    lf_pages = jnp.swapaxes(cache_logf[l], 1, 2)
    at_p, at_s = _fox_fused(pt_flat, qa_p, ka_p, vb_p, kt_pages, vt_pages, lf_pages, qbd, kb_s, vb_s, dt_s,
                            nseq=nbp, seq_len=seq, dh=dh, npages=npages, s_new=s_new, heads=heads)

    tm_p = min(512, seq)
    h1_p, qx_p = _post_attn(xp, gc_p, at_p, g_p, wuc, wua, wmo, g_x, wxq, tm=tm_p, name="post_attn_prompt")
    mk_p, mv_p = _memkv(mem_prompt.reshape(nbp * nmem, dm), norm_mem[l][None, :], w_xk[l].astype(BF16),
                        w_xv[l].astype(BF16), xh=xh)
    o_p = _xattn(qx_p.reshape(nbp, seq, dx), mk_p.reshape(nbp, nmem * xh, xd), mv_p.reshape(nbp, nmem * xh, xd),
                 tq=tm_p, xh=xh, nbb=1, name="xattn_prompt").reshape(tp, dx)
    cnt0 = jnp.zeros((LANES, LANES), F32)
    h2_p, xn_p, rt_p, rr_p, cnt1 = _pre_moe(h1_p, o_p, wxo, g_f, wrh, wrl, cnt0, tm=tm_p, ngroups=ngroups, epg=epg,
                                            name="pre_moe_prompt")

    h1_s, qx_s = _post_attn(xs_, gc_s, at_s.reshape(ts, da), g_s, wuc, wua, wmo, g_x, wxq, tm=ts,
                            name="post_attn_sample")
    qx_s8 = jnp.pad(qx_s.astype(F32).reshape(nbs, s_new, dx), ((0, 0), (0, SUBLANES - s_new), (0, 0)))
    o_s = _xattn(qx_s8, cache_mem_k[l].reshape(nbs, nmem * xh, xd), cache_mem_v[l].reshape(nbs, nmem * xh, xd),
                 tq=SUBLANES, xh=xh, nbb=4 if nbs % 4 == 0 else 1, name="xattn_sample")[:, :s_new].reshape(ts, dx)
    h2_s, xn_s, rt_s, rr_s, cnt2 = _pre_moe(h1_s, o_s, wxo, g_f, wrh, wrl, cnt1, tm=ts, ngroups=ngroups, epg=epg,
                                            name="pre_moe_sample")

    tall = tp + ts
    counts = cnt2[0:nexp, 0].astype(jnp.int32)
    padded = (counts + EXPERT_ROWS - 1) // EXPERT_ROWS * EXPERT_ROWS
    pend = jnp.cumsum(padded).astype(jnp.int32)
    pstart = pend - padded
    nblk = (tall * TOP_K + nexp * (EXPERT_ROWS - 1)) // EXPERT_ROWS
    n_rows = nblk * EXPERT_ROWS
    blk_row0 = jnp.arange(nblk, dtype=jnp.int32) * EXPERT_ROWS
    blk_e = jnp.minimum(jnp.sum((pend[None, :] <= blk_row0[:, None]).astype(jnp.int32), axis=1), nexp - 1)
    nused = (pend[nexp - 1:nexp] // EXPERT_ROWS).astype(jnp.int32)
    rt_all = jnp.concatenate([rt_p, rt_s], axis=1)
    e_flat = rt_all[0:TOP_K].astype(jnp.int32).reshape(-1)
    rank_flat = rt_all[4:4 + TOP_K].astype(jnp.int32).reshape(-1)
    xsg = _scatter(e_flat, rank_flat, pstart, pend, xn_p, xn_s, n_rows=n_rows, tm_p=tm_p, nexp=nexp, dm=dm)
    y = _experts(blk_e, nused, xsg, w_expert_gate[l], w_expert_up[l], w_expert_down[l])
    g_fin = norm_final[None, :]
    y_p = _combine(e_flat, rank_flat, pstart, h2_p, rr_p, g_fin, y, tm=tm_p, tok0=0, name="combine_prompt")
    y_s = _combine(e_flat, rank_flat, pstart, h2_s, rr_s, g_fin, y, tm=ts, tok0=tp, name="combine_sample")

    return (y_p.reshape(nbp, seq, dm), y_s.reshape(nbs, s_new, dm),
            ulast_p[None],
            jnp.transpose(kt_p.reshape(nbp, heads, dh, seq), (0, 3, 1, 2))[None],
            jnp.transpose(vt_p.reshape(nbp, heads, dh, seq), (0, 3, 1, 2))[None],
            jnp.transpose(lf_p, (0, 2, 1))[None],
            mk_p.reshape(1, nbp, nmem, xh, xd), mv_p.reshape(1, nbp, nmem, xh, xd),
            u_s.reshape(nbs, s_new, dc)[None, :, s_new - 2:], kf_s.reshape(1, nbs, s_new, heads, dh),
            vf_s.reshape(1, nbs, s_new, heads, dh), lf_s[:, :heads].reshape(1, nbs, s_new, heads))
```

```python
import functools

import numpy as np
import jax
import jax.numpy as jnp
from jax import lax
from jax.experimental import pallas as pl
from jax.experimental.pallas import tpu as pltpu

F32 = jnp.float32
BF16 = jnp.bfloat16

RMS_EPS = 1e-6
TOP_K = 2
LANES = 128
SUBLANES = 8
VMEM_LIMIT = 56 * 1024 * 1024
NEG_BIG = -1e30
EXPERT_ROWS = 256
SAMPLE_PAGES_PER_STEP = 32
FUSED_PAGES_PER_ITERATION = 8
FUSED_RING = 3
FOX_QUERY_BLOCK = 512
FOX_KEY_BLOCK = 512
ROUTE_ROWS = 8
EXPERT_ROW0 = 8


def _cparams(sem, vmem=VMEM_LIMIT):
    return pltpu.CompilerParams(dimension_semantics=sem, vmem_limit_bytes=vmem)


def _rms(x, g):
    ms = jnp.mean(x * x, axis=-1, keepdims=True)
    return x * lax.rsqrt(ms + RMS_EPS) * g


def _split3(x):
    hi = x.astype(BF16)
    r = x - hi.astype(F32)
    mid = r.astype(BF16)
    lo = (r - mid.astype(F32)).astype(BF16)
    return hi, mid, lo


def _dot(a, b):
    return jnp.dot(a, b, preferred_element_type=F32)


def _dot_nt(a, b):
    return lax.dot_general(a, b, (((1,), (1,)), ((), ())), preferred_element_type=F32)


def _lane_tile(x, width):
    if width % LANES == 0:
        return jnp.concatenate([x] * (width // LANES), axis=1)
    return jnp.broadcast_to(x[:, 0:1], (x.shape[0], width))


def _const_spec(shape):
    nd = len(shape)
    return pl.BlockSpec(shape, lambda *_: (0,) * nd)


def _inproj_kernel(*refs, tm, seq_blocks, sample, seq_len):
    if sample:
        (x_ref, gain_ref, wc_ref, wqkv_ref, wg_ref, wf2_ref, bf_ref, cw_ref, cb_ref, fix1_ref, fix2_ref,
         gc_ref, q_ref, kbf_ref, vbf_ref, kf_ref, vf_ref, g_ref, logf_ref, dt_ref, u_ref) = refs
    else:
        (x_ref, gain_ref, wc_ref, wqkv_ref, wg_ref, wf2_ref, bf_ref, cw_ref, cb_ref,
         selq_ref, selk_ref, oneq_ref, onek_ref,
         gc_ref, qa_ref, ka_ref, vbf_ref, kf_ref, vf_ref, g_ref, logf_ref, u_ref,
         carry_u, carry_d) = refs
    i = pl.program_id(0)
    dc = cw_ref.shape[1]
    da = wqkv_ref.shape[1] // 3

    xn = _rms(x_ref[...], gain_ref[...])
    xb = xn.astype(BF16)

    cb = _dot(xb, wc_ref[:, 0:dc])
    cc = _dot(xb, wc_ref[:, dc:2 * dc])
    cx = _dot(xb, wc_ref[:, 2 * dc:3 * dc])
    u = cc * cx
    row = lax.broadcasted_iota(jnp.int32, (tm, 1), 0)
    r1 = pltpu.roll(u, 1, axis=0)
    r2 = pltpu.roll(u, 2, axis=0)
    if sample:
        pos = row % seq_len
        p1 = jnp.where(pos == 0, fix1_ref[...], r1)
        p2 = jnp.where(pos < 2, fix2_ref[...], r2)
        u_ref[...] = u
    else:
        @pl.when(i % seq_blocks == 0)
        def _():
            carry_u[...] = jnp.zeros_like(carry_u)
            carry_d[...] = jnp.zeros_like(carry_d)
        c0 = carry_u[0:1, :]
        c1 = carry_u[1:2, :]
        p1 = jnp.where(row == 0, c1, r1)
        p2 = jnp.where(row == 0, c0, jnp.where(row == 1, c1, r2))
        carry_u[0:2, :] = u[tm - 2:tm, :]
        u_ref[...] = u[tm - 2:tm, :]
    cw = cw_ref[...]
    cy = cb_ref[...] + cw[0:1, :] * p2 + cw[1:2, :] * p1 + cw[2:3, :] * u
    gc_ref[...] = (cb * cy).astype(BF16)

    qs = (_dot(xb, wqkv_ref[:, 0:da]) * (1.0 / 8.0)).astype(BF16)
    k = _dot(xb, wqkv_ref[:, da:2 * da])
    kb = k.astype(BF16)
    v = _dot(xb, wqkv_ref[:, 2 * da:3 * da])
    vbf_ref[...] = v.astype(BF16)
    if sample:
        q_ref[...] = qs
        kbf_ref[...] = kb
        kf_ref[...] = k
        vf_ref[...] = v
    else:
        kf_ref[...] = jnp.transpose(k)
        vf_ref[...] = jnp.transpose(v)

    gw = g_ref.shape[1]
    for c in range(gw // 512):
        g_ref[:, c * 512:(c + 1) * 512] = _dot(xb, wg_ref[:, c * 512:(c + 1) * 512]).astype(BF16)

    xl = (xn - xb.astype(F32)).astype(BF16)
    hh_hl = _dot(xb, wf2_ref[...])
    fz = hh_hl[:, 0:LANES] + (hh_hl[:, LANES:2 * LANES] + _dot(xl, wf2_ref[:, 0:LANES]))
    z = fz + bf_ref[...]
    logf = jnp.minimum(z, 0.0) - jnp.log1p(jnp.exp(-jnp.abs(z)))
    lane = lax.broadcasted_iota(jnp.int32, (1, LANES), 1)
    nh = da // 64
    logf = jnp.where(lane < nh, logf, 0.0)
    if sample:
        logf_ref[...] = logf
    else:
        logf_ref[...] = jnp.transpose(logf)[0:SUBLANES, :]

    rr = lax.broadcasted_iota(jnp.int32, (tm, tm), 0)
    cc_ = lax.broadcasted_iota(jnp.int32, (tm, tm), 1)
    if sample:
        tri = (cc_ <= rr) & ((rr // seq_len) == (cc_ // seq_len))
    else:
        tri = cc_ <= rr
    tri = jnp.where(tri, 1.0, 0.0).astype(BF16)
    d3 = _dot(tri, jnp.concatenate(_split3(logf), axis=1))
    d = d3[:, 0:LANES] + (d3[:, LANES:2 * LANES] + d3[:, 2 * LANES:3 * LANES])
    if sample:
        dt_ref[...] = jnp.transpose(d)[0:SUBLANES, :]
    else:
        d = d + carry_d[0:1, :]
        carry_d[0:1, :] = d[tm - 1:tm, :]
        dcat = jnp.concatenate(_split3(d), axis=1)
        aq = (_dot(dcat, selq_ref[...]) + oneq_ref[...]).astype(BF16)
        ak = (_dot(dcat, selk_ref[...]) + onek_ref[...]).astype(BF16)
        for p in range(da // LANES):
            lo_, hi_ = p * LANES, (p + 1) * LANES
            qa_ref[:, 2 * lo_:2 * lo_ + LANES] = qs[:, lo_:hi_]
            qa_ref[:, 2 * lo_ + LANES:2 * hi_] = aq[:, lo_:hi_]
            ka_ref[:, 2 * lo_:2 * lo_ + LANES] = kb[:, lo_:hi_]
            ka_ref[:, 2 * lo_ + LANES:2 * hi_] = ak[:, lo_:hi_]


def _decay_columns(heads):
    npair = heads // 2
    selq = np.zeros((3 * LANES, npair * LANES), np.float32)
    selk = np.zeros((3 * LANES, npair * LANES), np.float32)
    oneq = np.zeros((1, npair * LANES), np.float32)
    onek = np.zeros((1, npair * LANES), np.float32)
    for p in range(npair):
        for hh in range(2):
            for term in range(3):
                selq[term * LANES + 2 * p + hh, p * LANES + 3 * hh + term] = 1.0
                selk[term * LANES + 2 * p + hh, p * LANES + 6 + 3 * hh + term] = -1.0
                oneq[0, p * LANES + 6 + 3 * hh + term] = 1.0
                onek[0, p * LANES + 3 * hh + term] = 1.0
    return jnp.asarray(selq, BF16), jnp.asarray(selk, BF16), jnp.asarray(oneq), jnp.asarray(onek)


def _inproj(x, gain, wc, wqkv, wg, wf2, bfp, cw, cbias, *, seq_len, sample, fix=None, aug=None):
    t, dm = x.shape
    dc = cw.shape[1]
    da = wqkv.shape[1] // 3
    gw = wg.shape[1]
    if sample:
        tm = t
        seq_blocks = 1
    else:
        tm = min(512, seq_len)
        seq_blocks = seq_len // tm
    nblk = t // tm
    nseq = t // seq_len
    kern = functools.partial(_inproj_kernel, tm=tm, seq_blocks=seq_blocks, sample=sample, seq_len=seq_len)
    rows = lambda w: pl.BlockSpec((tm, w), lambda i: (i, 0))
    in_specs = [rows(dm), _const_spec((1, dm)), _const_spec(wc.shape), _const_spec(wqkv.shape), _const_spec(wg.shape),
                _const_spec(wf2.shape), _const_spec((1, LANES)), _const_spec(cw.shape), _const_spec((1, dc))]
    args = [x, gain, wc, wqkv, wg, wf2, bfp, cw, cbias]
    sds = jax.ShapeDtypeStruct
    if sample:
        in_specs += [rows(dc), rows(dc)]
        args += list(fix)
        qk_shapes = [sds((t, da), BF16), sds((t, da), BF16)]
        qk_specs = [rows(da), rows(da)]
        kv_shapes = [sds((t, da), F32), sds((t, da), F32)]
        kv_specs = [rows(da), rows(da)]
        tail_shapes = [sds((t, LANES), F32), sds((SUBLANES, t), F32), sds((t, dc), F32)]
        tail_specs = [rows(LANES), pl.BlockSpec((SUBLANES, tm), lambda i: (0, i)), rows(dc)]
        scratch = []
    else:
        in_specs += [_const_spec(a.shape) for a in aug]
        args += list(aug)
        qk_shapes = [sds((t, 2 * da), BF16), sds((t, 2 * da), BF16)]
        qk_specs = [rows(2 * da), rows(2 * da)]
        kv_shapes = [sds((nseq, da, seq_len), F32), sds((nseq, da, seq_len), F32)]
        kv_specs = [pl.BlockSpec((None, da, tm), lambda i: (i // seq_blocks, 0, i % seq_blocks))] * 2
        tail_shapes = [sds((nseq, SUBLANES, seq_len), F32), sds((nseq, 2, dc), F32)]
        tail_specs = [pl.BlockSpec((None, SUBLANES, tm), lambda i: (i // seq_blocks, 0, i % seq_blocks)),
                      pl.BlockSpec((None, 2, dc), lambda i: (i // seq_blocks, 0, 0))]
        scratch = [pltpu.VMEM((SUBLANES, dc), F32), pltpu.VMEM((SUBLANES, LANES), F32)]
    out_shape = ([sds((t, dc), BF16)] + qk_shapes +
                 [sds((t, da), BF16)] + kv_shapes +
                 [sds((t, gw), BF16)] + tail_shapes)
    out_specs = [rows(dc)] + qk_specs + [rows(da)] + kv_specs + [rows(gw)] + tail_specs
    return pl.pallas_call(
        kern, grid=(nblk,), in_specs=in_specs, out_specs=out_specs, out_shape=out_shape,
        scratch_shapes=scratch, compiler_params=_cparams(("arbitrary",)),
        name="inproj_sample" if sample else "inproj_prompt")(*args)


def _fox_prompt_kernel(qa_ref, ka_ref, v_ref, o_ref, m_sc, l_sc, acc_sc, *, tq, tk, dh):
    qi = pl.program_id(2)
    lane = lax.broadcasted_iota(jnp.int32, (1, 2 * LANES), 1)
    ext = lane - LANES
    qf = qa_ref[...].astype(F32)
    halves = []
    for h in range(2):
        keep = (((lane >= h * dh) & (lane < (h + 1) * dh))
                | ((ext >= 3 * h) & (ext < 3 * h + 3)) | ((ext >= 6 + 3 * h) & (ext < 9 + 3 * h)))
        halves.append(jnp.where(keep, qf, 0.0))
    qs = jnp.concatenate(halves, axis=0).astype(BF16)
    m_sc[...] = jnp.full_like(m_sc, NEG_BIG)
    l_sc[...] = jnp.zeros_like(l_sc)
    acc_sc[...] = jnp.zeros_like(acc_sc)
    nfull = (qi * tq) // tk

    def step(j, masked):
        ks = pl.multiple_of(j * tk, tk)
        kb = ka_ref[pl.ds(ks, tk), :]
        vb = v_ref[pl.ds(ks, tk), :]
        s = _dot_nt(qs, kb)
        if masked:
            rloc = lax.broadcasted_iota(jnp.int32, (2 * tq, tk), 0)
            rloc = jnp.where(rloc >= tq, rloc - tq, rloc) + qi * tq
            cloc = lax.broadcasted_iota(jnp.int32, (2 * tq, tk), 1) + ks
            s = jnp.where(cloc <= rloc, s, NEG_BIG)
        m_prev = m_sc[...]
        m_new = jnp.maximum(m_prev, jnp.max(s, axis=1, keepdims=True))
        p = jnp.exp(s - _lane_tile(m_new, tk))
        alpha = jnp.exp(m_prev - m_new)
        l_sc[...] = alpha * l_sc[...] + jnp.sum(p, axis=1, keepdims=True)
        acc_sc[...] = alpha * acc_sc[...] + _dot(p.astype(BF16), vb)
        m_sc[...] = m_new

    def body(j, c):
        step(j, False)
        return c

    lax.fori_loop(0, nfull, body, 0)
    step(nfull, True)
    o = acc_sc[...] / l_sc[...]
    lane_o = lax.broadcasted_iota(jnp.int32, (1, LANES), 1)
    o_ref[...] = jnp.where(lane_o < dh, o[0:tq], o[tq:2 * tq]).astype(o_ref.dtype)


def _fox_prompt(qa, ka, v, *, nseq, seq_len, dh):
    t, da = v.shape
    tk = min(FOX_KEY_BLOCK, seq_len)
    tq = min(FOX_QUERY_BLOCK, tk)
    nq = seq_len // tq
    npair = da // LANES
    kern = functools.partial(_fox_prompt_kernel, tq=tq, tk=tk, dh=dh)
    return pl.pallas_call(
        kern, grid=(nseq, npair, nq),
        in_specs=[pl.BlockSpec((tq, 2 * LANES), lambda b, hp, qi: (b * nq + qi, hp)),
                  pl.BlockSpec((seq_len, 2 * LANES), lambda b, hp, qi: (b, hp)),
                  pl.BlockSpec((seq_len, LANES), lambda b, hp, qi: (b, hp))],
        out_specs=pl.BlockSpec((tq, LANES), lambda b, hp, qi: (b * nq + qi, hp)),
        out_shape=jax.ShapeDtypeStruct((t, da), BF16),
        scratch_shapes=[pltpu.VMEM((2 * tq, LANES), F32), pltpu.VMEM((2 * tq, LANES), F32),
                        pltpu.VMEM((2 * tq, LANES), F32)],
        compiler_params=_cparams(("parallel", "parallel", "arbitrary")),
        name="fox_prompt")(qa, ka, v)


def _fox_sample_kernel(pt_ref, *refs, pp, page, s_new, heads, dh):
    k_refs = refs[:pp]
    v_refs = refs[pp:2 * pp]
    lf_refs = refs[2 * pp:3 * pp]
    qbd_ref, kn_ref, vn_ref, dtn_ref, o_ref, m_sc, l_sc, acc_sc, run_sc = refs[3 * pp:]
    b = pl.program_id(0)
    j = pl.program_id(1)
    nrow = s_new * heads
    da = heads * dh

    @pl.when(j == 0)
    def _():
        m_sc[...] = jnp.full_like(m_sc, NEG_BIG)
        l_sc[...] = jnp.zeros_like(l_sc)
        acc_sc[...] = jnp.zeros_like(acc_sc)
        run_sc[...] = jnp.zeros_like(run_sc)

    qbd = qbd_ref[...]

    def online(s, pv_fn):
        m_prev = m_sc[...]
        m_new = jnp.maximum(m_prev, jnp.max(s, axis=1, keepdims=True))
        p = jnp.exp(s - _lane_tile(m_new, s.shape[1]))
        alpha = jnp.exp(m_prev - m_new)
        l_sc[...] = alpha * l_sc[...] + jnp.sum(p, axis=1, keepdims=True)
        acc_sc[...] = _lane_tile(alpha, da) * acc_sc[...] + pv_fn(p.astype(BF16))
        m_sc[...] = m_new

    rr = lax.broadcasted_iota(jnp.int32, (page, 2 * page), 0)
    cc = lax.broadcasted_iota(jnp.int32, (page, 2 * page), 1)
    after = jnp.where((rr > cc) | (cc >= page), 1.0, 0.0).astype(BF16)
    lf = jnp.concatenate([lf_refs[i][...] for i in range(pp)], axis=0)
    n8 = pp * heads
    r3 = _dot(jnp.concatenate(_split3(lf), axis=0), after)
    both = r3[0:n8] + (r3[n8:2 * n8] + r3[2 * n8:3 * n8])
    run = run_sc[...]
    scores = []
    for i in range(pp):
        inner = both[i * heads:(i + 1) * heads, 0:page]
        rev = inner + run
        run = run + both[i * heads:(i + 1) * heads, page:2 * page]
        kb = k_refs[i][...].astype(BF16)
        scores.append(_dot(qbd, kb) + jnp.concatenate([rev] * s_new, axis=0))
    run_sc[...] = run

    def pv_pages(p):
        acc = None
        for i in range(pp):
            term = _dot_nt(p[:, i * page:(i + 1) * page], v_refs[i][...].astype(BF16))
            acc = term if acc is None else acc + term
        return acc

    online(jnp.concatenate(scores, axis=1), pv_pages)

    @pl.when(j == pl.num_programs(1) - 1)
    def _():
        ntok = kn_ref.shape[0]
        dtn = dtn_ref[...]
        lane = lax.broadcasted_iota(jnp.int32, (1, ntok), 1)
        dq_rows = [jnp.sum(jnp.where(lane == b * s_new + t, dtn, 0.0), axis=1, keepdims=True) for t in range(s_new)]
        dq = jnp.concatenate(dq_rows, axis=0)
        dk = jnp.concatenate([dtn] * s_new, axis=0)
        s = _dot_nt(qbd, kn_ref[...]) + (dq - dk)
        rowt = lax.broadcasted_iota(jnp.int32, (nrow, ntok), 0) // heads
        col = lax.broadcasted_iota(jnp.int32, (nrow, ntok), 1)
        keep = ((col // s_new) == b) & ((col % s_new) <= rowt)
        s = jnp.where(keep, s, NEG_BIG)
        online(s, lambda p: _dot(p, vn_ref[...]))
        o = acc_sc[...] / _lane_tile(l_sc[...], da)
        rowh = lax.broadcasted_iota(jnp.int32, (nrow, da), 0) % heads
        colh = lax.broadcasted_iota(jnp.int32, (nrow, da), 1) // dh
        o = jnp.where(rowh == colh, o, 0.0)
        o_ref[...] = jnp.sum(o.reshape(s_new, heads, da), axis=1)


def _fox_sample(page_table_flat, kt_pages, vt_pages, lf_pages, qbd, kn, vn, dtn, *, nb, npages, s_new, heads, dh):
    pp = min(SAMPLE_PAGES_PER_STEP, npages)
    nsteps = npages // pp
    page = kt_pages.shape[2]
    da = heads * dh
    nrow = s_new * heads
    ntok = kn.shape[0]
    kern = functools.partial(_fox_sample_kernel, pp=pp, page=page, s_new=s_new, heads=heads, dh=dh)

    def page_map(i):
        return lambda b, j, pt: (pt[b * npages + (npages - 1 - (j * pp + i))], 0, 0)

    page_specs = [pl.BlockSpec((None, da, page), page_map(i)) for i in range(pp)]
    lf_specs = [pl.BlockSpec((None, heads, page), page_map(i)) for i in range(pp)]
    grid_spec = pltpu.PrefetchScalarGridSpec(
        num_scalar_prefetch=1, grid=(nb, nsteps),
        in_specs=page_specs + page_specs + lf_specs + [
            pl.BlockSpec((None, nrow, da), lambda b, j, pt: (b, 0, 0)),
            pl.BlockSpec((ntok, da), lambda b, j, pt: (0, 0)),
            pl.BlockSpec((ntok, da), lambda b, j, pt: (0, 0)),
            pl.BlockSpec((SUBLANES, ntok), lambda b, j, pt: (0, 0))],
        out_specs=pl.BlockSpec((None, s_new, da), lambda b, j, pt: (b, 0, 0)),
        scratch_shapes=[pltpu.VMEM((nrow, LANES), F32), pltpu.VMEM((nrow, LANES), F32), pltpu.VMEM((nrow, da), F32),
                        pltpu.VMEM((heads, LANES), F32)])
    return pl.pallas_call(
        kern, grid_spec=grid_spec,
        out_shape=jax.ShapeDtypeStruct((nb, s_new, da), F32),
        compiler_params=_cparams(("parallel", "arbitrary")),
        name="fox_sample")(page_table_flat, *([kt_pages] * pp), *([vt_pages] * pp), *([lf_pages] * pp),
                           qbd, kn, vn, dtn)


def _fox_fused_kernel(pt_ref, qa_ref, ka_ref, v_ref, kt_hbm, vt_hbm, lf_hbm, qbd_ref, kn_ref, vn_ref, dtn_ref,
                      o_ref, os_ref,
                      m_sc, l_sc, acc_sc, ms_sc, ls_sc, as_sc, run_sc, kbuf, vbuf, lbuf, sem, g_ref,
                      *, tq, tk, dh, pp, ring, npages, nchunks, s_new, heads):
    first_step = (pl.program_id(0) == 0) & (pl.program_id(1) == 0) & (pl.program_id(2) == 0)
    qi = pl.program_id(2)
    page = kbuf.shape[3]
    cpb = npages // pp
    nrow = s_new * heads
    da = heads * dh
    nbs = qbd_ref.shape[0]

    def chunk_copies(c):
        slot = c % ring
        bs = c // cpb
        jc = c % cpb
        copies = []
        for i in range(pp):
            pid = pt_ref[bs * npages + (npages - 1 - (jc * pp + i))]
            copies.append(pltpu.make_async_copy(kt_hbm.at[pid], kbuf.at[slot, i], sem.at[slot]))
            copies.append(pltpu.make_async_copy(vt_hbm.at[pid], vbuf.at[slot, i], sem.at[slot]))
            copies.append(pltpu.make_async_copy(lf_hbm.at[pid], lbuf.at[slot, i], sem.at[slot]))
        return copies

    @pl.when(first_step)
    def _():
        g_ref[0] = 0
        for c in range(min(ring - 1, nchunks)):
            for cp in chunk_copies(c):
                cp.start()

    lane = lax.broadcasted_iota(jnp.int32, (1, 2 * LANES), 1)
    ext = lane - LANES
    qf = qa_ref[...].astype(F32)
    halves = []
    for h in range(2):
        keep = (((lane >= h * dh) & (lane < (h + 1) * dh))
                | ((ext >= 3 * h) & (ext < 3 * h + 3)) | ((ext >= 6 + 3 * h) & (ext < 9 + 3 * h)))
        halves.append(jnp.where(keep, qf, 0.0))
    qs = jnp.concatenate(halves, axis=0).astype(BF16)
    m_sc[...] = jnp.full_like(m_sc, NEG_BIG)
    l_sc[...] = jnp.zeros_like(l_sc)
    acc_sc[...] = jnp.zeros_like(acc_sc)
    nfull = (qi * tq) // tk

    def prompt_step(j, masked):
        ks = pl.multiple_of(j * tk, tk)
        kb = ka_ref[pl.ds(ks, tk), :]
        vb = v_ref[pl.ds(ks, tk), :]
        s = _dot_nt(qs, kb)
        if masked:
            rloc = lax.broadcasted_iota(jnp.int32, (2 * tq, tk), 0)
            rloc = jnp.where(rloc >= tq, rloc - tq, rloc) + qi * tq
            cloc = lax.broadcasted_iota(jnp.int32, (2 * tq, tk), 1) + ks
            s = jnp.where(cloc <= rloc, s, NEG_BIG)
        m_prev = m_sc[...]
        m_new = jnp.maximum(m_prev, jnp.max(s, axis=1, keepdims=True))
        p = jnp.exp(s - _lane_tile(m_new, tk))
        alpha = jnp.exp(m_prev - m_new)
        l_sc[...] = alpha * l_sc[...] + jnp.sum(p, axis=1, keepdims=True)
        acc_sc[...] = alpha * acc_sc[...] + _dot(p.astype(BF16), vb)
        m_sc[...] = m_new

    def sample_update(s, pv_fn):
        m_prev = ms_sc[...]
        m_new = jnp.maximum(m_prev, jnp.max(s, axis=1, keepdims=True))
        p = jnp.exp(s - _lane_tile(m_new, s.shape[1]))
        alpha = jnp.exp(m_prev - m_new)
        ls_sc[...] = alpha * ls_sc[...] + jnp.sum(p, axis=1, keepdims=True)
        as_sc[...] = _lane_tile(alpha, da) * as_sc[...] + pv_fn(p.astype(BF16))
        ms_sc[...] = m_new

    def sample_chunk(g, valid):
        slot = g % ring
        bs = jnp.minimum(g // cpb, nbs - 1)
        qbd = qbd_ref[bs]
        rr = lax.broadcasted_iota(jnp.int32, (page, 2 * page), 0)
        cc = lax.broadcasted_iota(jnp.int32, (page, 2 * page), 1)
        after = jnp.where((rr > cc) | (cc >= page), 1.0, 0.0).astype(BF16)
        lf = jnp.concatenate([lbuf[slot, i] for i in range(pp)], axis=0)
        n8 = pp * heads
        r3 = _dot(jnp.concatenate(_split3(lf), axis=0), after)
        both = r3[0:n8] + (r3[n8:2 * n8] + r3[2 * n8:3 * n8])
        run = run_sc[...]
        scores = []
        for i in range(pp):
            rev = both[i * heads:(i + 1) * heads, 0:page] + run
            run = run + both[i * heads:(i + 1) * heads, page:2 * page]
            scores.append(_dot(qbd, kbuf[slot, i].astype(BF16)) + jnp.concatenate([rev] * s_new, axis=0))
        run_sc[...] = run
        s = jnp.where(valid, jnp.concatenate(scores, axis=1), NEG_BIG)

        def pv_pages(p):
            acc = None
            for i in range(pp):
                term = _dot_nt(p[:, i * page:(i + 1) * page], vbuf[slot, i].astype(BF16))
                acc = term if acc is None else acc + term
            return acc

        sample_update(s, pv_pages)

    def sample_finish(bs):
        ntok = kn_ref.shape[0]
        qbd = qbd_ref[bs]
        dtn = dtn_ref[...]
        lane_t = lax.broadcasted_iota(jnp.int32, (1, ntok), 1)
        dq_rows = [jnp.sum(jnp.where(lane_t == bs * s_new + t, dtn, 0.0), axis=1, keepdims=True)
                   for t in range(s_new)]
        dq = jnp.concatenate(dq_rows, axis=0)
        dk = jnp.concatenate([dtn] * s_new, axis=0)
        s = _dot_nt(qbd, kn_ref[...]) + (dq - dk)
        rowt = lax.broadcasted_iota(jnp.int32, (nrow, ntok), 0) // heads
        col = lax.broadcasted_iota(jnp.int32, (nrow, ntok), 1)
        keep = ((col // s_new) == bs) & ((col % s_new) <= rowt)
        sample_update(jnp.where(keep, s, NEG_BIG), lambda p: _dot(p, vn_ref[...]))
        o = as_sc[...] / _lane_tile(ls_sc[...], da)
        rowh = lax.broadcasted_iota(jnp.int32, (nrow, da), 0) % heads
        colh = lax.broadcasted_iota(jnp.int32, (nrow, da), 1) // dh
        o = jnp.where(rowh == colh, o, 0.0)
        os_ref[bs] = jnp.sum(o.reshape(s_new, heads, da), axis=1)

    def iteration(j, masked):
        g = g_ref[0]
        valid = g < nchunks

        @pl.when(g + (ring - 1) < nchunks)
        def _():
            for cp in chunk_copies(g + (ring - 1)):
                cp.start()

        @pl.when(valid)
        def _():
            for cp in chunk_copies(g):
                cp.wait()

        @pl.when(valid & (g % cpb == 0))
        def _():
            ms_sc[...] = jnp.full_like(ms_sc, NEG_BIG)
            ls_sc[...] = jnp.zeros_like(ls_sc)
            as_sc[...] = jnp.zeros_like(as_sc)
            run_sc[...] = jnp.zeros_like(run_sc)

        prompt_step(j, masked)
        sample_chunk(g, valid)

        @pl.when(valid & (g % cpb == cpb - 1))
        def _():
            sample_finish(g // cpb)

        g_ref[0] = g + 1

    def body(j, c):
        iteration(j, False)
        return c

    lax.fori_loop(0, nfull, body, 0)
    iteration(nfull, True)
    o = acc_sc[...] / l_sc[...]
    lane_o = lax.broadcasted_iota(jnp.int32, (1, LANES), 1)
    o_ref[...] = jnp.where(lane_o < dh, o[0:tq], o[tq:2 * tq]).astype(o_ref.dtype)


def _fox_fused(page_table_flat, qa, ka, v, kt_pages, vt_pages, lf_pages, qbd, kn, vn, dtn,
               *, nseq, seq_len, dh, npages, s_new, heads):
    t, da = v.shape
    nbs, nrow, _ = qbd.shape
    page = kt_pages.shape[2]
    ntok = kn.shape[0]
    tk = min(FOX_KEY_BLOCK, seq_len)
    tq = min(FOX_QUERY_BLOCK, tk)
    nq = seq_len // tq
    npair = da // LANES
    pp = min(FUSED_PAGES_PER_ITERATION, npages)
    nchunks = nbs * (npages // pp)
    iters = nseq * npair * sum((qi * tq) // tk + 1 for qi in range(nq))
    assert npages % pp == 0 and nchunks <= iters, "the page stream must fit in the prompt attention's iterations"
    kern = functools.partial(_fox_fused_kernel, tq=tq, tk=tk, dh=dh, pp=pp, ring=FUSED_RING, npages=npages,
                             nchunks=nchunks, s_new=s_new, heads=heads)
    const = lambda shape: pl.BlockSpec(shape, lambda b, hp, qi, pt: (0,) * len(shape))
    grid_spec = pltpu.PrefetchScalarGridSpec(
        num_scalar_prefetch=1, grid=(nseq, npair, nq),
        in_specs=[pl.BlockSpec((tq, 2 * LANES), lambda b, hp, qi, pt: (b * nq + qi, hp)),
                  pl.BlockSpec((seq_len, 2 * LANES), lambda b, hp, qi, pt: (b, hp)),
                  pl.BlockSpec((seq_len, LANES), lambda b, hp, qi, pt: (b, hp)),
                  pl.BlockSpec(memory_space=pl.ANY), pl.BlockSpec(memory_space=pl.ANY),
                  pl.BlockSpec(memory_space=pl.ANY),
                  const((nbs, nrow, da)), const((ntok, da)), const((ntok, da)), const((SUBLANES, ntok))],
        out_specs=[pl.BlockSpec((tq, LANES), lambda b, hp, qi, pt: (b * nq + qi, hp)),
                   const((nbs, s_new, da))],
        scratch_shapes=[pltpu.VMEM((2 * tq, LANES), F32), pltpu.VMEM((2 * tq, LANES), F32),
                        pltpu.VMEM((2 * tq, LANES), F32),
                        pltpu.VMEM((nrow, LANES), F32), pltpu.VMEM((nrow, LANES), F32), pltpu.VMEM((nrow, da), F32),
                        pltpu.VMEM((heads, LANES), F32),
                        pltpu.VMEM((FUSED_RING, pp, da, page), F32), pltpu.VMEM((FUSED_RING, pp, da, page), F32),
                        pltpu.VMEM((FUSED_RING, pp, heads, page), F32),
                        pltpu.SemaphoreType.DMA((FUSED_RING,)), pltpu.SMEM((1,), jnp.int32)])
    return pl.pallas_call(
        kern, grid_spec=grid_spec,
        out_shape=[jax.ShapeDtypeStruct((t, da), BF16), jax.ShapeDtypeStruct((nbs, s_new, da), F32)],
        compiler_params=_cparams(("arbitrary", "arbitrary", "arbitrary")),
        name="fox_fused")(page_table_flat, qa, ka, v, kt_pages, vt_pages, lf_pages, qbd, kn, vn, dtn)


def _post_attn_kernel(x_ref, gc_ref, at_ref, g_ref, wuc_ref, wua_ref, wmo_ref, gain_ref, wxq_ref, h_ref, qx_ref):
    dm = x_ref.shape[1]
    y_conv = _dot(gc_ref[...], wuc_ref[...])
    y_attn = _dot(at_ref[...].astype(BF16), wua_ref[...])
    g_conv = g_ref[:, 0:dm].astype(F32)
    g_attn = g_ref[:, dm:2 * dm].astype(F32)
    mixed = jax.nn.sigmoid(g_conv) * y_conv + jax.nn.sigmoid(g_attn) * y_attn
    h = x_ref[...] + _dot(mixed.astype(BF16), wmo_ref[...])
    h_ref[...] = h
    xn = _rms(h, gain_ref[...]).astype(BF16)
    qx_ref[...] = _dot(xn, wxq_ref[...]).astype(qx_ref.dtype)


def _post_attn(x, gc, at, g, wuc, wua, wmo, gain, wxq, *, tm, name):
    t, dm = x.shape
    dx = wxq.shape[1]
    rows = lambda w: pl.BlockSpec((tm, w), lambda i: (i, 0))
    return pl.pallas_call(
        _post_attn_kernel, grid=(t // tm,),
        in_specs=[rows(dm), rows(gc.shape[1]), rows(at.shape[1]), rows(g.shape[1]), _const_spec(wuc.shape),
                  _const_spec(wua.shape), _const_spec(wmo.shape), _const_spec((1, dm)), _const_spec(wxq.shape)],
        out_specs=[rows(dm), rows(dx)],
        out_shape=[jax.ShapeDtypeStruct((t, dm), F32), jax.ShapeDtypeStruct((t, dx), BF16)],
        compiler_params=_cparams(("parallel",)), name=name)(x, gc, at, g, wuc, wua, wmo, gain, wxq)


def _memkv_kernel(m_ref, gain_ref, wk_ref, wv_ref, k_ref, v_ref, *, xh):
    mn = _rms(m_ref[...], gain_ref[...]).astype(BF16)
    tm = m_ref.shape[0]
    xd = wk_ref.shape[1] // xh
    k = _dot(mn, wk_ref[...])
    v = _dot(mn, wv_ref[...])
    for h in range(xh):
        k_ref[pl.ds(h, tm, stride=xh), :] = k[:, h * xd:(h + 1) * xd]
        v_ref[pl.ds(h, tm, stride=xh), :] = v[:, h * xd:(h + 1) * xd]


def _memkv(mem, gain, wk, wv, *, xh):
    t, dm = mem.shape
    tm = min(512, t)
    xd = wk.shape[1] // xh
    rows = lambda w: pl.BlockSpec((tm, w), lambda i: (i, 0))
    return pl.pallas_call(
        functools.partial(_memkv_kernel, xh=xh), grid=(t // tm,),
        in_specs=[rows(dm), _const_spec((1, dm)), _const_spec(wk.shape), _const_spec(wv.shape)],
        out_specs=[pl.BlockSpec((tm * xh, xd), lambda i: (i, 0))] * 2,
        out_shape=[jax.ShapeDtypeStruct((t * xh, xd), F32)] * 2,
        compiler_params=_cparams(("parallel",)), name="memkv")(mem, gain, wk, wv)


def _xattn_kernel(q_ref, mk_ref, mv_ref, o_ref, *, xh, xd):
    nm = mk_ref.shape[1] // xh
    scale = xd ** -0.5
    for b in range(q_ref.shape[0]):
        q = q_ref[b].astype(BF16)
        outs = []
        for h in range(xh):
            mk = mk_ref[b, pl.ds(h, nm, stride=xh), :].astype(BF16)
            mv = mv_ref[b, pl.ds(h, nm, stride=xh), :].astype(BF16)
            s = _dot_nt(q[:, h * xd:(h + 1) * xd], mk) * scale
            m = jnp.max(s, axis=1, keepdims=True)
            p = jnp.exp(s - m)
            p = p / jnp.sum(p, axis=1, keepdims=True)
            outs.append(_dot(p.astype(BF16), mv))
        o_ref[b] = jnp.concatenate(outs, axis=1).astype(o_ref.dtype)


def _xattn(q, mk, mv, *, tq, xh, nbb, name):
    nb, s, dx = q.shape
    nm, xd = mk.shape[1:]
    kern = functools.partial(_xattn_kernel, xh=xh, xd=xd)
    return pl.pallas_call(
        kern, grid=(nb // nbb, s // tq),
        in_specs=[pl.BlockSpec((nbb, tq, dx), lambda b, i: (b, i, 0)),
                  pl.BlockSpec((nbb, nm, xd), lambda b, i: (b, 0, 0)),
                  pl.BlockSpec((nbb, nm, xd), lambda b, i: (b, 0, 0))],
        out_specs=pl.BlockSpec((nbb, tq, dx), lambda b, i: (b, i, 0)),
        out_shape=jax.ShapeDtypeStruct((nb, s, dx), q.dtype),
        compiler_params=_cparams(("parallel", "parallel")), name=name)(q, mk, mv)


def _pre_moe_kernel(h_ref, o_ref, wxo_ref, gain_ref, wrh_ref, wrl_ref, cnt_in_ref,
                    h2_ref, xn_ref, rt_ref, rr_ref, cnt_out_ref, base, *, tm, ngroups, epg):
    i = pl.program_id(0)

    @pl.when(i == 0)
    def _():
        base[...] = cnt_in_ref[...]

    h2 = h_ref[...] + _dot(o_ref[...].astype(BF16), wxo_ref[...])
    h2_ref[...] = h2
    xn = _rms(h2, gain_ref[...])
    xw = _pack_bf16_pairs(xn)
    nc = xw.shape[1] // LANES
    for c in range(nc):
        xn_ref[pl.ds(c, tm, stride=nc), :] = xw[:, c * LANES:(c + 1) * LANES]

    xh = xn.astype(BF16)
    xl = (xn - xh.astype(F32)).astype(BF16)
    lt = _dot_nt(wrh_ref[...], xh) + (_dot_nt(wrh_ref[...], xl) + _dot_nt(wrl_ref[...], xh))

    sub = lax.broadcasted_iota(jnp.int32, (SUBLANES, tm), 0)
    gl = jnp.where(sub < ngroups, lt[0:SUBLANES, :], -jnp.inf)
    gmax = jnp.max(gl, axis=0, keepdims=True)
    gidx = jnp.min(jnp.where(gl == gmax, sub, SUBLANES), axis=0, keepdims=True)
    pg = 1.0 / jnp.sum(jnp.exp(gl - gmax), axis=0, keepdims=True)
    el = jnp.zeros((epg, tm), F32)
    for g in range(ngroups):
        el = jnp.where(gidx == g, lt[EXPERT_ROW0 + g * epg:EXPERT_ROW0 + (g + 1) * epg, :], el)
    v1 = jnp.max(el, axis=0, keepdims=True)
    i1 = jnp.min(jnp.where(el == v1, sub, epg), axis=0, keepdims=True)
    el2 = jnp.where(sub == i1, -jnp.inf, el)
    v2 = jnp.max(el2, axis=0, keepdims=True)
    i2 = jnp.min(jnp.where(el2 == v2, sub, epg), axis=0, keepdims=True)
    t2 = jnp.exp(v2 - v1)
    den = 1.0 + t2
    w0 = (1.0 / den) * pg
    w1 = (t2 / den) * pg
    e0 = gidx * epg + i1
    e1 = gidx * epg + i2

    erow = lax.broadcasted_iota(jnp.int32, (LANES, tm), 0)
    oh0 = erow == e0
    oh1 = erow == e1
    rr = lax.broadcasted_iota(jnp.int32, (tm, tm), 0)
    cc = lax.broadcasted_iota(jnp.int32, (tm, tm), 1)
    triu = jnp.where(rr <= cc, 1.0, 0.0).astype(BF16)
    pre0 = _dot(jnp.where(oh0, 1.0, 0.0).astype(BF16), triu)
    pre1 = _dot(jnp.where(oh1, 1.0, 0.0).astype(BF16), triu)
    b0 = base[:, 0:1]
    tot0 = pre0[:, tm - 1:tm]
    tot1 = pre1[:, tm - 1:tm]
    rank0 = jnp.sum(jnp.where(oh0, pre0 - 1.0 + b0, 0.0), axis=0, keepdims=True)
    rank1 = jnp.sum(jnp.where(oh1, pre1 - 1.0 + (b0 + tot0), 0.0), axis=0, keepdims=True)
    newb = b0 + tot0 + tot1
    base[...] = jnp.broadcast_to(newb, base.shape)
    cnt_out_ref[...] = jnp.broadcast_to(newb, cnt_out_ref.shape)

    zero = jnp.zeros((1, tm), F32)
    rt = jnp.concatenate([e0.astype(F32), e1.astype(F32), w0, w1, rank0, rank1, zero, zero], axis=0)
    rt_ref[...] = rt
    rt_pad = jnp.concatenate([rt, jnp.zeros((LANES - ROUTE_ROWS, tm), F32)], axis=0)
    rr_ref[...] = jnp.transpose(rt_pad)


def _pre_moe(h, o, wxo, gain, wrh, wrl, cnt_in, *, tm, ngroups, epg, name):
    t, dm = h.shape
    dx = o.shape[1]
    kern = functools.partial(_pre_moe_kernel, tm=tm, ngroups=ngroups, epg=epg)
    rows = lambda w: pl.BlockSpec((tm, w), lambda i: (i, 0))
    return pl.pallas_call(
        kern, grid=(t // tm,),
        in_specs=[rows(dm), rows(dx), _const_spec(wxo.shape), _const_spec((1, dm)), _const_spec(wrh.shape),
                  _const_spec(wrl.shape), _const_spec((LANES, LANES))],
        out_specs=[rows(dm), pl.BlockSpec((tm * (dm // (2 * LANES)), LANES), lambda i: (i, 0)),
                   pl.BlockSpec((ROUTE_ROWS, tm), lambda i: (0, i)), rows(LANES), _const_spec((LANES, LANES))],
        out_shape=[jax.ShapeDtypeStruct((t, dm), F32), jax.ShapeDtypeStruct((t * (dm // (2 * LANES)), LANES), jnp.uint32),
                   jax.ShapeDtypeStruct((ROUTE_ROWS, t), F32), jax.ShapeDtypeStruct((t, LANES), F32),
                   jax.ShapeDtypeStruct((LANES, LANES), F32)],
        scratch_shapes=[pltpu.VMEM((LANES, LANES), F32)],
        compiler_params=_cparams(("arbitrary",)), name=name)(h, o, wxo, gain, wrh, wrl, cnt_in)


def _row_copy(src, r_src, dst, r_dst, sem, nc):
    return pltpu.make_async_copy(src.at[pl.ds(pl.multiple_of(r_src * nc, nc), nc)],
                                 dst.at[pl.ds(pl.multiple_of(r_dst * nc, nc), nc)], sem)


def _slab_rows(ref, nc):
    rows = ref.shape[0] // nc
    return jnp.concatenate([ref[pl.ds(c, rows, stride=nc), :] for c in range(nc)], axis=1)


def _pack_bf16_pairs(x):
    half = x.shape[1] // 2
    lo = pltpu.bitcast(x[:, 0:half].astype(BF16).astype(F32), jnp.uint32)
    hi = pltpu.bitcast(x[:, half:2 * half].astype(BF16).astype(F32), jnp.uint32)
    return lax.shift_right_logical(lo, jnp.uint32(16)) | (hi & jnp.uint32(0xFFFF0000))


def _unpack_bf16_pairs(w):
    lo = pltpu.bitcast(lax.shift_left(w, jnp.uint32(16)), F32)
    hi = pltpu.bitcast(w & jnp.uint32(0xFFFF0000), F32)
    return jnp.concatenate([lo, hi], axis=1)


def _dest_kernel(rt_ref, pstart_ref, o_ref):
    rt = rt_ref[...]
    pst = pstart_ref[:, 0:1]
    erow = lax.broadcasted_iota(jnp.int32, (LANES, rt.shape[1]), 0)
    rows = []
    for k in range(TOP_K):
        e = rt[k:k + 1, :].astype(jnp.int32)
        rows.append(jnp.sum(jnp.where(erow == e, pst, 0.0), axis=0, keepdims=True) + rt[4 + k:5 + k, :])
    rows.append(jnp.zeros((ROUTE_ROWS - TOP_K, rt.shape[1]), F32))
    o_ref[...] = jnp.concatenate(rows, axis=0).astype(jnp.int32)


def _dest(rt_all, pstart_col):
    tall = rt_all.shape[1]
    nchunk = 3 if tall % (3 * LANES) == 0 else 1
    w = tall // nchunk
    return pl.pallas_call(
        _dest_kernel, grid=(nchunk,),
        in_specs=[pl.BlockSpec((ROUTE_ROWS, w), lambda i: (0, i)), _const_spec((LANES, LANES))],
        out_specs=pl.BlockSpec((ROUTE_ROWS, w), lambda i: (0, i)),
        out_shape=jax.ShapeDtypeStruct((ROUTE_ROWS, tall), jnp.int32),
        compiler_params=_cparams(("parallel",)), name="dest")(rt_all, pstart_col)


def _scatter_kernel(dest_ref, pstart_ref, pend_ref, xp_ref, xs_ref, out_ref, zeros, sem, zsem,
                    *, tm_p, nblk_p, nexp, nc):
    i = pl.program_id(0)
    tall = dest_ref.shape[0] // TOP_K

    @pl.when(i == 0)
    def _():
        zeros[...] = jnp.zeros_like(zeros)

        def zero_block(blk):
            start = pl.multiple_of(blk * (EXPERT_ROWS * nc), EXPERT_ROWS * nc)
            return pltpu.make_async_copy(zeros, out_ref.at[pl.ds(start, EXPERT_ROWS * nc)], zsem)

        for e in range(nexp):
            @pl.when(pend_ref[e] > pstart_ref[e])
            def _():
                zero_block(pend_ref[e] // EXPERT_ROWS - 1).start()
        for e in range(nexp):
            @pl.when(pend_ref[e] > pstart_ref[e])
            def _():
                zero_block(pend_ref[e] // EXPERT_ROWS - 1).wait()
        nblk = out_ref.shape[0] // (EXPERT_ROWS * nc)
        nused = pend_ref[nexp - 1] // EXPERT_ROWS

        def start_unused(blk, c):
            zero_block(blk).start()
            return c

        def wait_unused(blk, c):
            zero_block(blk).wait()
            return c

        lax.fori_loop(nused, nblk, start_unused, 0)
        lax.fori_loop(nused, nblk, wait_unused, 0)

    def copy_rows(x_ref, base):
        tm = x_ref.shape[0] // nc

        def issue(r, c):
            for k in range(TOP_K):
                _row_copy(x_ref, r, out_ref, dest_ref[k * tall + base + r], sem, nc).start(priority=k % 2)
            return c

        lax.fori_loop(0, tm, issue, 0, unroll=8)
        for k in range(TOP_K):
            pltpu.make_async_copy(x_ref, x_ref, sem).wait()

    @pl.when(i < nblk_p)
    def _():
        copy_rows(xp_ref, i * tm_p)

    @pl.when(i == nblk_p)
    def _():
        copy_rows(xs_ref, nblk_p * tm_p)


def _scatter(dest_flat, pstart, pend, x_p, x_s, *, n_rows, tm_p, nexp, nc):
    tp = x_p.shape[0] // nc
    ts = x_s.shape[0] // nc
    nblk_p = tp // tm_p
    kern = functools.partial(_scatter_kernel, tm_p=tm_p, nblk_p=nblk_p, nexp=nexp, nc=nc)
    grid_spec = pltpu.PrefetchScalarGridSpec(
        num_scalar_prefetch=3, grid=(nblk_p + 1,),
        in_specs=[pl.BlockSpec((tm_p * nc, LANES), lambda i, *_: (jnp.minimum(i, nblk_p - 1), 0)),
                  pl.BlockSpec((ts * nc, LANES), lambda i, *_: (0, 0))],
        out_specs=pl.BlockSpec(memory_space=pl.ANY),
        scratch_shapes=[pltpu.VMEM((EXPERT_ROWS * nc, LANES), jnp.uint32), pltpu.SemaphoreType.DMA(()),
                        pltpu.SemaphoreType.DMA(())])
    return pl.pallas_call(
        kern, grid_spec=grid_spec, out_shape=jax.ShapeDtypeStruct((n_rows * nc, LANES), jnp.uint32),
        compiler_params=_cparams(("arbitrary",)), name="scatter")(dest_flat, pstart, pend, x_p, x_s)


def _experts_kernel(blk_e_ref, nused_ref, xs_ref, wg_ref, wu_ref, wd_ref, y_ref, wgb, wub, wdb, *, nc):
    i = pl.program_id(0)
    prev = blk_e_ref[jnp.maximum(i - 1, 0)]
    fresh = (i == 0) | (blk_e_ref[i] != prev)

    @pl.when(i < nused_ref[0])
    def _():
        @pl.when(fresh)
        def _():
            wgb[...] = wg_ref[...].astype(BF16)
            wub[...] = wu_ref[...].astype(BF16)
            wdb[...] = wd_ref[...].astype(BF16)
        x = _unpack_bf16_pairs(_slab_rows(xs_ref, nc)).astype(BF16)
        a = _dot(x, wgb[...])
        u = _dot(x, wub[...])
        hmid = (a * jax.nn.sigmoid(a)) * u
        y = _pack_bf16_pairs(_dot(hmid.astype(BF16), wdb[...]))
        for c in range(nc):
            y_ref[pl.ds(c, EXPERT_ROWS, stride=nc), :] = y[:, c * LANES:(c + 1) * LANES]

    @pl.when(i >= nused_ref[0])
    def _():
        y_ref[...] = jnp.zeros_like(y_ref)


def _experts(blk_e, nused, xs, wg, wu, wd):
    dm, de = wg.shape[1:]
    nc = dm // (2 * LANES)
    nblk = xs.shape[0] // (EXPERT_ROWS * nc)

    def row_map(i, be, nu):
        return (jnp.minimum(i, nu[0] - 1), 0)

    grid_spec = pltpu.PrefetchScalarGridSpec(
        num_scalar_prefetch=2, grid=(nblk,),
        in_specs=[pl.BlockSpec((EXPERT_ROWS * nc, LANES), row_map),
                  pl.BlockSpec((None, dm, de), lambda i, be, nu: (be[i], 0, 0)),
                  pl.BlockSpec((None, dm, de), lambda i, be, nu: (be[i], 0, 0)),
                  pl.BlockSpec((None, de, dm), lambda i, be, nu: (be[i], 0, 0))],
        out_specs=pl.BlockSpec((EXPERT_ROWS * nc, LANES), lambda i, be, nu: (i, 0)),
        scratch_shapes=[pltpu.VMEM((dm, de), BF16), pltpu.VMEM((dm, de), BF16), pltpu.VMEM((de, dm), BF16)])
    return pl.pallas_call(
        functools.partial(_experts_kernel, nc=nc), grid_spec=grid_spec,
        out_shape=jax.ShapeDtypeStruct(xs.shape, jnp.uint32),
        compiler_params=_cparams(("arbitrary",)), name="experts")(blk_e, nused, xs, wg, wu, wd)


def _combine_kernel(dest_ref, h_ref, rr_ref, gain_ref, y_hbm, o_ref, buf, sem, *, tm, tok0):
    i = pl.program_id(0)
    nc = h_ref.shape[1] // (2 * LANES)
    tall = dest_ref.shape[0] // TOP_K
    base = tok0 + i * tm

    def issue(r, c):
        for k in range(TOP_K):
            _row_copy(y_hbm, dest_ref[k * tall + base + r], buf.at[k], r, sem, nc).start(priority=k % 2)
        return c

    lax.fori_loop(0, tm, issue, 0, unroll=8)
    for k in range(TOP_K):
        pltpu.make_async_copy(buf.at[k], buf.at[k], sem).wait()
    rr = rr_ref[...]
    y0 = _unpack_bf16_pairs(_slab_rows(buf.at[0], nc))
    y1 = _unpack_bf16_pairs(_slab_rows(buf.at[1], nc))
    h = h_ref[...] + (rr[:, 2:3] * y0 + rr[:, 3:4] * y1)
    o_ref[...] = _rms(h, gain_ref[...])


def _combine(dest_flat, h, rr, gain, y, *, tm, tok0, name):
    t, dm = h.shape
    kern = functools.partial(_combine_kernel, tm=tm, tok0=tok0)
    grid_spec = pltpu.PrefetchScalarGridSpec(
        num_scalar_prefetch=1, grid=(t // tm,),
        in_specs=[pl.BlockSpec((tm, dm), lambda i, *_: (i, 0)), pl.BlockSpec((tm, LANES), lambda i, *_: (i, 0)),
                  pl.BlockSpec((1, dm), lambda i, *_: (0, 0)), pl.BlockSpec(memory_space=pl.ANY)],
        out_specs=pl.BlockSpec((tm, dm), lambda i, *_: (i, 0)),
        scratch_shapes=[pltpu.VMEM((TOP_K, tm * (dm // (2 * LANES)), LANES), jnp.uint32),
                        pltpu.SemaphoreType.DMA(())])
    return pl.pallas_call(
        kern, grid_spec=grid_spec, out_shape=jax.ShapeDtypeStruct((t, dm), F32),
        compiler_params=_cparams(("arbitrary",)), name=name)(dest_flat, h, rr, gain, y)


def kernel(x_prompt, x_sample, cache_k, cache_v, cache_logf, cache_mem_k, cache_mem_v, state_conv, page_table,
           mem_prompt, norm_mix, w_in, b_forget, conv_w, conv_b, w_up_conv, w_up_attn, w_mix_out, norm_xattn,
           norm_mem, w_xq, w_xk, w_xv, w_xo, norm_ffn, w_router_group, w_router_expert, w_expert_gate,
           w_expert_up, w_expert_down, norm_final):
    depth = w_in.shape[0]
    assert depth == 1, "single-layer trunk"
    nbp, seq, dm = x_prompt.shape
    nbs, s_new, _ = x_sample.shape
    _, n_pool, page, heads, dh = cache_k.shape
    npages = page_table.shape[1]
    nmem, xh, xd = cache_mem_k.shape[2:]
    dc = conv_w.shape[2]
    da = heads * dh
    dx = xh * xd
    ngroups, _, epg = w_router_expert.shape[1:]
    nexp = ngroups * epg
    tp = nbp * seq
    ts = nbs * s_new
    assert conv_w.shape[1] == 3 and s_new >= 2 and dh == 64 and heads == SUBLANES and epg == SUBLANES
    assert page == LANES

    l = 0
    wi = w_in[l]
    wc = wi[:, 0:3 * dc].astype(BF16)
    wqkv = wi[:, 3 * dc:3 * dc + 3 * da].astype(BF16)
    o_f = 3 * dc + 3 * da
    wf = jnp.pad(wi[:, o_f:o_f + heads], ((0, 0), (0, LANES - heads)))
    wfh = wf.astype(BF16)
    wf2 = jnp.concatenate([wfh, (wf - wfh.astype(F32)).astype(BF16)], axis=1)
    wg = wi[:, o_f + heads:].astype(BF16)
    bfp = jnp.pad(b_forget[l][None, :], ((0, 0), (0, LANES - heads)))
    cw = conv_w[l]
    cbias = conv_b[l][None, :]
    g_mix = norm_mix[l][None, :]
    wuc = w_up_conv[l].astype(BF16)
    wua = w_up_attn[l].astype(BF16)
    wmo = w_mix_out[l].astype(BF16)
    g_x = norm_xattn[l][None, :]
    wxq = w_xq[l].astype(BF16)
    wxo = w_xo[l].astype(BF16)
    g_f = norm_ffn[l][None, :]
    wr = jnp.zeros((LANES, dm), F32)
    wr = wr.at[0:ngroups].set(w_router_group[l].T)
    wr = wr.at[EXPERT_ROW0:EXPERT_ROW0 + nexp].set(jnp.transpose(w_router_expert[l], (0, 2, 1)).reshape(nexp, dm))
    wrh = wr.astype(BF16)
    wrl = (wr - wrh.astype(F32)).astype(BF16)

    xp = x_prompt.reshape(tp, dm)
    (gc_p, qa_p, ka_p, vb_p, kt_p, vt_p, g_p, lf_p, ulast_p) = _inproj(
        xp, g_mix, wc, wqkv, wg, wf2, bfp, cw, cbias, seq_len=seq, sample=False, aug=_decay_columns(heads))
    xs_ = x_sample.reshape(ts, dm)
    st = state_conv[l]
    zeros_row = jnp.zeros((nbs, 1, dc), F32)
    fix1 = jnp.concatenate([st[:, 1:2], jnp.tile(zeros_row, (1, s_new - 1, 1))], axis=1).reshape(ts, dc)
    fix2 = jnp.concatenate([st[:, 0:1], st[:, 1:2], jnp.tile(zeros_row, (1, s_new - 2, 1))], axis=1).reshape(ts, dc)
    (gc_s, q_s, kb_s, vb_s, kf_s, vf_s, g_s, lf_s, dt_s, u_s) = _inproj(
        xs_, g_mix, wc, wqkv, wg, wf2, bfp, cw, cbias, seq_len=s_new, sample=True, fix=(fix1, fix2))
    pt_flat = page_table.reshape(-1).astype(jnp.int32)
    head_of_col = jnp.arange(da) // dh
    qbd = jnp.where(head_of_col[None, None, None, :] == jnp.arange(heads)[None, None, :, None],
                    q_s.reshape(nbs, s_new, 1, da), jnp.zeros((), BF16)).reshape(nbs, s_new * heads, da)
    kt_pages = jnp.transpose(cache_k[l], (0, 2, 3, 1)).reshape(n_pool, da, page)
    vt_pages = jnp.transpose(cache_v[l], (0, 2, 3, 1)).reshape(n_pool, da, page)
    lf_pages = jnp.swapaxes(cache_logf[l], 1, 2)
    at_p, at_s = _fox_fused(pt_flat, qa_p, ka_p, vb_p, kt_pages, vt_pages, lf_pages, qbd, kb_s, vb_s, dt_s,
                            nseq=nbp, seq_len=seq, dh=dh, npages=npages, s_new=s_new, heads=heads)

    tm_p = min(512, seq)
    h1_p, qx_p = _post_attn(xp, gc_p, at_p, g_p, wuc, wua, wmo, g_x, wxq, tm=tm_p, name="post_attn_prompt")
    mk_p, mv_p = _memkv(mem_prompt.reshape(nbp * nmem, dm), norm_mem[l][None, :], w_xk[l].astype(BF16),
                        w_xv[l].astype(BF16), xh=xh)
    o_p = _xattn(qx_p.reshape(nbp, seq, dx), mk_p.reshape(nbp, nmem * xh, xd), mv_p.reshape(nbp, nmem * xh, xd),
                 tq=tm_p, xh=xh, nbb=1, name="xattn_prompt").reshape(tp, dx)
    cnt0 = jnp.zeros((LANES, LANES), F32)
    h2_p, xn_p, rt_p, rr_p, cnt1 = _pre_moe(h1_p, o_p, wxo, g_f, wrh, wrl, cnt0, tm=tm_p, ngroups=ngroups, epg=epg,
                                            name="pre_moe_prompt")

    h1_s, qx_s = _post_attn(xs_, gc_s, at_s.reshape(ts, da), g_s, wuc, wua, wmo, g_x, wxq, tm=ts,
                            name="post_attn_sample")
    qx_s8 = jnp.pad(qx_s.astype(F32).reshape(nbs, s_new, dx), ((0, 0), (0, SUBLANES - s_new), (0, 0)))
    o_s = _xattn(qx_s8, cache_mem_k[l].reshape(nbs, nmem * xh, xd), cache_mem_v[l].reshape(nbs, nmem * xh, xd),
                 tq=SUBLANES, xh=xh, nbb=4 if nbs % 4 == 0 else 1, name="xattn_sample")[:, :s_new].reshape(ts, dx)
    h2_s, xn_s, rt_s, rr_s, cnt2 = _pre_moe(h1_s, o_s, wxo, g_f, wrh, wrl, cnt1, tm=ts, ngroups=ngroups, epg=epg,
                                            name="pre_moe_sample")

    tall = tp + ts
    counts = cnt2[0:nexp, 0].astype(jnp.int32)
    padded = (counts + EXPERT_ROWS - 1) // EXPERT_ROWS * EXPERT_ROWS
    pend = jnp.cumsum(padded).astype(jnp.int32)
    pstart = pend - padded
    nblk = (tall * TOP_K + nexp * (EXPERT_ROWS - 1)) // EXPERT_ROWS
    n_rows = nblk * EXPERT_ROWS
    blk_row0 = jnp.arange(nblk, dtype=jnp.int32) * EXPERT_ROWS
    blk_e = jnp.minimum(jnp.sum((pend[None, :] <= blk_row0[:, None]).astype(jnp.int32), axis=1), nexp - 1)
    nused = (pend[nexp - 1:nexp] // EXPERT_ROWS).astype(jnp.int32)
    rt_all = jnp.concatenate([rt_p, rt_s], axis=1)
    pstart_col = jnp.zeros((LANES, LANES), F32).at[0:nexp, :].set(pstart.astype(F32)[:, None])
    dest_flat = _dest(rt_all, pstart_col)[0:TOP_K].reshape(-1)
    xsg = _scatter(dest_flat, pstart, pend, xn_p, xn_s, n_rows=n_rows, tm_p=tm_p, nexp=nexp, nc=dm // (2 * LANES))
    y = _experts(blk_e, nused, xsg, w_expert_gate[l], w_expert_up[l], w_expert_down[l])
    g_fin = norm_final[None, :]
    y_p = _combine(dest_flat, h2_p, rr_p, g_fin, y, tm=tm_p, tok0=0, name="combine_prompt")
    y_s = _combine(dest_flat, h2_s, rr_s, g_fin, y, tm=ts, tok0=tp, name="combine_sample")

    return (y_p.reshape(nbp, seq, dm), y_s.reshape(nbs, s_new, dm),
            ulast_p[None],
            jnp.transpose(kt_p.reshape(nbp, heads, dh, seq), (0, 3, 1, 2))[None],
            jnp.transpose(vt_p.reshape(nbp, heads, dh, seq), (0, 3, 1, 2))[None],
            jnp.transpose(lf_p, (0, 2, 1))[None],
            mk_p.reshape(1, nbp, nmem, xh, xd), mv_p.reshape(1, nbp, nmem, xh, xd),
            u_s.reshape(nbs, s_new, dc)[None, :, s_new - 2:], kf_s.reshape(1, nbs, s_new, heads, dh),
            vf_s.reshape(1, nbs, s_new, heads, dh), lf_s[:, :heads].reshape(1, nbs, s_new, heads))
```

```python
import functools

import numpy as np
import jax
import jax.numpy as jnp
from jax import lax
from jax.experimental import pallas as pl
from jax.experimental.pallas import tpu as pltpu

F32 = jnp.float32
BF16 = jnp.bfloat16

RMS_EPS = 1e-6
TOP_K = 2
LANES = 128
SUBLANES = 8
VMEM_LIMIT = 56 * 1024 * 1024
NEG_BIG = -1e30
EXPERT_ROWS = 256
FUSED_PAGES_PER_ITERATION = 8
FUSED_RING = 3
FOX_QUERY_BLOCK = 512
FOX_KEY_BLOCK = 512
ROUTE_ROWS = 8
EXPERT_ROW0 = 8


def _cparams(sem, vmem=VMEM_LIMIT):
    return pltpu.CompilerParams(dimension_semantics=sem, vmem_limit_bytes=vmem)


def _rms(x, g):
    ms = jnp.mean(x * x, axis=-1, keepdims=True)
    return x * lax.rsqrt(ms + RMS_EPS) * g


def _split3(x):
    hi = x.astype(BF16)
    r = x - hi.astype(F32)
    mid = r.astype(BF16)
    lo = (r - mid.astype(F32)).astype(BF16)
    return hi, mid, lo


def _dot(a, b):
    return jnp.dot(a, b, preferred_element_type=F32)


def _dot_nt(a, b):
    return lax.dot_general(a, b, (((1,), (1,)), ((), ())), preferred_element_type=F32)


def _lane_tile(x, width):
    if width % LANES == 0:
        return jnp.concatenate([x] * (width // LANES), axis=1)
    return jnp.broadcast_to(x[:, 0:1], (x.shape[0], width))


def _const_spec(shape):
    nd = len(shape)
    return pl.BlockSpec(shape, lambda *_: (0,) * nd)


def _inproj_kernel(*refs, tm, seq_blocks, sample, seq_len):
    if sample:
        (x_ref, gain_ref, wc_ref, wqkv_ref, wg_ref, wf2_ref, bf_ref, cw_ref, cb_ref, fix1_ref, fix2_ref,
         gc_ref, q_ref, kbf_ref, vbf_ref, kf_ref, vf_ref, g_ref, logf_ref, dt_ref, u_ref) = refs
    else:
        (x_ref, gain_ref, wc_ref, wqkv_ref, wg_ref, wf2_ref, bf_ref, cw_ref, cb_ref,
         selq_ref, selk_ref, oneq_ref, onek_ref,
         gc_ref, qa_ref, ka_ref, vbf_ref, kf_ref, vf_ref, g_ref, logf_ref, u_ref,
         carry_u, carry_d) = refs
    i = pl.program_id(0)
    dc = cw_ref.shape[1]
    da = wqkv_ref.shape[1] // 3

    xn = _rms(x_ref[...], gain_ref[...])
    xb = xn.astype(BF16)

    cb = _dot(xb, wc_ref[:, 0:dc])
    cc = _dot(xb, wc_ref[:, dc:2 * dc])
    cx = _dot(xb, wc_ref[:, 2 * dc:3 * dc])
    u = cc * cx
    row = lax.broadcasted_iota(jnp.int32, (tm, 1), 0)
    r1 = pltpu.roll(u, 1, axis=0)
    r2 = pltpu.roll(u, 2, axis=0)
    if sample:
        pos = row % seq_len
        p1 = jnp.where(pos == 0, fix1_ref[...], r1)
        p2 = jnp.where(pos < 2, fix2_ref[...], r2)
        u_ref[...] = u
    else:
        @pl.when(i % seq_blocks == 0)
        def _():
            carry_u[...] = jnp.zeros_like(carry_u)
            carry_d[...] = jnp.zeros_like(carry_d)
        c0 = carry_u[0:1, :]
        c1 = carry_u[1:2, :]
        p1 = jnp.where(row == 0, c1, r1)
        p2 = jnp.where(row == 0, c0, jnp.where(row == 1, c1, r2))
        carry_u[0:2, :] = u[tm - 2:tm, :]
        u_ref[...] = u[tm - 2:tm, :]
    cw = cw_ref[...]
    cy = cb_ref[...] + cw[0:1, :] * p2 + cw[1:2, :] * p1 + cw[2:3, :] * u
    gc_ref[...] = (cb * cy).astype(BF16)

    qs = (_dot(xb, wqkv_ref[:, 0:da]) * (1.0 / 8.0)).astype(BF16)
    k = _dot(xb, wqkv_ref[:, da:2 * da])
    kb = k.astype(BF16)
    v = _dot(xb, wqkv_ref[:, 2 * da:3 * da])
    vbf_ref[...] = v.astype(BF16)
    if sample:
        q_ref[...] = qs
        kbf_ref[...] = kb
        kf_ref[...] = k
        vf_ref[...] = v
    else:
        kf_ref[...] = jnp.transpose(k)
        vf_ref[...] = jnp.transpose(v)

    gw = g_ref.shape[1]
    for c in range(gw // 512):
        g_ref[:, c * 512:(c + 1) * 512] = _dot(xb, wg_ref[:, c * 512:(c + 1) * 512]).astype(BF16)

    xl = (xn - xb.astype(F32)).astype(BF16)
    hh_hl = _dot(xb, wf2_ref[...])
    fz = hh_hl[:, 0:LANES] + (hh_hl[:, LANES:2 * LANES] + _dot(xl, wf2_ref[:, 0:LANES]))
    z = fz + bf_ref[...]
    logf = jnp.minimum(z, 0.0) - jnp.log1p(jnp.exp(-jnp.abs(z)))
    lane = lax.broadcasted_iota(jnp.int32, (1, LANES), 1)
    nh = da // 64
    logf = jnp.where(lane < nh, logf, 0.0)
    if sample:
        logf_ref[...] = logf
    else:
        logf_ref[...] = jnp.transpose(logf)[0:SUBLANES, :]

    rr = lax.broadcasted_iota(jnp.int32, (tm, tm), 0)
    cc_ = lax.broadcasted_iota(jnp.int32, (tm, tm), 1)
    if sample:
        tri = (cc_ <= rr) & ((rr // seq_len) == (cc_ // seq_len))
    else:
        tri = cc_ <= rr
    tri = jnp.where(tri, 1.0, 0.0).astype(BF16)
    d3 = _dot(tri, jnp.concatenate(_split3(logf), axis=1))
    d = d3[:, 0:LANES] + (d3[:, LANES:2 * LANES] + d3[:, 2 * LANES:3 * LANES])
    if sample:
        dt_ref[...] = jnp.transpose(d)[0:SUBLANES, :]
    else:
        d = d + carry_d[0:1, :]
        carry_d[0:1, :] = d[tm - 1:tm, :]
        dcat = jnp.concatenate(_split3(d), axis=1)
        aq = (_dot(dcat, selq_ref[...]) + oneq_ref[...]).astype(BF16)
        ak = (_dot(dcat, selk_ref[...]) + onek_ref[...]).astype(BF16)
        for p in range(da // LANES):
            lo_, hi_ = p * LANES, (p + 1) * LANES
            qa_ref[:, 2 * lo_:2 * lo_ + LANES] = qs[:, lo_:hi_]
            qa_ref[:, 2 * lo_ + LANES:2 * hi_] = aq[:, lo_:hi_]
            ka_ref[:, 2 * lo_:2 * lo_ + LANES] = kb[:, lo_:hi_]
            ka_ref[:, 2 * lo_ + LANES:2 * hi_] = ak[:, lo_:hi_]


def _decay_columns(heads):
    npair = heads // 2
    selq = np.zeros((3 * LANES, npair * LANES), np.float32)
    selk = np.zeros((3 * LANES, npair * LANES), np.float32)
    oneq = np.zeros((1, npair * LANES), np.float32)
    onek = np.zeros((1, npair * LANES), np.float32)
    for p in range(npair):
        for hh in range(2):
            for term in range(3):
                selq[term * LANES + 2 * p + hh, p * LANES + 3 * hh + term] = 1.0
                selk[term * LANES + 2 * p + hh, p * LANES + 6 + 3 * hh + term] = -1.0
                oneq[0, p * LANES + 6 + 3 * hh + term] = 1.0
                onek[0, p * LANES + 3 * hh + term] = 1.0
    return jnp.asarray(selq, BF16), jnp.asarray(selk, BF16), jnp.asarray(oneq), jnp.asarray(onek)


def _inproj(x, gain, wc, wqkv, wg, wf2, bfp, cw, cbias, *, seq_len, sample, fix=None, aug=None):
    t, dm = x.shape
    dc = cw.shape[1]
    da = wqkv.shape[1] // 3
    gw = wg.shape[1]
    if sample:
        tm = t
        seq_blocks = 1
    else:
        tm = min(512, seq_len)
        seq_blocks = seq_len // tm
    nblk = t // tm
    nseq = t // seq_len
    kern = functools.partial(_inproj_kernel, tm=tm, seq_blocks=seq_blocks, sample=sample, seq_len=seq_len)
    rows = lambda w: pl.BlockSpec((tm, w), lambda i: (i, 0))
    in_specs = [rows(dm), _const_spec((1, dm)), _const_spec(wc.shape), _const_spec(wqkv.shape), _const_spec(wg.shape),
                _const_spec(wf2.shape), _const_spec((1, LANES)), _const_spec(cw.shape), _const_spec((1, dc))]
    args = [x, gain, wc, wqkv, wg, wf2, bfp, cw, cbias]
    sds = jax.ShapeDtypeStruct
    if sample:
        in_specs += [rows(dc), rows(dc)]
        args += list(fix)
        qk_shapes = [sds((t, da), BF16), sds((t, da), BF16)]
        qk_specs = [rows(da), rows(da)]
        kv_shapes = [sds((t, da), F32), sds((t, da), F32)]
        kv_specs = [rows(da), rows(da)]
        tail_shapes = [sds((t, LANES), F32), sds((SUBLANES, t), F32), sds((t, dc), F32)]
        tail_specs = [rows(LANES), pl.BlockSpec((SUBLANES, tm), lambda i: (0, i)), rows(dc)]
        scratch = []
    else:
        in_specs += [_const_spec(a.shape) for a in aug]
        args += list(aug)
        qk_shapes = [sds((t, 2 * da), BF16), sds((t, 2 * da), BF16)]
        qk_specs = [rows(2 * da), rows(2 * da)]
        kv_shapes = [sds((nseq, da, seq_len), F32), sds((nseq, da, seq_len), F32)]
        kv_specs = [pl.BlockSpec((None, da, tm), lambda i: (i // seq_blocks, 0, i % seq_blocks))] * 2
        tail_shapes = [sds((nseq, SUBLANES, seq_len), F32), sds((nseq, 2, dc), F32)]
        tail_specs = [pl.BlockSpec((None, SUBLANES, tm), lambda i: (i // seq_blocks, 0, i % seq_blocks)),
                      pl.BlockSpec((None, 2, dc), lambda i: (i // seq_blocks, 0, 0))]
        scratch = [pltpu.VMEM((SUBLANES, dc), F32), pltpu.VMEM((SUBLANES, LANES), F32)]
    out_shape = ([sds((t, dc), BF16)] + qk_shapes +
                 [sds((t, da), BF16)] + kv_shapes +
                 [sds((t, gw), BF16)] + tail_shapes)
    out_specs = [rows(dc)] + qk_specs + [rows(da)] + kv_specs + [rows(gw)] + tail_specs
    return pl.pallas_call(
        kern, grid=(nblk,), in_specs=in_specs, out_specs=out_specs, out_shape=out_shape,
        scratch_shapes=scratch, compiler_params=_cparams(("arbitrary",)),
        name="inproj_sample" if sample else "inproj_prompt")(*args)


def _fox_fused_kernel(pt_ref, qa_ref, ka_ref, v_ref, kt_hbm, vt_hbm, lf_hbm, qbd_ref, kn_ref, vn_ref, dtn_ref,
                      o_ref, os_ref,
                      m_sc, l_sc, acc_sc, ms_sc, ls_sc, as_sc, run_sc, kbuf, vbuf, lbuf, sem, g_ref,
                      *, tq, tk, dh, pp, ring, npages, nchunks, s_new, heads):
    first_step = (pl.program_id(0) == 0) & (pl.program_id(1) == 0) & (pl.program_id(2) == 0)
    qi = pl.program_id(2)
    page = kbuf.shape[3]
    cpb = npages // pp
    nrow = s_new * heads
    da = heads * dh
    nbs = qbd_ref.shape[0]

    def chunk_copies(c):
        slot = c % ring
        bs = c // cpb
        jc = c % cpb
        copies = []
        for i in range(pp):
            pid = pt_ref[bs * npages + (npages - 1 - (jc * pp + i))]
            copies.append(pltpu.make_async_copy(kt_hbm.at[pid], kbuf.at[slot, i], sem.at[slot]))
            copies.append(pltpu.make_async_copy(vt_hbm.at[pid], vbuf.at[slot, i], sem.at[slot]))
            copies.append(pltpu.make_async_copy(lf_hbm.at[pid], lbuf.at[slot, i], sem.at[slot]))
        return copies

    @pl.when(first_step)
    def _():
        g_ref[0] = 0
        for c in range(min(ring - 1, nchunks)):
            for cp in chunk_copies(c):
                cp.start()

    lane = lax.broadcasted_iota(jnp.int32, (1, 2 * LANES), 1)
    ext = lane - LANES
    qf = qa_ref[...].astype(F32)
    halves = []
    for h in range(2):
        keep = (((lane >= h * dh) & (lane < (h + 1) * dh))
                | ((ext >= 3 * h) & (ext < 3 * h + 3)) | ((ext >= 6 + 3 * h) & (ext < 9 + 3 * h)))
        halves.append(jnp.where(keep, qf, 0.0))
    qs = jnp.concatenate(halves, axis=0).astype(BF16)
    m_sc[...] = jnp.full_like(m_sc, NEG_BIG)
    l_sc[...] = jnp.zeros_like(l_sc)
    acc_sc[...] = jnp.zeros_like(acc_sc)
    nfull = (qi * tq) // tk

    def prompt_step(j, masked):
        ks = pl.multiple_of(j * tk, tk)
        kb = ka_ref[pl.ds(ks, tk), :]
        vb = v_ref[pl.ds(ks, tk), :]
        s = _dot_nt(qs, kb)
        if masked:
            rloc = lax.broadcasted_iota(jnp.int32, (2 * tq, tk), 0)
            rloc = jnp.where(rloc >= tq, rloc - tq, rloc) + qi * tq
            cloc = lax.broadcasted_iota(jnp.int32, (2 * tq, tk), 1) + ks
            s = jnp.where(cloc <= rloc, s, NEG_BIG)
        m_prev = m_sc[...]
        m_new = jnp.maximum(m_prev, jnp.max(s, axis=1, keepdims=True))
        p = jnp.exp(s - _lane_tile(m_new, tk))
        alpha = jnp.exp(m_prev - m_new)
        l_sc[...] = alpha * l_sc[...] + jnp.sum(p, axis=1, keepdims=True)
        acc_sc[...] = alpha * acc_sc[...] + _dot(p.astype(BF16), vb)
        m_sc[...] = m_new

    def sample_update(s, pv_fn):
        m_prev = ms_sc[...]
        m_new = jnp.maximum(m_prev, jnp.max(s, axis=1, keepdims=True))
        p = jnp.exp(s - _lane_tile(m_new, s.shape[1]))
        alpha = jnp.exp(m_prev - m_new)
        ls_sc[...] = alpha * ls_sc[...] + jnp.sum(p, axis=1, keepdims=True)
        as_sc[...] = _lane_tile(alpha, da) * as_sc[...] + pv_fn(p.astype(BF16))
        ms_sc[...] = m_new

    def sample_chunk(g, valid):
        slot = g % ring
        bs = jnp.minimum(g // cpb, nbs - 1)
        qbd = qbd_ref[bs]
        rr = lax.broadcasted_iota(jnp.int32, (page, 2 * page), 0)
        cc = lax.broadcasted_iota(jnp.int32, (page, 2 * page), 1)
        after = jnp.where((rr > cc) | (cc >= page), 1.0, 0.0).astype(BF16)
        lf = jnp.concatenate([lbuf[slot, i] for i in range(pp)], axis=0)
        n8 = pp * heads
        r3 = _dot(jnp.concatenate(_split3(lf), axis=0), after)
        both = r3[0:n8] + (r3[n8:2 * n8] + r3[2 * n8:3 * n8])
        run = run_sc[...]
        scores = []
        for i in range(pp):
            rev = both[i * heads:(i + 1) * heads, 0:page] + run
            run = run + both[i * heads:(i + 1) * heads, page:2 * page]
            scores.append(_dot(qbd, kbuf[slot, i].astype(BF16)) + jnp.concatenate([rev] * s_new, axis=0))
        run_sc[...] = run
        s = jnp.where(valid, jnp.concatenate(scores, axis=1), NEG_BIG)

        def pv_pages(p):
            acc = None
            for i in range(pp):
                term = _dot_nt(p[:, i * page:(i + 1) * page], vbuf[slot, i].astype(BF16))
                acc = term if acc is None else acc + term
            return acc

        sample_update(s, pv_pages)

    def sample_finish(bs):
        ntok = kn_ref.shape[0]
        qbd = qbd_ref[bs]
        dtn = dtn_ref[...]
        lane_t = lax.broadcasted_iota(jnp.int32, (1, ntok), 1)
        dq_rows = [jnp.sum(jnp.where(lane_t == bs * s_new + t, dtn, 0.0), axis=1, keepdims=True)
                   for t in range(s_new)]
        dq = jnp.concatenate(dq_rows, axis=0)
        dk = jnp.concatenate([dtn] * s_new, axis=0)
        s = _dot_nt(qbd, kn_ref[...]) + (dq - dk)
        rowt = lax.broadcasted_iota(jnp.int32, (nrow, ntok), 0) // heads
        col = lax.broadcasted_iota(jnp.int32, (nrow, ntok), 1)
        keep = ((col // s_new) == bs) & ((col % s_new) <= rowt)
        sample_update(jnp.where(keep, s, NEG_BIG), lambda p: _dot(p, vn_ref[...]))
        o = as_sc[...] / _lane_tile(ls_sc[...], da)
        rowh = lax.broadcasted_iota(jnp.int32, (nrow, da), 0) % heads
        colh = lax.broadcasted_iota(jnp.int32, (nrow, da), 1) // dh
        o = jnp.where(rowh == colh, o, 0.0)
        os_ref[bs] = jnp.sum(o.reshape(s_new, heads, da), axis=1)

    def iteration(j, masked):
        g = g_ref[0]
        valid = g < nchunks

        @pl.when(g + (ring - 1) < nchunks)
        def _():
            for cp in chunk_copies(g + (ring - 1)):
                cp.start()

        @pl.when(valid)
        def _():
            for cp in chunk_copies(g):
                cp.wait()

        @pl.when(valid & (g % cpb == 0))
        def _():
            ms_sc[...] = jnp.full_like(ms_sc, NEG_BIG)
            ls_sc[...] = jnp.zeros_like(ls_sc)
            as_sc[...] = jnp.zeros_like(as_sc)
            run_sc[...] = jnp.zeros_like(run_sc)

        prompt_step(j, masked)
        sample_chunk(g, valid)

        @pl.when(valid & (g % cpb == cpb - 1))
        def _():
            sample_finish(g // cpb)

        g_ref[0] = g + 1

    def body(j, c):
        iteration(j, False)
        return c

    lax.fori_loop(0, nfull, body, 0)
    iteration(nfull, True)
    o = acc_sc[...] / l_sc[...]
    lane_o = lax.broadcasted_iota(jnp.int32, (1, LANES), 1)
    o_ref[...] = jnp.where(lane_o < dh, o[0:tq], o[tq:2 * tq]).astype(o_ref.dtype)


def _fox_fused(page_table_flat, qa, ka, v, kt_pages, vt_pages, lf_pages, qbd, kn, vn, dtn,
               *, nseq, seq_len, dh, npages, s_new, heads):
    t, da = v.shape
    nbs, nrow, _ = qbd.shape
    page = kt_pages.shape[2]
    ntok = kn.shape[0]
    tk = min(FOX_KEY_BLOCK, seq_len)
    tq = min(FOX_QUERY_BLOCK, tk)
    nq = seq_len // tq
    npair = da // LANES
    pp = min(FUSED_PAGES_PER_ITERATION, npages)
    nchunks = nbs * (npages // pp)
    iters = nseq * npair * sum((qi * tq) // tk + 1 for qi in range(nq))
    assert npages % pp == 0 and nchunks <= iters, "the page stream must fit in the prompt attention's iterations"
    kern = functools.partial(_fox_fused_kernel, tq=tq, tk=tk, dh=dh, pp=pp, ring=FUSED_RING, npages=npages,
                             nchunks=nchunks, s_new=s_new, heads=heads)
    const = lambda shape: pl.BlockSpec(shape, lambda b, hp, qi, pt: (0,) * len(shape))
    grid_spec = pltpu.PrefetchScalarGridSpec(
        num_scalar_prefetch=1, grid=(nseq, npair, nq),
        in_specs=[pl.BlockSpec((tq, 2 * LANES), lambda b, hp, qi, pt: (b * nq + qi, hp)),
                  pl.BlockSpec((seq_len, 2 * LANES), lambda b, hp, qi, pt: (b, hp)),
                  pl.BlockSpec((seq_len, LANES), lambda b, hp, qi, pt: (b, hp)),
                  pl.BlockSpec(memory_space=pl.ANY), pl.BlockSpec(memory_space=pl.ANY),
                  pl.BlockSpec(memory_space=pl.ANY),
                  const((nbs, nrow, da)), const((ntok, da)), const((ntok, da)), const((SUBLANES, ntok))],
        out_specs=[pl.BlockSpec((tq, LANES), lambda b, hp, qi, pt: (b * nq + qi, hp)),
                   const((nbs, s_new, da))],
        scratch_shapes=[pltpu.VMEM((2 * tq, LANES), F32), pltpu.VMEM((2 * tq, LANES), F32),
                        pltpu.VMEM((2 * tq, LANES), F32),
                        pltpu.VMEM((nrow, LANES), F32), pltpu.VMEM((nrow, LANES), F32), pltpu.VMEM((nrow, da), F32),
                        pltpu.VMEM((heads, LANES), F32),
                        pltpu.VMEM((FUSED_RING, pp, da, page), F32), pltpu.VMEM((FUSED_RING, pp, da, page), F32),
                        pltpu.VMEM((FUSED_RING, pp, heads, page), F32),
                        pltpu.SemaphoreType.DMA((FUSED_RING,)), pltpu.SMEM((1,), jnp.int32)])
    return pl.pallas_call(
        kern, grid_spec=grid_spec,
        out_shape=[jax.ShapeDtypeStruct((t, da), BF16), jax.ShapeDtypeStruct((nbs, s_new, da), F32)],
        compiler_params=_cparams(("arbitrary", "arbitrary", "arbitrary")),
        name="fox_fused")(page_table_flat, qa, ka, v, kt_pages, vt_pages, lf_pages, qbd, kn, vn, dtn)


def _post_attn_math(x_ref, gc_ref, at_ref, g_ref, wuc_ref, wua_ref, wmo_ref, gain_ref, wxq_ref):
    dm = x_ref.shape[1]
    y_conv = _dot(gc_ref[...], wuc_ref[...])
    y_attn = _dot(at_ref[...].astype(BF16), wua_ref[...])
    g_conv = g_ref[:, 0:dm].astype(F32)
    g_attn = g_ref[:, dm:2 * dm].astype(F32)
    mixed = jax.nn.sigmoid(g_conv) * y_conv + jax.nn.sigmoid(g_attn) * y_attn
    h = x_ref[...] + _dot(mixed.astype(BF16), wmo_ref[...])
    xn = _rms(h, gain_ref[...]).astype(BF16)
    return h, _dot(xn, wxq_ref[...]).astype(BF16)


def _post_attn_kernel(x_ref, gc_ref, at_ref, g_ref, wuc_ref, wua_ref, wmo_ref, gain_ref, wxq_ref, h_ref, qx_ref):
    h, qx = _post_attn_math(x_ref, gc_ref, at_ref, g_ref, wuc_ref, wua_ref, wmo_ref, gain_ref, wxq_ref)
    h_ref[...] = h
    qx_ref[...] = qx.astype(qx_ref.dtype)


def _post_attn(x, gc, at, g, wuc, wua, wmo, gain, wxq, *, tm, name):
    t, dm = x.shape
    dx = wxq.shape[1]
    rows = lambda w: pl.BlockSpec((tm, w), lambda i: (i, 0))
    return pl.pallas_call(
        _post_attn_kernel, grid=(t // tm,),
        in_specs=[rows(dm), rows(gc.shape[1]), rows(at.shape[1]), rows(g.shape[1]), _const_spec(wuc.shape),
                  _const_spec(wua.shape), _const_spec(wmo.shape), _const_spec((1, dm)), _const_spec(wxq.shape)],
        out_specs=[rows(dm), rows(dx)],
        out_shape=[jax.ShapeDtypeStruct((t, dm), F32), jax.ShapeDtypeStruct((t, dx), BF16)],
        compiler_params=_cparams(("parallel",)), name=name)(x, gc, at, g, wuc, wua, wmo, gain, wxq)


def _memkv_kernel(m_ref, gain_ref, wk_ref, wv_ref, k_ref, v_ref, *, xh):
    mn = _rms(m_ref[...], gain_ref[...]).astype(BF16)
    tm = m_ref.shape[0]
    xd = wk_ref.shape[1] // xh
    k = _dot(mn, wk_ref[...])
    v = _dot(mn, wv_ref[...])
    for h in range(xh):
        k_ref[pl.ds(h, tm, stride=xh), :] = k[:, h * xd:(h + 1) * xd]
        v_ref[pl.ds(h, tm, stride=xh), :] = v[:, h * xd:(h + 1) * xd]


def _memkv(mem, gain, wk, wv, *, xh):
    t, dm = mem.shape
    tm = min(512, t)
    xd = wk.shape[1] // xh
    rows = lambda w: pl.BlockSpec((tm, w), lambda i: (i, 0))
    return pl.pallas_call(
        functools.partial(_memkv_kernel, xh=xh), grid=(t // tm,),
        in_specs=[rows(dm), _const_spec((1, dm)), _const_spec(wk.shape), _const_spec(wv.shape)],
        out_specs=[pl.BlockSpec((tm * xh, xd), lambda i: (i, 0))] * 2,
        out_shape=[jax.ShapeDtypeStruct((t * xh, xd), F32)] * 2,
        compiler_params=_cparams(("parallel",)), name="memkv")(mem, gain, wk, wv)


def _xattn_math(q, mk_ref, mv_ref, xh):
    nm = mk_ref.shape[0] // xh
    xd = mk_ref.shape[1]
    scale = xd ** -0.5
    outs = []
    for h in range(xh):
        mk = mk_ref[pl.ds(h, nm, stride=xh), :].astype(BF16)
        mv = mv_ref[pl.ds(h, nm, stride=xh), :].astype(BF16)
        s = _dot_nt(q[:, h * xd:(h + 1) * xd], mk) * scale
        m = jnp.max(s, axis=1, keepdims=True)
        p = jnp.exp(s - m)
        p = p / jnp.sum(p, axis=1, keepdims=True)
        outs.append(_dot(p.astype(BF16), mv))
    return jnp.concatenate(outs, axis=1)


def _xattn_kernel(q_ref, mk_ref, mv_ref, o_ref, *, xh):
    for b in range(q_ref.shape[0]):
        o_ref[b] = _xattn_math(q_ref[b].astype(BF16), mk_ref.at[b], mv_ref.at[b], xh).astype(o_ref.dtype)


def _xattn(q, mk, mv, *, tq, xh, nbb, name):
    nb, s, dx = q.shape
    nm, xd = mk.shape[1:]
    kern = functools.partial(_xattn_kernel, xh=xh)
    return pl.pallas_call(
        kern, grid=(nb // nbb, s // tq),
        in_specs=[pl.BlockSpec((nbb, tq, dx), lambda b, i: (b, i, 0)),
                  pl.BlockSpec((nbb, nm, xd), lambda b, i: (b, 0, 0)),
                  pl.BlockSpec((nbb, nm, xd), lambda b, i: (b, 0, 0))],
        out_specs=pl.BlockSpec((nbb, tq, dx), lambda b, i: (b, i, 0)),
        out_shape=jax.ShapeDtypeStruct((nb, s, dx), q.dtype),
        compiler_params=_cparams(("parallel", "parallel")), name=name)(q, mk, mv)


def _pre_moe_kernel(h_ref, o_ref, wxo_ref, gain_ref, wrh_ref, wrl_ref, cnt_in_ref,
                    h2_ref, xn_ref, rt_ref, rr_ref, cnt_out_ref, base, *, tm, ngroups, epg):
    @pl.when(pl.program_id(0) == 0)
    def _():
        base[...] = cnt_in_ref[...]

    _pre_moe_math(h_ref[...], o_ref[...].astype(BF16), wxo_ref, gain_ref, wrh_ref, wrl_ref,
                  h2_ref, xn_ref, rt_ref, rr_ref, cnt_out_ref, base, tm=tm, ngroups=ngroups, epg=epg)


def _mid_kernel(x_ref, gc_ref, at_ref, g_ref, mk_ref, mv_ref, wuc_ref, wua_ref, wmo_ref, gx_ref, wxq_ref,
                wxo_ref, gf_ref, wrh_ref, wrl_ref, cnt_in_ref,
                h2_ref, xn_ref, rt_ref, rr_ref, cnt_out_ref, base, *, tm, ngroups, epg, xh):
    @pl.when(pl.program_id(0) == 0)
    def _():
        base[...] = cnt_in_ref[...]

    h1, qx = _post_attn_math(x_ref, gc_ref, at_ref, g_ref, wuc_ref, wua_ref, wmo_ref, gx_ref, wxq_ref)
    o = _xattn_math(qx, mk_ref, mv_ref, xh).astype(BF16)
    _pre_moe_math(h1, o, wxo_ref, gf_ref, wrh_ref, wrl_ref,
                  h2_ref, xn_ref, rt_ref, rr_ref, cnt_out_ref, base, tm=tm, ngroups=ngroups, epg=epg)


def _pre_moe_math(h, o, wxo_ref, gain_ref, wrh_ref, wrl_ref, h2_ref, xn_ref, rt_ref, rr_ref, cnt_out_ref, base,
                  *, tm, ngroups, epg):
    h2 = h + _dot(o, wxo_ref[...])
    h2_ref[...] = h2
    xn = _rms(h2, gain_ref[...])
    xw = _pack_bf16_pairs(xn)
    nc = xw.shape[1] // LANES
    for c in range(nc):
        xn_ref[pl.ds(c, tm, stride=nc), :] = xw[:, c * LANES:(c + 1) * LANES]

    xh = xn.astype(BF16)
    xl = (xn - xh.astype(F32)).astype(BF16)
    lt = _dot_nt(wrh_ref[...], xh) + (_dot_nt(wrh_ref[...], xl) + _dot_nt(wrl_ref[...], xh))

    sub = lax.broadcasted_iota(jnp.int32, (SUBLANES, tm), 0)
    gl = jnp.where(sub < ngroups, lt[0:SUBLANES, :], -jnp.inf)
    gmax = jnp.max(gl, axis=0, keepdims=True)
    gidx = jnp.min(jnp.where(gl == gmax, sub, SUBLANES), axis=0, keepdims=True)
    pg = 1.0 / jnp.sum(jnp.exp(gl - gmax), axis=0, keepdims=True)
    el = jnp.zeros((epg, tm), F32)
    for g in range(ngroups):
        el = jnp.where(gidx == g, lt[EXPERT_ROW0 + g * epg:EXPERT_ROW0 + (g + 1) * epg, :], el)
    v1 = jnp.max(el, axis=0, keepdims=True)
    i1 = jnp.min(jnp.where(el == v1, sub, epg), axis=0, keepdims=True)
    el2 = jnp.where(sub == i1, -jnp.inf, el)
    v2 = jnp.max(el2, axis=0, keepdims=True)
    i2 = jnp.min(jnp.where(el2 == v2, sub, epg), axis=0, keepdims=True)
    t2 = jnp.exp(v2 - v1)
    den = 1.0 + t2
    w0 = (1.0 / den) * pg
    w1 = (t2 / den) * pg
    e0 = gidx * epg + i1
    e1 = gidx * epg + i2

    erow = lax.broadcasted_iota(jnp.int32, (LANES, tm), 0)
    oh0 = erow == e0
    oh1 = erow == e1
    rr = lax.broadcasted_iota(jnp.int32, (tm, tm), 0)
    cc = lax.broadcasted_iota(jnp.int32, (tm, tm), 1)
    triu = jnp.where(rr <= cc, 1.0, 0.0).astype(BF16)
    pre0 = _dot(jnp.where(oh0, 1.0, 0.0).astype(BF16), triu)
    pre1 = _dot(jnp.where(oh1, 1.0, 0.0).astype(BF16), triu)
    b0 = base[:, 0:1]
    tot0 = pre0[:, tm - 1:tm]
    tot1 = pre1[:, tm - 1:tm]
    rank0 = jnp.sum(jnp.where(oh0, pre0 - 1.0 + b0, 0.0), axis=0, keepdims=True)
    rank1 = jnp.sum(jnp.where(oh1, pre1 - 1.0 + (b0 + tot0), 0.0), axis=0, keepdims=True)
    newb = b0 + tot0 + tot1
    base[...] = jnp.broadcast_to(newb, base.shape)
    cnt_out_ref[...] = jnp.broadcast_to(newb, cnt_out_ref.shape)

    zero = jnp.zeros((1, tm), F32)
    rt = jnp.concatenate([e0.astype(F32), e1.astype(F32), w0, w1, rank0, rank1, zero, zero], axis=0)
    rt_ref[...] = rt
    rt_pad = jnp.concatenate([rt, jnp.zeros((LANES - ROUTE_ROWS, tm), F32)], axis=0)
    rr_ref[...] = jnp.transpose(rt_pad)


def _pre_moe(h, o, wxo, gain, wrh, wrl, cnt_in, *, tm, ngroups, epg, name):
    t, dm = h.shape
    dx = o.shape[1]
    kern = functools.partial(_pre_moe_kernel, tm=tm, ngroups=ngroups, epg=epg)
    rows = lambda w: pl.BlockSpec((tm, w), lambda i: (i, 0))
    return pl.pallas_call(
        kern, grid=(t // tm,),
        in_specs=[rows(dm), rows(dx), _const_spec(wxo.shape), _const_spec((1, dm)), _const_spec(wrh.shape),
                  _const_spec(wrl.shape), _const_spec((LANES, LANES))],
        out_specs=[rows(dm), pl.BlockSpec((tm * (dm // (2 * LANES)), LANES), lambda i: (i, 0)),
                   pl.BlockSpec((ROUTE_ROWS, tm), lambda i: (0, i)), rows(LANES), _const_spec((LANES, LANES))],
        out_shape=[jax.ShapeDtypeStruct((t, dm), F32), jax.ShapeDtypeStruct((t * (dm // (2 * LANES)), LANES), jnp.uint32),
                   jax.ShapeDtypeStruct((ROUTE_ROWS, t), F32), jax.ShapeDtypeStruct((t, LANES), F32),
                   jax.ShapeDtypeStruct((LANES, LANES), F32)],
        scratch_shapes=[pltpu.VMEM((LANES, LANES), F32)],
        compiler_params=_cparams(("arbitrary",)), name=name)(h, o, wxo, gain, wrh, wrl, cnt_in)


def _mid(x, gc, at, g, mk, mv, wuc, wua, wmo, gx, wxq, wxo, gf, wrh, wrl, cnt_in, *, tm, seq_len, ngroups, epg, xh):
    t, dm = x.shape
    nm, xd = mk.shape[1:]
    seq_blocks = seq_len // tm
    kern = functools.partial(_mid_kernel, tm=tm, ngroups=ngroups, epg=epg, xh=xh)
    rows = lambda w: pl.BlockSpec((tm, w), lambda i: (i, 0))
    mem = pl.BlockSpec((None, nm, xd), lambda i: (i // seq_blocks, 0, 0))
    consts = [wuc, wua, wmo, gx, wxq, wxo, gf, wrh, wrl, cnt_in]
    return pl.pallas_call(
        kern, grid=(t // tm,),
        in_specs=[rows(dm), rows(gc.shape[1]), rows(at.shape[1]), rows(g.shape[1]), mem, mem]
        + [_const_spec(a.shape) for a in consts],
        out_specs=[rows(dm), pl.BlockSpec((tm * (dm // (2 * LANES)), LANES), lambda i: (i, 0)),
                   pl.BlockSpec((ROUTE_ROWS, tm), lambda i: (0, i)), rows(LANES), _const_spec((LANES, LANES))],
        out_shape=[jax.ShapeDtypeStruct((t, dm), F32), jax.ShapeDtypeStruct((t * (dm // (2 * LANES)), LANES), jnp.uint32),
                   jax.ShapeDtypeStruct((ROUTE_ROWS, t), F32), jax.ShapeDtypeStruct((t, LANES), F32),
                   jax.ShapeDtypeStruct((LANES, LANES), F32)],
        scratch_shapes=[pltpu.VMEM((LANES, LANES), F32)],
        compiler_params=_cparams(("arbitrary",)), name="mid_prompt")(x, gc, at, g, mk, mv, *consts)


def _row_copy(src, r_src, dst, r_dst, sem, nc):
    return pltpu.make_async_copy(src.at[pl.ds(pl.multiple_of(r_src * nc, nc), nc)],
                                 dst.at[pl.ds(pl.multiple_of(r_dst * nc, nc), nc)], sem)


def _slab_rows(ref, nc):
    rows = ref.shape[0] // nc
    return jnp.concatenate([ref[pl.ds(c, rows, stride=nc), :] for c in range(nc)], axis=1)


def _pack_bf16_pairs(x):
    half = x.shape[1] // 2
    lo = pltpu.bitcast(x[:, 0:half].astype(BF16).astype(F32), jnp.uint32)
    hi = pltpu.bitcast(x[:, half:2 * half].astype(BF16).astype(F32), jnp.uint32)
    return lax.shift_right_logical(lo, jnp.uint32(16)) | (hi & jnp.uint32(0xFFFF0000))


def _unpack_bf16_pairs(w):
    lo = pltpu.bitcast(lax.shift_left(w, jnp.uint32(16)), F32)
    hi = pltpu.bitcast(w & jnp.uint32(0xFFFF0000), F32)
    return jnp.concatenate([lo, hi], axis=1)


def _dest_kernel(rt_ref, pstart_ref, o_ref):
    rt = rt_ref[...]
    pst = pstart_ref[:, 0:1]
    erow = lax.broadcasted_iota(jnp.int32, (LANES, rt.shape[1]), 0)
    rows = []
    for k in range(TOP_K):
        e = rt[k:k + 1, :].astype(jnp.int32)
        rows.append(jnp.sum(jnp.where(erow == e, pst, 0.0), axis=0, keepdims=True) + rt[4 + k:5 + k, :])
    rows.append(jnp.zeros((ROUTE_ROWS - TOP_K, rt.shape[1]), F32))
    o_ref[...] = jnp.concatenate(rows, axis=0).astype(jnp.int32)


def _dest(rt_all, pstart_col):
    tall = rt_all.shape[1]
    nchunk = 3 if tall % (3 * LANES) == 0 else 1
    w = tall // nchunk
    return pl.pallas_call(
        _dest_kernel, grid=(nchunk,),
        in_specs=[pl.BlockSpec((ROUTE_ROWS, w), lambda i: (0, i)), _const_spec((LANES, LANES))],
        out_specs=pl.BlockSpec((ROUTE_ROWS, w), lambda i: (0, i)),
        out_shape=jax.ShapeDtypeStruct((ROUTE_ROWS, tall), jnp.int32),
        compiler_params=_cparams(("parallel",)), name="dest")(rt_all, pstart_col)


def _scatter_kernel(dest_ref, pstart_ref, pend_ref, xp_ref, xs_ref, out_ref, zeros, sem, zsem,
                    *, tm_p, nblk_p, nexp, nc):
    i = pl.program_id(0)
    tall = dest_ref.shape[0] // TOP_K

    @pl.when(i == 0)
    def _():
        zeros[...] = jnp.zeros_like(zeros)

        def zero_block(blk):
            start = pl.multiple_of(blk * (EXPERT_ROWS * nc), EXPERT_ROWS * nc)
            return pltpu.make_async_copy(zeros, out_ref.at[pl.ds(start, EXPERT_ROWS * nc)], zsem)

        for e in range(nexp):
            @pl.when(pend_ref[e] > pstart_ref[e])
            def _():
                zero_block(pend_ref[e] // EXPERT_ROWS - 1).start()
        for e in range(nexp):
            @pl.when(pend_ref[e] > pstart_ref[e])
            def _():
                zero_block(pend_ref[e] // EXPERT_ROWS - 1).wait()
        nblk = out_ref.shape[0] // (EXPERT_ROWS * nc)
        nused = pend_ref[nexp - 1] // EXPERT_ROWS

        def start_unused(blk, c):
            zero_block(blk).start()
            return c

        def wait_unused(blk, c):
            zero_block(blk).wait()
            return c

        lax.fori_loop(nused, nblk, start_unused, 0)
        lax.fori_loop(nused, nblk, wait_unused, 0)

    def copy_rows(x_ref, base):
        tm = x_ref.shape[0] // nc

        def issue(r, c):
            for k in range(TOP_K):
                _row_copy(x_ref, r, out_ref, dest_ref[k * tall + base + r], sem, nc).start(priority=k % 2)
            return c

        lax.fori_loop(0, tm, issue, 0, unroll=8)
        for k in range(TOP_K):
            pltpu.make_async_copy(x_ref, x_ref, sem).wait()

    @pl.when(i < nblk_p)
    def _():
        copy_rows(xp_ref, i * tm_p)

    @pl.when(i == nblk_p)
    def _():
        copy_rows(xs_ref, nblk_p * tm_p)


def _scatter(dest_flat, pstart, pend, x_p, x_s, *, n_rows, tm_p, nexp, nc):
    tp = x_p.shape[0] // nc
    ts = x_s.shape[0] // nc
    nblk_p = tp // tm_p
    kern = functools.partial(_scatter_kernel, tm_p=tm_p, nblk_p=nblk_p, nexp=nexp, nc=nc)
    grid_spec = pltpu.PrefetchScalarGridSpec(
        num_scalar_prefetch=3, grid=(nblk_p + 1,),
        in_specs=[pl.BlockSpec((tm_p * nc, LANES), lambda i, *_: (jnp.minimum(i, nblk_p - 1), 0)),
                  pl.BlockSpec((ts * nc, LANES), lambda i, *_: (0, 0))],
        out_specs=pl.BlockSpec(memory_space=pl.ANY),
        scratch_shapes=[pltpu.VMEM((EXPERT_ROWS * nc, LANES), jnp.uint32), pltpu.SemaphoreType.DMA(()),
                        pltpu.SemaphoreType.DMA(())])
    return pl.pallas_call(
        kern, grid_spec=grid_spec, out_shape=jax.ShapeDtypeStruct((n_rows * nc, LANES), jnp.uint32),
        compiler_params=_cparams(("arbitrary",)), name="scatter")(dest_flat, pstart, pend, x_p, x_s)


def _experts_kernel(blk_e_ref, nused_ref, xs_ref, wg_ref, wu_ref, wd_ref, y_ref, wgb, wub, wdb, *, nc):
    i = pl.program_id(0)
    prev = blk_e_ref[jnp.maximum(i - 1, 0)]
    fresh = (i == 0) | (blk_e_ref[i] != prev)

    @pl.when(i < nused_ref[0])
    def _():
        @pl.when(fresh)
        def _():
            wgb[...] = wg_ref[...].astype(BF16)
            wub[...] = wu_ref[...].astype(BF16)
            wdb[...] = wd_ref[...].astype(BF16)
        x = _unpack_bf16_pairs(_slab_rows(xs_ref, nc)).astype(BF16)
        a = _dot(x, wgb[...])
        u = _dot(x, wub[...])
        hmid = (a * jax.nn.sigmoid(a)) * u
        y = _pack_bf16_pairs(_dot(hmid.astype(BF16), wdb[...]))
        for c in range(nc):
            y_ref[pl.ds(c, EXPERT_ROWS, stride=nc), :] = y[:, c * LANES:(c + 1) * LANES]

    @pl.when(i >= nused_ref[0])
    def _():
        y_ref[...] = jnp.zeros_like(y_ref)


def _experts(blk_e, nused, xs, wg, wu, wd):
    dm, de = wg.shape[1:]
    nc = dm // (2 * LANES)
    nblk = xs.shape[0] // (EXPERT_ROWS * nc)

    def row_map(i, be, nu):
        return (jnp.minimum(i, nu[0] - 1), 0)

    grid_spec = pltpu.PrefetchScalarGridSpec(
        num_scalar_prefetch=2, grid=(nblk,),
        in_specs=[pl.BlockSpec((EXPERT_ROWS * nc, LANES), row_map),
                  pl.BlockSpec((None, dm, de), lambda i, be, nu: (be[i], 0, 0)),
                  pl.BlockSpec((None, dm, de), lambda i, be, nu: (be[i], 0, 0)),
                  pl.BlockSpec((None, de, dm), lambda i, be, nu: (be[i], 0, 0))],
        out_specs=pl.BlockSpec((EXPERT_ROWS * nc, LANES), lambda i, be, nu: (i, 0)),
        scratch_shapes=[pltpu.VMEM((dm, de), BF16), pltpu.VMEM((dm, de), BF16), pltpu.VMEM((de, dm), BF16)])
    return pl.pallas_call(
        functools.partial(_experts_kernel, nc=nc), grid_spec=grid_spec,
        out_shape=jax.ShapeDtypeStruct(xs.shape, jnp.uint32),
        compiler_params=_cparams(("arbitrary",)), name="experts")(blk_e, nused, xs, wg, wu, wd)


def _combine_kernel(dest_ref, h_ref, rr_ref, gain_ref, y_hbm, o_ref, buf, sem, *, tm, tok0):
    i = pl.program_id(0)
    nc = h_ref.shape[1] // (2 * LANES)
    tall = dest_ref.shape[0] // TOP_K
    base = tok0 + i * tm

    def issue(r, c):
        for k in range(TOP_K):
            _row_copy(y_hbm, dest_ref[k * tall + base + r], buf.at[k], r, sem, nc).start(priority=k % 2)
        return c

    lax.fori_loop(0, tm, issue, 0, unroll=8)
    for k in range(TOP_K):
        pltpu.make_async_copy(buf.at[k], buf.at[k], sem).wait()
    rr = rr_ref[...]
    y0 = _unpack_bf16_pairs(_slab_rows(buf.at[0], nc))
    y1 = _unpack_bf16_pairs(_slab_rows(buf.at[1], nc))
    h = h_ref[...] + (rr[:, 2:3] * y0 + rr[:, 3:4] * y1)
    o_ref[...] = _rms(h, gain_ref[...])


def _combine(dest_flat, h, rr, gain, y, *, tm, tok0, name):
    t, dm = h.shape
    kern = functools.partial(_combine_kernel, tm=tm, tok0=tok0)
    grid_spec = pltpu.PrefetchScalarGridSpec(
        num_scalar_prefetch=1, grid=(t // tm,),
        in_specs=[pl.BlockSpec((tm, dm), lambda i, *_: (i, 0)), pl.BlockSpec((tm, LANES), lambda i, *_: (i, 0)),
                  pl.BlockSpec((1, dm), lambda i, *_: (0, 0)), pl.BlockSpec(memory_space=pl.ANY)],
        out_specs=pl.BlockSpec((tm, dm), lambda i, *_: (i, 0)),
        scratch_shapes=[pltpu.VMEM((TOP_K, tm * (dm // (2 * LANES)), LANES), jnp.uint32),
                        pltpu.SemaphoreType.DMA(())])
    return pl.pallas_call(
        kern, grid_spec=grid_spec, out_shape=jax.ShapeDtypeStruct((t, dm), F32),
        compiler_params=_cparams(("arbitrary",)), name=name)(dest_flat, h, rr, gain, y)


def kernel(x_prompt, x_sample, cache_k, cache_v, cache_logf, cache_mem_k, cache_mem_v, state_conv, page_table,
           mem_prompt, norm_mix, w_in, b_forget, conv_w, conv_b, w_up_conv, w_up_attn, w_mix_out, norm_xattn,
           norm_mem, w_xq, w_xk, w_xv, w_xo, norm_ffn, w_router_group, w_router_expert, w_expert_gate,
           w_expert_up, w_expert_down, norm_final):
    depth = w_in.shape[0]
    assert depth == 1, "single-layer trunk"
    nbp, seq, dm = x_prompt.shape
    nbs, s_new, _ = x_sample.shape
    _, n_pool, page, heads, dh = cache_k.shape
    npages = page_table.shape[1]
    nmem, xh, xd = cache_mem_k.shape[2:]
    dc = conv_w.shape[2]
    da = heads * dh
    dx = xh * xd
    ngroups, _, epg = w_router_expert.shape[1:]
    nexp = ngroups * epg
    tp = nbp * seq
    ts = nbs * s_new
    assert conv_w.shape[1] == 3 and s_new >= 2 and dh == 64 and heads == SUBLANES and epg == SUBLANES
    assert page == LANES

    l = 0
    wi = w_in[l]
    wc = wi[:, 0:3 * dc].astype(BF16)
    wqkv = wi[:, 3 * dc:3 * dc + 3 * da].astype(BF16)
    o_f = 3 * dc + 3 * da
    wf = jnp.pad(wi[:, o_f:o_f + heads], ((0, 0), (0, LANES - heads)))
    wfh = wf.astype(BF16)
    wf2 = jnp.concatenate([wfh, (wf - wfh.astype(F32)).astype(BF16)], axis=1)
    wg = wi[:, o_f + heads:].astype(BF16)
    bfp = jnp.pad(b_forget[l][None, :], ((0, 0), (0, LANES - heads)))
    cw = conv_w[l]
    cbias = conv_b[l][None, :]
    g_mix = norm_mix[l][None, :]
    wuc = w_up_conv[l].astype(BF16)
    wua = w_up_attn[l].astype(BF16)
    wmo = w_mix_out[l].astype(BF16)
    g_x = norm_xattn[l][None, :]
    wxq = w_xq[l].astype(BF16)
    wxo = w_xo[l].astype(BF16)
    g_f = norm_ffn[l][None, :]
    wr = jnp.zeros((LANES, dm), F32)
    wr = wr.at[0:ngroups].set(w_router_group[l].T)
    wr = wr.at[EXPERT_ROW0:EXPERT_ROW0 + nexp].set(jnp.transpose(w_router_expert[l], (0, 2, 1)).reshape(nexp, dm))
    wrh = wr.astype(BF16)
    wrl = (wr - wrh.astype(F32)).astype(BF16)

    xp = x_prompt.reshape(tp, dm)
    (gc_p, qa_p, ka_p, vb_p, kt_p, vt_p, g_p, lf_p, ulast_p) = _inproj(
        xp, g_mix, wc, wqkv, wg, wf2, bfp, cw, cbias, seq_len=seq, sample=False, aug=_decay_columns(heads))
    xs_ = x_sample.reshape(ts, dm)
    st = state_conv[l]
    zeros_row = jnp.zeros((nbs, 1, dc), F32)
    fix1 = jnp.concatenate([st[:, 1:2], jnp.tile(zeros_row, (1, s_new - 1, 1))], axis=1).reshape(ts, dc)
    fix2 = jnp.concatenate([st[:, 0:1], st[:, 1:2], jnp.tile(zeros_row, (1, s_new - 2, 1))], axis=1).reshape(ts, dc)
    (gc_s, q_s, kb_s, vb_s, kf_s, vf_s, g_s, lf_s, dt_s, u_s) = _inproj(
        xs_, g_mix, wc, wqkv, wg, wf2, bfp, cw, cbias, seq_len=s_new, sample=True, fix=(fix1, fix2))
    pt_flat = page_table.reshape(-1).astype(jnp.int32)
    head_of_col = jnp.arange(da) // dh
    qbd = jnp.where(head_of_col[None, None, None, :] == jnp.arange(heads)[None, None, :, None],
                    q_s.reshape(nbs, s_new, 1, da), jnp.zeros((), BF16)).reshape(nbs, s_new * heads, da)
    kt_pages = jnp.transpose(cache_k[l], (0, 2, 3, 1)).reshape(n_pool, da, page)
    vt_pages = jnp.transpose(cache_v[l], (0, 2, 3, 1)).reshape(n_pool, da, page)
    lf_pages = jnp.swapaxes(cache_logf[l], 1, 2)
    at_p, at_s = _fox_fused(pt_flat, qa_p, ka_p, vb_p, kt_pages, vt_pages, lf_pages, qbd, kb_s, vb_s, dt_s,
                            nseq=nbp, seq_len=seq, dh=dh, npages=npages, s_new=s_new, heads=heads)

    tm_p = min(512, seq)
    mk_p, mv_p = _memkv(mem_prompt.reshape(nbp * nmem, dm), norm_mem[l][None, :], w_xk[l].astype(BF16),
                        w_xv[l].astype(BF16), xh=xh)
    cnt0 = jnp.zeros((LANES, LANES), F32)
    h2_p, xn_p, rt_p, rr_p, cnt1 = _mid(
        xp, gc_p, at_p, g_p, mk_p.reshape(nbp, nmem * xh, xd), mv_p.reshape(nbp, nmem * xh, xd),
        wuc, wua, wmo, g_x, wxq, wxo, g_f, wrh, wrl, cnt0, tm=tm_p, seq_len=seq, ngroups=ngroups, epg=epg, xh=xh)

    h1_s, qx_s = _post_attn(xs_, gc_s, at_s.reshape(ts, da), g_s, wuc, wua, wmo, g_x, wxq, tm=ts,
                            name="post_attn_sample")
    qx_s8 = jnp.pad(qx_s.astype(F32).reshape(nbs, s_new, dx), ((0, 0), (0, SUBLANES - s_new), (0, 0)))
    o_s = _xattn(qx_s8, cache_mem_k[l].reshape(nbs, nmem * xh, xd), cache_mem_v[l].reshape(nbs, nmem * xh, xd),
                 tq=SUBLANES, xh=xh, nbb=4 if nbs % 4 == 0 else 1, name="xattn_sample")[:, :s_new].reshape(ts, dx)
    h2_s, xn_s, rt_s, rr_s, cnt2 = _pre_moe(h1_s, o_s, wxo, g_f, wrh, wrl, cnt1, tm=ts, ngroups=ngroups, epg=epg,
                                            name="pre_moe_sample")

    tall = tp + ts
    counts = cnt2[0:nexp, 0].astype(jnp.int32)
    padded = (counts + EXPERT_ROWS - 1) // EXPERT_ROWS * EXPERT_ROWS
    pend = jnp.cumsum(padded).astype(jnp.int32)
    pstart = pend - padded
    nblk = (tall * TOP_K + nexp * (EXPERT_ROWS - 1)) // EXPERT_ROWS
    n_rows = nblk * EXPERT_ROWS
    blk_row0 = jnp.arange(nblk, dtype=jnp.int32) * EXPERT_ROWS
    blk_e = jnp.minimum(jnp.sum((pend[None, :] <= blk_row0[:, None]).astype(jnp.int32), axis=1), nexp - 1)
    nused = (pend[nexp - 1:nexp] // EXPERT_ROWS).astype(jnp.int32)
    rt_all = jnp.concatenate([rt_p, rt_s], axis=1)
    pstart_col = jnp.zeros((LANES, LANES), F32).at[0:nexp, :].set(pstart.astype(F32)[:, None])
    dest_flat = _dest(rt_all, pstart_col)[0:TOP_K].reshape(-1)
    xsg = _scatter(dest_flat, pstart, pend, xn_p, xn_s, n_rows=n_rows, tm_p=tm_p, nexp=nexp, nc=dm // (2 * LANES))
    y = _experts(blk_e, nused, xsg, w_expert_gate[l], w_expert_up[l], w_expert_down[l])
    g_fin = norm_final[None, :]
    y_p = _combine(dest_flat, h2_p, rr_p, g_fin, y, tm=tm_p, tok0=0, name="combine_prompt")
    y_s = _combine(dest_flat, h2_s, rr_s, g_fin, y, tm=ts, tok0=tp, name="combine_sample")

    return (y_p.reshape(nbp, seq, dm), y_s.reshape(nbs, s_new, dm),
            ulast_p[None],
            jnp.transpose(kt_p.reshape(nbp, heads, dh, seq), (0, 3, 1, 2))[None],
            jnp.transpose(vt_p.reshape(nbp, heads, dh, seq), (0, 3, 1, 2))[None],
            jnp.transpose(lf_p, (0, 2, 1))[None],
            mk_p.reshape(1, nbp, nmem, xh, xd), mv_p.reshape(1, nbp, nmem, xh, xd),
            u_s.reshape(nbs, s_new, dc)[None, :, s_new - 2:], kf_s.reshape(1, nbs, s_new, heads, dh),
            vf_s.reshape(1, nbs, s_new, heads, dh), lf_s[:, :heads].reshape(1, nbs, s_new, heads))
```

```python
import functools

import numpy as np
import jax
import jax.numpy as jnp
from jax import lax
from jax.experimental import pallas as pl
from jax.experimental.pallas import tpu as pltpu

F32 = jnp.float32
BF16 = jnp.bfloat16

RMS_EPS = 1e-6
TOP_K = 2
LANES = 128
SUBLANES = 8
VMEM_LIMIT = 56 * 1024 * 1024
NEG_BIG = -1e30
EXPERT_ROWS = 256
FUSED_PAGES_PER_ITERATION = 8
FUSED_RING = 3
FOX_QUERY_BLOCK = 512
FOX_KEY_BLOCK = 512
ROUTE_ROWS = 8
EXPERT_ROW0 = 8


def _cparams(sem, vmem=VMEM_LIMIT):
    return pltpu.CompilerParams(dimension_semantics=sem, vmem_limit_bytes=vmem)


def _rms(x, g):
    ms = jnp.mean(x * x, axis=-1, keepdims=True)
    return x * lax.rsqrt(ms + RMS_EPS) * g


def _split3(x):
    hi = x.astype(BF16)
    r = x - hi.astype(F32)
    mid = r.astype(BF16)
    lo = (r - mid.astype(F32)).astype(BF16)
    return hi, mid, lo


def _dot(a, b):
    return jnp.dot(a, b, preferred_element_type=F32)


def _dot_nt(a, b):
    return lax.dot_general(a, b, (((1,), (1,)), ((), ())), preferred_element_type=F32)


def _lane_tile(x, width):
    if width % LANES == 0:
        return jnp.concatenate([x] * (width // LANES), axis=1)
    return jnp.broadcast_to(x[:, 0:1], (x.shape[0], width))


def _const_spec(shape):
    nd = len(shape)
    return pl.BlockSpec(shape, lambda *_: (0,) * nd)


def _inproj_kernel(*refs, tm, seq_blocks, sample, seq_len):
    if sample:
        (x_ref, gain_ref, wc_ref, wqkv_ref, wg_ref, wf2_ref, bf_ref, cw_ref, cb_ref, fix1_ref, fix2_ref,
         gc_ref, q_ref, kbf_ref, vbf_ref, kf_ref, vf_ref, g_ref, logf_ref, dt_ref, u_ref) = refs
    else:
        (x_ref, gain_ref, wc_ref, wqkv_ref, wg_ref, wf2_ref, bf_ref, cw_ref, cb_ref,
         selq_ref, selk_ref, oneq_ref, onek_ref,
         gc_ref, qa_ref, ka_ref, vbf_ref, kf_ref, vf_ref, g_ref, logf_ref, u_ref,
         carry_u, carry_d) = refs
    i = pl.program_id(0)
    dc = cw_ref.shape[1]
    da = wqkv_ref.shape[1] // 3

    xn = _rms(x_ref[...], gain_ref[...])
    xb = xn.astype(BF16)

    cb = _dot(xb, wc_ref[:, 0:dc])
    cc = _dot(xb, wc_ref[:, dc:2 * dc])
    cx = _dot(xb, wc_ref[:, 2 * dc:3 * dc])
    u = cc * cx
    row = lax.broadcasted_iota(jnp.int32, (tm, 1), 0)
    r1 = pltpu.roll(u, 1, axis=0)
    r2 = pltpu.roll(u, 2, axis=0)
    if sample:
        pos = row % seq_len
        p1 = jnp.where(pos == 0, fix1_ref[...], r1)
        p2 = jnp.where(pos < 2, fix2_ref[...], r2)
        u_ref[...] = u
    else:
        @pl.when(i % seq_blocks == 0)
        def _():
            carry_u[...] = jnp.zeros_like(carry_u)
            carry_d[...] = jnp.zeros_like(carry_d)
        c0 = carry_u[0:1, :]
        c1 = carry_u[1:2, :]
        p1 = jnp.where(row == 0, c1, r1)
        p2 = jnp.where(row == 0, c0, jnp.where(row == 1, c1, r2))
        carry_u[0:2, :] = u[tm - 2:tm, :]
        u_ref[...] = u[tm - 2:tm, :]
    cw = cw_ref[...]
    cy = cb_ref[...] + cw[0:1, :] * p2 + cw[1:2, :] * p1 + cw[2:3, :] * u
    gc_ref[...] = (cb * cy).astype(BF16)

    qs = (_dot(xb, wqkv_ref[:, 0:da]) * (1.0 / 8.0)).astype(BF16)
    k = _dot(xb, wqkv_ref[:, da:2 * da])
    kb = k.astype(BF16)
    v = _dot(xb, wqkv_ref[:, 2 * da:3 * da])
    vbf_ref[...] = v.astype(BF16)
    if sample:
        q_ref[...] = qs
        kbf_ref[...] = kb
        kf_ref[...] = k
        vf_ref[...] = v
    else:
        kf_ref[...] = jnp.transpose(k)
        vf_ref[...] = jnp.transpose(v)

    gw = g_ref.shape[1]
    for c in range(gw // 512):
        g_ref[:, c * 512:(c + 1) * 512] = _dot(xb, wg_ref[:, c * 512:(c + 1) * 512]).astype(BF16)

    xl = (xn - xb.astype(F32)).astype(BF16)
    hh_hl = _dot(xb, wf2_ref[...])
    fz = hh_hl[:, 0:LANES] + (hh_hl[:, LANES:2 * LANES] + _dot(xl, wf2_ref[:, 0:LANES]))
    z = fz + bf_ref[...]
    logf = jnp.minimum(z, 0.0) - jnp.log1p(jnp.exp(-jnp.abs(z)))
    lane = lax.broadcasted_iota(jnp.int32, (1, LANES), 1)
    nh = da // 64
    logf = jnp.where(lane < nh, logf, 0.0)
    if sample:
        logf_ref[...] = logf
    else:
        logf_ref[...] = jnp.transpose(logf)[0:SUBLANES, :]

    rr = lax.broadcasted_iota(jnp.int32, (tm, tm), 0)
    cc_ = lax.broadcasted_iota(jnp.int32, (tm, tm), 1)
    if sample:
        tri = (cc_ <= rr) & ((rr // seq_len) == (cc_ // seq_len))
    else:
        tri = cc_ <= rr
    tri = jnp.where(tri, 1.0, 0.0).astype(BF16)
    d3 = _dot(tri, jnp.concatenate(_split3(logf), axis=1))
    d = d3[:, 0:LANES] + (d3[:, LANES:2 * LANES] + d3[:, 2 * LANES:3 * LANES])
    if sample:
        dt_ref[...] = jnp.transpose(d)[0:SUBLANES, :]
    else:
        d = d + carry_d[0:1, :]
        carry_d[0:1, :] = d[tm - 1:tm, :]
        dcat = jnp.concatenate(_split3(d), axis=1)
        aq = (_dot(dcat, selq_ref[...]) + oneq_ref[...]).astype(BF16)
        ak = (_dot(dcat, selk_ref[...]) + onek_ref[...]).astype(BF16)
        for p in range(da // LANES):
            lo_, hi_ = p * LANES, (p + 1) * LANES
            qa_ref[:, 2 * lo_:2 * lo_ + LANES] = qs[:, lo_:hi_]
            qa_ref[:, 2 * lo_ + LANES:2 * hi_] = aq[:, lo_:hi_]
            ka_ref[:, 2 * lo_:2 * lo_ + LANES] = kb[:, lo_:hi_]
            ka_ref[:, 2 * lo_ + LANES:2 * hi_] = ak[:, lo_:hi_]


def _decay_columns(heads):
    npair = heads // 2
    selq = np.zeros((3 * LANES, npair * LANES), np.float32)
    selk = np.zeros((3 * LANES, npair * LANES), np.float32)
    oneq = np.zeros((1, npair * LANES), np.float32)
    onek = np.zeros((1, npair * LANES), np.float32)
    for p in range(npair):
        for hh in range(2):
            for term in range(3):
                selq[term * LANES + 2 * p + hh, p * LANES + 3 * hh + term] = 1.0
                selk[term * LANES + 2 * p + hh, p * LANES + 6 + 3 * hh + term] = -1.0
                oneq[0, p * LANES + 6 + 3 * hh + term] = 1.0
                onek[0, p * LANES + 3 * hh + term] = 1.0
    return jnp.asarray(selq, BF16), jnp.asarray(selk, BF16), jnp.asarray(oneq), jnp.asarray(onek)


def _inproj(x, gain, wc, wqkv, wg, wf2, bfp, cw, cbias, *, seq_len, sample, fix=None, aug=None):
    t, dm = x.shape
    dc = cw.shape[1]
    da = wqkv.shape[1] // 3
    gw = wg.shape[1]
    if sample:
        tm = t
        seq_blocks = 1
    else:
        tm = min(512, seq_len)
        seq_blocks = seq_len // tm
    nblk = t // tm
    nseq = t // seq_len
    kern = functools.partial(_inproj_kernel, tm=tm, seq_blocks=seq_blocks, sample=sample, seq_len=seq_len)
    rows = lambda w: pl.BlockSpec((tm, w), lambda i: (i, 0))
    in_specs = [rows(dm), _const_spec((1, dm)), _const_spec(wc.shape), _const_spec(wqkv.shape), _const_spec(wg.shape),
                _const_spec(wf2.shape), _const_spec((1, LANES)), _const_spec(cw.shape), _const_spec((1, dc))]
    args = [x, gain, wc, wqkv, wg, wf2, bfp, cw, cbias]
    sds = jax.ShapeDtypeStruct
    if sample:
        in_specs += [rows(dc), rows(dc)]
        args += list(fix)
        qk_shapes = [sds((t, da), BF16), sds((t, da), BF16)]
        qk_specs = [rows(da), rows(da)]
        kv_shapes = [sds((t, da), F32), sds((t, da), F32)]
        kv_specs = [rows(da), rows(da)]
        tail_shapes = [sds((t, LANES), F32), sds((SUBLANES, t), F32), sds((t, dc), F32)]
        tail_specs = [rows(LANES), pl.BlockSpec((SUBLANES, tm), lambda i: (0, i)), rows(dc)]
        scratch = []
    else:
        in_specs += [_const_spec(a.shape) for a in aug]
        args += list(aug)
        qk_shapes = [sds((t, 2 * da), BF16), sds((t, 2 * da), BF16)]
        qk_specs = [rows(2 * da), rows(2 * da)]
        kv_shapes = [sds((nseq, da, seq_len), F32), sds((nseq, da, seq_len), F32)]
        kv_specs = [pl.BlockSpec((None, da, tm), lambda i: (i // seq_blocks, 0, i % seq_blocks))] * 2
        tail_shapes = [sds((nseq, SUBLANES, seq_len), F32), sds((nseq, 2, dc), F32)]
        tail_specs = [pl.BlockSpec((None, SUBLANES, tm), lambda i: (i // seq_blocks, 0, i % seq_blocks)),
                      pl.BlockSpec((None, 2, dc), lambda i: (i // seq_blocks, 0, 0))]
        scratch = [pltpu.VMEM((SUBLANES, dc), F32), pltpu.VMEM((SUBLANES, LANES), F32)]
    out_shape = ([sds((t, dc), BF16)] + qk_shapes +
                 [sds((t, da), BF16)] + kv_shapes +
                 [sds((t, gw), BF16)] + tail_shapes)
    out_specs = [rows(dc)] + qk_specs + [rows(da)] + kv_specs + [rows(gw)] + tail_specs
    return pl.pallas_call(
        kern, grid=(nblk,), in_specs=in_specs, out_specs=out_specs, out_shape=out_shape,
        scratch_shapes=scratch, compiler_params=_cparams(("arbitrary",)),
        name="inproj_sample" if sample else "inproj_prompt")(*args)


def _fox_fused_kernel(pt_ref, qa_ref, ka_ref, v_ref, kt_hbm, vt_hbm, lf_hbm, qbd_ref, kn_ref, vn_ref, dtn_ref,
                      o_ref, os_ref,
                      m_sc, l_sc, acc_sc, ms_sc, ls_sc, as_sc, run_sc, kbuf, vbuf, lbuf, sem, g_ref,
                      *, tq, tk, dh, pp, ring, npages, nchunks, s_new, heads):
    first_step = (pl.program_id(0) == 0) & (pl.program_id(1) == 0) & (pl.program_id(2) == 0)
    qi = pl.program_id(2)
    page = kbuf.shape[3]
    cpb = npages // pp
    nrow = s_new * heads
    da = heads * dh
    nbs = qbd_ref.shape[0]

    def chunk_copies(c):
        slot = c % ring
        bs = c // cpb
        jc = c % cpb
        copies = []
        for i in range(pp):
            pid = pt_ref[bs * npages + (npages - 1 - (jc * pp + i))]
            copies.append(pltpu.make_async_copy(kt_hbm.at[pid], kbuf.at[slot, i], sem.at[slot]))
            copies.append(pltpu.make_async_copy(vt_hbm.at[pid], vbuf.at[slot, i], sem.at[slot]))
            copies.append(pltpu.make_async_copy(lf_hbm.at[pid], lbuf.at[slot, i], sem.at[slot]))
        return copies

    @pl.when(first_step)
    def _():
        g_ref[0] = 0
        for c in range(min(ring - 1, nchunks)):
            for cp in chunk_copies(c):
                cp.start()

    lane = lax.broadcasted_iota(jnp.int32, (1, 2 * LANES), 1)
    ext = lane - LANES
    qf = qa_ref[...].astype(F32)
    halves = []
    for h in range(2):
        keep = (((lane >= h * dh) & (lane < (h + 1) * dh))
                | ((ext >= 3 * h) & (ext < 3 * h + 3)) | ((ext >= 6 + 3 * h) & (ext < 9 + 3 * h)))
        halves.append(jnp.where(keep, qf, 0.0))
    qs = jnp.concatenate(halves, axis=0).astype(BF16)
    m_sc[...] = jnp.full_like(m_sc, NEG_BIG)
    l_sc[...] = jnp.zeros_like(l_sc)
    acc_sc[...] = jnp.zeros_like(acc_sc)
    nfull = (qi * tq) // tk

    def prompt_step(j, masked):
        ks = pl.multiple_of(j * tk, tk)
        kb = ka_ref[pl.ds(ks, tk), :]
        vb = v_ref[pl.ds(ks, tk), :]
        s = _dot_nt(qs, kb)
        if masked:
            rloc = lax.broadcasted_iota(jnp.int32, (2 * tq, tk), 0)
            rloc = jnp.where(rloc >= tq, rloc - tq, rloc) + qi * tq
            cloc = lax.broadcasted_iota(jnp.int32, (2 * tq, tk), 1) + ks
            s = jnp.where(cloc <= rloc, s, NEG_BIG)
        m_prev = m_sc[...]
        m_new = jnp.maximum(m_prev, jnp.max(s, axis=1, keepdims=True))
        p = jnp.exp(s - _lane_tile(m_new, tk))
        alpha = jnp.exp(m_prev - m_new)
        l_sc[...] = alpha * l_sc[...] + jnp.sum(p, axis=1, keepdims=True)
        acc_sc[...] = alpha * acc_sc[...] + _dot(p.astype(BF16), vb)
        m_sc[...] = m_new

    def sample_update(s, pv_fn):
        m_prev = ms_sc[...]
        m_new = jnp.maximum(m_prev, jnp.max(s, axis=1, keepdims=True))
        p = jnp.exp(s - _lane_tile(m_new, s.shape[1]))
        alpha = jnp.exp(m_prev - m_new)
        ls_sc[...] = alpha * ls_sc[...] + jnp.sum(p, axis=1, keepdims=True)
        as_sc[...] = _lane_tile(alpha, da) * as_sc[...] + pv_fn(p.astype(BF16))
        ms_sc[...] = m_new

    def sample_chunk(g, valid):
        slot = g % ring
        bs = jnp.minimum(g // cpb, nbs - 1)
        qbd = qbd_ref[bs]
        rr = lax.broadcasted_iota(jnp.int32, (page, 2 * page), 0)
        cc = lax.broadcasted_iota(jnp.int32, (page, 2 * page), 1)
        after = jnp.where((rr > cc) | (cc >= page), 1.0, 0.0).astype(BF16)
        lf = jnp.concatenate([lbuf[slot, i] for i in range(pp)], axis=0)
        n8 = pp * heads
        r3 = _dot(jnp.concatenate(_split3(lf), axis=0), after)
        both = r3[0:n8] + (r3[n8:2 * n8] + r3[2 * n8:3 * n8])
        run = run_sc[...]
        scores = []
        for i in range(pp):
            rev = both[i * heads:(i + 1) * heads, 0:page] + run
            run = run + both[i * heads:(i + 1) * heads, page:2 * page]
            scores.append(_dot(qbd, kbuf[slot, i].astype(BF16)) + jnp.concatenate([rev] * s_new, axis=0))
        run_sc[...] = run
        s = jnp.where(valid, jnp.concatenate(scores, axis=1), NEG_BIG)

        def pv_pages(p):
            acc = None
            for i in range(pp):
                term = _dot_nt(p[:, i * page:(i + 1) * page], vbuf[slot, i].astype(BF16))
                acc = term if acc is None else acc + term
            return acc

        sample_update(s, pv_pages)

    def sample_finish(bs):
        ntok = kn_ref.shape[0]
        qbd = qbd_ref[bs]
        dtn = dtn_ref[...]
        lane_t = lax.broadcasted_iota(jnp.int32, (1, ntok), 1)
        dq_rows = [jnp.sum(jnp.where(lane_t == bs * s_new + t, dtn, 0.0), axis=1, keepdims=True)
                   for t in range(s_new)]
        dq = jnp.concatenate(dq_rows, axis=0)
        dk = jnp.concatenate([dtn] * s_new, axis=0)
        s = _dot_nt(qbd, kn_ref[...]) + (dq - dk)
        rowt = lax.broadcasted_iota(jnp.int32, (nrow, ntok), 0) // heads
        col = lax.broadcasted_iota(jnp.int32, (nrow, ntok), 1)
        keep = ((col // s_new) == bs) & ((col % s_new) <= rowt)
        sample_update(jnp.where(keep, s, NEG_BIG), lambda p: _dot(p, vn_ref[...]))
        o = as_sc[...] / _lane_tile(ls_sc[...], da)
        rowh = lax.broadcasted_iota(jnp.int32, (nrow, da), 0) % heads
        colh = lax.broadcasted_iota(jnp.int32, (nrow, da), 1) // dh
        o = jnp.where(rowh == colh, o, 0.0)
        os_ref[bs] = jnp.sum(o.reshape(s_new, heads, da), axis=1)

    def iteration(j, masked):
        g = g_ref[0]
        valid = g < nchunks

        @pl.when(g + (ring - 1) < nchunks)
        def _():
            for cp in chunk_copies(g + (ring - 1)):
                cp.start()

        @pl.when(valid)
        def _():
            for cp in chunk_copies(g):
                cp.wait()

        @pl.when(valid & (g % cpb == 0))
        def _():
            ms_sc[...] = jnp.full_like(ms_sc, NEG_BIG)
            ls_sc[...] = jnp.zeros_like(ls_sc)
            as_sc[...] = jnp.zeros_like(as_sc)
            run_sc[...] = jnp.zeros_like(run_sc)

        prompt_step(j, masked)
        sample_chunk(g, valid)

        @pl.when(valid & (g % cpb == cpb - 1))
        def _():
            sample_finish(g // cpb)

        g_ref[0] = g + 1

    def body(j, c):
        iteration(j, False)
        return c

    def prompt_body(j, c):
        prompt_step(j, False)
        return c

    stream_live = g_ref[0] < nchunks

    @pl.when(stream_live)
    def _():
        lax.fori_loop(0, nfull, body, 0)
        iteration(nfull, True)

    @pl.when(jnp.logical_not(stream_live))
    def _():
        lax.fori_loop(0, nfull, prompt_body, 0)
        prompt_step(nfull, True)

    o = acc_sc[...] / l_sc[...]
    lane_o = lax.broadcasted_iota(jnp.int32, (1, LANES), 1)
    o_ref[...] = jnp.where(lane_o < dh, o[0:tq], o[tq:2 * tq]).astype(o_ref.dtype)


def _fox_fused(page_table_flat, qa, ka, v, kt_pages, vt_pages, lf_pages, qbd, kn, vn, dtn,
               *, nseq, seq_len, dh, npages, s_new, heads):
    t, da = v.shape
    nbs, nrow, _ = qbd.shape
    page = kt_pages.shape[2]
    ntok = kn.shape[0]
    tk = min(FOX_KEY_BLOCK, seq_len)
    tq = min(FOX_QUERY_BLOCK, tk)
    nq = seq_len // tq
    npair = da // LANES
    pp = min(FUSED_PAGES_PER_ITERATION, npages)
    nchunks = nbs * (npages // pp)
    iters = nseq * npair * sum((qi * tq) // tk + 1 for qi in range(nq))
    assert npages % pp == 0 and nchunks <= iters, "the page stream must fit in the prompt attention's iterations"
    kern = functools.partial(_fox_fused_kernel, tq=tq, tk=tk, dh=dh, pp=pp, ring=FUSED_RING, npages=npages,
                             nchunks=nchunks, s_new=s_new, heads=heads)
    const = lambda shape: pl.BlockSpec(shape, lambda b, hp, qi, pt: (0,) * len(shape))
    grid_spec = pltpu.PrefetchScalarGridSpec(
        num_scalar_prefetch=1, grid=(nseq, npair, nq),
        in_specs=[pl.BlockSpec((tq, 2 * LANES), lambda b, hp, qi, pt: (b * nq + qi, hp)),
                  pl.BlockSpec((seq_len, 2 * LANES), lambda b, hp, qi, pt: (b, hp)),
                  pl.BlockSpec((seq_len, LANES), lambda b, hp, qi, pt: (b, hp)),
                  pl.BlockSpec(memory_space=pl.ANY), pl.BlockSpec(memory_space=pl.ANY),
                  pl.BlockSpec(memory_space=pl.ANY),
                  const((nbs, nrow, da)), const((ntok, da)), const((ntok, da)), const((SUBLANES, ntok))],
        out_specs=[pl.BlockSpec((tq, LANES), lambda b, hp, qi, pt: (b * nq + qi, hp)),
                   const((nbs, s_new, da))],
        scratch_shapes=[pltpu.VMEM((2 * tq, LANES), F32), pltpu.VMEM((2 * tq, LANES), F32),
                        pltpu.VMEM((2 * tq, LANES), F32),
                        pltpu.VMEM((nrow, LANES), F32), pltpu.VMEM((nrow, LANES), F32), pltpu.VMEM((nrow, da), F32),
                        pltpu.VMEM((heads, LANES), F32),
                        pltpu.VMEM((FUSED_RING, pp, da, page), F32), pltpu.VMEM((FUSED_RING, pp, da, page), F32),
                        pltpu.VMEM((FUSED_RING, pp, heads, page), F32),
                        pltpu.SemaphoreType.DMA((FUSED_RING,)), pltpu.SMEM((1,), jnp.int32)])
    return pl.pallas_call(
        kern, grid_spec=grid_spec,
        out_shape=[jax.ShapeDtypeStruct((t, da), BF16), jax.ShapeDtypeStruct((nbs, s_new, da), F32)],
        compiler_params=_cparams(("arbitrary", "arbitrary", "arbitrary")),
        name="fox_fused")(page_table_flat, qa, ka, v, kt_pages, vt_pages, lf_pages, qbd, kn, vn, dtn)


def _post_attn_math(x_ref, gc_ref, at_ref, g_ref, wuc_ref, wua_ref, wmo_ref, gain_ref, wxq_ref):
    dm = x_ref.shape[1]
    y_conv = _dot(gc_ref[...], wuc_ref[...])
    y_attn = _dot(at_ref[...].astype(BF16), wua_ref[...])
    g_conv = g_ref[:, 0:dm].astype(F32)
    g_attn = g_ref[:, dm:2 * dm].astype(F32)
    mixed = jax.nn.sigmoid(g_conv) * y_conv + jax.nn.sigmoid(g_attn) * y_attn
    h = x_ref[...] + _dot(mixed.astype(BF16), wmo_ref[...])
    xn = _rms(h, gain_ref[...]).astype(BF16)
    return h, _dot(xn, wxq_ref[...]).astype(BF16)


def _post_attn_kernel(x_ref, gc_ref, at_ref, g_ref, wuc_ref, wua_ref, wmo_ref, gain_ref, wxq_ref, h_ref, qx_ref):
    h, qx = _post_attn_math(x_ref, gc_ref, at_ref, g_ref, wuc_ref, wua_ref, wmo_ref, gain_ref, wxq_ref)
    h_ref[...] = h
    qx_ref[...] = qx.astype(qx_ref.dtype)


def _post_attn(x, gc, at, g, wuc, wua, wmo, gain, wxq, *, tm, name):
    t, dm = x.shape
    dx = wxq.shape[1]
    rows = lambda w: pl.BlockSpec((tm, w), lambda i: (i, 0))
    return pl.pallas_call(
        _post_attn_kernel, grid=(t // tm,),
        in_specs=[rows(dm), rows(gc.shape[1]), rows(at.shape[1]), rows(g.shape[1]), _const_spec(wuc.shape),
                  _const_spec(wua.shape), _const_spec(wmo.shape), _const_spec((1, dm)), _const_spec(wxq.shape)],
        out_specs=[rows(dm), rows(dx)],
        out_shape=[jax.ShapeDtypeStruct((t, dm), F32), jax.ShapeDtypeStruct((t, dx), BF16)],
        compiler_params=_cparams(("parallel",)), name=name)(x, gc, at, g, wuc, wua, wmo, gain, wxq)


def _memkv_kernel(m_ref, gain_ref, wk_ref, wv_ref, k_ref, v_ref, *, xh):
    mn = _rms(m_ref[...], gain_ref[...]).astype(BF16)
    tm = m_ref.shape[0]
    xd = wk_ref.shape[1] // xh
    k = _dot(mn, wk_ref[...])
    v = _dot(mn, wv_ref[...])
    for h in range(xh):
        k_ref[pl.ds(h, tm, stride=xh), :] = k[:, h * xd:(h + 1) * xd]
        v_ref[pl.ds(h, tm, stride=xh), :] = v[:, h * xd:(h + 1) * xd]


def _memkv(mem, gain, wk, wv, *, xh):
    t, dm = mem.shape
    tm = min(512, t)
    xd = wk.shape[1] // xh
    rows = lambda w: pl.BlockSpec((tm, w), lambda i: (i, 0))
    return pl.pallas_call(
        functools.partial(_memkv_kernel, xh=xh), grid=(t // tm,),
        in_specs=[rows(dm), _const_spec((1, dm)), _const_spec(wk.shape), _const_spec(wv.shape)],
        out_specs=[pl.BlockSpec((tm * xh, xd), lambda i: (i, 0))] * 2,
        out_shape=[jax.ShapeDtypeStruct((t * xh, xd), F32)] * 2,
        compiler_params=_cparams(("parallel",)), name="memkv")(mem, gain, wk, wv)


def _xattn_math(q, mk_ref, mv_ref, xh):
    nm = mk_ref.shape[0] // xh
    xd = mk_ref.shape[1]
    scale = xd ** -0.5
    outs = []
    for h in range(xh):
        mk = mk_ref[pl.ds(h, nm, stride=xh), :].astype(BF16)
        mv = mv_ref[pl.ds(h, nm, stride=xh), :].astype(BF16)
        s = _dot_nt(q[:, h * xd:(h + 1) * xd], mk) * scale
        m = jnp.max(s, axis=1, keepdims=True)
        p = jnp.exp(s - m)
        p = p / jnp.sum(p, axis=1, keepdims=True)
        outs.append(_dot(p.astype(BF16), mv))
    return jnp.concatenate(outs, axis=1)


def _xattn_kernel(q_ref, mk_ref, mv_ref, o_ref, *, xh):
    tq, dx = q_ref.shape[1:]
    xd = dx // xh
    nmh = mk_ref.shape[1]
    scale = xd ** -0.5
    rowh = lax.broadcasted_iota(jnp.int32, (xh * tq, nmh), 0) // tq
    colh = lax.broadcasted_iota(jnp.int32, (xh * tq, nmh), 1) % xh
    same_head = rowh == colh
    for b in range(q_ref.shape[0]):
        q = q_ref[b].astype(BF16)
        qs = jnp.concatenate([q[:, h * xd:(h + 1) * xd] for h in range(xh)], axis=0)
        s = jnp.where(same_head, _dot_nt(qs, mk_ref[b].astype(BF16)) * scale, NEG_BIG)
        m = jnp.max(s, axis=1, keepdims=True)
        p = jnp.exp(s - m)
        p = p / jnp.sum(p, axis=1, keepdims=True)
        o = _dot(p.astype(BF16), mv_ref[b].astype(BF16))
        o_ref[b] = jnp.concatenate([o[h * tq:(h + 1) * tq] for h in range(xh)], axis=1).astype(o_ref.dtype)


def _xattn(q, mk, mv, *, tq, xh, nbb, name):
    nb, s, dx = q.shape
    nm, xd = mk.shape[1:]
    kern = functools.partial(_xattn_kernel, xh=xh)
    return pl.pallas_call(
        kern, grid=(nb // nbb, s // tq),
        in_specs=[pl.BlockSpec((nbb, tq, dx), lambda b, i: (b, i, 0)),
                  pl.BlockSpec((nbb, nm, xd), lambda b, i: (b, 0, 0)),
                  pl.BlockSpec((nbb, nm, xd), lambda b, i: (b, 0, 0))],
        out_specs=pl.BlockSpec((nbb, tq, dx), lambda b, i: (b, i, 0)),
        out_shape=jax.ShapeDtypeStruct((nb, s, dx), q.dtype),
        compiler_params=_cparams(("parallel", "parallel")), name=name)(q, mk, mv)


def _pre_moe_kernel(h_ref, o_ref, wxo_ref, gain_ref, wrh_ref, wrl_ref, cnt_in_ref,
                    h2_ref, xn_ref, rt_ref, rr_ref, cnt_out_ref, base, *, tm, ngroups, epg):
    @pl.when(pl.program_id(0) == 0)
    def _():
        base[...] = cnt_in_ref[...]

    _pre_moe_math(h_ref[...], o_ref[...].astype(BF16), wxo_ref, gain_ref, wrh_ref, wrl_ref,
                  h2_ref, xn_ref, rt_ref, rr_ref, cnt_out_ref, base, tm=tm, ngroups=ngroups, epg=epg)


def _mid_kernel(x_ref, gc_ref, at_ref, g_ref, mk_ref, mv_ref, wuc_ref, wua_ref, wmo_ref, gx_ref, wxq_ref,
                wxo_ref, gf_ref, wrh_ref, wrl_ref, cnt_in_ref,
                h2_ref, xn_ref, rt_ref, rr_ref, cnt_out_ref, base, *, tm, ngroups, epg, xh):
    @pl.when(pl.program_id(0) == 0)
    def _():
        base[...] = cnt_in_ref[...]

    h1, qx = _post_attn_math(x_ref, gc_ref, at_ref, g_ref, wuc_ref, wua_ref, wmo_ref, gx_ref, wxq_ref)
    o = _xattn_math(qx, mk_ref, mv_ref, xh).astype(BF16)
    _pre_moe_math(h1, o, wxo_ref, gf_ref, wrh_ref, wrl_ref,
                  h2_ref, xn_ref, rt_ref, rr_ref, cnt_out_ref, base, tm=tm, ngroups=ngroups, epg=epg)


def _pre_moe_math(h, o, wxo_ref, gain_ref, wrh_ref, wrl_ref, h2_ref, xn_ref, rt_ref, rr_ref, cnt_out_ref, base,
                  *, tm, ngroups, epg):
    h2 = h + _dot(o, wxo_ref[...])
    h2_ref[...] = h2
    xn = _rms(h2, gain_ref[...])
    xw = _pack_bf16_pairs(xn)
    nc = xw.shape[1] // LANES
    for c in range(nc):
        xn_ref[pl.ds(c, tm, stride=nc), :] = xw[:, c * LANES:(c + 1) * LANES]

    xh = xn.astype(BF16)
    xl = (xn - xh.astype(F32)).astype(BF16)
    lt = _dot_nt(wrh_ref[...], xh) + (_dot_nt(wrh_ref[...], xl) + _dot_nt(wrl_ref[...], xh))

    sub = lax.broadcasted_iota(jnp.int32, (SUBLANES, tm), 0)
    gl = jnp.where(sub < ngroups, lt[0:SUBLANES, :], -jnp.inf)
    gmax = jnp.max(gl, axis=0, keepdims=True)
    gidx = jnp.min(jnp.where(gl == gmax, sub, SUBLANES), axis=0, keepdims=True)
    pg = 1.0 / jnp.sum(jnp.exp(gl - gmax), axis=0, keepdims=True)
    el = jnp.zeros((epg, tm), F32)
    for g in range(ngroups):
        el = jnp.where(gidx == g, lt[EXPERT_ROW0 + g * epg:EXPERT_ROW0 + (g + 1) * epg, :], el)
    v1 = jnp.max(el, axis=0, keepdims=True)
    i1 = jnp.min(jnp.where(el == v1, sub, epg), axis=0, keepdims=True)
    el2 = jnp.where(sub == i1, -jnp.inf, el)
    v2 = jnp.max(el2, axis=0, keepdims=True)
    i2 = jnp.min(jnp.where(el2 == v2, sub, epg), axis=0, keepdims=True)
    t2 = jnp.exp(v2 - v1)
    den = 1.0 + t2
    w0 = (1.0 / den) * pg
    w1 = (t2 / den) * pg
    e0 = gidx * epg + i1
    e1 = gidx * epg + i2

    erow = lax.broadcasted_iota(jnp.int32, (LANES, tm), 0)
    oh0 = erow == e0
    oh1 = erow == e1
    rr = lax.broadcasted_iota(jnp.int32, (tm, tm), 0)
    cc = lax.broadcasted_iota(jnp.int32, (tm, tm), 1)
    triu = jnp.where(rr <= cc, 1.0, 0.0).astype(BF16)
    pre0 = _dot(jnp.where(oh0, 1.0, 0.0).astype(BF16), triu)
    pre1 = _dot(jnp.where(oh1, 1.0, 0.0).astype(BF16), triu)
    b0 = base[:, 0:1]
    tot0 = pre0[:, tm - 1:tm]
    tot1 = pre1[:, tm - 1:tm]
    rank0 = jnp.sum(jnp.where(oh0, pre0 - 1.0 + b0, 0.0), axis=0, keepdims=True)
    rank1 = jnp.sum(jnp.where(oh1, pre1 - 1.0 + (b0 + tot0), 0.0), axis=0, keepdims=True)
    newb = b0 + tot0 + tot1
    base[...] = jnp.broadcast_to(newb, base.shape)
    cnt_out_ref[...] = jnp.broadcast_to(newb, cnt_out_ref.shape)

    zero = jnp.zeros((1, tm), F32)
    rt = jnp.concatenate([e0.astype(F32), e1.astype(F32), w0, w1, rank0, rank1, zero, zero], axis=0)
    rt_ref[...] = rt
    rt_pad = jnp.concatenate([rt, jnp.zeros((LANES - ROUTE_ROWS, tm), F32)], axis=0)
    rr_ref[...] = jnp.transpose(rt_pad)


def _pre_moe(h, o, wxo, gain, wrh, wrl, cnt_in, *, tm, ngroups, epg, name):
    t, dm = h.shape
    dx = o.shape[1]
    kern = functools.partial(_pre_moe_kernel, tm=tm, ngroups=ngroups, epg=epg)
    rows = lambda w: pl.BlockSpec((tm, w), lambda i: (i, 0))
    return pl.pallas_call(
        kern, grid=(t // tm,),
        in_specs=[rows(dm), rows(dx), _const_spec(wxo.shape), _const_spec((1, dm)), _const_spec(wrh.shape),
                  _const_spec(wrl.shape), _const_spec((LANES, LANES))],
        out_specs=[rows(dm), pl.BlockSpec((tm * (dm // (2 * LANES)), LANES), lambda i: (i, 0)),
                   pl.BlockSpec((ROUTE_ROWS, tm), lambda i: (0, i)), rows(LANES), _const_spec((LANES, LANES))],
        out_shape=[jax.ShapeDtypeStruct((t, dm), F32), jax.ShapeDtypeStruct((t * (dm // (2 * LANES)), LANES), jnp.uint32),
                   jax.ShapeDtypeStruct((ROUTE_ROWS, t), F32), jax.ShapeDtypeStruct((t, LANES), F32),
                   jax.ShapeDtypeStruct((LANES, LANES), F32)],
        scratch_shapes=[pltpu.VMEM((LANES, LANES), F32)],
        compiler_params=_cparams(("arbitrary",)), name=name)(h, o, wxo, gain, wrh, wrl, cnt_in)


def _mid(x, gc, at, g, mk, mv, wuc, wua, wmo, gx, wxq, wxo, gf, wrh, wrl, cnt_in, *, tm, seq_len, ngroups, epg, xh):
    t, dm = x.shape
    nm, xd = mk.shape[1:]
    seq_blocks = seq_len // tm
    kern = functools.partial(_mid_kernel, tm=tm, ngroups=ngroups, epg=epg, xh=xh)
    rows = lambda w: pl.BlockSpec((tm, w), lambda i: (i, 0))
    mem = pl.BlockSpec((None, nm, xd), lambda i: (i // seq_blocks, 0, 0))
    consts = [wuc, wua, wmo, gx, wxq, wxo, gf, wrh, wrl, cnt_in]
    return pl.pallas_call(
        kern, grid=(t // tm,),
        in_specs=[rows(dm), rows(gc.shape[1]), rows(at.shape[1]), rows(g.shape[1]), mem, mem]
        + [_const_spec(a.shape) for a in consts],
        out_specs=[rows(dm), pl.BlockSpec((tm * (dm // (2 * LANES)), LANES), lambda i: (i, 0)),
                   pl.BlockSpec((ROUTE_ROWS, tm), lambda i: (0, i)), rows(LANES), _const_spec((LANES, LANES))],
        out_shape=[jax.ShapeDtypeStruct((t, dm), F32), jax.ShapeDtypeStruct((t * (dm // (2 * LANES)), LANES), jnp.uint32),
                   jax.ShapeDtypeStruct((ROUTE_ROWS, t), F32), jax.ShapeDtypeStruct((t, LANES), F32),
                   jax.ShapeDtypeStruct((LANES, LANES), F32)],
        scratch_shapes=[pltpu.VMEM((LANES, LANES), F32)],
        compiler_params=_cparams(("arbitrary",)), name="mid_prompt")(x, gc, at, g, mk, mv, *consts)


def _row_copy(src, r_src, dst, r_dst, sem, nc):
    return pltpu.make_async_copy(src.at[pl.ds(pl.multiple_of(r_src * nc, nc), nc)],
                                 dst.at[pl.ds(pl.multiple_of(r_dst * nc, nc), nc)], sem)


def _slab_rows(ref, nc):
    rows = ref.shape[0] // nc
    return jnp.concatenate([ref[pl.ds(c, rows, stride=nc), :] for c in range(nc)], axis=1)


def _pack_bf16_pairs(x):
    half = x.shape[1] // 2
    lo = pltpu.bitcast(x[:, 0:half].astype(BF16).astype(F32), jnp.uint32)
    hi = pltpu.bitcast(x[:, half:2 * half].astype(BF16).astype(F32), jnp.uint32)
    return lax.shift_right_logical(lo, jnp.uint32(16)) | (hi & jnp.uint32(0xFFFF0000))


def _unpack_bf16_pairs(w):
    lo = pltpu.bitcast(lax.shift_left(w, jnp.uint32(16)), F32)
    hi = pltpu.bitcast(w & jnp.uint32(0xFFFF0000), F32)
    return jnp.concatenate([lo, hi], axis=1)


def _dest_kernel(rt_ref, pstart_ref, o_ref):
    rt = rt_ref[...]
    pst = pstart_ref[:, 0:1]
    erow = lax.broadcasted_iota(jnp.int32, (LANES, rt.shape[1]), 0)
    rows = []
    for k in range(TOP_K):
        e = rt[k:k + 1, :].astype(jnp.int32)
        rows.append(jnp.sum(jnp.where(erow == e, pst, 0.0), axis=0, keepdims=True) + rt[4 + k:5 + k, :])
    rows.append(jnp.zeros((ROUTE_ROWS - TOP_K, rt.shape[1]), F32))
    o_ref[...] = jnp.concatenate(rows, axis=0).astype(jnp.int32)


def _dest(rt_all, pstart_col):
    tall = rt_all.shape[1]
    nchunk = 3 if tall % (3 * LANES) == 0 else 1
    w = tall // nchunk
    return pl.pallas_call(
        _dest_kernel, grid=(nchunk,),
        in_specs=[pl.BlockSpec((ROUTE_ROWS, w), lambda i: (0, i)), _const_spec((LANES, LANES))],
        out_specs=pl.BlockSpec((ROUTE_ROWS, w), lambda i: (0, i)),
        out_shape=jax.ShapeDtypeStruct((ROUTE_ROWS, tall), jnp.int32),
        compiler_params=_cparams(("parallel",)), name="dest")(rt_all, pstart_col)


def _scatter_kernel(dest_ref, pstart_ref, pend_ref, xp_ref, xs_ref, out_ref, zeros, sem, zsem,
                    *, tm_p, nblk_p, nexp, nc):
    i = pl.program_id(0)
    tall = dest_ref.shape[0] // TOP_K

    @pl.when(i == 0)
    def _():
        zeros[...] = jnp.zeros_like(zeros)

        def zero_block(blk):
            start = pl.multiple_of(blk * (EXPERT_ROWS * nc), EXPERT_ROWS * nc)
            return pltpu.make_async_copy(zeros, out_ref.at[pl.ds(start, EXPERT_ROWS * nc)], zsem)

        for e in range(nexp):
            @pl.when(pend_ref[e] > pstart_ref[e])
            def _():
                zero_block(pend_ref[e] // EXPERT_ROWS - 1).start()
        for e in range(nexp):
            @pl.when(pend_ref[e] > pstart_ref[e])
            def _():
                zero_block(pend_ref[e] // EXPERT_ROWS - 1).wait()
        nblk = out_ref.shape[0] // (EXPERT_ROWS * nc)
        nused = pend_ref[nexp - 1] // EXPERT_ROWS

        def start_unused(blk, c):
            zero_block(blk).start()
            return c

        def wait_unused(blk, c):
            zero_block(blk).wait()
            return c

        lax.fori_loop(nused, nblk, start_unused, 0)
        lax.fori_loop(nused, nblk, wait_unused, 0)

    def copy_rows(x_ref, base):
        tm = x_ref.shape[0] // nc

        def issue(r, c):
            for k in range(TOP_K):
                _row_copy(x_ref, r, out_ref, dest_ref[k * tall + base + r], sem, nc).start(priority=k % 2)
            return c

        lax.fori_loop(0, tm, issue, 0, unroll=8)
        for k in range(TOP_K):
            pltpu.make_async_copy(x_ref, x_ref, sem).wait()

    @pl.when(i < nblk_p)
    def _():
        copy_rows(xp_ref, i * tm_p)

    @pl.when(i == nblk_p)
    def _():
        copy_rows(xs_ref, nblk_p * tm_p)


def _scatter(dest_flat, pstart, pend, x_p, x_s, *, n_rows, tm_p, nexp, nc):
    tp = x_p.shape[0] // nc
    ts = x_s.shape[0] // nc
    nblk_p = tp // tm_p
    kern = functools.partial(_scatter_kernel, tm_p=tm_p, nblk_p=nblk_p, nexp=nexp, nc=nc)
    grid_spec = pltpu.PrefetchScalarGridSpec(
        num_scalar_prefetch=3, grid=(nblk_p + 1,),
        in_specs=[pl.BlockSpec((tm_p * nc, LANES), lambda i, *_: (jnp.minimum(i, nblk_p - 1), 0)),
                  pl.BlockSpec((ts * nc, LANES), lambda i, *_: (0, 0))],
        out_specs=pl.BlockSpec(memory_space=pl.ANY),
        scratch_shapes=[pltpu.VMEM((EXPERT_ROWS * nc, LANES), jnp.uint32), pltpu.SemaphoreType.DMA(()),
                        pltpu.SemaphoreType.DMA(())])
    return pl.pallas_call(
        kern, grid_spec=grid_spec, out_shape=jax.ShapeDtypeStruct((n_rows * nc, LANES), jnp.uint32),
        compiler_params=_cparams(("arbitrary",)), name="scatter")(dest_flat, pstart, pend, x_p, x_s)


def _experts_kernel(blk_e_ref, nused_ref, xs_ref, wg_ref, wu_ref, wd_ref, y_ref, wgb, wub, wdb, *, nc):
    i = pl.program_id(0)
    prev = blk_e_ref[jnp.maximum(i - 1, 0)]
    fresh = (i == 0) | (blk_e_ref[i] != prev)

    @pl.when(i < nused_ref[0])
    def _():
        @pl.when(fresh)
        def _():
            wgb[...] = wg_ref[...].astype(BF16)
            wub[...] = wu_ref[...].astype(BF16)
            wdb[...] = wd_ref[...].astype(BF16)
        x = _unpack_bf16_pairs(_slab_rows(xs_ref, nc)).astype(BF16)
        a = _dot(x, wgb[...])
        u = _dot(x, wub[...])
        hmid = (a * jax.nn.sigmoid(a)) * u
        y = _pack_bf16_pairs(_dot(hmid.astype(BF16), wdb[...]))
        for c in range(nc):
            y_ref[pl.ds(c, EXPERT_ROWS, stride=nc), :] = y[:, c * LANES:(c + 1) * LANES]

    @pl.when(i >= nused_ref[0])
    def _():
        y_ref[...] = jnp.zeros_like(y_ref)


def _experts(blk_e, nused, xs, wg, wu, wd):
    dm, de = wg.shape[1:]
    nc = dm // (2 * LANES)
    nblk = xs.shape[0] // (EXPERT_ROWS * nc)

    def row_map(i, be, nu):
        return (jnp.minimum(i, nu[0] - 1), 0)

    grid_spec = pltpu.PrefetchScalarGridSpec(
        num_scalar_prefetch=2, grid=(nblk,),
        in_specs=[pl.BlockSpec((EXPERT_ROWS * nc, LANES), row_map),
                  pl.BlockSpec((None, dm, de), lambda i, be, nu: (be[i], 0, 0)),
                  pl.BlockSpec((None, dm, de), lambda i, be, nu: (be[i], 0, 0)),
                  pl.BlockSpec((None, de, dm), lambda i, be, nu: (be[i], 0, 0))],
        out_specs=pl.BlockSpec((EXPERT_ROWS * nc, LANES), lambda i, be, nu: (i, 0)),
        scratch_shapes=[pltpu.VMEM((dm, de), BF16), pltpu.VMEM((dm, de), BF16), pltpu.VMEM((de, dm), BF16)])
    return pl.pallas_call(
        functools.partial(_experts_kernel, nc=nc), grid_spec=grid_spec,
        out_shape=jax.ShapeDtypeStruct(xs.shape, jnp.uint32),
        compiler_params=_cparams(("arbitrary",)), name="experts")(blk_e, nused, xs, wg, wu, wd)


def _combine_kernel(dest_ref, h_ref, rr_ref, gain_ref, y_hbm, o_ref, buf, sem, *, tm, tok0):
    i = pl.program_id(0)
    nc = h_ref.shape[1] // (2 * LANES)
    tall = dest_ref.shape[0] // TOP_K
    base = tok0 + i * tm

    def issue(r, c):
        for k in range(TOP_K):
            _row_copy(y_hbm, dest_ref[k * tall + base + r], buf.at[k], r, sem, nc).start(priority=k % 2)
        return c

    lax.fori_loop(0, tm, issue, 0, unroll=8)
    for k in range(TOP_K):
        pltpu.make_async_copy(buf.at[k], buf.at[k], sem).wait()
    rr = rr_ref[...]
    y0 = _unpack_bf16_pairs(_slab_rows(buf.at[0], nc))
    y1 = _unpack_bf16_pairs(_slab_rows(buf.at[1], nc))
    h = h_ref[...] + (rr[:, 2:3] * y0 + rr[:, 3:4] * y1)
    o_ref[...] = _rms(h, gain_ref[...])


def _combine(dest_flat, h, rr, gain, y, *, tm, tok0, name):
    t, dm = h.shape
    kern = functools.partial(_combine_kernel, tm=tm, tok0=tok0)
    grid_spec = pltpu.PrefetchScalarGridSpec(
        num_scalar_prefetch=1, grid=(t // tm,),
        in_specs=[pl.BlockSpec((tm, dm), lambda i, *_: (i, 0)), pl.BlockSpec((tm, LANES), lambda i, *_: (i, 0)),
                  pl.BlockSpec((1, dm), lambda i, *_: (0, 0)), pl.BlockSpec(memory_space=pl.ANY)],
        out_specs=pl.BlockSpec((tm, dm), lambda i, *_: (i, 0)),
        scratch_shapes=[pltpu.VMEM((TOP_K, tm * (dm // (2 * LANES)), LANES), jnp.uint32),
                        pltpu.SemaphoreType.DMA(())])
    return pl.pallas_call(
        kern, grid_spec=grid_spec, out_shape=jax.ShapeDtypeStruct((t, dm), F32),
        compiler_params=_cparams(("arbitrary",)), name=name)(dest_flat, h, rr, gain, y)


def kernel(x_prompt, x_sample, cache_k, cache_v, cache_logf, cache_mem_k, cache_mem_v, state_conv, page_table,
           mem_prompt, norm_mix, w_in, b_forget, conv_w, conv_b, w_up_conv, w_up_attn, w_mix_out, norm_xattn,
           norm_mem, w_xq, w_xk, w_xv, w_xo, norm_ffn, w_router_group, w_router_expert, w_expert_gate,
           w_expert_up, w_expert_down, norm_final):
    depth = w_in.shape[0]
    assert depth == 1, "single-layer trunk"
    nbp, seq, dm = x_prompt.shape
    nbs, s_new, _ = x_sample.shape
    _, n_pool, page, heads, dh = cache_k.shape
    npages = page_table.shape[1]
    nmem, xh, xd = cache_mem_k.shape[2:]
    dc = conv_w.shape[2]
    da = heads * dh
    dx = xh * xd
    ngroups, _, epg = w_router_expert.shape[1:]
    nexp = ngroups * epg
    tp = nbp * seq
    ts = nbs * s_new
    assert conv_w.shape[1] == 3 and s_new >= 2 and dh == 64 and heads == SUBLANES and epg == SUBLANES
    assert page == LANES

    l = 0
    wi = w_in[l]
    wc = wi[:, 0:3 * dc].astype(BF16)
    wqkv = wi[:, 3 * dc:3 * dc + 3 * da].astype(BF16)
    o_f = 3 * dc + 3 * da
    wf = jnp.pad(wi[:, o_f:o_f + heads], ((0, 0), (0, LANES - heads)))
    wfh = wf.astype(BF16)
    wf2 = jnp.concatenate([wfh, (wf - wfh.astype(F32)).astype(BF16)], axis=1)
    wg = wi[:, o_f + heads:].astype(BF16)
    bfp = jnp.pad(b_forget[l][None, :], ((0, 0), (0, LANES - heads)))
    cw = conv_w[l]
    cbias = conv_b[l][None, :]
    g_mix = norm_mix[l][None, :]
    wuc = w_up_conv[l].astype(BF16)
    wua = w_up_attn[l].astype(BF16)
    wmo = w_mix_out[l].astype(BF16)
    g_x = norm_xattn[l][None, :]
    wxq = w_xq[l].astype(BF16)
    wxo = w_xo[l].astype(BF16)
    g_f = norm_ffn[l][None, :]
    wr = jnp.zeros((LANES, dm), F32)
    wr = wr.at[0:ngroups].set(w_router_group[l].T)
    wr = wr.at[EXPERT_ROW0:EXPERT_ROW0 + nexp].set(jnp.transpose(w_router_expert[l], (0, 2, 1)).reshape(nexp, dm))
    wrh = wr.astype(BF16)
    wrl = (wr - wrh.astype(F32)).astype(BF16)

    xp = x_prompt.reshape(tp, dm)
    (gc_p, qa_p, ka_p, vb_p, kt_p, vt_p, g_p, lf_p, ulast_p) = _inproj(
        xp, g_mix, wc, wqkv, wg, wf2, bfp, cw, cbias, seq_len=seq, sample=False, aug=_decay_columns(heads))
    xs_ = x_sample.reshape(ts, dm)
    st = state_conv[l]
    zeros_row = jnp.zeros((nbs, 1, dc), F32)
    fix1 = jnp.concatenate([st[:, 1:2], jnp.tile(zeros_row, (1, s_new - 1, 1))], axis=1).reshape(ts, dc)
    fix2 = jnp.concatenate([st[:, 0:1], st[:, 1:2], jnp.tile(zeros_row, (1, s_new - 2, 1))], axis=1).reshape(ts, dc)
    (gc_s, q_s, kb_s, vb_s, kf_s, vf_s, g_s, lf_s, dt_s, u_s) = _inproj(
        xs_, g_mix, wc, wqkv, wg, wf2, bfp, cw, cbias, seq_len=s_new, sample=True, fix=(fix1, fix2))
    pt_flat = page_table.reshape(-1).astype(jnp.int32)
    head_of_col = jnp.arange(da) // dh
    qbd = jnp.where(head_of_col[None, None, None, :] == jnp.arange(heads)[None, None, :, None],
                    q_s.reshape(nbs, s_new, 1, da), jnp.zeros((), BF16)).reshape(nbs, s_new * heads, da)
    kt_pages = jnp.transpose(cache_k[l], (0, 2, 3, 1)).reshape(n_pool, da, page)
    vt_pages = jnp.transpose(cache_v[l], (0, 2, 3, 1)).reshape(n_pool, da, page)
    lf_pages = jnp.swapaxes(cache_logf[l], 1, 2)
    at_p, at_s = _fox_fused(pt_flat, qa_p, ka_p, vb_p, kt_pages, vt_pages, lf_pages, qbd, kb_s, vb_s, dt_s,
                            nseq=nbp, seq_len=seq, dh=dh, npages=npages, s_new=s_new, heads=heads)

    tm_p = min(512, seq)
    mk_p, mv_p = _memkv(mem_prompt.reshape(nbp * nmem, dm), norm_mem[l][None, :], w_xk[l].astype(BF16),
                        w_xv[l].astype(BF16), xh=xh)
    cnt0 = jnp.zeros((LANES, LANES), F32)
    h2_p, xn_p, rt_p, rr_p, cnt1 = _mid(
        xp, gc_p, at_p, g_p, mk_p.reshape(nbp, nmem * xh, xd), mv_p.reshape(nbp, nmem * xh, xd),
        wuc, wua, wmo, g_x, wxq, wxo, g_f, wrh, wrl, cnt0, tm=tm_p, seq_len=seq, ngroups=ngroups, epg=epg, xh=xh)

    h1_s, qx_s = _post_attn(xs_, gc_s, at_s.reshape(ts, da), g_s, wuc, wua, wmo, g_x, wxq, tm=ts,
                            name="post_attn_sample")
    qx_s8 = jnp.pad(qx_s.astype(F32).reshape(nbs, s_new, dx), ((0, 0), (0, SUBLANES - s_new), (0, 0)))
    o_s = _xattn(qx_s8, cache_mem_k[l].reshape(nbs, nmem * xh, xd), cache_mem_v[l].reshape(nbs, nmem * xh, xd),
                 tq=SUBLANES, xh=xh, nbb=4 if nbs % 4 == 0 else 1, name="xattn_sample")[:, :s_new].reshape(ts, dx)
    h2_s, xn_s, rt_s, rr_s, cnt2 = _pre_moe(h1_s, o_s, wxo, g_f, wrh, wrl, cnt1, tm=ts, ngroups=ngroups, epg=epg,
                                            name="pre_moe_sample")

    tall = tp + ts
    counts = cnt2[0:nexp, 0].astype(jnp.int32)
    padded = (counts + EXPERT_ROWS - 1) // EXPERT_ROWS * EXPERT_ROWS
    pend = jnp.cumsum(padded).astype(jnp.int32)
    pstart = pend - padded
    nblk = (tall * TOP_K + nexp * (EXPERT_ROWS - 1)) // EXPERT_ROWS
    n_rows = nblk * EXPERT_ROWS
    blk_row0 = jnp.arange(nblk, dtype=jnp.int32) * EXPERT_ROWS
    blk_e = jnp.minimum(jnp.sum((pend[None, :] <= blk_row0[:, None]).astype(jnp.int32), axis=1), nexp - 1)
    nused = (pend[nexp - 1:nexp] // EXPERT_ROWS).astype(jnp.int32)
    rt_all = jnp.concatenate([rt_p, rt_s], axis=1)
    pstart_col = jnp.zeros((LANES, LANES), F32).at[0:nexp, :].set(pstart.astype(F32)[:, None])
    dest_flat = _dest(rt_all, pstart_col)[0:TOP_K].reshape(-1)
    xsg = _scatter(dest_flat, pstart, pend, xn_p, xn_s, n_rows=n_rows, tm_p=tm_p, nexp=nexp, nc=dm // (2 * LANES))
    y = _experts(blk_e, nused, xsg, w_expert_gate[l], w_expert_up[l], w_expert_down[l])
    g_fin = norm_final[None, :]
    y_p = _combine(dest_flat, h2_p, rr_p, g_fin, y, tm=tm_p, tok0=0, name="combine_prompt")
    y_s = _combine(dest_flat, h2_s, rr_s, g_fin, y, tm=ts, tok0=tp, name="combine_sample")

    return (y_p.reshape(nbp, seq, dm), y_s.reshape(nbs, s_new, dm),
            ulast_p[None],
            jnp.transpose(kt_p.reshape(nbp, heads, dh, seq), (0, 3, 1, 2))[None],
            jnp.transpose(vt_p.reshape(nbp, heads, dh, seq), (0, 3, 1, 2))[None],
            jnp.transpose(lf_p, (0, 2, 1))[None],
            mk_p.reshape(1, nbp, nmem, xh, xd), mv_p.reshape(1, nbp, nmem, xh, xd),
            u_s.reshape(nbs, s_new, dc)[None, :, s_new - 2:], kf_s.reshape(1, nbs, s_new, heads, dh),
            vf_s.reshape(1, nbs, s_new, heads, dh), lf_s[:, :heads].reshape(1, nbs, s_new, heads))
```

```python
import functools

import numpy as np
import jax
import jax.numpy as jnp
from jax import lax
from jax.experimental import pallas as pl
from jax.experimental.pallas import tpu as pltpu

F32 = jnp.float32
BF16 = jnp.bfloat16

RMS_EPS = 1e-6
TOP_K = 2
LANES = 128
SUBLANES = 8
VMEM_LIMIT = 56 * 1024 * 1024
NEG_BIG = -1e30
EXPERT_ROWS = 512
FUSED_PAGES_PER_ITERATION = 8
FUSED_RING = 3
FOX_QUERY_BLOCK = 512
FOX_KEY_BLOCK = 512
ROUTE_ROWS = 8
EXPERT_ROW0 = 8


def _cparams(sem, vmem=VMEM_LIMIT):
    return pltpu.CompilerParams(dimension_semantics=sem, vmem_limit_bytes=vmem)


def _rms(x, g):
    ms = jnp.mean(x * x, axis=-1, keepdims=True)
    return x * lax.rsqrt(ms + RMS_EPS) * g


def _split3(x):
    hi = x.astype(BF16)
    r = x - hi.astype(F32)
    mid = r.astype(BF16)
    lo = (r - mid.astype(F32)).astype(BF16)
    return hi, mid, lo


def _dot(a, b):
    return jnp.dot(a, b, preferred_element_type=F32)


def _dot_nt(a, b):
    return lax.dot_general(a, b, (((1,), (1,)), ((), ())), preferred_element_type=F32)


def _lane_tile(x, width):
    if width % LANES == 0:
        return jnp.concatenate([x] * (width // LANES), axis=1)
    return jnp.broadcast_to(x[:, 0:1], (x.shape[0], width))


def _const_spec(shape):
    nd = len(shape)
    return pl.BlockSpec(shape, lambda *_: (0,) * nd)


def _inproj_kernel(*refs, tm, seq_blocks, sample, seq_len):
    if sample:
        (x_ref, gain_ref, wc_ref, wqkv_ref, wg_ref, wf2_ref, bf_ref, cw_ref, cb_ref, fix1_ref, fix2_ref,
         gc_ref, q_ref, kbf_ref, vbf_ref, kf_ref, vf_ref, g_ref, logf_ref, dt_ref, u_ref) = refs
    else:
        (x_ref, gain_ref, wc_ref, wqkv_ref, wg_ref, wf2_ref, bf_ref, cw_ref, cb_ref,
         selq_ref, selk_ref, oneq_ref, onek_ref,
         gc_ref, qa_ref, ka_ref, vbf_ref, kf_ref, vf_ref, g_ref, logf_ref, u_ref,
         carry_u, carry_d) = refs
    i = pl.program_id(0)
    dc = cw_ref.shape[1]
    da = wqkv_ref.shape[1] // 3

    xn = _rms(x_ref[...], gain_ref[...])
    xb = xn.astype(BF16)

    cb = _dot(xb, wc_ref[:, 0:dc])
    cc = _dot(xb, wc_ref[:, dc:2 * dc])
    cx = _dot(xb, wc_ref[:, 2 * dc:3 * dc])
    u = cc * cx
    row = lax.broadcasted_iota(jnp.int32, (tm, 1), 0)
    r1 = pltpu.roll(u, 1, axis=0)
    r2 = pltpu.roll(u, 2, axis=0)
    if sample:
        pos = row % seq_len
        p1 = jnp.where(pos == 0, fix1_ref[...], r1)
        p2 = jnp.where(pos < 2, fix2_ref[...], r2)
        u_ref[...] = u
    else:
        @pl.when(i % seq_blocks == 0)
        def _():
            carry_u[...] = jnp.zeros_like(carry_u)
            carry_d[...] = jnp.zeros_like(carry_d)
        c0 = carry_u[0:1, :]
        c1 = carry_u[1:2, :]
        p1 = jnp.where(row == 0, c1, r1)
        p2 = jnp.where(row == 0, c0, jnp.where(row == 1, c1, r2))
        carry_u[0:2, :] = u[tm - 2:tm, :]
        u_ref[...] = u[tm - 2:tm, :]
    cw = cw_ref[...]
    cy = cb_ref[...] + cw[0:1, :] * p2 + cw[1:2, :] * p1 + cw[2:3, :] * u
    gc_ref[...] = (cb * cy).astype(BF16)

    qs = (_dot(xb, wqkv_ref[:, 0:da]) * (1.0 / 8.0)).astype(BF16)
    k = _dot(xb, wqkv_ref[:, da:2 * da])
    kb = k.astype(BF16)
    v = _dot(xb, wqkv_ref[:, 2 * da:3 * da])
    vbf_ref[...] = v.astype(BF16)
    if sample:
        q_ref[...] = qs
        kbf_ref[...] = kb
        kf_ref[...] = k
        vf_ref[...] = v
    else:
        kf_ref[...] = jnp.transpose(k)
        vf_ref[...] = jnp.transpose(v)

    gw = g_ref.shape[1]
    for c in range(gw // 512):
        g_ref[:, c * 512:(c + 1) * 512] = _dot(xb, wg_ref[:, c * 512:(c + 1) * 512]).astype(BF16)

    xl = (xn - xb.astype(F32)).astype(BF16)
    hh_hl = _dot(xb, wf2_ref[...])
    fz = hh_hl[:, 0:LANES] + (hh_hl[:, LANES:2 * LANES] + _dot(xl, wf2_ref[:, 0:LANES]))
    z = fz + bf_ref[...]
    logf = jnp.minimum(z, 0.0) - jnp.log1p(jnp.exp(-jnp.abs(z)))
    lane = lax.broadcasted_iota(jnp.int32, (1, LANES), 1)
    nh = da // 64
    logf = jnp.where(lane < nh, logf, 0.0)
    if sample:
        logf_ref[...] = logf
    else:
        logf_ref[...] = jnp.transpose(logf)[0:SUBLANES, :]

    rr = lax.broadcasted_iota(jnp.int32, (tm, tm), 0)
    cc_ = lax.broadcasted_iota(jnp.int32, (tm, tm), 1)
    if sample:
        tri = (cc_ <= rr) & ((rr // seq_len) == (cc_ // seq_len))
    else:
        tri = cc_ <= rr
    tri = jnp.where(tri, 1.0, 0.0).astype(BF16)
    d3 = _dot(tri, jnp.concatenate(_split3(logf), axis=1))
    d = d3[:, 0:LANES] + (d3[:, LANES:2 * LANES] + d3[:, 2 * LANES:3 * LANES])
    if sample:
        dt_ref[...] = jnp.transpose(d)[0:SUBLANES, :]
    else:
        d = d + carry_d[0:1, :]
        carry_d[0:1, :] = d[tm - 1:tm, :]
        dcat = jnp.concatenate(_split3(d), axis=1)
        aq = (_dot(dcat, selq_ref[...]) + oneq_ref[...]).astype(BF16)
        ak = (_dot(dcat, selk_ref[...]) + onek_ref[...]).astype(BF16)
        for p in range(da // LANES):
            lo_, hi_ = p * LANES, (p + 1) * LANES
            qa_ref[:, 2 * lo_:2 * lo_ + LANES] = qs[:, lo_:hi_]
            qa_ref[:, 2 * lo_ + LANES:2 * hi_] = aq[:, lo_:hi_]
            ka_ref[:, 2 * lo_:2 * lo_ + LANES] = kb[:, lo_:hi_]
            ka_ref[:, 2 * lo_ + LANES:2 * hi_] = ak[:, lo_:hi_]


def _decay_columns(heads):
    npair = heads // 2
    selq = np.zeros((3 * LANES, npair * LANES), np.float32)
    selk = np.zeros((3 * LANES, npair * LANES), np.float32)
    oneq = np.zeros((1, npair * LANES), np.float32)
    onek = np.zeros((1, npair * LANES), np.float32)
    for p in range(npair):
        for hh in range(2):
            for term in range(3):
                selq[term * LANES + 2 * p + hh, p * LANES + 3 * hh + term] = 1.0
                selk[term * LANES + 2 * p + hh, p * LANES + 6 + 3 * hh + term] = -1.0
                oneq[0, p * LANES + 6 + 3 * hh + term] = 1.0
                onek[0, p * LANES + 3 * hh + term] = 1.0
    return jnp.asarray(selq, BF16), jnp.asarray(selk, BF16), jnp.asarray(oneq), jnp.asarray(onek)


def _inproj(x, gain, wc, wqkv, wg, wf2, bfp, cw, cbias, *, seq_len, sample, fix=None, aug=None):
    t, dm = x.shape
    dc = cw.shape[1]
    da = wqkv.shape[1] // 3
    gw = wg.shape[1]
    if sample:
        tm = t
        seq_blocks = 1
    else:
        tm = min(512, seq_len)
        seq_blocks = seq_len // tm
    nblk = t // tm
    nseq = t // seq_len
    kern = functools.partial(_inproj_kernel, tm=tm, seq_blocks=seq_blocks, sample=sample, seq_len=seq_len)
    rows = lambda w: pl.BlockSpec((tm, w), lambda i: (i, 0))
    in_specs = [rows(dm), _const_spec((1, dm)), _const_spec(wc.shape), _const_spec(wqkv.shape), _const_spec(wg.shape),
                _const_spec(wf2.shape), _const_spec((1, LANES)), _const_spec(cw.shape), _const_spec((1, dc))]
    args = [x, gain, wc, wqkv, wg, wf2, bfp, cw, cbias]
    sds = jax.ShapeDtypeStruct
    if sample:
        in_specs += [rows(dc), rows(dc)]
        args += list(fix)
        qk_shapes = [sds((t, da), BF16), sds((t, da), BF16)]
        qk_specs = [rows(da), rows(da)]
        kv_shapes = [sds((t, da), F32), sds((t, da), F32)]
        kv_specs = [rows(da), rows(da)]
        tail_shapes = [sds((t, LANES), F32), sds((SUBLANES, t), F32), sds((t, dc), F32)]
        tail_specs = [rows(LANES), pl.BlockSpec((SUBLANES, tm), lambda i: (0, i)), rows(dc)]
        scratch = []
    else:
        in_specs += [_const_spec(a.shape) for a in aug]
        args += list(aug)
        qk_shapes = [sds((t, 2 * da), BF16), sds((t, 2 * da), BF16)]
        qk_specs = [rows(2 * da), rows(2 * da)]
        kv_shapes = [sds((nseq, da, seq_len), F32), sds((nseq, da, seq_len), F32)]
        kv_specs = [pl.BlockSpec((None, da, tm), lambda i: (i // seq_blocks, 0, i % seq_blocks))] * 2
        tail_shapes = [sds((nseq, SUBLANES, seq_len), F32), sds((nseq, 2, dc), F32)]
        tail_specs = [pl.BlockSpec((None, SUBLANES, tm), lambda i: (i // seq_blocks, 0, i % seq_blocks)),
                      pl.BlockSpec((None, 2, dc), lambda i: (i // seq_blocks, 0, 0))]
        scratch = [pltpu.VMEM((SUBLANES, dc), F32), pltpu.VMEM((SUBLANES, LANES), F32)]
    out_shape = ([sds((t, dc), BF16)] + qk_shapes +
                 [sds((t, da), BF16)] + kv_shapes +
                 [sds((t, gw), BF16)] + tail_shapes)
    out_specs = [rows(dc)] + qk_specs + [rows(da)] + kv_specs + [rows(gw)] + tail_specs
    return pl.pallas_call(
        kern, grid=(nblk,), in_specs=in_specs, out_specs=out_specs, out_shape=out_shape,
        scratch_shapes=scratch, compiler_params=_cparams(("arbitrary",)),
        name="inproj_sample" if sample else "inproj_prompt")(*args)


def _fox_fused_kernel(pt_ref, qa_ref, ka_ref, v_ref, kt_hbm, vt_hbm, lf_hbm, qbd_ref, kn_ref, vn_ref, dtn_ref,
                      o_ref, os_ref,
                      m_sc, l_sc, acc_sc, ms_sc, ls_sc, as_sc, run_sc, kbuf, vbuf, lbuf, sem, g_ref,
                      *, tq, tk, dh, pp, ring, npages, nchunks, s_new, heads):
    first_step = (pl.program_id(0) == 0) & (pl.program_id(1) == 0) & (pl.program_id(2) == 0)
    qi = pl.program_id(2)
    page = kbuf.shape[3]
    cpb = npages // pp
    nrow = s_new * heads
    da = heads * dh
    nbs = qbd_ref.shape[0]

    def chunk_copies(c):
        slot = c % ring
        bs = c // cpb
        jc = c % cpb
        copies = []
        for i in range(pp):
            pid = pt_ref[bs * npages + (npages - 1 - (jc * pp + i))]
            copies.append(pltpu.make_async_copy(kt_hbm.at[pid], kbuf.at[slot, i], sem.at[slot]))
            copies.append(pltpu.make_async_copy(vt_hbm.at[pid], vbuf.at[slot, i], sem.at[slot]))
            copies.append(pltpu.make_async_copy(lf_hbm.at[pid], lbuf.at[slot, i], sem.at[slot]))
        return copies

    @pl.when(first_step)
    def _():
        g_ref[0] = 0
        for c in range(min(ring - 1, nchunks)):
            for cp in chunk_copies(c):
                cp.start()

    lane = lax.broadcasted_iota(jnp.int32, (1, 2 * LANES), 1)
    ext = lane - LANES
    qf = qa_ref[...].astype(F32)
    halves = []
    for h in range(2):
        keep = (((lane >= h * dh) & (lane < (h + 1) * dh))
                | ((ext >= 3 * h) & (ext < 3 * h + 3)) | ((ext >= 6 + 3 * h) & (ext < 9 + 3 * h)))
        halves.append(jnp.where(keep, qf, 0.0))
    qs = jnp.concatenate(halves, axis=0).astype(BF16)
    m_sc[...] = jnp.full_like(m_sc, NEG_BIG)
    l_sc[...] = jnp.zeros_like(l_sc)
    acc_sc[...] = jnp.zeros_like(acc_sc)
    nfull = (qi * tq) // tk

    def prompt_step(j, masked):
        ks = pl.multiple_of(j * tk, tk)
        kb = ka_ref[pl.ds(ks, tk), :]
        vb = v_ref[pl.ds(ks, tk), :]
        s = _dot_nt(qs, kb)
        if masked:
            rloc = lax.broadcasted_iota(jnp.int32, (2 * tq, tk), 0)
            rloc = jnp.where(rloc >= tq, rloc - tq, rloc) + qi * tq
            cloc = lax.broadcasted_iota(jnp.int32, (2 * tq, tk), 1) + ks
            s = jnp.where(cloc <= rloc, s, NEG_BIG)
        m_prev = m_sc[...]
        m_new = jnp.maximum(m_prev, jnp.max(s, axis=1, keepdims=True))
        p = jnp.exp(s - _lane_tile(m_new, tk))
        alpha = jnp.exp(m_prev - m_new)
        l_sc[...] = alpha * l_sc[...] + jnp.sum(p, axis=1, keepdims=True)
        acc_sc[...] = alpha * acc_sc[...] + _dot(p.astype(BF16), vb)
        m_sc[...] = m_new

    def sample_update(s, pv_fn):
        m_prev = ms_sc[...]
        m_new = jnp.maximum(m_prev, jnp.max(s, axis=1, keepdims=True))
        p = jnp.exp(s - _lane_tile(m_new, s.shape[1]))
        alpha = jnp.exp(m_prev - m_new)
        ls_sc[...] = alpha * ls_sc[...] + jnp.sum(p, axis=1, keepdims=True)
        as_sc[...] = _lane_tile(alpha, da) * as_sc[...] + pv_fn(p.astype(BF16))
        ms_sc[...] = m_new

    def sample_chunk(g, valid):
        slot = g % ring
        bs = jnp.minimum(g // cpb, nbs - 1)
        qbd = qbd_ref[bs]
        rr = lax.broadcasted_iota(jnp.int32, (page, 2 * page), 0)
        cc = lax.broadcasted_iota(jnp.int32, (page, 2 * page), 1)
        after = jnp.where((rr > cc) | (cc >= page), 1.0, 0.0).astype(BF16)
        lf = jnp.concatenate([lbuf[slot, i] for i in range(pp)], axis=0)
        n8 = pp * heads
        r3 = _dot(jnp.concatenate(_split3(lf), axis=0), after)
        both = r3[0:n8] + (r3[n8:2 * n8] + r3[2 * n8:3 * n8])
        run = run_sc[...]
        scores = []
        for i in range(pp):
            rev = both[i * heads:(i + 1) * heads, 0:page] + run
            run = run + both[i * heads:(i + 1) * heads, page:2 * page]
            scores.append(_dot(qbd, kbuf[slot, i].astype(BF16)) + jnp.concatenate([rev] * s_new, axis=0))
        run_sc[...] = run
        s = jnp.where(valid, jnp.concatenate(scores, axis=1), NEG_BIG)

        def pv_pages(p):
            acc = None
            for i in range(pp):
                term = _dot_nt(p[:, i * page:(i + 1) * page], vbuf[slot, i].astype(BF16))
                acc = term if acc is None else acc + term
            return acc

        sample_update(s, pv_pages)

    def sample_finish(bs):
        ntok = kn_ref.shape[0]
        qbd = qbd_ref[bs]
        dtn = dtn_ref[...]
        lane_t = lax.broadcasted_iota(jnp.int32, (1, ntok), 1)
        dq_rows = [jnp.sum(jnp.where(lane_t == bs * s_new + t, dtn, 0.0), axis=1, keepdims=True)
                   for t in range(s_new)]
        dq = jnp.concatenate(dq_rows, axis=0)
        dk = jnp.concatenate([dtn] * s_new, axis=0)
        s = _dot_nt(qbd, kn_ref[...]) + (dq - dk)
        rowt = lax.broadcasted_iota(jnp.int32, (nrow, ntok), 0) // heads
        col = lax.broadcasted_iota(jnp.int32, (nrow, ntok), 1)
        keep = ((col // s_new) == bs) & ((col % s_new) <= rowt)
        sample_update(jnp.where(keep, s, NEG_BIG), lambda p: _dot(p, vn_ref[...]))
        o = as_sc[...] / _lane_tile(ls_sc[...], da)
        rowh = lax.broadcasted_iota(jnp.int32, (nrow, da), 0) % heads
        colh = lax.broadcasted_iota(jnp.int32, (nrow, da), 1) // dh
        o = jnp.where(rowh == colh, o, 0.0)
        os_ref[bs] = jnp.sum(o.reshape(s_new, heads, da), axis=1)

    def iteration(j, masked):
        g = g_ref[0]
        valid = g < nchunks

        @pl.when(g + (ring - 1) < nchunks)
        def _():
            for cp in chunk_copies(g + (ring - 1)):
                cp.start()

        @pl.when(valid)
        def _():
            for cp in chunk_copies(g):
                cp.wait()

        @pl.when(valid & (g % cpb == 0))
        def _():
            ms_sc[...] = jnp.full_like(ms_sc, NEG_BIG)
            ls_sc[...] = jnp.zeros_like(ls_sc)
            as_sc[...] = jnp.zeros_like(as_sc)
            run_sc[...] = jnp.zeros_like(run_sc)

        prompt_step(j, masked)
        sample_chunk(g, valid)

        @pl.when(valid & (g % cpb == cpb - 1))
        def _():
            sample_finish(g // cpb)

        g_ref[0] = g + 1

    def body(j, c):
        iteration(j, False)
        return c

    def prompt_body(j, c):
        prompt_step(j, False)
        return c

    stream_live = g_ref[0] < nchunks

    @pl.when(stream_live)
    def _():
        lax.fori_loop(0, nfull, body, 0)
        iteration(nfull, True)

    @pl.when(jnp.logical_not(stream_live))
    def _():
        lax.fori_loop(0, nfull, prompt_body, 0)
        prompt_step(nfull, True)

    o = acc_sc[...] / l_sc[...]
    lane_o = lax.broadcasted_iota(jnp.int32, (1, LANES), 1)
    o_ref[...] = jnp.where(lane_o < dh, o[0:tq], o[tq:2 * tq]).astype(o_ref.dtype)


def _fox_fused(page_table_flat, qa, ka, v, kt_pages, vt_pages, lf_pages, qbd, kn, vn, dtn,
               *, nseq, seq_len, dh, npages, s_new, heads):
    t, da = v.shape
    nbs, nrow, _ = qbd.shape
    page = kt_pages.shape[2]
    ntok = kn.shape[0]
    tk = min(FOX_KEY_BLOCK, seq_len)
    tq = min(FOX_QUERY_BLOCK, tk)
    nq = seq_len // tq
    npair = da // LANES
    pp = min(FUSED_PAGES_PER_ITERATION, npages)
    nchunks = nbs * (npages // pp)
    iters = nseq * npair * sum((qi * tq) // tk + 1 for qi in range(nq))
    assert npages % pp == 0 and nchunks <= iters, "the page stream must fit in the prompt attention's iterations"
    kern = functools.partial(_fox_fused_kernel, tq=tq, tk=tk, dh=dh, pp=pp, ring=FUSED_RING, npages=npages,
                             nchunks=nchunks, s_new=s_new, heads=heads)
    const = lambda shape: pl.BlockSpec(shape, lambda b, hp, qi, pt: (0,) * len(shape))
    grid_spec = pltpu.PrefetchScalarGridSpec(
        num_scalar_prefetch=1, grid=(nseq, npair, nq),
        in_specs=[pl.BlockSpec((tq, 2 * LANES), lambda b, hp, qi, pt: (b * nq + qi, hp)),
                  pl.BlockSpec((seq_len, 2 * LANES), lambda b, hp, qi, pt: (b, hp)),
                  pl.BlockSpec((seq_len, LANES), lambda b, hp, qi, pt: (b, hp)),
                  pl.BlockSpec(memory_space=pl.ANY), pl.BlockSpec(memory_space=pl.ANY),
                  pl.BlockSpec(memory_space=pl.ANY),
                  const((nbs, nrow, da)), const((ntok, da)), const((ntok, da)), const((SUBLANES, ntok))],
        out_specs=[pl.BlockSpec((tq, LANES), lambda b, hp, qi, pt: (b * nq + qi, hp)),
                   const((nbs, s_new, da))],
        scratch_shapes=[pltpu.VMEM((2 * tq, LANES), F32), pltpu.VMEM((2 * tq, LANES), F32),
                        pltpu.VMEM((2 * tq, LANES), F32),
                        pltpu.VMEM((nrow, LANES), F32), pltpu.VMEM((nrow, LANES), F32), pltpu.VMEM((nrow, da), F32),
                        pltpu.VMEM((heads, LANES), F32),
                        pltpu.VMEM((FUSED_RING, pp, da, page), F32), pltpu.VMEM((FUSED_RING, pp, da, page), F32),
                        pltpu.VMEM((FUSED_RING, pp, heads, page), F32),
                        pltpu.SemaphoreType.DMA((FUSED_RING,)), pltpu.SMEM((1,), jnp.int32)])
    return pl.pallas_call(
        kern, grid_spec=grid_spec,
        out_shape=[jax.ShapeDtypeStruct((t, da), BF16), jax.ShapeDtypeStruct((nbs, s_new, da), F32)],
        compiler_params=_cparams(("arbitrary", "arbitrary", "arbitrary")),
        name="fox_fused")(page_table_flat, qa, ka, v, kt_pages, vt_pages, lf_pages, qbd, kn, vn, dtn)


def _post_attn_math(x_ref, gc_ref, at_ref, g_ref, wuc_ref, wua_ref, wmo_ref, gain_ref, wxq_ref):
    dm = x_ref.shape[1]
    y_conv = _dot(gc_ref[...], wuc_ref[...])
    y_attn = _dot(at_ref[...].astype(BF16), wua_ref[...])
    g_conv = g_ref[:, 0:dm].astype(F32)
    g_attn = g_ref[:, dm:2 * dm].astype(F32)
    mixed = jax.nn.sigmoid(g_conv) * y_conv + jax.nn.sigmoid(g_attn) * y_attn
    h = x_ref[...] + _dot(mixed.astype(BF16), wmo_ref[...])
    xn = _rms(h, gain_ref[...]).astype(BF16)
    return h, _dot(xn, wxq_ref[...]).astype(BF16)


def _post_attn_kernel(x_ref, gc_ref, at_ref, g_ref, wuc_ref, wua_ref, wmo_ref, gain_ref, wxq_ref, h_ref, qx_ref):
    h, qx = _post_attn_math(x_ref, gc_ref, at_ref, g_ref, wuc_ref, wua_ref, wmo_ref, gain_ref, wxq_ref)
    h_ref[...] = h
    qx_ref[...] = qx.astype(qx_ref.dtype)


def _post_attn(x, gc, at, g, wuc, wua, wmo, gain, wxq, *, tm, name):
    t, dm = x.shape
    dx = wxq.shape[1]
    rows = lambda w: pl.BlockSpec((tm, w), lambda i: (i, 0))
    return pl.pallas_call(
        _post_attn_kernel, grid=(t // tm,),
        in_specs=[rows(dm), rows(gc.shape[1]), rows(at.shape[1]), rows(g.shape[1]), _const_spec(wuc.shape),
                  _const_spec(wua.shape), _const_spec(wmo.shape), _const_spec((1, dm)), _const_spec(wxq.shape)],
        out_specs=[rows(dm), rows(dx)],
        out_shape=[jax.ShapeDtypeStruct((t, dm), F32), jax.ShapeDtypeStruct((t, dx), BF16)],
        compiler_params=_cparams(("parallel",)), name=name)(x, gc, at, g, wuc, wua, wmo, gain, wxq)


def _memkv_kernel(m_ref, gain_ref, wk_ref, wv_ref, k_ref, v_ref, *, xh):
    mn = _rms(m_ref[...], gain_ref[...]).astype(BF16)
    tm = m_ref.shape[0]
    xd = wk_ref.shape[1] // xh
    k = _dot(mn, wk_ref[...])
    v = _dot(mn, wv_ref[...])
    for h in range(xh):
        k_ref[pl.ds(h, tm, stride=xh), :] = k[:, h * xd:(h + 1) * xd]
        v_ref[pl.ds(h, tm, stride=xh), :] = v[:, h * xd:(h + 1) * xd]


def _memkv(mem, gain, wk, wv, *, xh):
    t, dm = mem.shape
    tm = min(512, t)
    xd = wk.shape[1] // xh
    rows = lambda w: pl.BlockSpec((tm, w), lambda i: (i, 0))
    return pl.pallas_call(
        functools.partial(_memkv_kernel, xh=xh), grid=(t // tm,),
        in_specs=[rows(dm), _const_spec((1, dm)), _const_spec(wk.shape), _const_spec(wv.shape)],
        out_specs=[pl.BlockSpec((tm * xh, xd), lambda i: (i, 0))] * 2,
        out_shape=[jax.ShapeDtypeStruct((t * xh, xd), F32)] * 2,
        compiler_params=_cparams(("parallel",)), name="memkv")(mem, gain, wk, wv)


def _xattn_math(q, mk_ref, mv_ref, xh):
    nm = mk_ref.shape[0] // xh
    xd = mk_ref.shape[1]
    scale = xd ** -0.5
    outs = []
    for h in range(xh):
        mk = mk_ref[pl.ds(h, nm, stride=xh), :].astype(BF16)
        mv = mv_ref[pl.ds(h, nm, stride=xh), :].astype(BF16)
        s = _dot_nt(q[:, h * xd:(h + 1) * xd], mk) * scale
        m = jnp.max(s, axis=1, keepdims=True)
        p = jnp.exp(s - m)
        p = p / jnp.sum(p, axis=1, keepdims=True)
        outs.append(_dot(p.astype(BF16), mv))
    return jnp.concatenate(outs, axis=1)


def _xattn_kernel(q_ref, mk_ref, mv_ref, o_ref, *, xh):
    tq, dx = q_ref.shape[1:]
    xd = dx // xh
    nmh = mk_ref.shape[1]
    scale = xd ** -0.5
    rowh = lax.broadcasted_iota(jnp.int32, (xh * tq, nmh), 0) // tq
    colh = lax.broadcasted_iota(jnp.int32, (xh * tq, nmh), 1) % xh
    same_head = rowh == colh
    for b in range(q_ref.shape[0]):
        q = q_ref[b].astype(BF16)
        qs = jnp.concatenate([q[:, h * xd:(h + 1) * xd] for h in range(xh)], axis=0)
        s = jnp.where(same_head, _dot_nt(qs, mk_ref[b].astype(BF16)) * scale, NEG_BIG)
        m = jnp.max(s, axis=1, keepdims=True)
        p = jnp.exp(s - m)
        p = p / jnp.sum(p, axis=1, keepdims=True)
        o = _dot(p.astype(BF16), mv_ref[b].astype(BF16))
        o_ref[b] = jnp.concatenate([o[h * tq:(h + 1) * tq] for h in range(xh)], axis=1).astype(o_ref.dtype)


def _xattn(q, mk, mv, *, tq, xh, nbb, name):
    nb, s, dx = q.shape
    nm, xd = mk.shape[1:]
    kern = functools.partial(_xattn_kernel, xh=xh)
    return pl.pallas_call(
        kern, grid=(nb // nbb, s // tq),
        in_specs=[pl.BlockSpec((nbb, tq, dx), lambda b, i: (b, i, 0)),
                  pl.BlockSpec((nbb, nm, xd), lambda b, i: (b, 0, 0)),
                  pl.BlockSpec((nbb, nm, xd), lambda b, i: (b, 0, 0))],
        out_specs=pl.BlockSpec((nbb, tq, dx), lambda b, i: (b, i, 0)),
        out_shape=jax.ShapeDtypeStruct((nb, s, dx), q.dtype),
        compiler_params=_cparams(("parallel", "parallel")), name=name)(q, mk, mv)


def _pre_moe_kernel(h_ref, o_ref, wxo_ref, gain_ref, wrh_ref, wrl_ref, cnt_in_ref,
                    h2_ref, xn_ref, rt_ref, rr_ref, cnt_out_ref, base, *, tm, ngroups, epg):
    @pl.when(pl.program_id(0) == 0)
    def _():
        base[...] = cnt_in_ref[...]

    _pre_moe_math(h_ref[...], o_ref[...].astype(BF16), wxo_ref, gain_ref, wrh_ref, wrl_ref,
                  h2_ref, xn_ref, rt_ref, rr_ref, cnt_out_ref, base, tm=tm, ngroups=ngroups, epg=epg)


def _mid_kernel(x_ref, gc_ref, at_ref, g_ref, mk_ref, mv_ref, wuc_ref, wua_ref, wmo_ref, gx_ref, wxq_ref,
                wxo_ref, gf_ref, wrh_ref, wrl_ref, cnt_in_ref,
                h2_ref, xn_ref, rt_ref, rr_ref, cnt_out_ref, base, *, tm, ngroups, epg, xh):
    @pl.when(pl.program_id(0) == 0)
    def _():
        base[...] = cnt_in_ref[...]

    h1, qx = _post_attn_math(x_ref, gc_ref, at_ref, g_ref, wuc_ref, wua_ref, wmo_ref, gx_ref, wxq_ref)
    o = _xattn_math(qx, mk_ref, mv_ref, xh).astype(BF16)
    _pre_moe_math(h1, o, wxo_ref, gf_ref, wrh_ref, wrl_ref,
                  h2_ref, xn_ref, rt_ref, rr_ref, cnt_out_ref, base, tm=tm, ngroups=ngroups, epg=epg)


def _pre_moe_math(h, o, wxo_ref, gain_ref, wrh_ref, wrl_ref, h2_ref, xn_ref, rt_ref, rr_ref, cnt_out_ref, base,
                  *, tm, ngroups, epg):
    h2 = h + _dot(o, wxo_ref[...])
    h2_ref[...] = h2
    xn = _rms(h2, gain_ref[...])
    xw = _pack_bf16_pairs(xn)
    nc = xw.shape[1] // LANES
    for c in range(nc):
        xn_ref[pl.ds(c, tm, stride=nc), :] = xw[:, c * LANES:(c + 1) * LANES]

    xh = xn.astype(BF16)
    xl = (xn - xh.astype(F32)).astype(BF16)
    lt = _dot_nt(wrh_ref[...], xh) + (_dot_nt(wrh_ref[...], xl) + _dot_nt(wrl_ref[...], xh))

    sub = lax.broadcasted_iota(jnp.int32, (SUBLANES, tm), 0)
    gl = jnp.where(sub < ngroups, lt[0:SUBLANES, :], -jnp.inf)
    gmax = jnp.max(gl, axis=0, keepdims=True)
    gidx = jnp.min(jnp.where(gl == gmax, sub, SUBLANES), axis=0, keepdims=True)
    pg = 1.0 / jnp.sum(jnp.exp(gl - gmax), axis=0, keepdims=True)
    el = jnp.zeros((epg, tm), F32)
    for g in range(ngroups):
        el = jnp.where(gidx == g, lt[EXPERT_ROW0 + g * epg:EXPERT_ROW0 + (g + 1) * epg, :], el)
    v1 = jnp.max(el, axis=0, keepdims=True)
    i1 = jnp.min(jnp.where(el == v1, sub, epg), axis=0, keepdims=True)
    el2 = jnp.where(sub == i1, -jnp.inf, el)
    v2 = jnp.max(el2, axis=0, keepdims=True)
    i2 = jnp.min(jnp.where(el2 == v2, sub, epg), axis=0, keepdims=True)
    t2 = jnp.exp(v2 - v1)
    den = 1.0 + t2
    w0 = (1.0 / den) * pg
    w1 = (t2 / den) * pg
    e0 = gidx * epg + i1
    e1 = gidx * epg + i2

    erow = lax.broadcasted_iota(jnp.int32, (LANES, tm), 0)
    oh0 = erow == e0
    oh1 = erow == e1
    rr = lax.broadcasted_iota(jnp.int32, (tm, tm), 0)
    cc = lax.broadcasted_iota(jnp.int32, (tm, tm), 1)
    triu = jnp.where(rr <= cc, 1.0, 0.0).astype(BF16)
    pre0 = _dot(jnp.where(oh0, 1.0, 0.0).astype(BF16), triu)
    pre1 = _dot(jnp.where(oh1, 1.0, 0.0).astype(BF16), triu)
    b0 = base[:, 0:1]
    tot0 = pre0[:, tm - 1:tm]
    tot1 = pre1[:, tm - 1:tm]
    rank0 = jnp.sum(jnp.where(oh0, pre0 - 1.0 + b0, 0.0), axis=0, keepdims=True)
    rank1 = jnp.sum(jnp.where(oh1, pre1 - 1.0 + (b0 + tot0), 0.0), axis=0, keepdims=True)
    newb = b0 + tot0 + tot1
    base[...] = jnp.broadcast_to(newb, base.shape)
    cnt_out_ref[...] = jnp.broadcast_to(newb, cnt_out_ref.shape)

    zero = jnp.zeros((1, tm), F32)
    rt = jnp.concatenate([e0.astype(F32), e1.astype(F32), w0, w1, rank0, rank1, zero, zero], axis=0)
    rt_ref[...] = rt
    rt_pad = jnp.concatenate([rt, jnp.zeros((LANES - ROUTE_ROWS, tm), F32)], axis=0)
    rr_ref[...] = jnp.transpose(rt_pad)


def _pre_moe(h, o, wxo, gain, wrh, wrl, cnt_in, *, tm, ngroups, epg, name):
    t, dm = h.shape
    dx = o.shape[1]
    kern = functools.partial(_pre_moe_kernel, tm=tm, ngroups=ngroups, epg=epg)
    rows = lambda w: pl.BlockSpec((tm, w), lambda i: (i, 0))
    return pl.pallas_call(
        kern, grid=(t // tm,),
        in_specs=[rows(dm), rows(dx), _const_spec(wxo.shape), _const_spec((1, dm)), _const_spec(wrh.shape),
                  _const_spec(wrl.shape), _const_spec((LANES, LANES))],
        out_specs=[rows(dm), pl.BlockSpec((tm * (dm // (2 * LANES)), LANES), lambda i: (i, 0)),
                   pl.BlockSpec((ROUTE_ROWS, tm), lambda i: (0, i)), rows(LANES), _const_spec((LANES, LANES))],
        out_shape=[jax.ShapeDtypeStruct((t, dm), F32), jax.ShapeDtypeStruct((t * (dm // (2 * LANES)), LANES), jnp.uint32),
                   jax.ShapeDtypeStruct((ROUTE_ROWS, t), F32), jax.ShapeDtypeStruct((t, LANES), F32),
                   jax.ShapeDtypeStruct((LANES, LANES), F32)],
        scratch_shapes=[pltpu.VMEM((LANES, LANES), F32)],
        compiler_params=_cparams(("arbitrary",)), name=name)(h, o, wxo, gain, wrh, wrl, cnt_in)


def _mid(x, gc, at, g, mk, mv, wuc, wua, wmo, gx, wxq, wxo, gf, wrh, wrl, cnt_in, *, tm, seq_len, ngroups, epg, xh):
    t, dm = x.shape
    nm, xd = mk.shape[1:]
    seq_blocks = seq_len // tm
    kern = functools.partial(_mid_kernel, tm=tm, ngroups=ngroups, epg=epg, xh=xh)
    rows = lambda w: pl.BlockSpec((tm, w), lambda i: (i, 0))
    mem = pl.BlockSpec((None, nm, xd), lambda i: (i // seq_blocks, 0, 0))
    consts = [wuc, wua, wmo, gx, wxq, wxo, gf, wrh, wrl, cnt_in]
    return pl.pallas_call(
        kern, grid=(t // tm,),
        in_specs=[rows(dm), rows(gc.shape[1]), rows(at.shape[1]), rows(g.shape[1]), mem, mem]
        + [_const_spec(a.shape) for a in consts],
        out_specs=[rows(dm), pl.BlockSpec((tm * (dm // (2 * LANES)), LANES), lambda i: (i, 0)),
                   pl.BlockSpec((ROUTE_ROWS, tm), lambda i: (0, i)), rows(LANES), _const_spec((LANES, LANES))],
        out_shape=[jax.ShapeDtypeStruct((t, dm), F32), jax.ShapeDtypeStruct((t * (dm // (2 * LANES)), LANES), jnp.uint32),
                   jax.ShapeDtypeStruct((ROUTE_ROWS, t), F32), jax.ShapeDtypeStruct((t, LANES), F32),
                   jax.ShapeDtypeStruct((LANES, LANES), F32)],
        scratch_shapes=[pltpu.VMEM((LANES, LANES), F32)],
        compiler_params=_cparams(("arbitrary",)), name="mid_prompt")(x, gc, at, g, mk, mv, *consts)


def _row_copy(src, r_src, dst, r_dst, sem, nc):
    return pltpu.make_async_copy(src.at[pl.ds(pl.multiple_of(r_src * nc, nc), nc)],
                                 dst.at[pl.ds(pl.multiple_of(r_dst * nc, nc), nc)], sem)


def _slab_rows(ref, nc):
    rows = ref.shape[0] // nc
    return jnp.concatenate([ref[pl.ds(c, rows, stride=nc), :] for c in range(nc)], axis=1)


def _pack_bf16_pairs(x):
    half = x.shape[1] // 2
    lo = pltpu.bitcast(x[:, 0:half].astype(BF16).astype(F32), jnp.uint32)
    hi = pltpu.bitcast(x[:, half:2 * half].astype(BF16).astype(F32), jnp.uint32)
    return lax.shift_right_logical(lo, jnp.uint32(16)) | (hi & jnp.uint32(0xFFFF0000))


def _unpack_bf16_pairs(w):
    lo = pltpu.bitcast(lax.shift_left(w, jnp.uint32(16)), F32)
    hi = pltpu.bitcast(w & jnp.uint32(0xFFFF0000), F32)
    return jnp.concatenate([lo, hi], axis=1)


def _dest_kernel(rt_ref, pstart_ref, o_ref):
    rt = rt_ref[...]
    pst = pstart_ref[:, 0:1]
    erow = lax.broadcasted_iota(jnp.int32, (LANES, rt.shape[1]), 0)
    rows = []
    for k in range(TOP_K):
        e = rt[k:k + 1, :].astype(jnp.int32)
        rows.append(jnp.sum(jnp.where(erow == e, pst, 0.0), axis=0, keepdims=True) + rt[4 + k:5 + k, :])
    rows.append(jnp.zeros((ROUTE_ROWS - TOP_K, rt.shape[1]), F32))
    o_ref[...] = jnp.concatenate(rows, axis=0).astype(jnp.int32)


def _dest(rt_all, pstart_col):
    tall = rt_all.shape[1]
    nchunk = 3 if tall % (3 * LANES) == 0 else 1
    w = tall // nchunk
    return pl.pallas_call(
        _dest_kernel, grid=(nchunk,),
        in_specs=[pl.BlockSpec((ROUTE_ROWS, w), lambda i: (0, i)), _const_spec((LANES, LANES))],
        out_specs=pl.BlockSpec((ROUTE_ROWS, w), lambda i: (0, i)),
        out_shape=jax.ShapeDtypeStruct((ROUTE_ROWS, tall), jnp.int32),
        compiler_params=_cparams(("parallel",)), name="dest")(rt_all, pstart_col)


def _scatter_kernel(dest_ref, pstart_ref, pend_ref, xp_ref, xs_ref, out_ref, zeros, sem, zsem,
                    *, tm_p, nblk_p, nexp, nc):
    i = pl.program_id(0)
    tall = dest_ref.shape[0] // TOP_K

    @pl.when(i == 0)
    def _():
        zeros[...] = jnp.zeros_like(zeros)

        def zero_block(blk):
            start = pl.multiple_of(blk * (EXPERT_ROWS * nc), EXPERT_ROWS * nc)
            return pltpu.make_async_copy(zeros, out_ref.at[pl.ds(start, EXPERT_ROWS * nc)], zsem)

        for e in range(nexp):
            @pl.when(pend_ref[e] > pstart_ref[e])
            def _():
                zero_block(pend_ref[e] // EXPERT_ROWS - 1).start()
        for e in range(nexp):
            @pl.when(pend_ref[e] > pstart_ref[e])
            def _():
                zero_block(pend_ref[e] // EXPERT_ROWS - 1).wait()
        nblk = out_ref.shape[0] // (EXPERT_ROWS * nc)
        nused = pend_ref[nexp - 1] // EXPERT_ROWS

        def start_unused(blk, c):
            zero_block(blk).start()
            return c

        def wait_unused(blk, c):
            zero_block(blk).wait()
            return c

        lax.fori_loop(nused, nblk, start_unused, 0)
        lax.fori_loop(nused, nblk, wait_unused, 0)

    def copy_rows(x_ref, base):
        tm = x_ref.shape[0] // nc

        def issue(r, c):
            for k in range(TOP_K):
                _row_copy(x_ref, r, out_ref, dest_ref[k * tall + base + r], sem, nc).start(priority=k % 2)
            return c

        lax.fori_loop(0, tm, issue, 0, unroll=8)
        for k in range(TOP_K):
            pltpu.make_async_copy(x_ref, x_ref, sem).wait()

    @pl.when(i < nblk_p)
    def _():
        copy_rows(xp_ref, i * tm_p)

    @pl.when(i == nblk_p)
    def _():
        copy_rows(xs_ref, nblk_p * tm_p)


def _scatter(dest_flat, pstart, pend, x_p, x_s, *, n_rows, tm_p, nexp, nc):
    tp = x_p.shape[0] // nc
    ts = x_s.shape[0] // nc
    nblk_p = tp // tm_p
    kern = functools.partial(_scatter_kernel, tm_p=tm_p, nblk_p=nblk_p, nexp=nexp, nc=nc)
    grid_spec = pltpu.PrefetchScalarGridSpec(
        num_scalar_prefetch=3, grid=(nblk_p + 1,),
        in_specs=[pl.BlockSpec((tm_p * nc, LANES), lambda i, *_: (jnp.minimum(i, nblk_p - 1), 0)),
                  pl.BlockSpec((ts * nc, LANES), lambda i, *_: (0, 0))],
        out_specs=pl.BlockSpec(memory_space=pl.ANY),
        scratch_shapes=[pltpu.VMEM((EXPERT_ROWS * nc, LANES), jnp.uint32), pltpu.SemaphoreType.DMA(()),
                        pltpu.SemaphoreType.DMA(())])
    return pl.pallas_call(
        kern, grid_spec=grid_spec, out_shape=jax.ShapeDtypeStruct((n_rows * nc, LANES), jnp.uint32),
        compiler_params=_cparams(("arbitrary",)), name="scatter")(dest_flat, pstart, pend, x_p, x_s)


def _experts_kernel(blk_e_ref, nused_ref, xs_ref, wg_ref, wu_ref, wd_ref, y_ref, wgb, wub, wdb, *, nc):
    i = pl.program_id(0)
    prev = blk_e_ref[jnp.maximum(i - 1, 0)]
    fresh = (i == 0) | (blk_e_ref[i] != prev)

    @pl.when(i < nused_ref[0])
    def _():
        @pl.when(fresh)
        def _():
            wgb[...] = wg_ref[...].astype(BF16)
            wub[...] = wu_ref[...].astype(BF16)
            wdb[...] = wd_ref[...].astype(BF16)
        x = _unpack_bf16_pairs(_slab_rows(xs_ref, nc)).astype(BF16)
        a = _dot(x, wgb[...])
        u = _dot(x, wub[...])
        hmid = (a * jax.nn.sigmoid(a)) * u
        y = _pack_bf16_pairs(_dot(hmid.astype(BF16), wdb[...]))
        for c in range(nc):
            y_ref[pl.ds(c, EXPERT_ROWS, stride=nc), :] = y[:, c * LANES:(c + 1) * LANES]

    @pl.when(i >= nused_ref[0])
    def _():
        y_ref[...] = jnp.zeros_like(y_ref)


def _experts(blk_e, nused, xs, wg, wu, wd):
    dm, de = wg.shape[1:]
    nc = dm // (2 * LANES)
    nblk = xs.shape[0] // (EXPERT_ROWS * nc)

    def row_map(i, be, nu):
        return (jnp.minimum(i, nu[0] - 1), 0)

    grid_spec = pltpu.PrefetchScalarGridSpec(
        num_scalar_prefetch=2, grid=(nblk,),
        in_specs=[pl.BlockSpec((EXPERT_ROWS * nc, LANES), row_map),
                  pl.BlockSpec((None, dm, de), lambda i, be, nu: (be[i], 0, 0)),
                  pl.BlockSpec((None, dm, de), lambda i, be, nu: (be[i], 0, 0)),
                  pl.BlockSpec((None, de, dm), lambda i, be, nu: (be[i], 0, 0))],
        out_specs=pl.BlockSpec((EXPERT_ROWS * nc, LANES), lambda i, be, nu: (i, 0)),
        scratch_shapes=[pltpu.VMEM((dm, de), BF16), pltpu.VMEM((dm, de), BF16), pltpu.VMEM((de, dm), BF16)])
    return pl.pallas_call(
        functools.partial(_experts_kernel, nc=nc), grid_spec=grid_spec,
        out_shape=jax.ShapeDtypeStruct(xs.shape, jnp.uint32),
        compiler_params=_cparams(("arbitrary",)), name="experts")(blk_e, nused, xs, wg, wu, wd)


def _combine_kernel(dest_ref, h_ref, rr_ref, gain_ref, y_hbm, o_ref, buf, sem, *, tm, tok0):
    i = pl.program_id(0)
    nc = h_ref.shape[1] // (2 * LANES)
    tall = dest_ref.shape[0] // TOP_K
    base = tok0 + i * tm

    def issue(r, c):
        for k in range(TOP_K):
            _row_copy(y_hbm, dest_ref[k * tall + base + r], buf.at[k], r, sem, nc).start(priority=k % 2)
        return c

    lax.fori_loop(0, tm, issue, 0, unroll=8)
    for k in range(TOP_K):
        pltpu.make_async_copy(buf.at[k], buf.at[k], sem).wait()
    rr = rr_ref[...]
    y0 = _unpack_bf16_pairs(_slab_rows(buf.at[0], nc))
    y1 = _unpack_bf16_pairs(_slab_rows(buf.at[1], nc))
    h = h_ref[...] + (rr[:, 2:3] * y0 + rr[:, 3:4] * y1)
    o_ref[...] = _rms(h, gain_ref[...])


def _combine(dest_flat, h, rr, gain, y, *, tm, tok0, name):
    t, dm = h.shape
    kern = functools.partial(_combine_kernel, tm=tm, tok0=tok0)
    grid_spec = pltpu.PrefetchScalarGridSpec(
        num_scalar_prefetch=1, grid=(t // tm,),
        in_specs=[pl.BlockSpec((tm, dm), lambda i, *_: (i, 0)), pl.BlockSpec((tm, LANES), lambda i, *_: (i, 0)),
                  pl.BlockSpec((1, dm), lambda i, *_: (0, 0)), pl.BlockSpec(memory_space=pl.ANY)],
        out_specs=pl.BlockSpec((tm, dm), lambda i, *_: (i, 0)),
        scratch_shapes=[pltpu.VMEM((TOP_K, tm * (dm // (2 * LANES)), LANES), jnp.uint32),
                        pltpu.SemaphoreType.DMA(())])
    return pl.pallas_call(
        kern, grid_spec=grid_spec, out_shape=jax.ShapeDtypeStruct((t, dm), F32),
        compiler_params=_cparams(("arbitrary",)), name=name)(dest_flat, h, rr, gain, y)


def kernel(x_prompt, x_sample, cache_k, cache_v, cache_logf, cache_mem_k, cache_mem_v, state_conv, page_table,
           mem_prompt, norm_mix, w_in, b_forget, conv_w, conv_b, w_up_conv, w_up_attn, w_mix_out, norm_xattn,
           norm_mem, w_xq, w_xk, w_xv, w_xo, norm_ffn, w_router_group, w_router_expert, w_expert_gate,
           w_expert_up, w_expert_down, norm_final):
    depth = w_in.shape[0]
    assert depth == 1, "single-layer trunk"
    nbp, seq, dm = x_prompt.shape
    nbs, s_new, _ = x_sample.shape
    _, n_pool, page, heads, dh = cache_k.shape
    npages = page_table.shape[1]
    nmem, xh, xd = cache_mem_k.shape[2:]
    dc = conv_w.shape[2]
    da = heads * dh
    dx = xh * xd
    ngroups, _, epg = w_router_expert.shape[1:]
    nexp = ngroups * epg
    tp = nbp * seq
    ts = nbs * s_new
    assert conv_w.shape[1] == 3 and s_new >= 2 and dh == 64 and heads == SUBLANES and epg == SUBLANES
    assert page == LANES

    l = 0
    wi = w_in[l]
    wc = wi[:, 0:3 * dc].astype(BF16)
    wqkv = wi[:, 3 * dc:3 * dc + 3 * da].astype(BF16)
    o_f = 3 * dc + 3 * da
    wf = jnp.pad(wi[:, o_f:o_f + heads], ((0, 0), (0, LANES - heads)))
    wfh = wf.astype(BF16)
    wf2 = jnp.concatenate([wfh, (wf - wfh.astype(F32)).astype(BF16)], axis=1)
    wg = wi[:, o_f + heads:].astype(BF16)
    bfp = jnp.pad(b_forget[l][None, :], ((0, 0), (0, LANES - heads)))
    cw = conv_w[l]
    cbias = conv_b[l][None, :]
    g_mix = norm_mix[l][None, :]
    wuc = w_up_conv[l].astype(BF16)
    wua = w_up_attn[l].astype(BF16)
    wmo = w_mix_out[l].astype(BF16)
    g_x = norm_xattn[l][None, :]
    wxq = w_xq[l].astype(BF16)
    wxo = w_xo[l].astype(BF16)
    g_f = norm_ffn[l][None, :]
    wr = jnp.zeros((LANES, dm), F32)
    wr = wr.at[0:ngroups].set(w_router_group[l].T)
    wr = wr.at[EXPERT_ROW0:EXPERT_ROW0 + nexp].set(jnp.transpose(w_router_expert[l], (0, 2, 1)).reshape(nexp, dm))
    wrh = wr.astype(BF16)
    wrl = (wr - wrh.astype(F32)).astype(BF16)

    xp = x_prompt.reshape(tp, dm)
    (gc_p, qa_p, ka_p, vb_p, kt_p, vt_p, g_p, lf_p, ulast_p) = _inproj(
        xp, g_mix, wc, wqkv, wg, wf2, bfp, cw, cbias, seq_len=seq, sample=False, aug=_decay_columns(heads))
    xs_ = x_sample.reshape(ts, dm)
    st = state_conv[l]
    zeros_row = jnp.zeros((nbs, 1, dc), F32)
    fix1 = jnp.concatenate([st[:, 1:2], jnp.tile(zeros_row, (1, s_new - 1, 1))], axis=1).reshape(ts, dc)
    fix2 = jnp.concatenate([st[:, 0:1], st[:, 1:2], jnp.tile(zeros_row, (1, s_new - 2, 1))], axis=1).reshape(ts, dc)
    (gc_s, q_s, kb_s, vb_s, kf_s, vf_s, g_s, lf_s, dt_s, u_s) = _inproj(
        xs_, g_mix, wc, wqkv, wg, wf2, bfp, cw, cbias, seq_len=s_new, sample=True, fix=(fix1, fix2))
    pt_flat = page_table.reshape(-1).astype(jnp.int32)
    head_of_col = jnp.arange(da) // dh
    qbd = jnp.where(head_of_col[None, None, None, :] == jnp.arange(heads)[None, None, :, None],
                    q_s.reshape(nbs, s_new, 1, da), jnp.zeros((), BF16)).reshape(nbs, s_new * heads, da)
    kt_pages = jnp.transpose(cache_k[l], (0, 2, 3, 1)).reshape(n_pool, da, page)
    vt_pages = jnp.transpose(cache_v[l], (0, 2, 3, 1)).reshape(n_pool, da, page)
    lf_pages = jnp.swapaxes(cache_logf[l], 1, 2)
    at_p, at_s = _fox_fused(pt_flat, qa_p, ka_p, vb_p, kt_pages, vt_pages, lf_pages, qbd, kb_s, vb_s, dt_s,
                            nseq=nbp, seq_len=seq, dh=dh, npages=npages, s_new=s_new, heads=heads)

    tm_p = min(512, seq)
    mk_p, mv_p = _memkv(mem_prompt.reshape(nbp * nmem, dm), norm_mem[l][None, :], w_xk[l].astype(BF16),
                        w_xv[l].astype(BF16), xh=xh)
    cnt0 = jnp.zeros((LANES, LANES), F32)
    h2_p, xn_p, rt_p, rr_p, cnt1 = _mid(
        xp, gc_p, at_p, g_p, mk_p.reshape(nbp, nmem * xh, xd), mv_p.reshape(nbp, nmem * xh, xd),
        wuc, wua, wmo, g_x, wxq, wxo, g_f, wrh, wrl, cnt0, tm=tm_p, seq_len=seq, ngroups=ngroups, epg=epg, xh=xh)

    h1_s, qx_s = _post_attn(xs_, gc_s, at_s.reshape(ts, da), g_s, wuc, wua, wmo, g_x, wxq, tm=ts,
                            name="post_attn_sample")
    qx_s8 = jnp.pad(qx_s.astype(F32).reshape(nbs, s_new, dx), ((0, 0), (0, SUBLANES - s_new), (0, 0)))
    o_s = _xattn(qx_s8, cache_mem_k[l].reshape(nbs, nmem * xh, xd), cache_mem_v[l].reshape(nbs, nmem * xh, xd),
                 tq=SUBLANES, xh=xh, nbb=4 if nbs % 4 == 0 else 1, name="xattn_sample")[:, :s_new].reshape(ts, dx)
    h2_s, xn_s, rt_s, rr_s, cnt2 = _pre_moe(h1_s, o_s, wxo, g_f, wrh, wrl, cnt1, tm=ts, ngroups=ngroups, epg=epg,
                                            name="pre_moe_sample")

    tall = tp + ts
    counts = cnt2[0:nexp, 0].astype(jnp.int32)
    padded = (counts + EXPERT_ROWS - 1) // EXPERT_ROWS * EXPERT_ROWS
    pend = jnp.cumsum(padded).astype(jnp.int32)
    pstart = pend - padded
    nblk = (tall * TOP_K + nexp * (EXPERT_ROWS - 1)) // EXPERT_ROWS
    n_rows = nblk * EXPERT_ROWS
    blk_row0 = jnp.arange(nblk, dtype=jnp.int32) * EXPERT_ROWS
    blk_e = jnp.minimum(jnp.sum((pend[None, :] <= blk_row0[:, None]).astype(jnp.int32), axis=1), nexp - 1)
    nused = (pend[nexp - 1:nexp] // EXPERT_ROWS).astype(jnp.int32)
    rt_all = jnp.concatenate([rt_p, rt_s], axis=1)
    pstart_col = jnp.zeros((LANES, LANES), F32).at[0:nexp, :].set(pstart.astype(F32)[:, None])
    dest_flat = _dest(rt_all, pstart_col)[0:TOP_K].reshape(-1)
    xsg = _scatter(dest_flat, pstart, pend, xn_p, xn_s, n_rows=n_rows, tm_p=tm_p, nexp=nexp, nc=dm // (2 * LANES))
    y = _experts(blk_e, nused, xsg, w_expert_gate[l], w_expert_up[l], w_expert_down[l])
    g_fin = norm_final[None, :]
    y_p = _combine(dest_flat, h2_p, rr_p, g_fin, y, tm=tm_p, tok0=0, name="combine_prompt")
    y_s = _combine(dest_flat, h2_s, rr_s, g_fin, y, tm=ts, tok0=tp, name="combine_sample")

    return (y_p.reshape(nbp, seq, dm), y_s.reshape(nbs, s_new, dm),
            ulast_p[None],
            jnp.transpose(kt_p.reshape(nbp, heads, dh, seq), (0, 3, 1, 2))[None],
            jnp.transpose(vt_p.reshape(nbp, heads, dh, seq), (0, 3, 1, 2))[None],
            jnp.transpose(lf_p, (0, 2, 1))[None],
            mk_p.reshape(1, nbp, nmem, xh, xd), mv_p.reshape(1, nbp, nmem, xh, xd),
            u_s.reshape(nbs, s_new, dc)[None, :, s_new - 2:], kf_s.reshape(1, nbs, s_new, heads, dh),
            vf_s.reshape(1, nbs, s_new, heads, dh), lf_s[:, :heads].reshape(1, nbs, s_new, heads))
```

```python
import functools

import numpy as np
import jax
import jax.numpy as jnp
from jax import lax
from jax.experimental import pallas as pl
from jax.experimental.pallas import tpu as pltpu

F32 = jnp.float32
BF16 = jnp.bfloat16

RMS_EPS = 1e-6
TOP_K = 2
LANES = 128
SUBLANES = 8
VMEM_LIMIT = 56 * 1024 * 1024
NEG_BIG = -1e30
EXPERT_ROWS = 512
FUSED_PAGES_PER_ITERATION = 8
FUSED_RING = 3
FOX_QUERY_BLOCK = 512
FOX_KEY_BLOCK = 512
ROUTE_ROWS = 8
EXPERT_ROW0 = 8


def _cparams(sem, vmem=VMEM_LIMIT):
    return pltpu.CompilerParams(dimension_semantics=sem, vmem_limit_bytes=vmem)


def _rms(x, g):
    ms = jnp.mean(x * x, axis=-1, keepdims=True)
    return x * lax.rsqrt(ms + RMS_EPS) * g


def _split3(x):
    hi = x.astype(BF16)
    r = x - hi.astype(F32)
    mid = r.astype(BF16)
    lo = (r - mid.astype(F32)).astype(BF16)
    return hi, mid, lo


def _dot(a, b):
    return jnp.dot(a, b, preferred_element_type=F32)


def _dot_nt(a, b):
    return lax.dot_general(a, b, (((1,), (1,)), ((), ())), preferred_element_type=F32)


def _lane_tile(x, width):
    if width % LANES == 0:
        return jnp.concatenate([x] * (width // LANES), axis=1)
    return jnp.broadcast_to(x[:, 0:1], (x.shape[0], width))


def _const_spec(shape):
    nd = len(shape)
    return pl.BlockSpec(shape, lambda *_: (0,) * nd)


def _inproj_kernel(*refs, tm, seq_blocks, sample, seq_len):
    if sample:
        (x_ref, gain_ref, wc_ref, wqkv_ref, wg_ref, wf2_ref, bf_ref, cw_ref, cb_ref, fix1_ref, fix2_ref,
         gc_ref, q_ref, kbf_ref, vbf_ref, kf_ref, vf_ref, g_ref, logf_ref, dt_ref, u_ref) = refs
    else:
        (x_ref, gain_ref, wc_ref, wqkv_ref, wg_ref, wf2_ref, bf_ref, cw_ref, cb_ref,
         selq_ref, selk_ref, oneq_ref, onek_ref,
         gc_ref, qa_ref, ka_ref, vbf_ref, kf_ref, vf_ref, g_ref, logf_ref, u_ref,
         carry_u, carry_d) = refs
    i = pl.program_id(0)
    dc = cw_ref.shape[1]
    da = wqkv_ref.shape[1] // 3

    xn = _rms(x_ref[...], gain_ref[...])
    xb = xn.astype(BF16)

    cb = _dot(xb, wc_ref[:, 0:dc])
    cc = _dot(xb, wc_ref[:, dc:2 * dc])
    cx = _dot(xb, wc_ref[:, 2 * dc:3 * dc])
    u = cc * cx
    row = lax.broadcasted_iota(jnp.int32, (tm, 1), 0)
    r1 = pltpu.roll(u, 1, axis=0)
    r2 = pltpu.roll(u, 2, axis=0)
    if sample:
        pos = row % seq_len
        p1 = jnp.where(pos == 0, fix1_ref[...], r1)
        p2 = jnp.where(pos < 2, fix2_ref[...], r2)
        u_ref[...] = u
    else:
        @pl.when(i % seq_blocks == 0)
        def _():
            carry_u[...] = jnp.zeros_like(carry_u)
            carry_d[...] = jnp.zeros_like(carry_d)
        c0 = carry_u[0:1, :]
        c1 = carry_u[1:2, :]
        p1 = jnp.where(row == 0, c1, r1)
        p2 = jnp.where(row == 0, c0, jnp.where(row == 1, c1, r2))
        carry_u[0:2, :] = u[tm - 2:tm, :]
        u_ref[...] = u[tm - 2:tm, :]
    cw = cw_ref[...]
    cy = cb_ref[...] + cw[0:1, :] * p2 + cw[1:2, :] * p1 + cw[2:3, :] * u
    gc_ref[...] = (cb * cy).astype(BF16)

    qs = (_dot(xb, wqkv_ref[:, 0:da]) * (1.0 / 8.0)).astype(BF16)
    k = _dot(xb, wqkv_ref[:, da:2 * da])
    kb = k.astype(BF16)
    v = _dot(xb, wqkv_ref[:, 2 * da:3 * da])
    vbf_ref[...] = v.astype(BF16)
    if sample:
        q_ref[...] = qs
        kbf_ref[...] = kb
        kf_ref[...] = k
        vf_ref[...] = v
    else:
        kf_ref[...] = jnp.transpose(k)
        vf_ref[...] = jnp.transpose(v)

    gw = g_ref.shape[1]
    for c in range(gw // 512):
        g_ref[:, c * 512:(c + 1) * 512] = _dot(xb, wg_ref[:, c * 512:(c + 1) * 512]).astype(BF16)

    xl = (xn - xb.astype(F32)).astype(BF16)
    hh_hl = _dot(xb, wf2_ref[...])
    fz = hh_hl[:, 0:LANES] + (hh_hl[:, LANES:2 * LANES] + _dot(xl, wf2_ref[:, 0:LANES]))
    z = fz + bf_ref[...]
    logf = jnp.minimum(z, 0.0) - jnp.log1p(jnp.exp(-jnp.abs(z)))
    lane = lax.broadcasted_iota(jnp.int32, (1, LANES), 1)
    nh = da // 64
    logf = jnp.where(lane < nh, logf, 0.0)
    if sample:
        logf_ref[...] = logf
    else:
        logf_ref[...] = jnp.transpose(logf)[0:SUBLANES, :]

    rr = lax.broadcasted_iota(jnp.int32, (tm, tm), 0)
    cc_ = lax.broadcasted_iota(jnp.int32, (tm, tm), 1)
    if sample:
        tri = (cc_ <= rr) & ((rr // seq_len) == (cc_ // seq_len))
    else:
        tri = cc_ <= rr
    tri = jnp.where(tri, 1.0, 0.0).astype(BF16)
    d3 = _dot(tri, jnp.concatenate(_split3(logf), axis=1))
    d = d3[:, 0:LANES] + (d3[:, LANES:2 * LANES] + d3[:, 2 * LANES:3 * LANES])
    if sample:
        dt_ref[...] = jnp.transpose(d)[0:SUBLANES, :]
    else:
        d = d + carry_d[0:1, :]
        carry_d[0:1, :] = d[tm - 1:tm, :]
        dcat = jnp.concatenate(_split3(d), axis=1)
        aq = (_dot(dcat, selq_ref[...]) + oneq_ref[...]).astype(BF16)
        ak = (_dot(dcat, selk_ref[...]) + onek_ref[...]).astype(BF16)
        for p in range(da // LANES):
            lo_, hi_ = p * LANES, (p + 1) * LANES
            qa_ref[:, 2 * lo_:2 * lo_ + LANES] = qs[:, lo_:hi_]
            qa_ref[:, 2 * lo_ + LANES:2 * hi_] = aq[:, lo_:hi_]
            ka_ref[:, 2 * lo_:2 * lo_ + LANES] = kb[:, lo_:hi_]
            ka_ref[:, 2 * lo_ + LANES:2 * hi_] = ak[:, lo_:hi_]


def _decay_columns(heads):
    npair = heads // 2
    selq = np.zeros((3 * LANES, npair * LANES), np.float32)
    selk = np.zeros((3 * LANES, npair * LANES), np.float32)
    oneq = np.zeros((1, npair * LANES), np.float32)
    onek = np.zeros((1, npair * LANES), np.float32)
    for p in range(npair):
        for hh in range(2):
            for term in range(3):
                selq[term * LANES + 2 * p + hh, p * LANES + 3 * hh + term] = 1.0
                selk[term * LANES + 2 * p + hh, p * LANES + 6 + 3 * hh + term] = -1.0
                oneq[0, p * LANES + 6 + 3 * hh + term] = 1.0
                onek[0, p * LANES + 3 * hh + term] = 1.0
    return jnp.asarray(selq, BF16), jnp.asarray(selk, BF16), jnp.asarray(oneq), jnp.asarray(onek)


def _inproj(x, gain, wc, wqkv, wg, wf2, bfp, cw, cbias, *, seq_len, sample, fix=None, aug=None):
    t, dm = x.shape
    dc = cw.shape[1]
    da = wqkv.shape[1] // 3
    gw = wg.shape[1]
    if sample:
        tm = t
        seq_blocks = 1
    else:
        tm = min(512, seq_len)
        seq_blocks = seq_len // tm
    nblk = t // tm
    nseq = t // seq_len
    kern = functools.partial(_inproj_kernel, tm=tm, seq_blocks=seq_blocks, sample=sample, seq_len=seq_len)
    rows = lambda w: pl.BlockSpec((tm, w), lambda i: (i, 0))
    in_specs = [rows(dm), _const_spec((1, dm)), _const_spec(wc.shape), _const_spec(wqkv.shape), _const_spec(wg.shape),
                _const_spec(wf2.shape), _const_spec((1, LANES)), _const_spec(cw.shape), _const_spec((1, dc))]
    args = [x, gain, wc, wqkv, wg, wf2, bfp, cw, cbias]
    sds = jax.ShapeDtypeStruct
    if sample:
        in_specs += [rows(dc), rows(dc)]
        args += list(fix)
        qk_shapes = [sds((t, da), BF16), sds((t, da), BF16)]
        qk_specs = [rows(da), rows(da)]
        kv_shapes = [sds((t, da), F32), sds((t, da), F32)]
        kv_specs = [rows(da), rows(da)]
        tail_shapes = [sds((t, LANES), F32), sds((SUBLANES, t), F32), sds((t, dc), F32)]
        tail_specs = [rows(LANES), pl.BlockSpec((SUBLANES, tm), lambda i: (0, i)), rows(dc)]
        scratch = []
    else:
        in_specs += [_const_spec(a.shape) for a in aug]
        args += list(aug)
        qk_shapes = [sds((t, 2 * da), BF16), sds((t, 2 * da), BF16)]
        qk_specs = [rows(2 * da), rows(2 * da)]
        kv_shapes = [sds((nseq, da, seq_len), F32), sds((nseq, da, seq_len), F32)]
        kv_specs = [pl.BlockSpec((None, da, tm), lambda i: (i // seq_blocks, 0, i % seq_blocks))] * 2
        tail_shapes = [sds((nseq, SUBLANES, seq_len), F32), sds((nseq, 2, dc), F32)]
        tail_specs = [pl.BlockSpec((None, SUBLANES, tm), lambda i: (i // seq_blocks, 0, i % seq_blocks)),
                      pl.BlockSpec((None, 2, dc), lambda i: (i // seq_blocks, 0, 0))]
        scratch = [pltpu.VMEM((SUBLANES, dc), F32), pltpu.VMEM((SUBLANES, LANES), F32)]
    out_shape = ([sds((t, dc), BF16)] + qk_shapes +
                 [sds((t, da), BF16)] + kv_shapes +
                 [sds((t, gw), BF16)] + tail_shapes)
    out_specs = [rows(dc)] + qk_specs + [rows(da)] + kv_specs + [rows(gw)] + tail_specs
    return pl.pallas_call(
        kern, grid=(nblk,), in_specs=in_specs, out_specs=out_specs, out_shape=out_shape,
        scratch_shapes=scratch, compiler_params=_cparams(("arbitrary",)),
        name="inproj_sample" if sample else "inproj_prompt")(*args)


def _fox_fused_kernel(pt_ref, qa_ref, ka_ref, v_ref, kt_hbm, vt_hbm, lf_hbm, qbd_ref, kn_ref, vn_ref, dtn_ref,
                      o_ref, os_ref,
                      m_sc, l_sc, acc_sc, ms_sc, ls_sc, as_sc, run_sc, kbuf, vbuf, lbuf, sem, g_ref,
                      *, tq, tk, dh, pp, ring, npages, nchunks, s_new, heads):
    first_step = (pl.program_id(0) == 0) & (pl.program_id(1) == 0) & (pl.program_id(2) == 0)
    qi = pl.program_id(2)
    page = kbuf.shape[3]
    cpb = npages // pp
    nrow = s_new * heads
    da = heads * dh
    nbs = qbd_ref.shape[0]

    def chunk_copies(c):
        slot = c % ring
        bs = c // cpb
        jc = c % cpb
        copies = []
        for i in range(pp):
            pid = pt_ref[bs * npages + (npages - 1 - (jc * pp + i))]
            copies.append(pltpu.make_async_copy(kt_hbm.at[pid], kbuf.at[slot, i], sem.at[slot]))
            copies.append(pltpu.make_async_copy(vt_hbm.at[pid], vbuf.at[slot, i], sem.at[slot]))
            copies.append(pltpu.make_async_copy(lf_hbm.at[pid], lbuf.at[slot, i], sem.at[slot]))
        return copies

    @pl.when(first_step)
    def _():
        g_ref[0] = 0
        for c in range(min(ring - 1, nchunks)):
            for cp in chunk_copies(c):
                cp.start()

    lane = lax.broadcasted_iota(jnp.int32, (1, 2 * LANES), 1)
    ext = lane - LANES
    qf = qa_ref[...].astype(F32)
    halves = []
    for h in range(2):
        keep = (((lane >= h * dh) & (lane < (h + 1) * dh))
                | ((ext >= 3 * h) & (ext < 3 * h + 3)) | ((ext >= 6 + 3 * h) & (ext < 9 + 3 * h)))
        halves.append(jnp.where(keep, qf, 0.0))
    qs = jnp.concatenate(halves, axis=0).astype(BF16)
    m_sc[...] = jnp.full_like(m_sc, NEG_BIG)
    l_sc[...] = jnp.zeros_like(l_sc)
    acc_sc[...] = jnp.zeros_like(acc_sc)
    nfull = (qi * tq) // tk

    def prompt_step(j, masked):
        ks = pl.multiple_of(j * tk, tk)
        kb = ka_ref[pl.ds(ks, tk), :]
        vb = v_ref[pl.ds(ks, tk), :]
        s = _dot_nt(qs, kb)
        if masked:
            rloc = lax.broadcasted_iota(jnp.int32, (2 * tq, tk), 0)
            rloc = jnp.where(rloc >= tq, rloc - tq, rloc) + qi * tq
            cloc = lax.broadcasted_iota(jnp.int32, (2 * tq, tk), 1) + ks
            s = jnp.where(cloc <= rloc, s, NEG_BIG)
        m_prev = m_sc[...]
        m_new = jnp.maximum(m_prev, jnp.max(s, axis=1, keepdims=True))
        p = jnp.exp(s - _lane_tile(m_new, tk))
        alpha = jnp.exp(m_prev - m_new)
        l_sc[...] = alpha * l_sc[...] + jnp.sum(p, axis=1, keepdims=True)
        acc_sc[...] = alpha * acc_sc[...] + _dot(p.astype(BF16), vb)
        m_sc[...] = m_new

    def sample_update(s, pv_fn):
        m_prev = ms_sc[...]
        m_new = jnp.maximum(m_prev, jnp.max(s, axis=1, keepdims=True))
        p = jnp.exp(s - _lane_tile(m_new, s.shape[1]))
        alpha = jnp.exp(m_prev - m_new)
        ls_sc[...] = alpha * ls_sc[...] + jnp.sum(p, axis=1, keepdims=True)
        as_sc[...] = _lane_tile(alpha, da) * as_sc[...] + pv_fn(p.astype(BF16))
        ms_sc[...] = m_new

    def sample_chunk(g, valid):
        slot = g % ring
        bs = jnp.minimum(g // cpb, nbs - 1)
        qbd = qbd_ref[bs]
        rr = lax.broadcasted_iota(jnp.int32, (page, 2 * page), 0)
        cc = lax.broadcasted_iota(jnp.int32, (page, 2 * page), 1)
        after = jnp.where((rr > cc) | (cc >= page), 1.0, 0.0).astype(BF16)
        lf = jnp.concatenate([lbuf[slot, i] for i in range(pp)], axis=0)
        n8 = pp * heads
        r3 = _dot(jnp.concatenate(_split3(lf), axis=0), after)
        both = r3[0:n8] + (r3[n8:2 * n8] + r3[2 * n8:3 * n8])
        run = run_sc[...]
        scores = []
        for i in range(pp):
            rev = both[i * heads:(i + 1) * heads, 0:page] + run
            run = run + both[i * heads:(i + 1) * heads, page:2 * page]
            scores.append(_dot(qbd, kbuf[slot, i].astype(BF16)) + jnp.concatenate([rev] * s_new, axis=0))
        run_sc[...] = run
        s = jnp.where(valid, jnp.concatenate(scores, axis=1), NEG_BIG)

        def pv_pages(p):
            acc = None
            for i in range(pp):
                term = _dot_nt(p[:, i * page:(i + 1) * page], vbuf[slot, i].astype(BF16))
                acc = term if acc is None else acc + term
            return acc

        sample_update(s, pv_pages)

    def sample_finish(bs):
        ntok = kn_ref.shape[0]
        qbd = qbd_ref[bs]
        dtn = dtn_ref[...]
        lane_t = lax.broadcasted_iota(jnp.int32, (1, ntok), 1)
        dq_rows = [jnp.sum(jnp.where(lane_t == bs * s_new + t, dtn, 0.0), axis=1, keepdims=True)
                   for t in range(s_new)]
        dq = jnp.concatenate(dq_rows, axis=0)
        dk = jnp.concatenate([dtn] * s_new, axis=0)
        s = _dot_nt(qbd, kn_ref[...]) + (dq - dk)
        rowt = lax.broadcasted_iota(jnp.int32, (nrow, ntok), 0) // heads
        col = lax.broadcasted_iota(jnp.int32, (nrow, ntok), 1)
        keep = ((col // s_new) == bs) & ((col % s_new) <= rowt)
        sample_update(jnp.where(keep, s, NEG_BIG), lambda p: _dot(p, vn_ref[...]))
        o = as_sc[...] / _lane_tile(ls_sc[...], da)
        rowh = lax.broadcasted_iota(jnp.int32, (nrow, da), 0) % heads
        colh = lax.broadcasted_iota(jnp.int32, (nrow, da), 1) // dh
        o = jnp.where(rowh == colh, o, 0.0)
        os_ref[bs] = jnp.sum(o.reshape(s_new, heads, da), axis=1)

    def iteration(j, masked):
        g = g_ref[0]
        valid = g < nchunks

        @pl.when(g + (ring - 1) < nchunks)
        def _():
            for cp in chunk_copies(g + (ring - 1)):
                cp.start()

        @pl.when(valid)
        def _():
            for cp in chunk_copies(g):
                cp.wait()

        @pl.when(valid & (g % cpb == 0))
        def _():
            ms_sc[...] = jnp.full_like(ms_sc, NEG_BIG)
            ls_sc[...] = jnp.zeros_like(ls_sc)
            as_sc[...] = jnp.zeros_like(as_sc)
            run_sc[...] = jnp.zeros_like(run_sc)

        prompt_step(j, masked)
        sample_chunk(g, valid)

        @pl.when(valid & (g % cpb == cpb - 1))
        def _():
            sample_finish(g // cpb)

        g_ref[0] = g + 1

    def body(j, c):
        iteration(j, False)
        return c

    def prompt_body(j, c):
        prompt_step(j, False)
        return c

    stream_live = g_ref[0] < nchunks

    @pl.when(stream_live)
    def _():
        lax.fori_loop(0, nfull, body, 0)
        iteration(nfull, True)

    @pl.when(jnp.logical_not(stream_live))
    def _():
        lax.fori_loop(0, nfull, prompt_body, 0)
        prompt_step(nfull, True)

    o = acc_sc[...] / l_sc[...]
    lane_o = lax.broadcasted_iota(jnp.int32, (1, LANES), 1)
    o_ref[...] = jnp.where(lane_o < dh, o[0:tq], o[tq:2 * tq]).astype(o_ref.dtype)


def _fox_fused(page_table_flat, qa, ka, v, kt_pages, vt_pages, lf_pages, qbd, kn, vn, dtn,
               *, nseq, seq_len, dh, npages, s_new, heads):
    t, da = v.shape
    nbs, nrow, _ = qbd.shape
    page = kt_pages.shape[2]
    ntok = kn.shape[0]
    tk = min(FOX_KEY_BLOCK, seq_len)
    tq = min(FOX_QUERY_BLOCK, tk)
    nq = seq_len // tq
    npair = da // LANES
    pp = min(FUSED_PAGES_PER_ITERATION, npages)
    nchunks = nbs * (npages // pp)
    iters = nseq * npair * sum((qi * tq) // tk + 1 for qi in range(nq))
    assert npages % pp == 0 and nchunks <= iters, "the page stream must fit in the prompt attention's iterations"
    kern = functools.partial(_fox_fused_kernel, tq=tq, tk=tk, dh=dh, pp=pp, ring=FUSED_RING, npages=npages,
                             nchunks=nchunks, s_new=s_new, heads=heads)
    const = lambda shape: pl.BlockSpec(shape, lambda b, hp, qi, pt: (0,) * len(shape))
    grid_spec = pltpu.PrefetchScalarGridSpec(
        num_scalar_prefetch=1, grid=(nseq, npair, nq),
        in_specs=[pl.BlockSpec((tq, 2 * LANES), lambda b, hp, qi, pt: (b * nq + qi, hp)),
                  pl.BlockSpec((seq_len, 2 * LANES), lambda b, hp, qi, pt: (b, hp)),
                  pl.BlockSpec((seq_len, LANES), lambda b, hp, qi, pt: (b, hp)),
                  pl.BlockSpec(memory_space=pl.ANY), pl.BlockSpec(memory_space=pl.ANY),
                  pl.BlockSpec(memory_space=pl.ANY),
                  const((nbs, nrow, da)), const((ntok, da)), const((ntok, da)), const((SUBLANES, ntok))],
        out_specs=[pl.BlockSpec((tq, LANES), lambda b, hp, qi, pt: (b * nq + qi, hp)),
                   const((nbs, s_new, da))],
        scratch_shapes=[pltpu.VMEM((2 * tq, LANES), F32), pltpu.VMEM((2 * tq, LANES), F32),
                        pltpu.VMEM((2 * tq, LANES), F32),
                        pltpu.VMEM((nrow, LANES), F32), pltpu.VMEM((nrow, LANES), F32), pltpu.VMEM((nrow, da), F32),
                        pltpu.VMEM((heads, LANES), F32),
                        pltpu.VMEM((FUSED_RING, pp, da, page), F32), pltpu.VMEM((FUSED_RING, pp, da, page), F32),
                        pltpu.VMEM((FUSED_RING, pp, heads, page), F32),
                        pltpu.SemaphoreType.DMA((FUSED_RING,)), pltpu.SMEM((1,), jnp.int32)])
    return pl.pallas_call(
        kern, grid_spec=grid_spec,
        out_shape=[jax.ShapeDtypeStruct((t, da), BF16), jax.ShapeDtypeStruct((nbs, s_new, da), F32)],
        compiler_params=_cparams(("arbitrary", "arbitrary", "arbitrary")),
        name="fox_fused")(page_table_flat, qa, ka, v, kt_pages, vt_pages, lf_pages, qbd, kn, vn, dtn)


def _post_attn_math(x_ref, gc_ref, at_ref, g_ref, wuc_ref, wua_ref, wmo_ref, gain_ref, wxq_ref):
    dm = x_ref.shape[1]
    y_conv = _dot(gc_ref[...], wuc_ref[...])
    y_attn = _dot(at_ref[...].astype(BF16), wua_ref[...])
    g_conv = g_ref[:, 0:dm].astype(F32)
    g_attn = g_ref[:, dm:2 * dm].astype(F32)
    mixed = jax.nn.sigmoid(g_conv) * y_conv + jax.nn.sigmoid(g_attn) * y_attn
    h = x_ref[...] + _dot(mixed.astype(BF16), wmo_ref[...])
    xn = _rms(h, gain_ref[...]).astype(BF16)
    return h, _dot(xn, wxq_ref[...]).astype(BF16)


def _post_attn_kernel(x_ref, gc_ref, at_ref, g_ref, wuc_ref, wua_ref, wmo_ref, gain_ref, wxq_ref, h_ref, qx_ref):
    h, qx = _post_attn_math(x_ref, gc_ref, at_ref, g_ref, wuc_ref, wua_ref, wmo_ref, gain_ref, wxq_ref)
    h_ref[...] = h
    qx_ref[...] = qx.astype(qx_ref.dtype)


def _post_attn(x, gc, at, g, wuc, wua, wmo, gain, wxq, *, tm, name):
    t, dm = x.shape
    dx = wxq.shape[1]
    rows = lambda w: pl.BlockSpec((tm, w), lambda i: (i, 0))
    return pl.pallas_call(
        _post_attn_kernel, grid=(t // tm,),
        in_specs=[rows(dm), rows(gc.shape[1]), rows(at.shape[1]), rows(g.shape[1]), _const_spec(wuc.shape),
                  _const_spec(wua.shape), _const_spec(wmo.shape), _const_spec((1, dm)), _const_spec(wxq.shape)],
        out_specs=[rows(dm), rows(dx)],
        out_shape=[jax.ShapeDtypeStruct((t, dm), F32), jax.ShapeDtypeStruct((t, dx), BF16)],
        compiler_params=_cparams(("parallel",)), name=name)(x, gc, at, g, wuc, wua, wmo, gain, wxq)


def _memkv_kernel(m_ref, gain_ref, wk_ref, wv_ref, k_ref, v_ref, *, xh):
    mn = _rms(m_ref[...], gain_ref[...]).astype(BF16)
    tm = m_ref.shape[0]
    xd = wk_ref.shape[1] // xh
    k = _dot(mn, wk_ref[...])
    v = _dot(mn, wv_ref[...])
    for h in range(xh):
        k_ref[pl.ds(h, tm, stride=xh), :] = k[:, h * xd:(h + 1) * xd]
        v_ref[pl.ds(h, tm, stride=xh), :] = v[:, h * xd:(h + 1) * xd]


def _memkv(mem, gain, wk, wv, *, xh):
    t, dm = mem.shape
    tm = min(512, t)
    xd = wk.shape[1] // xh
    rows = lambda w: pl.BlockSpec((tm, w), lambda i: (i, 0))
    return pl.pallas_call(
        functools.partial(_memkv_kernel, xh=xh), grid=(t // tm,),
        in_specs=[rows(dm), _const_spec((1, dm)), _const_spec(wk.shape), _const_spec(wv.shape)],
        out_specs=[pl.BlockSpec((tm * xh, xd), lambda i: (i, 0))] * 2,
        out_shape=[jax.ShapeDtypeStruct((t * xh, xd), F32)] * 2,
        compiler_params=_cparams(("parallel",)), name="memkv")(mem, gain, wk, wv)


def _xattn_math(q, mk_ref, mv_ref, xh):
    nm = mk_ref.shape[0] // xh
    xd = mk_ref.shape[1]
    scale = xd ** -0.5
    outs = []
    for h in range(xh):
        mk = mk_ref[pl.ds(h, nm, stride=xh), :].astype(BF16)
        mv = mv_ref[pl.ds(h, nm, stride=xh), :].astype(BF16)
        s = _dot_nt(q[:, h * xd:(h + 1) * xd], mk) * scale
        m = jnp.max(s, axis=1, keepdims=True)
        p = jnp.exp(s - m)
        p = p / jnp.sum(p, axis=1, keepdims=True)
        outs.append(_dot(p.astype(BF16), mv))
    return jnp.concatenate(outs, axis=1)


def _xattn_kernel(q_ref, mk_ref, mv_ref, o_ref, *, xh):
    tq, dx = q_ref.shape[1:]
    xd = dx // xh
    nmh = mk_ref.shape[1]
    scale = xd ** -0.5
    rowh = lax.broadcasted_iota(jnp.int32, (xh * tq, nmh), 0) // tq
    colh = lax.broadcasted_iota(jnp.int32, (xh * tq, nmh), 1) % xh
    same_head = rowh == colh
    for b in range(q_ref.shape[0]):
        q = q_ref[b].astype(BF16)
        qs = jnp.concatenate([q[:, h * xd:(h + 1) * xd] for h in range(xh)], axis=0)
        s = jnp.where(same_head, _dot_nt(qs, mk_ref[b].astype(BF16)) * scale, NEG_BIG)
        m = jnp.max(s, axis=1, keepdims=True)
        p = jnp.exp(s - m)
        p = p / jnp.sum(p, axis=1, keepdims=True)
        o = _dot(p.astype(BF16), mv_ref[b].astype(BF16))
        o_ref[b] = jnp.concatenate([o[h * tq:(h + 1) * tq] for h in range(xh)], axis=1).astype(o_ref.dtype)


def _xattn(q, mk, mv, *, tq, xh, nbb, name):
    nb, s, dx = q.shape
    nm, xd = mk.shape[1:]
    kern = functools.partial(_xattn_kernel, xh=xh)
    return pl.pallas_call(
        kern, grid=(nb // nbb, s // tq),
        in_specs=[pl.BlockSpec((nbb, tq, dx), lambda b, i: (b, i, 0)),
                  pl.BlockSpec((nbb, nm, xd), lambda b, i: (b, 0, 0)),
                  pl.BlockSpec((nbb, nm, xd), lambda b, i: (b, 0, 0))],
        out_specs=pl.BlockSpec((nbb, tq, dx), lambda b, i: (b, i, 0)),
        out_shape=jax.ShapeDtypeStruct((nb, s, dx), q.dtype),
        compiler_params=_cparams(("parallel", "parallel")), name=name)(q, mk, mv)


def _pre_moe_kernel(h_ref, o_ref, wxo_ref, gain_ref, wrh_ref, wrl_ref, cnt_in_ref,
                    h2_ref, xn_ref, rt_ref, rr_ref, cnt_out_ref, base, *, tm, ngroups, epg):
    @pl.when(pl.program_id(0) == 0)
    def _():
        base[...] = cnt_in_ref[...]

    _pre_moe_math(h_ref[...], o_ref[...].astype(BF16), wxo_ref, gain_ref, wrh_ref, wrl_ref,
                  h2_ref, xn_ref, rt_ref, rr_ref, cnt_out_ref, base, tm=tm, ngroups=ngroups, epg=epg)


def _mid_kernel(x_ref, gc_ref, at_ref, g_ref, mk_ref, mv_ref, wuc_ref, wua_ref, wmo_ref, gx_ref, wxq_ref,
                wxo_ref, gf_ref, wrh_ref, wrl_ref, cnt_in_ref,
                h2_ref, xn_ref, rt_ref, rr_ref, cnt_out_ref, base, *, tm, ngroups, epg, xh):
    @pl.when(pl.program_id(0) == 0)
    def _():
        base[...] = cnt_in_ref[...]

    h1, qx = _post_attn_math(x_ref, gc_ref, at_ref, g_ref, wuc_ref, wua_ref, wmo_ref, gx_ref, wxq_ref)
    o = _xattn_math(qx, mk_ref, mv_ref, xh).astype(BF16)
    _pre_moe_math(h1, o, wxo_ref, gf_ref, wrh_ref, wrl_ref,
                  h2_ref, xn_ref, rt_ref, rr_ref, cnt_out_ref, base, tm=tm, ngroups=ngroups, epg=epg)


def _pre_moe_math(h, o, wxo_ref, gain_ref, wrh_ref, wrl_ref, h2_ref, xn_ref, rt_ref, rr_ref, cnt_out_ref, base,
                  *, tm, ngroups, epg):
    h2 = h + _dot(o, wxo_ref[...])
    h2_ref[...] = h2
    xn = _rms(h2, gain_ref[...])
    xw = _pack_bf16_pairs(xn)
    nc = xw.shape[1] // LANES
    for c in range(nc):
        xn_ref[pl.ds(c, tm, stride=nc), :] = xw[:, c * LANES:(c + 1) * LANES]

    xh = xn.astype(BF16)
    xl = (xn - xh.astype(F32)).astype(BF16)
    lt = _dot_nt(wrh_ref[...], xh) + (_dot_nt(wrh_ref[...], xl) + _dot_nt(wrl_ref[...], xh))

    sub = lax.broadcasted_iota(jnp.int32, (SUBLANES, tm), 0)
    gl = jnp.where(sub < ngroups, lt[0:SUBLANES, :], -jnp.inf)
    gmax = jnp.max(gl, axis=0, keepdims=True)
    gidx = jnp.min(jnp.where(gl == gmax, sub, SUBLANES), axis=0, keepdims=True)
    pg = 1.0 / jnp.sum(jnp.exp(gl - gmax), axis=0, keepdims=True)
    el = jnp.zeros((epg, tm), F32)
    for g in range(ngroups):
        el = jnp.where(gidx == g, lt[EXPERT_ROW0 + g * epg:EXPERT_ROW0 + (g + 1) * epg, :], el)
    v1 = jnp.max(el, axis=0, keepdims=True)
    i1 = jnp.min(jnp.where(el == v1, sub, epg), axis=0, keepdims=True)
    el2 = jnp.where(sub == i1, -jnp.inf, el)
    v2 = jnp.max(el2, axis=0, keepdims=True)
    i2 = jnp.min(jnp.where(el2 == v2, sub, epg), axis=0, keepdims=True)
    t2 = jnp.exp(v2 - v1)
    den = 1.0 + t2
    w0 = (1.0 / den) * pg
    w1 = (t2 / den) * pg
    e0 = gidx * epg + i1
    e1 = gidx * epg + i2

    erow = lax.broadcasted_iota(jnp.int32, (LANES, tm), 0)
    oh0 = erow == e0
    oh1 = erow == e1
    rr = lax.broadcasted_iota(jnp.int32, (tm, tm), 0)
    cc = lax.broadcasted_iota(jnp.int32, (tm, tm), 1)
    triu = jnp.where(rr <= cc, 1.0, 0.0).astype(BF16)
    pre0 = _dot(jnp.where(oh0, 1.0, 0.0).astype(BF16), triu)
    pre1 = _dot(jnp.where(oh1, 1.0, 0.0).astype(BF16), triu)
    b0 = base[:, 0:1]
    tot0 = pre0[:, tm - 1:tm]
    tot1 = pre1[:, tm - 1:tm]
    rank0 = jnp.sum(jnp.where(oh0, pre0 - 1.0 + b0, 0.0), axis=0, keepdims=True)
    rank1 = jnp.sum(jnp.where(oh1, pre1 - 1.0 + (b0 + tot0), 0.0), axis=0, keepdims=True)
    newb = b0 + tot0 + tot1
    base[...] = jnp.broadcast_to(newb, base.shape)
    cnt_out_ref[...] = jnp.broadcast_to(newb, cnt_out_ref.shape)

    zero = jnp.zeros((1, tm), F32)
    rt = jnp.concatenate([e0.astype(F32), e1.astype(F32), w0, w1, rank0, rank1, zero, zero], axis=0)
    rt_ref[...] = rt
    rt_pad = jnp.concatenate([rt, jnp.zeros((LANES - ROUTE_ROWS, tm), F32)], axis=0)
    rr_ref[...] = jnp.transpose(rt_pad)


def _pre_moe(h, o, wxo, gain, wrh, wrl, cnt_in, *, tm, ngroups, epg, name):
    t, dm = h.shape
    dx = o.shape[1]
    kern = functools.partial(_pre_moe_kernel, tm=tm, ngroups=ngroups, epg=epg)
    rows = lambda w: pl.BlockSpec((tm, w), lambda i: (i, 0))
    return pl.pallas_call(
        kern, grid=(t // tm,),
        in_specs=[rows(dm), rows(dx), _const_spec(wxo.shape), _const_spec((1, dm)), _const_spec(wrh.shape),
                  _const_spec(wrl.shape), _const_spec((LANES, LANES))],
        out_specs=[rows(dm), pl.BlockSpec((tm * (dm // (2 * LANES)), LANES), lambda i: (i, 0)),
                   pl.BlockSpec((ROUTE_ROWS, tm), lambda i: (0, i)), rows(LANES), _const_spec((LANES, LANES))],
        out_shape=[jax.ShapeDtypeStruct((t, dm), F32), jax.ShapeDtypeStruct((t * (dm // (2 * LANES)), LANES), jnp.uint32),
                   jax.ShapeDtypeStruct((ROUTE_ROWS, t), F32), jax.ShapeDtypeStruct((t, LANES), F32),
                   jax.ShapeDtypeStruct((LANES, LANES), F32)],
        scratch_shapes=[pltpu.VMEM((LANES, LANES), F32)],
        compiler_params=_cparams(("arbitrary",)), name=name)(h, o, wxo, gain, wrh, wrl, cnt_in)


def _mid(x, gc, at, g, mk, mv, wuc, wua, wmo, gx, wxq, wxo, gf, wrh, wrl, cnt_in, *, tm, seq_len, ngroups, epg, xh):
    t, dm = x.shape
    nm, xd = mk.shape[1:]
    seq_blocks = seq_len // tm
    kern = functools.partial(_mid_kernel, tm=tm, ngroups=ngroups, epg=epg, xh=xh)
    rows = lambda w: pl.BlockSpec((tm, w), lambda i: (i, 0))
    mem = pl.BlockSpec((None, nm, xd), lambda i: (i // seq_blocks, 0, 0))
    consts = [wuc, wua, wmo, gx, wxq, wxo, gf, wrh, wrl, cnt_in]
    return pl.pallas_call(
        kern, grid=(t // tm,),
        in_specs=[rows(dm), rows(gc.shape[1]), rows(at.shape[1]), rows(g.shape[1]), mem, mem]
        + [_const_spec(a.shape) for a in consts],
        out_specs=[rows(dm), pl.BlockSpec((tm * (dm // (2 * LANES)), LANES), lambda i: (i, 0)),
                   pl.BlockSpec((ROUTE_ROWS, tm), lambda i: (0, i)), rows(LANES), _const_spec((LANES, LANES))],
        out_shape=[jax.ShapeDtypeStruct((t, dm), F32), jax.ShapeDtypeStruct((t * (dm // (2 * LANES)), LANES), jnp.uint32),
                   jax.ShapeDtypeStruct((ROUTE_ROWS, t), F32), jax.ShapeDtypeStruct((t, LANES), F32),
                   jax.ShapeDtypeStruct((LANES, LANES), F32)],
        scratch_shapes=[pltpu.VMEM((LANES, LANES), F32)],
        compiler_params=_cparams(("arbitrary",)), name="mid_prompt")(x, gc, at, g, mk, mv, *consts)


def _row_copy(src, r_src, dst, r_dst, sem, nc):
    return pltpu.make_async_copy(src.at[pl.ds(pl.multiple_of(r_src * nc, nc), nc)],
                                 dst.at[pl.ds(pl.multiple_of(r_dst * nc, nc), nc)], sem)


def _slab_rows(ref, nc):
    rows = ref.shape[0] // nc
    return jnp.concatenate([ref[pl.ds(c, rows, stride=nc), :] for c in range(nc)], axis=1)


def _pack_bf16_pairs(x):
    half = x.shape[1] // 2
    lo = pltpu.bitcast(x[:, 0:half].astype(BF16).astype(F32), jnp.uint32)
    hi = pltpu.bitcast(x[:, half:2 * half].astype(BF16).astype(F32), jnp.uint32)
    return lax.shift_right_logical(lo, jnp.uint32(16)) | (hi & jnp.uint32(0xFFFF0000))


def _unpack_bf16_pairs(w):
    lo = pltpu.bitcast(lax.shift_left(w, jnp.uint32(16)), F32)
    hi = pltpu.bitcast(w & jnp.uint32(0xFFFF0000), F32)
    return jnp.concatenate([lo, hi], axis=1)


def _dest_kernel(rt_ref, pstart_ref, o_ref):
    rt = rt_ref[...]
    pst = pstart_ref[:, 0:1]
    erow = lax.broadcasted_iota(jnp.int32, (LANES, rt.shape[1]), 0)
    rows = []
    for k in range(TOP_K):
        e = rt[k:k + 1, :].astype(jnp.int32)
        rows.append(jnp.sum(jnp.where(erow == e, pst, 0.0), axis=0, keepdims=True) + rt[4 + k:5 + k, :])
    rows.append(jnp.zeros((ROUTE_ROWS - TOP_K, rt.shape[1]), F32))
    o_ref[...] = jnp.concatenate(rows, axis=0).astype(jnp.int32)


def _dest(rt_all, pstart_col):
    tall = rt_all.shape[1]
    nchunk = 3 if tall % (3 * LANES) == 0 else 1
    w = tall // nchunk
    return pl.pallas_call(
        _dest_kernel, grid=(nchunk,),
        in_specs=[pl.BlockSpec((ROUTE_ROWS, w), lambda i: (0, i)), _const_spec((LANES, LANES))],
        out_specs=pl.BlockSpec((ROUTE_ROWS, w), lambda i: (0, i)),
        out_shape=jax.ShapeDtypeStruct((ROUTE_ROWS, tall), jnp.int32),
        compiler_params=_cparams(("parallel",)), name="dest")(rt_all, pstart_col)


def _scatter_kernel(dest_ref, pstart_ref, pend_ref, xp_ref, xs_ref, out_ref, zeros, sem, zsem,
                    *, tm_p, nblk_p, nexp, nc):
    i = pl.program_id(0)
    tall = dest_ref.shape[0] // TOP_K

    @pl.when(i == 0)
    def _():
        zeros[...] = jnp.zeros_like(zeros)

        def zero_block(blk):
            start = pl.multiple_of(blk * (EXPERT_ROWS * nc), EXPERT_ROWS * nc)
            return pltpu.make_async_copy(zeros, out_ref.at[pl.ds(start, EXPERT_ROWS * nc)], zsem)

        for e in range(nexp):
            @pl.when(pend_ref[e] > pstart_ref[e])
            def _():
                zero_block(pend_ref[e] // EXPERT_ROWS - 1).start()
        for e in range(nexp):
            @pl.when(pend_ref[e] > pstart_ref[e])
            def _():
                zero_block(pend_ref[e] // EXPERT_ROWS - 1).wait()
        nblk = out_ref.shape[0] // (EXPERT_ROWS * nc)
        nused = pend_ref[nexp - 1] // EXPERT_ROWS

        def start_unused(blk, c):
            zero_block(blk).start()
            return c

        def wait_unused(blk, c):
            zero_block(blk).wait()
            return c

        lax.fori_loop(nused, nblk, start_unused, 0)
        lax.fori_loop(nused, nblk, wait_unused, 0)

    def copy_rows(x_ref, base):
        tm = x_ref.shape[0] // nc

        def issue(r, c):
            for k in range(TOP_K):
                _row_copy(x_ref, r, out_ref, dest_ref[k * tall + base + r], sem, nc).start(priority=k % 2)
            return c

        lax.fori_loop(0, tm, issue, 0, unroll=8)
        for k in range(TOP_K):
            pltpu.make_async_copy(x_ref, x_ref, sem).wait()

    @pl.when(i < nblk_p)
    def _():
        copy_rows(xp_ref, i * tm_p)

    @pl.when(i == nblk_p)
    def _():
        copy_rows(xs_ref, nblk_p * tm_p)


def _scatter(dest_flat, pstart, pend, x_p, x_s, *, n_rows, tm_p, nexp, nc):
    tp = x_p.shape[0] // nc
    ts = x_s.shape[0] // nc
    nblk_p = tp // tm_p
    kern = functools.partial(_scatter_kernel, tm_p=tm_p, nblk_p=nblk_p, nexp=nexp, nc=nc)
    grid_spec = pltpu.PrefetchScalarGridSpec(
        num_scalar_prefetch=3, grid=(nblk_p + 1,),
        in_specs=[pl.BlockSpec((tm_p * nc, LANES), lambda i, *_: (jnp.minimum(i, nblk_p - 1), 0)),
                  pl.BlockSpec((ts * nc, LANES), lambda i, *_: (0, 0))],
        out_specs=pl.BlockSpec(memory_space=pl.ANY),
        scratch_shapes=[pltpu.VMEM((EXPERT_ROWS * nc, LANES), jnp.uint32), pltpu.SemaphoreType.DMA(()),
                        pltpu.SemaphoreType.DMA(())])
    return pl.pallas_call(
        kern, grid_spec=grid_spec, out_shape=jax.ShapeDtypeStruct((n_rows * nc, LANES), jnp.uint32),
        compiler_params=_cparams(("arbitrary",)), name="scatter")(dest_flat, pstart, pend, x_p, x_s)


def _experts_kernel(blk_e_ref, nused_ref, xs_ref, wg_ref, wu_ref, wd_ref, y_ref, wgb, wub, wdb, *, nc):
    i = pl.program_id(0)
    prev = blk_e_ref[jnp.maximum(i - 1, 0)]
    fresh = (i == 0) | (blk_e_ref[i] != prev)

    @pl.when(i < nused_ref[0])
    def _():
        @pl.when(fresh)
        def _():
            wgb[...] = wg_ref[...].astype(BF16)
            wub[...] = wu_ref[...].astype(BF16)
            wdb[...] = wd_ref[...].astype(BF16)
        x = _unpack_bf16_pairs(_slab_rows(xs_ref, nc)).astype(BF16)
        a = _dot(x, wgb[...])
        u = _dot(x, wub[...])
        hmid = (a * jax.nn.sigmoid(a)) * u
        y = _pack_bf16_pairs(_dot(hmid.astype(BF16), wdb[...]))
        for c in range(nc):
            y_ref[pl.ds(c, EXPERT_ROWS, stride=nc), :] = y[:, c * LANES:(c + 1) * LANES]

    @pl.when(i >= nused_ref[0])
    def _():
        y_ref[...] = jnp.zeros_like(y_ref)


def _experts(blk_e, nused, xs, wg, wu, wd):
    dm, de = wg.shape[1:]
    nc = dm // (2 * LANES)
    nblk = xs.shape[0] // (EXPERT_ROWS * nc)

    def row_map(i, be, nu):
        return (jnp.minimum(i, nu[0] - 1), 0)

    grid_spec = pltpu.PrefetchScalarGridSpec(
        num_scalar_prefetch=2, grid=(nblk,),
        in_specs=[pl.BlockSpec((EXPERT_ROWS * nc, LANES), row_map),
                  pl.BlockSpec((None, dm, de), lambda i, be, nu: (be[i], 0, 0)),
                  pl.BlockSpec((None, dm, de), lambda i, be, nu: (be[i], 0, 0)),
                  pl.BlockSpec((None, de, dm), lambda i, be, nu: (be[i], 0, 0))],
        out_specs=pl.BlockSpec((EXPERT_ROWS * nc, LANES), lambda i, be, nu: (i, 0)),
        scratch_shapes=[pltpu.VMEM((dm, de), BF16), pltpu.VMEM((dm, de), BF16), pltpu.VMEM((de, dm), BF16)])
    return pl.pallas_call(
        functools.partial(_experts_kernel, nc=nc), grid_spec=grid_spec,
        out_shape=jax.ShapeDtypeStruct(xs.shape, jnp.uint32),
        compiler_params=_cparams(("arbitrary",)), name="experts")(blk_e, nused, xs, wg, wu, wd)


def _combine_kernel(dest_ref, h_ref, rr_ref, gain_ref, y_hbm, o_ref, buf, sem, *, tm, tok0):
    i = pl.program_id(0)
    nsteps = pl.num_programs(0)
    nc = h_ref.shape[1] // (2 * LANES)
    tall = dest_ref.shape[0] // TOP_K

    def request(blk):
        slot = blk % 2
        base = tok0 + blk * tm

        def issue(r, c):
            for k in range(TOP_K):
                _row_copy(y_hbm, dest_ref[k * tall + base + r], buf.at[slot, k], r, sem.at[slot],
                          nc).start(priority=k % 2)
            return c

        lax.fori_loop(0, tm, issue, 0, unroll=8)

    @pl.when(i == 0)
    def _():
        request(0)

    @pl.when(i + 1 < nsteps)
    def _():
        request(i + 1)

    slot = i % 2
    for k in range(TOP_K):
        pltpu.make_async_copy(buf.at[slot, k], buf.at[slot, k], sem.at[slot]).wait()
    rr = rr_ref[...]
    y0 = _unpack_bf16_pairs(_slab_rows(buf.at[slot, 0], nc))
    y1 = _unpack_bf16_pairs(_slab_rows(buf.at[slot, 1], nc))
    h = h_ref[...] + (rr[:, 2:3] * y0 + rr[:, 3:4] * y1)
    o_ref[...] = _rms(h, gain_ref[...])


def _combine(dest_flat, h, rr, gain, y, *, tm, tok0, name):
    t, dm = h.shape
    kern = functools.partial(_combine_kernel, tm=tm, tok0=tok0)
    grid_spec = pltpu.PrefetchScalarGridSpec(
        num_scalar_prefetch=1, grid=(t // tm,),
        in_specs=[pl.BlockSpec((tm, dm), lambda i, *_: (i, 0)), pl.BlockSpec((tm, LANES), lambda i, *_: (i, 0)),
                  pl.BlockSpec((1, dm), lambda i, *_: (0, 0)), pl.BlockSpec(memory_space=pl.ANY)],
        out_specs=pl.BlockSpec((tm, dm), lambda i, *_: (i, 0)),
        scratch_shapes=[pltpu.VMEM((2, TOP_K, tm * (dm // (2 * LANES)), LANES), jnp.uint32),
                        pltpu.SemaphoreType.DMA((2,))])
    return pl.pallas_call(
        kern, grid_spec=grid_spec, out_shape=jax.ShapeDtypeStruct((t, dm), F32),
        compiler_params=_cparams(("arbitrary",)), name=name)(dest_flat, h, rr, gain, y)


def kernel(x_prompt, x_sample, cache_k, cache_v, cache_logf, cache_mem_k, cache_mem_v, state_conv, page_table,
           mem_prompt, norm_mix, w_in, b_forget, conv_w, conv_b, w_up_conv, w_up_attn, w_mix_out, norm_xattn,
           norm_mem, w_xq, w_xk, w_xv, w_xo, norm_ffn, w_router_group, w_router_expert, w_expert_gate,
           w_expert_up, w_expert_down, norm_final):
    depth = w_in.shape[0]
    assert depth == 1, "single-layer trunk"
    nbp, seq, dm = x_prompt.shape
    nbs, s_new, _ = x_sample.shape
    _, n_pool, page, heads, dh = cache_k.shape
    npages = page_table.shape[1]
    nmem, xh, xd = cache_mem_k.shape[2:]
    dc = conv_w.shape[2]
    da = heads * dh
    dx = xh * xd
    ngroups, _, epg = w_router_expert.shape[1:]
    nexp = ngroups * epg
    tp = nbp * seq
    ts = nbs * s_new
    assert conv_w.shape[1] == 3 and s_new >= 2 and dh == 64 and heads == SUBLANES and epg == SUBLANES
    assert page == LANES

    l = 0
    wi = w_in[l]
    wc = wi[:, 0:3 * dc].astype(BF16)
    wqkv = wi[:, 3 * dc:3 * dc + 3 * da].astype(BF16)
    o_f = 3 * dc + 3 * da
    wf = jnp.pad(wi[:, o_f:o_f + heads], ((0, 0), (0, LANES - heads)))
    wfh = wf.astype(BF16)
    wf2 = jnp.concatenate([wfh, (wf - wfh.astype(F32)).astype(BF16)], axis=1)
    wg = wi[:, o_f + heads:].astype(BF16)
    bfp = jnp.pad(b_forget[l][None, :], ((0, 0), (0, LANES - heads)))
    cw = conv_w[l]
    cbias = conv_b[l][None, :]
    g_mix = norm_mix[l][None, :]
    wuc = w_up_conv[l].astype(BF16)
    wua = w_up_attn[l].astype(BF16)
    wmo = w_mix_out[l].astype(BF16)
    g_x = norm_xattn[l][None, :]
    wxq = w_xq[l].astype(BF16)
    wxo = w_xo[l].astype(BF16)
    g_f = norm_ffn[l][None, :]
    wr = jnp.zeros((LANES, dm), F32)
    wr = wr.at[0:ngroups].set(w_router_group[l].T)
    wr = wr.at[EXPERT_ROW0:EXPERT_ROW0 + nexp].set(jnp.transpose(w_router_expert[l], (0, 2, 1)).reshape(nexp, dm))
    wrh = wr.astype(BF16)
    wrl = (wr - wrh.astype(F32)).astype(BF16)

    xp = x_prompt.reshape(tp, dm)
    (gc_p, qa_p, ka_p, vb_p, kt_p, vt_p, g_p, lf_p, ulast_p) = _inproj(
        xp, g_mix, wc, wqkv, wg, wf2, bfp, cw, cbias, seq_len=seq, sample=False, aug=_decay_columns(heads))
    xs_ = x_sample.reshape(ts, dm)
    st = state_conv[l]
    zeros_row = jnp.zeros((nbs, 1, dc), F32)
    fix1 = jnp.concatenate([st[:, 1:2], jnp.tile(zeros_row, (1, s_new - 1, 1))], axis=1).reshape(ts, dc)
    fix2 = jnp.concatenate([st[:, 0:1], st[:, 1:2], jnp.tile(zeros_row, (1, s_new - 2, 1))], axis=1).reshape(ts, dc)
    (gc_s, q_s, kb_s, vb_s, kf_s, vf_s, g_s, lf_s, dt_s, u_s) = _inproj(
        xs_, g_mix, wc, wqkv, wg, wf2, bfp, cw, cbias, seq_len=s_new, sample=True, fix=(fix1, fix2))
    pt_flat = page_table.reshape(-1).astype(jnp.int32)
    head_of_col = jnp.arange(da) // dh
    qbd = jnp.where(head_of_col[None, None, None, :] == jnp.arange(heads)[None, None, :, None],
                    q_s.reshape(nbs, s_new, 1, da), jnp.zeros((), BF16)).reshape(nbs, s_new * heads, da)
    kt_pages = jnp.transpose(cache_k[l], (0, 2, 3, 1)).reshape(n_pool, da, page)
    vt_pages = jnp.transpose(cache_v[l], (0, 2, 3, 1)).reshape(n_pool, da, page)
    lf_pages = jnp.swapaxes(cache_logf[l], 1, 2)
    at_p, at_s = _fox_fused(pt_flat, qa_p, ka_p, vb_p, kt_pages, vt_pages, lf_pages, qbd, kb_s, vb_s, dt_s,
                            nseq=nbp, seq_len=seq, dh=dh, npages=npages, s_new=s_new, heads=heads)

    tm_p = min(512, seq)
    mk_p, mv_p = _memkv(mem_prompt.reshape(nbp * nmem, dm), norm_mem[l][None, :], w_xk[l].astype(BF16),
                        w_xv[l].astype(BF16), xh=xh)
    cnt0 = jnp.zeros((LANES, LANES), F32)
    h2_p, xn_p, rt_p, rr_p, cnt1 = _mid(
        xp, gc_p, at_p, g_p, mk_p.reshape(nbp, nmem * xh, xd), mv_p.reshape(nbp, nmem * xh, xd),
        wuc, wua, wmo, g_x, wxq, wxo, g_f, wrh, wrl, cnt0, tm=tm_p, seq_len=seq, ngroups=ngroups, epg=epg, xh=xh)

    h1_s, qx_s = _post_attn(xs_, gc_s, at_s.reshape(ts, da), g_s, wuc, wua, wmo, g_x, wxq, tm=ts,
                            name="post_attn_sample")
    qx_s8 = jnp.pad(qx_s.astype(F32).reshape(nbs, s_new, dx), ((0, 0), (0, SUBLANES - s_new), (0, 0)))
    o_s = _xattn(qx_s8, cache_mem_k[l].reshape(nbs, nmem * xh, xd), cache_mem_v[l].reshape(nbs, nmem * xh, xd),
                 tq=SUBLANES, xh=xh, nbb=4 if nbs % 4 == 0 else 1, name="xattn_sample")[:, :s_new].reshape(ts, dx)
    h2_s, xn_s, rt_s, rr_s, cnt2 = _pre_moe(h1_s, o_s, wxo, g_f, wrh, wrl, cnt1, tm=ts, ngroups=ngroups, epg=epg,
                                            name="pre_moe_sample")

    tall = tp + ts
    counts = cnt2[0:nexp, 0].astype(jnp.int32)
    padded = (counts + EXPERT_ROWS - 1) // EXPERT_ROWS * EXPERT_ROWS
    pend = jnp.cumsum(padded).astype(jnp.int32)
    pstart = pend - padded
    nblk = (tall * TOP_K + nexp * (EXPERT_ROWS - 1)) // EXPERT_ROWS
    n_rows = nblk * EXPERT_ROWS
    blk_row0 = jnp.arange(nblk, dtype=jnp.int32) * EXPERT_ROWS
    blk_e = jnp.minimum(jnp.sum((pend[None, :] <= blk_row0[:, None]).astype(jnp.int32), axis=1), nexp - 1)
    nused = (pend[nexp - 1:nexp] // EXPERT_ROWS).astype(jnp.int32)
    rt_all = jnp.concatenate([rt_p, rt_s], axis=1)
    pstart_col = jnp.zeros((LANES, LANES), F32).at[0:nexp, :].set(pstart.astype(F32)[:, None])
    dest_flat = _dest(rt_all, pstart_col)[0:TOP_K].reshape(-1)
    xsg = _scatter(dest_flat, pstart, pend, xn_p, xn_s, n_rows=n_rows, tm_p=tm_p, nexp=nexp, nc=dm // (2 * LANES))
    y = _experts(blk_e, nused, xsg, w_expert_gate[l], w_expert_up[l], w_expert_down[l])
    g_fin = norm_final[None, :]
    y_p = _combine(dest_flat, h2_p, rr_p, g_fin, y, tm=tm_p, tok0=0, name="combine_prompt")
    y_s = _combine(dest_flat, h2_s, rr_s, g_fin, y, tm=ts, tok0=tp, name="combine_sample")

    return (y_p.reshape(nbp, seq, dm), y_s.reshape(nbs, s_new, dm),
            ulast_p[None],
            jnp.transpose(kt_p.reshape(nbp, heads, dh, seq), (0, 3, 1, 2))[None],
            jnp.transpose(vt_p.reshape(nbp, heads, dh, seq), (0, 3, 1, 2))[None],
            jnp.transpose(lf_p, (0, 2, 1))[None],
            mk_p.reshape(1, nbp, nmem, xh, xd), mv_p.reshape(1, nbp, nmem, xh, xd),
            u_s.reshape(nbs, s_new, dc)[None, :, s_new - 2:], kf_s.reshape(1, nbs, s_new, heads, dh),
            vf_s.reshape(1, nbs, s_new, heads, dh), lf_s[:, :heads].reshape(1, nbs, s_new, heads))
```

```python
import functools

import numpy as np
import jax
import jax.numpy as jnp
from jax import lax
from jax.experimental import pallas as pl
from jax.experimental.pallas import tpu as pltpu

F32 = jnp.float32
BF16 = jnp.bfloat16

RMS_EPS = 1e-6
TOP_K = 2
LANES = 128
SUBLANES = 8
VMEM_LIMIT = 56 * 1024 * 1024
NEG_BIG = -1e30
EXPERT_ROWS = 512
FUSED_PAGES_PER_ITERATION = 8
FUSED_RING = 3
FOX_QUERY_BLOCK = 512
FOX_KEY_BLOCK = 512
ROUTE_ROWS = 8
EXPERT_ROW0 = 8


def _cparams(sem, vmem=VMEM_LIMIT):
    return pltpu.CompilerParams(dimension_semantics=sem, vmem_limit_bytes=vmem)


def _rms(x, g):
    ms = jnp.mean(x * x, axis=-1, keepdims=True)
    return x * lax.rsqrt(ms + RMS_EPS) * g


def _split3(x):
    hi = x.astype(BF16)
    r = x - hi.astype(F32)
    mid = r.astype(BF16)
    lo = (r - mid.astype(F32)).astype(BF16)
    return hi, mid, lo


def _dot(a, b):
    return jnp.dot(a, b, preferred_element_type=F32)


def _dot_nt(a, b):
    return lax.dot_general(a, b, (((1,), (1,)), ((), ())), preferred_element_type=F32)


def _lane_tile(x, width):
    if width % LANES == 0:
        return jnp.concatenate([x] * (width // LANES), axis=1)
    return jnp.broadcast_to(x[:, 0:1], (x.shape[0], width))


def _const_spec(shape):
    nd = len(shape)
    return pl.BlockSpec(shape, lambda *_: (0,) * nd)


def _inproj_kernel(*refs, tm, seq_blocks, sample, seq_len):
    if sample:
        (x_ref, gain_ref, wc_ref, wqkv_ref, wg_ref, wf2_ref, bf_ref, cw_ref, cb_ref, fix1_ref, fix2_ref,
         gc_ref, q_ref, kbf_ref, vbf_ref, kf_ref, vf_ref, g_ref, logf_ref, dt_ref, u_ref) = refs
    else:
        (x_ref, gain_ref, wc_ref, wqkv_ref, wg_ref, wf2_ref, bf_ref, cw_ref, cb_ref,
         selq_ref, selk_ref, oneq_ref, onek_ref,
         gc_ref, qa_ref, ka_ref, vbf_ref, kf_ref, vf_ref, g_ref, logf_ref, u_ref,
         carry_u, carry_d) = refs
    i = pl.program_id(0)
    dc = cw_ref.shape[1]
    da = wqkv_ref.shape[1] // 3

    xn = _rms(x_ref[...], gain_ref[...])
    xb = xn.astype(BF16)

    cb = _dot(xb, wc_ref[:, 0:dc])
    cc = _dot(xb, wc_ref[:, dc:2 * dc])
    cx = _dot(xb, wc_ref[:, 2 * dc:3 * dc])
    u = cc * cx
    row = lax.broadcasted_iota(jnp.int32, (tm, 1), 0)
    r1 = pltpu.roll(u, 1, axis=0)
    r2 = pltpu.roll(u, 2, axis=0)
    if sample:
        pos = row % seq_len
        p1 = jnp.where(pos == 0, fix1_ref[...], r1)
        p2 = jnp.where(pos < 2, fix2_ref[...], r2)
        u_ref[...] = u
    else:
        @pl.when(i % seq_blocks == 0)
        def _():
            carry_u[...] = jnp.zeros_like(carry_u)
            carry_d[...] = jnp.zeros_like(carry_d)
        c0 = carry_u[0:1, :]
        c1 = carry_u[1:2, :]
        p1 = jnp.where(row == 0, c1, r1)
        p2 = jnp.where(row == 0, c0, jnp.where(row == 1, c1, r2))
        carry_u[0:2, :] = u[tm - 2:tm, :]
        u_ref[...] = u[tm - 2:tm, :]
    cw = cw_ref[...]
    cy = cb_ref[...] + cw[0:1, :] * p2 + cw[1:2, :] * p1 + cw[2:3, :] * u
    gc_ref[...] = (cb * cy).astype(BF16)

    qs = (_dot(xb, wqkv_ref[:, 0:da]) * (1.0 / 8.0)).astype(BF16)
    k = _dot(xb, wqkv_ref[:, da:2 * da])
    kb = k.astype(BF16)
    v = _dot(xb, wqkv_ref[:, 2 * da:3 * da])
    vbf_ref[...] = v.astype(BF16)
    if sample:
        q_ref[...] = qs
        kbf_ref[...] = kb
        kf_ref[...] = k
        vf_ref[...] = v
    else:
        kf_ref[...] = jnp.transpose(k)
        vf_ref[...] = jnp.transpose(v)

    gw = g_ref.shape[1]
    for c in range(gw // 512):
        g_ref[:, c * 512:(c + 1) * 512] = _dot(xb, wg_ref[:, c * 512:(c + 1) * 512]).astype(BF16)

    xl = (xn - xb.astype(F32)).astype(BF16)
    hh_hl = _dot(xb, wf2_ref[...])
    fz = hh_hl[:, 0:LANES] + (hh_hl[:, LANES:2 * LANES] + _dot(xl, wf2_ref[:, 0:LANES]))
    z = fz + bf_ref[...]
    logf = jnp.minimum(z, 0.0) - jnp.log1p(jnp.exp(-jnp.abs(z)))
    lane = lax.broadcasted_iota(jnp.int32, (1, LANES), 1)
    nh = da // 64
    logf = jnp.where(lane < nh, logf, 0.0)
    if sample:
        logf_ref[...] = logf
    else:
        logf_ref[...] = jnp.transpose(logf)[0:SUBLANES, :]

    rr = lax.broadcasted_iota(jnp.int32, (tm, tm), 0)
    cc_ = lax.broadcasted_iota(jnp.int32, (tm, tm), 1)
    if sample:
        tri = (cc_ <= rr) & ((rr // seq_len) == (cc_ // seq_len))
    else:
        tri = cc_ <= rr
    tri = jnp.where(tri, 1.0, 0.0).astype(BF16)
    d3 = _dot(tri, jnp.concatenate(_split3(logf), axis=1))
    d = d3[:, 0:LANES] + (d3[:, LANES:2 * LANES] + d3[:, 2 * LANES:3 * LANES])
    if sample:
        dt_ref[...] = jnp.transpose(d)[0:SUBLANES, :]
    else:
        d = d + carry_d[0:1, :]
        carry_d[0:1, :] = d[tm - 1:tm, :]
        dcat = jnp.concatenate(_split3(d), axis=1)
        aq = (_dot(dcat, selq_ref[...]) + oneq_ref[...]).astype(BF16)
        ak = (_dot(dcat, selk_ref[...]) + onek_ref[...]).astype(BF16)
        for p in range(da // LANES):
            lo_, hi_ = p * LANES, (p + 1) * LANES
            qa_ref[:, 2 * lo_:2 * lo_ + LANES] = qs[:, lo_:hi_]
            qa_ref[:, 2 * lo_ + LANES:2 * hi_] = aq[:, lo_:hi_]
            ka_ref[:, 2 * lo_:2 * lo_ + LANES] = kb[:, lo_:hi_]
            ka_ref[:, 2 * lo_ + LANES:2 * hi_] = ak[:, lo_:hi_]


def _decay_columns(heads):
    npair = heads // 2
    selq = np.zeros((3 * LANES, npair * LANES), np.float32)
    selk = np.zeros((3 * LANES, npair * LANES), np.float32)
    oneq = np.zeros((1, npair * LANES), np.float32)
    onek = np.zeros((1, npair * LANES), np.float32)
    for p in range(npair):
        for hh in range(2):
            for term in range(3):
                selq[term * LANES + 2 * p + hh, p * LANES + 3 * hh + term] = 1.0
                selk[term * LANES + 2 * p + hh, p * LANES + 6 + 3 * hh + term] = -1.0
                oneq[0, p * LANES + 6 + 3 * hh + term] = 1.0
                onek[0, p * LANES + 3 * hh + term] = 1.0
    return jnp.asarray(selq, BF16), jnp.asarray(selk, BF16), jnp.asarray(oneq), jnp.asarray(onek)


def _inproj(x, gain, wc, wqkv, wg, wf2, bfp, cw, cbias, *, seq_len, sample, fix=None, aug=None):
    t, dm = x.shape
    dc = cw.shape[1]
    da = wqkv.shape[1] // 3
    gw = wg.shape[1]
    if sample:
        tm = t
        seq_blocks = 1
    else:
        tm = min(512, seq_len)
        seq_blocks = seq_len // tm
    nblk = t // tm
    nseq = t // seq_len
    kern = functools.partial(_inproj_kernel, tm=tm, seq_blocks=seq_blocks, sample=sample, seq_len=seq_len)
    rows = lambda w: pl.BlockSpec((tm, w), lambda i: (i, 0))
    in_specs = [rows(dm), _const_spec((1, dm)), _const_spec(wc.shape), _const_spec(wqkv.shape), _const_spec(wg.shape),
                _const_spec(wf2.shape), _const_spec((1, LANES)), _const_spec(cw.shape), _const_spec((1, dc))]
    args = [x, gain, wc, wqkv, wg, wf2, bfp, cw, cbias]
    sds = jax.ShapeDtypeStruct
    if sample:
        in_specs += [rows(dc), rows(dc)]
        args += list(fix)
        qk_shapes = [sds((t, da), BF16), sds((t, da), BF16)]
        qk_specs = [rows(da), rows(da)]
        kv_shapes = [sds((t, da), F32), sds((t, da), F32)]
        kv_specs = [rows(da), rows(da)]
        tail_shapes = [sds((t, LANES), F32), sds((SUBLANES, t), F32), sds((t, dc), F32)]
        tail_specs = [rows(LANES), pl.BlockSpec((SUBLANES, tm), lambda i: (0, i)), rows(dc)]
        scratch = []
    else:
        in_specs += [_const_spec(a.shape) for a in aug]
        args += list(aug)
        qk_shapes = [sds((t, 2 * da), BF16), sds((t, 2 * da), BF16)]
        qk_specs = [rows(2 * da), rows(2 * da)]
        kv_shapes = [sds((nseq, da, seq_len), F32), sds((nseq, da, seq_len), F32)]
        kv_specs = [pl.BlockSpec((None, da, tm), lambda i: (i // seq_blocks, 0, i % seq_blocks))] * 2
        tail_shapes = [sds((nseq, SUBLANES, seq_len), F32), sds((nseq, 2, dc), F32)]
        tail_specs = [pl.BlockSpec((None, SUBLANES, tm), lambda i: (i // seq_blocks, 0, i % seq_blocks)),
                      pl.BlockSpec((None, 2, dc), lambda i: (i // seq_blocks, 0, 0))]
        scratch = [pltpu.VMEM((SUBLANES, dc), F32), pltpu.VMEM((SUBLANES, LANES), F32)]
    out_shape = ([sds((t, dc), BF16)] + qk_shapes +
                 [sds((t, da), BF16)] + kv_shapes +
                 [sds((t, gw), BF16)] + tail_shapes)
    out_specs = [rows(dc)] + qk_specs + [rows(da)] + kv_specs + [rows(gw)] + tail_specs
    return pl.pallas_call(
        kern, grid=(nblk,), in_specs=in_specs, out_specs=out_specs, out_shape=out_shape,
        scratch_shapes=scratch, compiler_params=_cparams(("arbitrary",)),
        name="inproj_sample" if sample else "inproj_prompt")(*args)


def _fox_fused_kernel(pt_ref, qa_ref, ka_ref, v_ref, kt_hbm, vt_hbm, lf_hbm, qbd_ref, kn_ref, vn_ref, dtn_ref,
                      o_ref, os_ref,
                      m_sc, l_sc, acc_sc, ms_sc, ls_sc, as_sc, run_sc, kbuf, vbuf, lbuf, sem, g_ref,
                      *, tq, tk, dh, pp, ring, npages, nchunks, s_new, heads):
    first_step = (pl.program_id(0) == 0) & (pl.program_id(1) == 0) & (pl.program_id(2) == 0)
    qi = pl.program_id(2)
    page = kbuf.shape[3]
    cpb = npages // pp
    nrow = s_new * heads
    da = heads * dh
    nbs = qbd_ref.shape[0]

    def chunk_copies(c):
        slot = c % ring
        bs = c // cpb
        jc = c % cpb
        copies = []
        for i in range(pp):
            pid = pt_ref[bs * npages + (npages - 1 - (jc * pp + i))]
            copies.append(pltpu.make_async_copy(kt_hbm.at[pid], kbuf.at[slot, i], sem.at[slot]))
            copies.append(pltpu.make_async_copy(vt_hbm.at[pid], vbuf.at[slot, i], sem.at[slot]))
            copies.append(pltpu.make_async_copy(lf_hbm.at[pid], lbuf.at[slot, i], sem.at[slot]))
        return copies

    @pl.when(first_step)
    def _():
        g_ref[0] = 0
        for c in range(min(ring - 1, nchunks)):
            for cp in chunk_copies(c):
                cp.start()

    lane = lax.broadcasted_iota(jnp.int32, (1, 2 * LANES), 1)
    ext = lane - LANES
    qf = qa_ref[...].astype(F32)
    halves = []
    for h in range(2):
        keep = (((lane >= h * dh) & (lane < (h + 1) * dh))
                | ((ext >= 3 * h) & (ext < 3 * h + 3)) | ((ext >= 6 + 3 * h) & (ext < 9 + 3 * h)))
        halves.append(jnp.where(keep, qf, 0.0))
    qs = jnp.concatenate(halves, axis=0).astype(BF16)
    m_sc[...] = jnp.full_like(m_sc, NEG_BIG)
    l_sc[...] = jnp.zeros_like(l_sc)
    acc_sc[...] = jnp.zeros_like(acc_sc)
    nfull = (qi * tq) // tk

    def prompt_step(j, masked):
        ks = pl.multiple_of(j * tk, tk)
        kb = ka_ref[pl.ds(ks, tk), :]
        vb = v_ref[pl.ds(ks, tk), :]
        s = _dot_nt(qs, kb)
        if masked:
            rloc = lax.broadcasted_iota(jnp.int32, (2 * tq, tk), 0)
            rloc = jnp.where(rloc >= tq, rloc - tq, rloc) + qi * tq
            cloc = lax.broadcasted_iota(jnp.int32, (2 * tq, tk), 1) + ks
            s = jnp.where(cloc <= rloc, s, NEG_BIG)
        m_prev = m_sc[...]
        m_new = jnp.maximum(m_prev, jnp.max(s, axis=1, keepdims=True))
        p = jnp.exp(s - _lane_tile(m_new, tk))
        alpha = jnp.exp(m_prev - m_new)
        l_sc[...] = alpha * l_sc[...] + jnp.sum(p, axis=1, keepdims=True)
        acc_sc[...] = alpha * acc_sc[...] + _dot(p.astype(BF16), vb)
        m_sc[...] = m_new

    def sample_update(s, pv_fn):
        m_prev = ms_sc[...]
        m_new = jnp.maximum(m_prev, jnp.max(s, axis=1, keepdims=True))
        p = jnp.exp(s - _lane_tile(m_new, s.shape[1]))
        alpha = jnp.exp(m_prev - m_new)
        ls_sc[...] = alpha * ls_sc[...] + jnp.sum(p, axis=1, keepdims=True)
        as_sc[...] = _lane_tile(alpha, da) * as_sc[...] + pv_fn(p.astype(BF16))
        ms_sc[...] = m_new

    def sample_chunk(g, valid):
        slot = g % ring
        bs = jnp.minimum(g // cpb, nbs - 1)
        qbd = qbd_ref[bs]
        rr = lax.broadcasted_iota(jnp.int32, (page, 2 * page), 0)
        cc = lax.broadcasted_iota(jnp.int32, (page, 2 * page), 1)
        after = jnp.where((rr > cc) | (cc >= page), 1.0, 0.0).astype(BF16)
        lf = jnp.concatenate([lbuf[slot, i] for i in range(pp)], axis=0)
        n8 = pp * heads
        r3 = _dot(jnp.concatenate(_split3(lf), axis=0), after)
        both = r3[0:n8] + (r3[n8:2 * n8] + r3[2 * n8:3 * n8])
        run = run_sc[...]
        scores = []
        for i in range(pp):
            rev = both[i * heads:(i + 1) * heads, 0:page] + run
            run = run + both[i * heads:(i + 1) * heads, page:2 * page]
            scores.append(_dot(qbd, kbuf[slot, i].astype(BF16)) + jnp.concatenate([rev] * s_new, axis=0))
        run_sc[...] = run
        s = jnp.where(valid, jnp.concatenate(scores, axis=1), NEG_BIG)

        def pv_pages(p):
            acc = None
            for i in range(pp):
                term = _dot_nt(p[:, i * page:(i + 1) * page], vbuf[slot, i].astype(BF16))
                acc = term if acc is None else acc + term
            return acc

        sample_update(s, pv_pages)

    def sample_finish(bs):
        ntok = kn_ref.shape[0]
        qbd = qbd_ref[bs]
        dtn = dtn_ref[...]
        lane_t = lax.broadcasted_iota(jnp.int32, (1, ntok), 1)
        dq_rows = [jnp.sum(jnp.where(lane_t == bs * s_new + t, dtn, 0.0), axis=1, keepdims=True)
                   for t in range(s_new)]
        dq = jnp.concatenate(dq_rows, axis=0)
        dk = jnp.concatenate([dtn] * s_new, axis=0)
        s = _dot_nt(qbd, kn_ref[...]) + (dq - dk)
        rowt = lax.broadcasted_iota(jnp.int32, (nrow, ntok), 0) // heads
        col = lax.broadcasted_iota(jnp.int32, (nrow, ntok), 1)
        keep = ((col // s_new) == bs) & ((col % s_new) <= rowt)
        sample_update(jnp.where(keep, s, NEG_BIG), lambda p: _dot(p, vn_ref[...]))
        o = as_sc[...] / _lane_tile(ls_sc[...], da)
        rowh = lax.broadcasted_iota(jnp.int32, (nrow, da), 0) % heads
        colh = lax.broadcasted_iota(jnp.int32, (nrow, da), 1) // dh
        o = jnp.where(rowh == colh, o, 0.0)
        os_ref[bs] = jnp.sum(o.reshape(s_new, heads, da), axis=1)

    def iteration(j, masked):
        g = g_ref[0]
        valid = g < nchunks

        @pl.when(g + (ring - 1) < nchunks)
        def _():
            for cp in chunk_copies(g + (ring - 1)):
                cp.start()

        @pl.when(valid)
        def _():
            for cp in chunk_copies(g):
                cp.wait()

        @pl.when(valid & (g % cpb == 0))
        def _():
            ms_sc[...] = jnp.full_like(ms_sc, NEG_BIG)
            ls_sc[...] = jnp.zeros_like(ls_sc)
            as_sc[...] = jnp.zeros_like(as_sc)
            run_sc[...] = jnp.zeros_like(run_sc)

        prompt_step(j, masked)
        sample_chunk(g, valid)

        @pl.when(valid & (g % cpb == cpb - 1))
        def _():
            sample_finish(g // cpb)

        g_ref[0] = g + 1

    def body(j, c):
        iteration(j, False)
        return c

    def prompt_body(j, c):
        prompt_step(j, False)
        return c

    stream_live = g_ref[0] < nchunks

    @pl.when(stream_live)
    def _():
        lax.fori_loop(0, nfull, body, 0)
        iteration(nfull, True)

    @pl.when(jnp.logical_not(stream_live))
    def _():
        lax.fori_loop(0, nfull, prompt_body, 0)
        prompt_step(nfull, True)

    o = acc_sc[...] / l_sc[...]
    lane_o = lax.broadcasted_iota(jnp.int32, (1, LANES), 1)
    o_ref[...] = jnp.where(lane_o < dh, o[0:tq], o[tq:2 * tq]).astype(o_ref.dtype)


def _fox_fused(page_table_flat, qa, ka, v, kt_pages, vt_pages, lf_pages, qbd, kn, vn, dtn,
               *, nseq, seq_len, dh, npages, s_new, heads):
    t, da = v.shape
    nbs, nrow, _ = qbd.shape
    page = kt_pages.shape[2]
    ntok = kn.shape[0]
    tk = min(FOX_KEY_BLOCK, seq_len)
    tq = min(FOX_QUERY_BLOCK, tk)
    nq = seq_len // tq
    npair = da // LANES
    pp = min(FUSED_PAGES_PER_ITERATION, npages)
    nchunks = nbs * (npages // pp)
    iters = nseq * npair * sum((qi * tq) // tk + 1 for qi in range(nq))
    assert npages % pp == 0 and nchunks <= iters, "the page stream must fit in the prompt attention's iterations"
    kern = functools.partial(_fox_fused_kernel, tq=tq, tk=tk, dh=dh, pp=pp, ring=FUSED_RING, npages=npages,
                             nchunks=nchunks, s_new=s_new, heads=heads)
    const = lambda shape: pl.BlockSpec(shape, lambda b, hp, qi, pt: (0,) * len(shape))
    grid_spec = pltpu.PrefetchScalarGridSpec(
        num_scalar_prefetch=1, grid=(nseq, npair, nq),
        in_specs=[pl.BlockSpec((tq, 2 * LANES), lambda b, hp, qi, pt: (b * nq + qi, hp)),
                  pl.BlockSpec((seq_len, 2 * LANES), lambda b, hp, qi, pt: (b, hp)),
                  pl.BlockSpec((seq_len, LANES), lambda b, hp, qi, pt: (b, hp)),
                  pl.BlockSpec(memory_space=pl.ANY), pl.BlockSpec(memory_space=pl.ANY),
                  pl.BlockSpec(memory_space=pl.ANY),
                  const((nbs, nrow, da)), const((ntok, da)), const((ntok, da)), const((SUBLANES, ntok))],
        out_specs=[pl.BlockSpec((tq, LANES), lambda b, hp, qi, pt: (b * nq + qi, hp)),
                   const((nbs, s_new, da))],
        scratch_shapes=[pltpu.VMEM((2 * tq, LANES), F32), pltpu.VMEM((2 * tq, LANES), F32),
                        pltpu.VMEM((2 * tq, LANES), F32),
                        pltpu.VMEM((nrow, LANES), F32), pltpu.VMEM((nrow, LANES), F32), pltpu.VMEM((nrow, da), F32),
                        pltpu.VMEM((heads, LANES), F32),
                        pltpu.VMEM((FUSED_RING, pp, da, page), F32), pltpu.VMEM((FUSED_RING, pp, da, page), F32),
                        pltpu.VMEM((FUSED_RING, pp, heads, page), F32),
                        pltpu.SemaphoreType.DMA((FUSED_RING,)), pltpu.SMEM((1,), jnp.int32)])
    return pl.pallas_call(
        kern, grid_spec=grid_spec,
        out_shape=[jax.ShapeDtypeStruct((t, da), BF16), jax.ShapeDtypeStruct((nbs, s_new, da), F32)],
        compiler_params=_cparams(("arbitrary", "arbitrary", "arbitrary")),
        name="fox_fused")(page_table_flat, qa, ka, v, kt_pages, vt_pages, lf_pages, qbd, kn, vn, dtn)


def _post_attn_math(x_ref, gc_ref, at_ref, g_ref, wuc_ref, wua_ref, wmo_ref, gain_ref, wxq_ref):
    dm = x_ref.shape[1]
    y_conv = _dot(gc_ref[...], wuc_ref[...])
    y_attn = _dot(at_ref[...].astype(BF16), wua_ref[...])
    g_conv = g_ref[:, 0:dm].astype(F32)
    g_attn = g_ref[:, dm:2 * dm].astype(F32)
    mixed = jax.nn.sigmoid(g_conv) * y_conv + jax.nn.sigmoid(g_attn) * y_attn
    h = x_ref[...] + _dot(mixed.astype(BF16), wmo_ref[...])
    xn = _rms(h, gain_ref[...]).astype(BF16)
    return h, _dot(xn, wxq_ref[...]).astype(BF16)


def _post_attn_kernel(x_ref, gc_ref, at_ref, g_ref, wuc_ref, wua_ref, wmo_ref, gain_ref, wxq_ref, h_ref, qx_ref):
    h, qx = _post_attn_math(x_ref, gc_ref, at_ref, g_ref, wuc_ref, wua_ref, wmo_ref, gain_ref, wxq_ref)
    h_ref[...] = h
    qx_ref[...] = qx.astype(qx_ref.dtype)


def _post_attn(x, gc, at, g, wuc, wua, wmo, gain, wxq, *, tm, name):
    t, dm = x.shape
    dx = wxq.shape[1]
    rows = lambda w: pl.BlockSpec((tm, w), lambda i: (i, 0))
    return pl.pallas_call(
        _post_attn_kernel, grid=(t // tm,),
        in_specs=[rows(dm), rows(gc.shape[1]), rows(at.shape[1]), rows(g.shape[1]), _const_spec(wuc.shape),
                  _const_spec(wua.shape), _const_spec(wmo.shape), _const_spec((1, dm)), _const_spec(wxq.shape)],
        out_specs=[rows(dm), rows(dx)],
        out_shape=[jax.ShapeDtypeStruct((t, dm), F32), jax.ShapeDtypeStruct((t, dx), BF16)],
        compiler_params=_cparams(("parallel",)), name=name)(x, gc, at, g, wuc, wua, wmo, gain, wxq)


def _memkv_kernel(m_ref, gain_ref, wk_ref, wv_ref, k_ref, v_ref, *, xh):
    mn = _rms(m_ref[...], gain_ref[...]).astype(BF16)
    tm = m_ref.shape[0]
    xd = wk_ref.shape[1] // xh
    k = _dot(mn, wk_ref[...])
    v = _dot(mn, wv_ref[...])
    for h in range(xh):
        k_ref[pl.ds(h, tm, stride=xh), :] = k[:, h * xd:(h + 1) * xd]
        v_ref[pl.ds(h, tm, stride=xh), :] = v[:, h * xd:(h + 1) * xd]


def _memkv(mem, gain, wk, wv, *, xh):
    t, dm = mem.shape
    tm = min(512, t)
    xd = wk.shape[1] // xh
    rows = lambda w: pl.BlockSpec((tm, w), lambda i: (i, 0))
    return pl.pallas_call(
        functools.partial(_memkv_kernel, xh=xh), grid=(t // tm,),
        in_specs=[rows(dm), _const_spec((1, dm)), _const_spec(wk.shape), _const_spec(wv.shape)],
        out_specs=[pl.BlockSpec((tm * xh, xd), lambda i: (i, 0))] * 2,
        out_shape=[jax.ShapeDtypeStruct((t * xh, xd), F32)] * 2,
        compiler_params=_cparams(("parallel",)), name="memkv")(mem, gain, wk, wv)


def _xattn_math(q, mk_ref, mv_ref, xh):
    nm = mk_ref.shape[0] // xh
    xd = mk_ref.shape[1]
    scale = xd ** -0.5
    outs = []
    for h in range(xh):
        mk = mk_ref[pl.ds(h, nm, stride=xh), :].astype(BF16)
        mv = mv_ref[pl.ds(h, nm, stride=xh), :].astype(BF16)
        s = _dot_nt(q[:, h * xd:(h + 1) * xd], mk) * scale
        m = jnp.max(s, axis=1, keepdims=True)
        p = jnp.exp(s - m)
        p = p / jnp.sum(p, axis=1, keepdims=True)
        outs.append(_dot(p.astype(BF16), mv))
    return jnp.concatenate(outs, axis=1)


def _xattn_kernel(q_ref, mk_ref, mv_ref, o_ref, *, xh):
    tq, dx = q_ref.shape[1:]
    xd = dx // xh
    nmh = mk_ref.shape[1]
    scale = xd ** -0.5
    rowh = lax.broadcasted_iota(jnp.int32, (xh * tq, nmh), 0) // tq
    colh = lax.broadcasted_iota(jnp.int32, (xh * tq, nmh), 1) % xh
    same_head = rowh == colh
    for b in range(q_ref.shape[0]):
        q = q_ref[b].astype(BF16)
        qs = jnp.concatenate([q[:, h * xd:(h + 1) * xd] for h in range(xh)], axis=0)
        s = jnp.where(same_head, _dot_nt(qs, mk_ref[b].astype(BF16)) * scale, NEG_BIG)
        m = jnp.max(s, axis=1, keepdims=True)
        p = jnp.exp(s - m)
        p = p / jnp.sum(p, axis=1, keepdims=True)
        o = _dot(p.astype(BF16), mv_ref[b].astype(BF16))
        o_ref[b] = jnp.concatenate([o[h * tq:(h + 1) * tq] for h in range(xh)], axis=1).astype(o_ref.dtype)


def _xattn(q, mk, mv, *, tq, xh, nbb, name):
    nb, s, dx = q.shape
    nm, xd = mk.shape[1:]
    kern = functools.partial(_xattn_kernel, xh=xh)
    return pl.pallas_call(
        kern, grid=(nb // nbb, s // tq),
        in_specs=[pl.BlockSpec((nbb, tq, dx), lambda b, i: (b, i, 0)),
                  pl.BlockSpec((nbb, nm, xd), lambda b, i: (b, 0, 0)),
                  pl.BlockSpec((nbb, nm, xd), lambda b, i: (b, 0, 0))],
        out_specs=pl.BlockSpec((nbb, tq, dx), lambda b, i: (b, i, 0)),
        out_shape=jax.ShapeDtypeStruct((nb, s, dx), q.dtype),
        compiler_params=_cparams(("parallel", "parallel")), name=name)(q, mk, mv)


def _pre_moe_kernel(h_ref, o_ref, wxo_ref, gain_ref, wrh_ref, wrl_ref, cnt_in_ref,
                    h2_ref, xn_ref, rt_ref, rr_ref, cnt_out_ref, base, *, tm, ngroups, epg):
    @pl.when(pl.program_id(0) == 0)
    def _():
        base[...] = cnt_in_ref[...]

    _pre_moe_math(h_ref[...], o_ref[...].astype(BF16), wxo_ref, gain_ref, wrh_ref, wrl_ref,
                  h2_ref, xn_ref, rt_ref, rr_ref, cnt_out_ref, base, tm=tm, ngroups=ngroups, epg=epg)


def _mid_kernel(x_ref, gc_ref, at_ref, g_ref, mk_ref, mv_ref, wuc_ref, wua_ref, wmo_ref, gx_ref, wxq_ref,
                wxo_ref, gf_ref, wrh_ref, wrl_ref, cnt_in_ref,
                h2_ref, xn_ref, rt_ref, rr_ref, cnt_out_ref, base, *, tm, ngroups, epg, xh):
    @pl.when(pl.program_id(0) == 0)
    def _():
        base[...] = cnt_in_ref[...]

    h1, qx = _post_attn_math(x_ref, gc_ref, at_ref, g_ref, wuc_ref, wua_ref, wmo_ref, gx_ref, wxq_ref)
    o = _xattn_math(qx, mk_ref, mv_ref, xh).astype(BF16)
    _pre_moe_math(h1, o, wxo_ref, gf_ref, wrh_ref, wrl_ref,
                  h2_ref, xn_ref, rt_ref, rr_ref, cnt_out_ref, base, tm=tm, ngroups=ngroups, epg=epg)


def _pre_moe_math(h, o, wxo_ref, gain_ref, wrh_ref, wrl_ref, h2_ref, xn_ref, rt_ref, rr_ref, cnt_out_ref, base,
                  *, tm, ngroups, epg):
    h2 = h + _dot(o, wxo_ref[...])
    h2_ref[...] = h2
    xn = _rms(h2, gain_ref[...])
    xw = _pack_bf16_pairs(xn)
    nc = xw.shape[1] // LANES
    for c in range(nc):
        xn_ref[pl.ds(c, tm, stride=nc), :] = xw[:, c * LANES:(c + 1) * LANES]

    xh = xn.astype(BF16)
    xl = (xn - xh.astype(F32)).astype(BF16)
    lt = _dot_nt(wrh_ref[...], xh) + (_dot_nt(wrh_ref[...], xl) + _dot_nt(wrl_ref[...], xh))

    sub = lax.broadcasted_iota(jnp.int32, (SUBLANES, tm), 0)
    gl = jnp.where(sub < ngroups, lt[0:SUBLANES, :], -jnp.inf)
    gmax = jnp.max(gl, axis=0, keepdims=True)
    gidx = jnp.min(jnp.where(gl == gmax, sub, SUBLANES), axis=0, keepdims=True)
    pg = 1.0 / jnp.sum(jnp.exp(gl - gmax), axis=0, keepdims=True)
    el = jnp.zeros((epg, tm), F32)
    for g in range(ngroups):
        el = jnp.where(gidx == g, lt[EXPERT_ROW0 + g * epg:EXPERT_ROW0 + (g + 1) * epg, :], el)
    v1 = jnp.max(el, axis=0, keepdims=True)
    i1 = jnp.min(jnp.where(el == v1, sub, epg), axis=0, keepdims=True)
    el2 = jnp.where(sub == i1, -jnp.inf, el)
    v2 = jnp.max(el2, axis=0, keepdims=True)
    i2 = jnp.min(jnp.where(el2 == v2, sub, epg), axis=0, keepdims=True)
    t2 = jnp.exp(v2 - v1)
    den = 1.0 + t2
    w0 = (1.0 / den) * pg
    w1 = (t2 / den) * pg
    e0 = gidx * epg + i1
    e1 = gidx * epg + i2

    erow = lax.broadcasted_iota(jnp.int32, (LANES, tm), 0)
    oh0 = erow == e0
    oh1 = erow == e1
    rr = lax.broadcasted_iota(jnp.int32, (tm, tm), 0)
    cc = lax.broadcasted_iota(jnp.int32, (tm, tm), 1)
    triu = jnp.where(rr <= cc, 1.0, 0.0).astype(BF16)
    pre0 = _dot(jnp.where(oh0, 1.0, 0.0).astype(BF16), triu)
    pre1 = _dot(jnp.where(oh1, 1.0, 0.0).astype(BF16), triu)
    b0 = base[:, 0:1]
    tot0 = pre0[:, tm - 1:tm]
    tot1 = pre1[:, tm - 1:tm]
    rank0 = jnp.sum(jnp.where(oh0, pre0 - 1.0 + b0, 0.0), axis=0, keepdims=True)
    rank1 = jnp.sum(jnp.where(oh1, pre1 - 1.0 + (b0 + tot0), 0.0), axis=0, keepdims=True)
    newb = b0 + tot0 + tot1
    base[...] = jnp.broadcast_to(newb, base.shape)
    cnt_out_ref[...] = jnp.broadcast_to(newb, cnt_out_ref.shape)

    zero = jnp.zeros((1, tm), F32)
    rt = jnp.concatenate([e0.astype(F32), e1.astype(F32), w0, w1, rank0, rank1, zero, zero], axis=0)
    rt_ref[...] = rt
    rt_pad = jnp.concatenate([rt, jnp.zeros((LANES - ROUTE_ROWS, tm), F32)], axis=0)
    rr_ref[...] = jnp.transpose(rt_pad)


def _pre_moe(h, o, wxo, gain, wrh, wrl, cnt_in, *, tm, ngroups, epg, name):
    t, dm = h.shape
    dx = o.shape[1]
    kern = functools.partial(_pre_moe_kernel, tm=tm, ngroups=ngroups, epg=epg)
    rows = lambda w: pl.BlockSpec((tm, w), lambda i: (i, 0))
    return pl.pallas_call(
        kern, grid=(t // tm,),
        in_specs=[rows(dm), rows(dx), _const_spec(wxo.shape), _const_spec((1, dm)), _const_spec(wrh.shape),
                  _const_spec(wrl.shape), _const_spec((LANES, LANES))],
        out_specs=[rows(dm), pl.BlockSpec((tm * (dm // (2 * LANES)), LANES), lambda i: (i, 0)),
                   pl.BlockSpec((ROUTE_ROWS, tm), lambda i: (0, i)), rows(LANES), _const_spec((LANES, LANES))],
        out_shape=[jax.ShapeDtypeStruct((t, dm), F32), jax.ShapeDtypeStruct((t * (dm // (2 * LANES)), LANES), jnp.uint32),
                   jax.ShapeDtypeStruct((ROUTE_ROWS, t), F32), jax.ShapeDtypeStruct((t, LANES), F32),
                   jax.ShapeDtypeStruct((LANES, LANES), F32)],
        scratch_shapes=[pltpu.VMEM((LANES, LANES), F32)],
        compiler_params=_cparams(("arbitrary",)), name=name)(h, o, wxo, gain, wrh, wrl, cnt_in)


def _mid(x, gc, at, g, mk, mv, wuc, wua, wmo, gx, wxq, wxo, gf, wrh, wrl, cnt_in, *, tm, seq_len, ngroups, epg, xh):
    t, dm = x.shape
    nm, xd = mk.shape[1:]
    seq_blocks = seq_len // tm
    kern = functools.partial(_mid_kernel, tm=tm, ngroups=ngroups, epg=epg, xh=xh)
    rows = lambda w: pl.BlockSpec((tm, w), lambda i: (i, 0))
    mem = pl.BlockSpec((None, nm, xd), lambda i: (i // seq_blocks, 0, 0))
    consts = [wuc, wua, wmo, gx, wxq, wxo, gf, wrh, wrl, cnt_in]
    return pl.pallas_call(
        kern, grid=(t // tm,),
        in_specs=[rows(dm), rows(gc.shape[1]), rows(at.shape[1]), rows(g.shape[1]), mem, mem]
        + [_const_spec(a.shape) for a in consts],
        out_specs=[rows(dm), pl.BlockSpec((tm * (dm // (2 * LANES)), LANES), lambda i: (i, 0)),
                   pl.BlockSpec((ROUTE_ROWS, tm), lambda i: (0, i)), rows(LANES), _const_spec((LANES, LANES))],
        out_shape=[jax.ShapeDtypeStruct((t, dm), F32), jax.ShapeDtypeStruct((t * (dm // (2 * LANES)), LANES), jnp.uint32),
                   jax.ShapeDtypeStruct((ROUTE_ROWS, t), F32), jax.ShapeDtypeStruct((t, LANES), F32),
                   jax.ShapeDtypeStruct((LANES, LANES), F32)],
        scratch_shapes=[pltpu.VMEM((LANES, LANES), F32)],
        compiler_params=_cparams(("arbitrary",)), name="mid_prompt")(x, gc, at, g, mk, mv, *consts)


def _row_copy(src, r_src, dst, r_dst, sem, nc):
    return pltpu.make_async_copy(src.at[pl.ds(pl.multiple_of(r_src * nc, nc), nc)],
                                 dst.at[pl.ds(pl.multiple_of(r_dst * nc, nc), nc)], sem)


def _slab_rows(ref, nc):
    rows = ref.shape[0] // nc
    return jnp.concatenate([ref[pl.ds(c, rows, stride=nc), :] for c in range(nc)], axis=1)


def _pack_bf16_pairs(x):
    half = x.shape[1] // 2
    lo = pltpu.bitcast(x[:, 0:half].astype(BF16).astype(F32), jnp.uint32)
    hi = pltpu.bitcast(x[:, half:2 * half].astype(BF16).astype(F32), jnp.uint32)
    return lax.shift_right_logical(lo, jnp.uint32(16)) | (hi & jnp.uint32(0xFFFF0000))


def _unpack_bf16_pairs(w):
    lo = pltpu.bitcast(lax.shift_left(w, jnp.uint32(16)), F32)
    hi = pltpu.bitcast(w & jnp.uint32(0xFFFF0000), F32)
    return jnp.concatenate([lo, hi], axis=1)


def _dest_kernel(rt_ref, pstart_ref, o_ref):
    rt = rt_ref[...]
    pst = pstart_ref[:, 0:1]
    erow = lax.broadcasted_iota(jnp.int32, (LANES, rt.shape[1]), 0)
    rows = []
    for k in range(TOP_K):
        e = rt[k:k + 1, :].astype(jnp.int32)
        rows.append(jnp.sum(jnp.where(erow == e, pst, 0.0), axis=0, keepdims=True) + rt[4 + k:5 + k, :])
    rows.append(jnp.zeros((ROUTE_ROWS - TOP_K, rt.shape[1]), F32))
    o_ref[...] = jnp.concatenate(rows, axis=0).astype(jnp.int32)


def _dest(rt_all, pstart_col):
    tall = rt_all.shape[1]
    nchunk = 3 if tall % (3 * LANES) == 0 else 1
    w = tall // nchunk
    return pl.pallas_call(
        _dest_kernel, grid=(nchunk,),
        in_specs=[pl.BlockSpec((ROUTE_ROWS, w), lambda i: (0, i)), _const_spec((LANES, LANES))],
        out_specs=pl.BlockSpec((ROUTE_ROWS, w), lambda i: (0, i)),
        out_shape=jax.ShapeDtypeStruct((ROUTE_ROWS, tall), jnp.int32),
        compiler_params=_cparams(("parallel",)), name="dest")(rt_all, pstart_col)


def _scatter_kernel(dest_ref, pstart_ref, pend_ref, xp_ref, xs_ref, out_ref, zeros, sem, zsem,
                    *, tm_p, nblk_p, nexp, nc):
    i = pl.program_id(0)
    tall = dest_ref.shape[0] // TOP_K

    @pl.when(i == 0)
    def _():
        zeros[...] = jnp.zeros_like(zeros)

        def zero_block(blk):
            start = pl.multiple_of(blk * (EXPERT_ROWS * nc), EXPERT_ROWS * nc)
            return pltpu.make_async_copy(zeros, out_ref.at[pl.ds(start, EXPERT_ROWS * nc)], zsem)

        for e in range(nexp):
            @pl.when(pend_ref[e] > pstart_ref[e])
            def _():
                zero_block(pend_ref[e] // EXPERT_ROWS - 1).start()
        for e in range(nexp):
            @pl.when(pend_ref[e] > pstart_ref[e])
            def _():
                zero_block(pend_ref[e] // EXPERT_ROWS - 1).wait()
        nblk = out_ref.shape[0] // (EXPERT_ROWS * nc)
        nused = pend_ref[nexp - 1] // EXPERT_ROWS

        def start_unused(blk, c):
            zero_block(blk).start()
            return c

        def wait_unused(blk, c):
            zero_block(blk).wait()
            return c

        lax.fori_loop(nused, nblk, start_unused, 0)
        lax.fori_loop(nused, nblk, wait_unused, 0)

    ts = xs_ref.shape[0] // nc

    def copy_rows(x_hbm, row0, tm, base):
        def issue(r, c):
            for k in range(TOP_K):
                _row_copy(x_hbm, row0 + r, out_ref, dest_ref[k * tall + base + r], sem, nc).start(priority=k % 2)
            return c

        lax.fori_loop(0, tm, issue, 0, unroll=8)

    def wait_rows(tm):
        for k in range(TOP_K):
            pltpu.make_async_copy(out_ref.at[pl.ds(0, tm * nc)], out_ref.at[pl.ds(0, tm * nc)], sem).wait()

    @pl.when(i < nblk_p)
    def _():
        copy_rows(xp_ref, i * tm_p, tm_p, i * tm_p)

    @pl.when(i == nblk_p)
    def _():
        copy_rows(xs_ref, 0, ts, nblk_p * tm_p)

    @pl.when(i > 0)
    def _():
        wait_rows(tm_p)

    @pl.when(i == nblk_p)
    def _():
        wait_rows(ts)


def _scatter(dest_flat, pstart, pend, x_p, x_s, *, n_rows, tm_p, nexp, nc):
    tp = x_p.shape[0] // nc
    ts = x_s.shape[0] // nc
    nblk_p = tp // tm_p
    kern = functools.partial(_scatter_kernel, tm_p=tm_p, nblk_p=nblk_p, nexp=nexp, nc=nc)
    grid_spec = pltpu.PrefetchScalarGridSpec(
        num_scalar_prefetch=3, grid=(nblk_p + 1,),
        in_specs=[pl.BlockSpec(memory_space=pl.ANY), pl.BlockSpec(memory_space=pl.ANY)],
        out_specs=pl.BlockSpec(memory_space=pl.ANY),
        scratch_shapes=[pltpu.VMEM((EXPERT_ROWS * nc, LANES), jnp.uint32), pltpu.SemaphoreType.DMA(()),
                        pltpu.SemaphoreType.DMA(())])
    return pl.pallas_call(
        kern, grid_spec=grid_spec, out_shape=jax.ShapeDtypeStruct((n_rows * nc, LANES), jnp.uint32),
        compiler_params=_cparams(("arbitrary",)), name="scatter")(dest_flat, pstart, pend, x_p, x_s)


def _experts_kernel(blk_e_ref, nused_ref, xs_ref, wg_ref, wu_ref, wd_ref, y_ref, wgb, wub, wdb, *, nc):
    i = pl.program_id(0)
    prev = blk_e_ref[jnp.maximum(i - 1, 0)]
    fresh = (i == 0) | (blk_e_ref[i] != prev)

    @pl.when(i < nused_ref[0])
    def _():
        @pl.when(fresh)
        def _():
            wgb[...] = wg_ref[...].astype(BF16)
            wub[...] = wu_ref[...].astype(BF16)
            wdb[...] = wd_ref[...].astype(BF16)
        x = _unpack_bf16_pairs(_slab_rows(xs_ref, nc)).astype(BF16)
        a = _dot(x, wgb[...])
        u = _dot(x, wub[...])
        hmid = (a * jax.nn.sigmoid(a)) * u
        y = _pack_bf16_pairs(_dot(hmid.astype(BF16), wdb[...]))
        for c in range(nc):
            y_ref[pl.ds(c, EXPERT_ROWS, stride=nc), :] = y[:, c * LANES:(c + 1) * LANES]

    @pl.when(i >= nused_ref[0])
    def _():
        y_ref[...] = jnp.zeros_like(y_ref)


def _experts(blk_e, nused, xs, wg, wu, wd):
    dm, de = wg.shape[1:]
    nc = dm // (2 * LANES)
    nblk = xs.shape[0] // (EXPERT_ROWS * nc)

    def row_map(i, be, nu):
        return (jnp.minimum(i, nu[0] - 1), 0)

    grid_spec = pltpu.PrefetchScalarGridSpec(
        num_scalar_prefetch=2, grid=(nblk,),
        in_specs=[pl.BlockSpec((EXPERT_ROWS * nc, LANES), row_map),
                  pl.BlockSpec((None, dm, de), lambda i, be, nu: (be[i], 0, 0)),
                  pl.BlockSpec((None, dm, de), lambda i, be, nu: (be[i], 0, 0)),
                  pl.BlockSpec((None, de, dm), lambda i, be, nu: (be[i], 0, 0))],
        out_specs=pl.BlockSpec((EXPERT_ROWS * nc, LANES), lambda i, be, nu: (i, 0)),
        scratch_shapes=[pltpu.VMEM((dm, de), BF16), pltpu.VMEM((dm, de), BF16), pltpu.VMEM((de, dm), BF16)])
    return pl.pallas_call(
        functools.partial(_experts_kernel, nc=nc), grid_spec=grid_spec,
        out_shape=jax.ShapeDtypeStruct(xs.shape, jnp.uint32),
        compiler_params=_cparams(("arbitrary",)), name="experts")(blk_e, nused, xs, wg, wu, wd)


def _combine_kernel(dest_ref, h_ref, rr_ref, gain_ref, y_hbm, o_ref, buf, sem, *, tm, tok0):
    i = pl.program_id(0)
    nsteps = pl.num_programs(0)
    nc = h_ref.shape[1] // (2 * LANES)
    tall = dest_ref.shape[0] // TOP_K

    def request(blk):
        slot = blk % 2
        base = tok0 + blk * tm

        def issue(r, c):
            for k in range(TOP_K):
                _row_copy(y_hbm, dest_ref[k * tall + base + r], buf.at[slot, k], r, sem.at[slot],
                          nc).start(priority=k % 2)
            return c

        lax.fori_loop(0, tm, issue, 0, unroll=8)

    @pl.when(i == 0)
    def _():
        request(0)

    @pl.when(i + 1 < nsteps)
    def _():
        request(i + 1)

    slot = i % 2
    for k in range(TOP_K):
        pltpu.make_async_copy(buf.at[slot, k], buf.at[slot, k], sem.at[slot]).wait()
    rr = rr_ref[...]
    y0 = _unpack_bf16_pairs(_slab_rows(buf.at[slot, 0], nc))
    y1 = _unpack_bf16_pairs(_slab_rows(buf.at[slot, 1], nc))
    h = h_ref[...] + (rr[:, 2:3] * y0 + rr[:, 3:4] * y1)
    o_ref[...] = _rms(h, gain_ref[...])


def _combine(dest_flat, h, rr, gain, y, *, tm, tok0, name):
    t, dm = h.shape
    kern = functools.partial(_combine_kernel, tm=tm, tok0=tok0)
    grid_spec = pltpu.PrefetchScalarGridSpec(
        num_scalar_prefetch=1, grid=(t // tm,),
        in_specs=[pl.BlockSpec((tm, dm), lambda i, *_: (i, 0)), pl.BlockSpec((tm, LANES), lambda i, *_: (i, 0)),
                  pl.BlockSpec((1, dm), lambda i, *_: (0, 0)), pl.BlockSpec(memory_space=pl.ANY)],
        out_specs=pl.BlockSpec((tm, dm), lambda i, *_: (i, 0)),
        scratch_shapes=[pltpu.VMEM((2, TOP_K, tm * (dm // (2 * LANES)), LANES), jnp.uint32),
                        pltpu.SemaphoreType.DMA((2,))])
    return pl.pallas_call(
        kern, grid_spec=grid_spec, out_shape=jax.ShapeDtypeStruct((t, dm), F32),
        compiler_params=_cparams(("arbitrary",)), name=name)(dest_flat, h, rr, gain, y)


def kernel(x_prompt, x_sample, cache_k, cache_v, cache_logf, cache_mem_k, cache_mem_v, state_conv, page_table,
           mem_prompt, norm_mix, w_in, b_forget, conv_w, conv_b, w_up_conv, w_up_attn, w_mix_out, norm_xattn,
           norm_mem, w_xq, w_xk, w_xv, w_xo, norm_ffn, w_router_group, w_router_expert, w_expert_gate,
           w_expert_up, w_expert_down, norm_final):
    depth = w_in.shape[0]
    assert depth == 1, "single-layer trunk"
    nbp, seq, dm = x_prompt.shape
    nbs, s_new, _ = x_sample.shape
    _, n_pool, page, heads, dh = cache_k.shape
    npages = page_table.shape[1]
    nmem, xh, xd = cache_mem_k.shape[2:]
    dc = conv_w.shape[2]
    da = heads * dh
    dx = xh * xd
    ngroups, _, epg = w_router_expert.shape[1:]
    nexp = ngroups * epg
    tp = nbp * seq
    ts = nbs * s_new
    assert conv_w.shape[1] == 3 and s_new >= 2 and dh == 64 and heads == SUBLANES and epg == SUBLANES
    assert page == LANES

    l = 0
    wi = w_in[l]
    wc = wi[:, 0:3 * dc].astype(BF16)
    wqkv = wi[:, 3 * dc:3 * dc + 3 * da].astype(BF16)
    o_f = 3 * dc + 3 * da
    wf = jnp.pad(wi[:, o_f:o_f + heads], ((0, 0), (0, LANES - heads)))
    wfh = wf.astype(BF16)
    wf2 = jnp.concatenate([wfh, (wf - wfh.astype(F32)).astype(BF16)], axis=1)
    wg = wi[:, o_f + heads:].astype(BF16)
    bfp = jnp.pad(b_forget[l][None, :], ((0, 0), (0, LANES - heads)))
    cw = conv_w[l]
    cbias = conv_b[l][None, :]
    g_mix = norm_mix[l][None, :]
    wuc = w_up_conv[l].astype(BF16)
    wua = w_up_attn[l].astype(BF16)
    wmo = w_mix_out[l].astype(BF16)
    g_x = norm_xattn[l][None, :]
    wxq = w_xq[l].astype(BF16)
    wxo = w_xo[l].astype(BF16)
    g_f = norm_ffn[l][None, :]
    wr = jnp.zeros((LANES, dm), F32)
    wr = wr.at[0:ngroups].set(w_router_group[l].T)
    wr = wr.at[EXPERT_ROW0:EXPERT_ROW0 + nexp].set(jnp.transpose(w_router_expert[l], (0, 2, 1)).reshape(nexp, dm))
    wrh = wr.astype(BF16)
    wrl = (wr - wrh.astype(F32)).astype(BF16)

    xp = x_prompt.reshape(tp, dm)
    (gc_p, qa_p, ka_p, vb_p, kt_p, vt_p, g_p, lf_p, ulast_p) = _inproj(
        xp, g_mix, wc, wqkv, wg, wf2, bfp, cw, cbias, seq_len=seq, sample=False, aug=_decay_columns(heads))
    xs_ = x_sample.reshape(ts, dm)
    st = state_conv[l]
    zeros_row = jnp.zeros((nbs, 1, dc), F32)
    fix1 = jnp.concatenate([st[:, 1:2], jnp.tile(zeros_row, (1, s_new - 1, 1))], axis=1).reshape(ts, dc)
    fix2 = jnp.concatenate([st[:, 0:1], st[:, 1:2], jnp.tile(zeros_row, (1, s_new - 2, 1))], axis=1).reshape(ts, dc)
    (gc_s, q_s, kb_s, vb_s, kf_s, vf_s, g_s, lf_s, dt_s, u_s) = _inproj(
        xs_, g_mix, wc, wqkv, wg, wf2, bfp, cw, cbias, seq_len=s_new, sample=True, fix=(fix1, fix2))
    pt_flat = page_table.reshape(-1).astype(jnp.int32)
    head_of_col = jnp.arange(da) // dh
    qbd = jnp.where(head_of_col[None, None, None, :] == jnp.arange(heads)[None, None, :, None],
                    q_s.reshape(nbs, s_new, 1, da), jnp.zeros((), BF16)).reshape(nbs, s_new * heads, da)
    kt_pages = jnp.transpose(cache_k[l], (0, 2, 3, 1)).reshape(n_pool, da, page)
    vt_pages = jnp.transpose(cache_v[l], (0, 2, 3, 1)).reshape(n_pool, da, page)
    lf_pages = jnp.swapaxes(cache_logf[l], 1, 2)
    at_p, at_s = _fox_fused(pt_flat, qa_p, ka_p, vb_p, kt_pages, vt_pages, lf_pages, qbd, kb_s, vb_s, dt_s,
                            nseq=nbp, seq_len=seq, dh=dh, npages=npages, s_new=s_new, heads=heads)

    tm_p = min(512, seq)
    mk_p, mv_p = _memkv(mem_prompt.reshape(nbp * nmem, dm), norm_mem[l][None, :], w_xk[l].astype(BF16),
                        w_xv[l].astype(BF16), xh=xh)
    cnt0 = jnp.zeros((LANES, LANES), F32)
    h2_p, xn_p, rt_p, rr_p, cnt1 = _mid(
        xp, gc_p, at_p, g_p, mk_p.reshape(nbp, nmem * xh, xd), mv_p.reshape(nbp, nmem * xh, xd),
        wuc, wua, wmo, g_x, wxq, wxo, g_f, wrh, wrl, cnt0, tm=tm_p, seq_len=seq, ngroups=ngroups, epg=epg, xh=xh)

    h1_s, qx_s = _post_attn(xs_, gc_s, at_s.reshape(ts, da), g_s, wuc, wua, wmo, g_x, wxq, tm=ts,
                            name="post_attn_sample")
    qx_s8 = jnp.pad(qx_s.astype(F32).reshape(nbs, s_new, dx), ((0, 0), (0, SUBLANES - s_new), (0, 0)))
    o_s = _xattn(qx_s8, cache_mem_k[l].reshape(nbs, nmem * xh, xd), cache_mem_v[l].reshape(nbs, nmem * xh, xd),
                 tq=SUBLANES, xh=xh, nbb=4 if nbs % 4 == 0 else 1, name="xattn_sample")[:, :s_new].reshape(ts, dx)
    h2_s, xn_s, rt_s, rr_s, cnt2 = _pre_moe(h1_s, o_s, wxo, g_f, wrh, wrl, cnt1, tm=ts, ngroups=ngroups, epg=epg,
                                            name="pre_moe_sample")

    tall = tp + ts
    counts = cnt2[0:nexp, 0].astype(jnp.int32)
    padded = (counts + EXPERT_ROWS - 1) // EXPERT_ROWS * EXPERT_ROWS
    pend = jnp.cumsum(padded).astype(jnp.int32)
    pstart = pend - padded
    nblk = (tall * TOP_K + nexp * (EXPERT_ROWS - 1)) // EXPERT_ROWS
    n_rows = nblk * EXPERT_ROWS
    blk_row0 = jnp.arange(nblk, dtype=jnp.int32) * EXPERT_ROWS
    blk_e = jnp.minimum(jnp.sum((pend[None, :] <= blk_row0[:, None]).astype(jnp.int32), axis=1), nexp - 1)
    nused = (pend[nexp - 1:nexp] // EXPERT_ROWS).astype(jnp.int32)
    rt_all = jnp.concatenate([rt_p, rt_s], axis=1)
    pstart_col = jnp.zeros((LANES, LANES), F32).at[0:nexp, :].set(pstart.astype(F32)[:, None])
    dest_flat = _dest(rt_all, pstart_col)[0:TOP_K].reshape(-1)
    xsg = _scatter(dest_flat, pstart, pend, xn_p, xn_s, n_rows=n_rows, tm_p=tm_p, nexp=nexp, nc=dm // (2 * LANES))
    y = _experts(blk_e, nused, xsg, w_expert_gate[l], w_expert_up[l], w_expert_down[l])
    g_fin = norm_final[None, :]
    y_p = _combine(dest_flat, h2_p, rr_p, g_fin, y, tm=tm_p, tok0=0, name="combine_prompt")
    y_s = _combine(dest_flat, h2_s, rr_s, g_fin, y, tm=ts, tok0=tp, name="combine_sample")

    return (y_p.reshape(nbp, seq, dm), y_s.reshape(nbs, s_new, dm),
            ulast_p[None],
            jnp.transpose(kt_p.reshape(nbp, heads, dh, seq), (0, 3, 1, 2))[None],
            jnp.transpose(vt_p.reshape(nbp, heads, dh, seq), (0, 3, 1, 2))[None],
            jnp.transpose(lf_p, (0, 2, 1))[None],
            mk_p.reshape(1, nbp, nmem, xh, xd), mv_p.reshape(1, nbp, nmem, xh, xd),
            u_s.reshape(nbs, s_new, dc)[None, :, s_new - 2:], kf_s.reshape(1, nbs, s_new, heads, dh),
            vf_s.reshape(1, nbs, s_new, heads, dh), lf_s[:, :heads].reshape(1, nbs, s_new, heads))
```

```python
import functools

import numpy as np
import jax
import jax.numpy as jnp
from jax import lax
from jax.experimental import pallas as pl
from jax.experimental.pallas import tpu as pltpu

F32 = jnp.float32
BF16 = jnp.bfloat16

RMS_EPS = 1e-6
TOP_K = 2
LANES = 128
SUBLANES = 8
VMEM_LIMIT = 56 * 1024 * 1024
NEG_BIG = -1e30
EXPERT_ROWS = 512
FUSED_PAGES_PER_ITERATION = 8
FUSED_RING = 4
FOX_QUERY_BLOCK = 512
FOX_KEY_BLOCK = 512
ROUTE_ROWS = 8
EXPERT_ROW0 = 8


def _cparams(sem, vmem=VMEM_LIMIT):
    return pltpu.CompilerParams(dimension_semantics=sem, vmem_limit_bytes=vmem)


def _rms(x, g):
    ms = jnp.mean(x * x, axis=-1, keepdims=True)
    return x * lax.rsqrt(ms + RMS_EPS) * g


def _split3(x):
    hi = x.astype(BF16)
    r = x - hi.astype(F32)
    mid = r.astype(BF16)
    lo = (r - mid.astype(F32)).astype(BF16)
    return hi, mid, lo


def _dot(a, b):
    return jnp.dot(a, b, preferred_element_type=F32)


def _dot_nt(a, b):
    return lax.dot_general(a, b, (((1,), (1,)), ((), ())), preferred_element_type=F32)


def _lane_tile(x, width):
    if width % LANES == 0:
        return jnp.concatenate([x] * (width // LANES), axis=1)
    return jnp.broadcast_to(x[:, 0:1], (x.shape[0], width))


def _const_spec(shape):
    nd = len(shape)
    return pl.BlockSpec(shape, lambda *_: (0,) * nd)


def _inproj_kernel(*refs, tm, seq_blocks, sample, seq_len):
    if sample:
        (x_ref, gain_ref, wc_ref, wqkv_ref, wg_ref, wf2_ref, bf_ref, cw_ref, cb_ref, fix1_ref, fix2_ref,
         gc_ref, q_ref, kbf_ref, vbf_ref, kf_ref, vf_ref, g_ref, logf_ref, dt_ref, u_ref) = refs
    else:
        (x_ref, gain_ref, wc_ref, wqkv_ref, wg_ref, wf2_ref, bf_ref, cw_ref, cb_ref,
         selq_ref, selk_ref, oneq_ref, onek_ref,
         gc_ref, qa_ref, ka_ref, vbf_ref, kf_ref, vf_ref, g_ref, logf_ref, u_ref,
         carry_u, carry_d) = refs
    i = pl.program_id(0)
    dc = cw_ref.shape[1]
    da = wqkv_ref.shape[1] // 3

    xn = _rms(x_ref[...], gain_ref[...])
    xb = xn.astype(BF16)

    cb = _dot(xb, wc_ref[:, 0:dc])
    cc = _dot(xb, wc_ref[:, dc:2 * dc])
    cx = _dot(xb, wc_ref[:, 2 * dc:3 * dc])
    u = cc * cx
    row = lax.broadcasted_iota(jnp.int32, (tm, 1), 0)
    r1 = pltpu.roll(u, 1, axis=0)
    r2 = pltpu.roll(u, 2, axis=0)
    if sample:
        pos = row % seq_len
        p1 = jnp.where(pos == 0, fix1_ref[...], r1)
        p2 = jnp.where(pos < 2, fix2_ref[...], r2)
        u_ref[...] = u
    else:
        @pl.when(i % seq_blocks == 0)
        def _():
            carry_u[...] = jnp.zeros_like(carry_u)
            carry_d[...] = jnp.zeros_like(carry_d)
        c0 = carry_u[0:1, :]
        c1 = carry_u[1:2, :]
        p1 = jnp.where(row == 0, c1, r1)
        p2 = jnp.where(row == 0, c0, jnp.where(row == 1, c1, r2))
        carry_u[0:2, :] = u[tm - 2:tm, :]
        u_ref[...] = u[tm - 2:tm, :]
    cw = cw_ref[...]
    cy = cb_ref[...] + cw[0:1, :] * p2 + cw[1:2, :] * p1 + cw[2:3, :] * u
    gc_ref[...] = (cb * cy).astype(BF16)

    qs = (_dot(xb, wqkv_ref[:, 0:da]) * (1.0 / 8.0)).astype(BF16)
    k = _dot(xb, wqkv_ref[:, da:2 * da])
    kb = k.astype(BF16)
    v = _dot(xb, wqkv_ref[:, 2 * da:3 * da])
    vbf_ref[...] = v.astype(BF16)
    if sample:
        q_ref[...] = qs
        kbf_ref[...] = kb
        kf_ref[...] = k
        vf_ref[...] = v
    else:
        kf_ref[...] = jnp.transpose(k)
        vf_ref[...] = jnp.transpose(v)

    gw = g_ref.shape[1]
    for c in range(gw // 512):
        g_ref[:, c * 512:(c + 1) * 512] = _dot(xb, wg_ref[:, c * 512:(c + 1) * 512]).astype(BF16)

    xl = (xn - xb.astype(F32)).astype(BF16)
    hh_hl = _dot(xb, wf2_ref[...])
    fz = hh_hl[:, 0:LANES] + (hh_hl[:, LANES:2 * LANES] + _dot(xl, wf2_ref[:, 0:LANES]))
    z = fz + bf_ref[...]
    logf = jnp.minimum(z, 0.0) - jnp.log1p(jnp.exp(-jnp.abs(z)))
    lane = lax.broadcasted_iota(jnp.int32, (1, LANES), 1)
    nh = da // 64
    logf = jnp.where(lane < nh, logf, 0.0)
    if sample:
        logf_ref[...] = logf
    else:
        logf_ref[...] = jnp.transpose(logf)[0:SUBLANES, :]

    rr = lax.broadcasted_iota(jnp.int32, (tm, tm), 0)
    cc_ = lax.broadcasted_iota(jnp.int32, (tm, tm), 1)
    if sample:
        tri = (cc_ <= rr) & ((rr // seq_len) == (cc_ // seq_len))
    else:
        tri = cc_ <= rr
    tri = jnp.where(tri, 1.0, 0.0).astype(BF16)
    d3 = _dot(tri, jnp.concatenate(_split3(logf), axis=1))
    d = d3[:, 0:LANES] + (d3[:, LANES:2 * LANES] + d3[:, 2 * LANES:3 * LANES])
    if sample:
        dt_ref[...] = jnp.transpose(d)[0:SUBLANES, :]
    else:
        d = d + carry_d[0:1, :]
        carry_d[0:1, :] = d[tm - 1:tm, :]
        dcat = jnp.concatenate(_split3(d), axis=1)
        aq = (_dot(dcat, selq_ref[...]) + oneq_ref[...]).astype(BF16)
        ak = (_dot(dcat, selk_ref[...]) + onek_ref[...]).astype(BF16)
        for p in range(da // LANES):
            lo_, hi_ = p * LANES, (p + 1) * LANES
            qa_ref[:, 2 * lo_:2 * lo_ + LANES] = qs[:, lo_:hi_]
            qa_ref[:, 2 * lo_ + LANES:2 * hi_] = aq[:, lo_:hi_]
            ka_ref[:, 2 * lo_:2 * lo_ + LANES] = kb[:, lo_:hi_]
            ka_ref[:, 2 * lo_ + LANES:2 * hi_] = ak[:, lo_:hi_]


def _decay_columns(heads):
    npair = heads // 2
    selq = np.zeros((3 * LANES, npair * LANES), np.float32)
    selk = np.zeros((3 * LANES, npair * LANES), np.float32)
    oneq = np.zeros((1, npair * LANES), np.float32)
    onek = np.zeros((1, npair * LANES), np.float32)
    for p in range(npair):
        for hh in range(2):
            for term in range(3):
                selq[term * LANES + 2 * p + hh, p * LANES + 3 * hh + term] = 1.0
                selk[term * LANES + 2 * p + hh, p * LANES + 6 + 3 * hh + term] = -1.0
                oneq[0, p * LANES + 6 + 3 * hh + term] = 1.0
                onek[0, p * LANES + 3 * hh + term] = 1.0
    return jnp.asarray(selq, BF16), jnp.asarray(selk, BF16), jnp.asarray(oneq), jnp.asarray(onek)


def _inproj(x, gain, wc, wqkv, wg, wf2, bfp, cw, cbias, *, seq_len, sample, fix=None, aug=None):
    t, dm = x.shape
    dc = cw.shape[1]
    da = wqkv.shape[1] // 3
    gw = wg.shape[1]
    if sample:
        tm = t
        seq_blocks = 1
    else:
        tm = min(512, seq_len)
        seq_blocks = seq_len // tm
    nblk = t // tm
    nseq = t // seq_len
    kern = functools.partial(_inproj_kernel, tm=tm, seq_blocks=seq_blocks, sample=sample, seq_len=seq_len)
    rows = lambda w: pl.BlockSpec((tm, w), lambda i: (i, 0))
    in_specs = [rows(dm), _const_spec((1, dm)), _const_spec(wc.shape), _const_spec(wqkv.shape), _const_spec(wg.shape),
                _const_spec(wf2.shape), _const_spec((1, LANES)), _const_spec(cw.shape), _const_spec((1, dc))]
    args = [x, gain, wc, wqkv, wg, wf2, bfp, cw, cbias]
    sds = jax.ShapeDtypeStruct
    if sample:
        in_specs += [rows(dc), rows(dc)]
        args += list(fix)
        qk_shapes = [sds((t, da), BF16), sds((t, da), BF16)]
        qk_specs = [rows(da), rows(da)]
        kv_shapes = [sds((t, da), F32), sds((t, da), F32)]
        kv_specs = [rows(da), rows(da)]
        tail_shapes = [sds((t, LANES), F32), sds((SUBLANES, t), F32), sds((t, dc), F32)]
        tail_specs = [rows(LANES), pl.BlockSpec((SUBLANES, tm), lambda i: (0, i)), rows(dc)]
        scratch = []
    else:
        in_specs += [_const_spec(a.shape) for a in aug]
        args += list(aug)
        qk_shapes = [sds((t, 2 * da), BF16), sds((t, 2 * da), BF16)]
        qk_specs = [rows(2 * da), rows(2 * da)]
        kv_shapes = [sds((nseq, da, seq_len), F32), sds((nseq, da, seq_len), F32)]
        kv_specs = [pl.BlockSpec((None, da, tm), lambda i: (i // seq_blocks, 0, i % seq_blocks))] * 2
        tail_shapes = [sds((nseq, SUBLANES, seq_len), F32), sds((nseq, 2, dc), F32)]
        tail_specs = [pl.BlockSpec((None, SUBLANES, tm), lambda i: (i // seq_blocks, 0, i % seq_blocks)),
                      pl.BlockSpec((None, 2, dc), lambda i: (i // seq_blocks, 0, 0))]
        scratch = [pltpu.VMEM((SUBLANES, dc), F32), pltpu.VMEM((SUBLANES, LANES), F32)]
    out_shape = ([sds((t, dc), BF16)] + qk_shapes +
                 [sds((t, da), BF16)] + kv_shapes +
                 [sds((t, gw), BF16)] + tail_shapes)
    out_specs = [rows(dc)] + qk_specs + [rows(da)] + kv_specs + [rows(gw)] + tail_specs
    return pl.pallas_call(
        kern, grid=(nblk,), in_specs=in_specs, out_specs=out_specs, out_shape=out_shape,
        scratch_shapes=scratch, compiler_params=_cparams(("arbitrary",)),
        name="inproj_sample" if sample else "inproj_prompt")(*args)


def _fox_fused_kernel(pt_ref, qa_ref, ka_ref, v_ref, kt_hbm, vt_hbm, lf_hbm, qbd_ref, kn_ref, vn_ref, dtn_ref,
                      o_ref, os_ref,
                      m_sc, l_sc, acc_sc, ms_sc, ls_sc, as_sc, run_sc, kbuf, vbuf, lbuf, sem, g_ref,
                      *, tq, tk, dh, pp, ring, npages, nchunks, s_new, heads):
    first_step = (pl.program_id(0) == 0) & (pl.program_id(1) == 0) & (pl.program_id(2) == 0)
    qi = pl.program_id(2)
    page = kbuf.shape[3]
    cpb = npages // pp
    nrow = s_new * heads
    da = heads * dh
    nbs = qbd_ref.shape[0]

    def chunk_copies(c):
        slot = c % ring
        bs = c // cpb
        jc = c % cpb
        copies = []
        for i in range(pp):
            pid = pt_ref[bs * npages + (npages - 1 - (jc * pp + i))]
            copies.append(pltpu.make_async_copy(kt_hbm.at[pid], kbuf.at[slot, i], sem.at[slot]))
            copies.append(pltpu.make_async_copy(vt_hbm.at[pid], vbuf.at[slot, i], sem.at[slot]))
            copies.append(pltpu.make_async_copy(lf_hbm.at[pid], lbuf.at[slot, i], sem.at[slot]))
        return copies

    @pl.when(first_step)
    def _():
        g_ref[0] = 0
        for c in range(min(ring - 1, nchunks)):
            for cp in chunk_copies(c):
                cp.start()

    lane = lax.broadcasted_iota(jnp.int32, (1, 2 * LANES), 1)
    ext = lane - LANES
    qf = qa_ref[...].astype(F32)
    halves = []
    for h in range(2):
        keep = (((lane >= h * dh) & (lane < (h + 1) * dh))
                | ((ext >= 3 * h) & (ext < 3 * h + 3)) | ((ext >= 6 + 3 * h) & (ext < 9 + 3 * h)))
        halves.append(jnp.where(keep, qf, 0.0))
    qs = jnp.concatenate(halves, axis=0).astype(BF16)
    m_sc[...] = jnp.full_like(m_sc, NEG_BIG)
    l_sc[...] = jnp.zeros_like(l_sc)
    acc_sc[...] = jnp.zeros_like(acc_sc)
    nfull = (qi * tq) // tk

    def prompt_step(j, masked):
        ks = pl.multiple_of(j * tk, tk)
        kb = ka_ref[pl.ds(ks, tk), :]
        vb = v_ref[pl.ds(ks, tk), :]
        s = _dot_nt(qs, kb)
        if masked:
            rloc = lax.broadcasted_iota(jnp.int32, (2 * tq, tk), 0)
            rloc = jnp.where(rloc >= tq, rloc - tq, rloc) + qi * tq
            cloc = lax.broadcasted_iota(jnp.int32, (2 * tq, tk), 1) + ks
            s = jnp.where(cloc <= rloc, s, NEG_BIG)
        m_prev = m_sc[...]
        m_new = jnp.maximum(m_prev, jnp.max(s, axis=1, keepdims=True))
        p = jnp.exp(s - _lane_tile(m_new, tk))
        alpha = jnp.exp(m_prev - m_new)
        l_sc[...] = alpha * l_sc[...] + jnp.sum(p, axis=1, keepdims=True)
        acc_sc[...] = alpha * acc_sc[...] + _dot(p.astype(BF16), vb)
        m_sc[...] = m_new

    def sample_update(s, pv_fn):
        m_prev = ms_sc[...]
        m_new = jnp.maximum(m_prev, jnp.max(s, axis=1, keepdims=True))
        p = jnp.exp(s - _lane_tile(m_new, s.shape[1]))
        alpha = jnp.exp(m_prev - m_new)
        ls_sc[...] = alpha * ls_sc[...] + jnp.sum(p, axis=1, keepdims=True)
        as_sc[...] = _lane_tile(alpha, da) * as_sc[...] + pv_fn(p.astype(BF16))
        ms_sc[...] = m_new

    rr_ = lax.broadcasted_iota(jnp.int32, (page, 2 * page), 0)
    cc_ = lax.broadcasted_iota(jnp.int32, (page, 2 * page), 1)
    after = jnp.where((rr_ > cc_) | (cc_ >= page), 1.0, 0.0).astype(BF16)

    def sample_chunk(g, valid):
        slot = g % ring
        bs = jnp.minimum(g // cpb, nbs - 1)
        qbd = qbd_ref[bs]
        lf = jnp.concatenate([lbuf[slot, i] for i in range(pp)], axis=0)
        n8 = pp * heads
        r3 = _dot(jnp.concatenate(_split3(lf), axis=0), after)
        both = r3[0:n8] + (r3[n8:2 * n8] + r3[2 * n8:3 * n8])
        run = run_sc[...]
        scores = []
        for i in range(pp):
            rev = both[i * heads:(i + 1) * heads, 0:page] + run
            run = run + both[i * heads:(i + 1) * heads, page:2 * page]
            scores.append(_dot(qbd, kbuf[slot, i].astype(BF16)) + jnp.concatenate([rev] * s_new, axis=0))
        run_sc[...] = run
        s = jnp.where(valid, jnp.concatenate(scores, axis=1), NEG_BIG)

        def pv_pages(p):
            acc = None
            for i in range(pp):
                term = _dot_nt(p[:, i * page:(i + 1) * page], vbuf[slot, i].astype(BF16))
                acc = term if acc is None else acc + term
            return acc

        sample_update(s, pv_pages)

    def sample_finish(bs):
        ntok = kn_ref.shape[0]
        qbd = qbd_ref[bs]
        dtn = dtn_ref[...]
        lane_t = lax.broadcasted_iota(jnp.int32, (1, ntok), 1)
        dq_rows = [jnp.sum(jnp.where(lane_t == bs * s_new + t, dtn, 0.0), axis=1, keepdims=True)
                   for t in range(s_new)]
        dq = jnp.concatenate(dq_rows, axis=0)
        dk = jnp.concatenate([dtn] * s_new, axis=0)
        s = _dot_nt(qbd, kn_ref[...]) + (dq - dk)
        rowt = lax.broadcasted_iota(jnp.int32, (nrow, ntok), 0) // heads
        col = lax.broadcasted_iota(jnp.int32, (nrow, ntok), 1)
        keep = ((col // s_new) == bs) & ((col % s_new) <= rowt)
        sample_update(jnp.where(keep, s, NEG_BIG), lambda p: _dot(p, vn_ref[...]))
        o = as_sc[...] / _lane_tile(ls_sc[...], da)
        rowh = lax.broadcasted_iota(jnp.int32, (nrow, da), 0) % heads
        colh = lax.broadcasted_iota(jnp.int32, (nrow, da), 1) // dh
        o = jnp.where(rowh == colh, o, 0.0)
        os_ref[bs] = jnp.sum(o.reshape(s_new, heads, da), axis=1)

    def iteration(j, masked):
        g = g_ref[0]
        valid = g < nchunks

        @pl.when(g + (ring - 1) < nchunks)
        def _():
            for cp in chunk_copies(g + (ring - 1)):
                cp.start()

        @pl.when(valid)
        def _():
            for cp in chunk_copies(g):
                cp.wait()

        @pl.when(valid & (g % cpb == 0))
        def _():
            ms_sc[...] = jnp.full_like(ms_sc, NEG_BIG)
            ls_sc[...] = jnp.zeros_like(ls_sc)
            as_sc[...] = jnp.zeros_like(as_sc)
            run_sc[...] = jnp.zeros_like(run_sc)

        prompt_step(j, masked)
        sample_chunk(g, valid)

        @pl.when(valid & (g % cpb == cpb - 1))
        def _():
            sample_finish(g // cpb)

        g_ref[0] = g + 1

    def body(j, c):
        iteration(j, False)
        return c

    def prompt_body(j, c):
        prompt_step(j, False)
        return c

    stream_live = g_ref[0] < nchunks

    @pl.when(stream_live)
    def _():
        lax.fori_loop(0, nfull, body, 0)
        iteration(nfull, True)

    @pl.when(jnp.logical_not(stream_live))
    def _():
        lax.fori_loop(0, nfull, prompt_body, 0)
        prompt_step(nfull, True)

    o = acc_sc[...] / l_sc[...]
    lane_o = lax.broadcasted_iota(jnp.int32, (1, LANES), 1)
    o_ref[...] = jnp.where(lane_o < dh, o[0:tq], o[tq:2 * tq]).astype(o_ref.dtype)


def _fox_fused(page_table_flat, qa, ka, v, kt_pages, vt_pages, lf_pages, qbd, kn, vn, dtn,
               *, nseq, seq_len, dh, npages, s_new, heads):
    t, da = v.shape
    nbs, nrow, _ = qbd.shape
    page = kt_pages.shape[2]
    ntok = kn.shape[0]
    tk = min(FOX_KEY_BLOCK, seq_len)
    tq = min(FOX_QUERY_BLOCK, tk)
    nq = seq_len // tq
    npair = da // LANES
    pp = min(FUSED_PAGES_PER_ITERATION, npages)
    nchunks = nbs * (npages // pp)
    iters = nseq * npair * sum((qi * tq) // tk + 1 for qi in range(nq))
    assert npages % pp == 0 and nchunks <= iters, "the page stream must fit in the prompt attention's iterations"
    kern = functools.partial(_fox_fused_kernel, tq=tq, tk=tk, dh=dh, pp=pp, ring=FUSED_RING, npages=npages,
                             nchunks=nchunks, s_new=s_new, heads=heads)
    const = lambda shape: pl.BlockSpec(shape, lambda b, hp, qi, pt: (0,) * len(shape))
    grid_spec = pltpu.PrefetchScalarGridSpec(
        num_scalar_prefetch=1, grid=(nseq, npair, nq),
        in_specs=[pl.BlockSpec((tq, 2 * LANES), lambda b, hp, qi, pt: (b * nq + qi, hp)),
                  pl.BlockSpec((seq_len, 2 * LANES), lambda b, hp, qi, pt: (b, hp)),
                  pl.BlockSpec((seq_len, LANES), lambda b, hp, qi, pt: (b, hp)),
                  pl.BlockSpec(memory_space=pl.ANY), pl.BlockSpec(memory_space=pl.ANY),
                  pl.BlockSpec(memory_space=pl.ANY),
                  const((nbs, nrow, da)), const((ntok, da)), const((ntok, da)), const((SUBLANES, ntok))],
        out_specs=[pl.BlockSpec((tq, LANES), lambda b, hp, qi, pt: (b * nq + qi, hp)),
                   const((nbs, s_new, da))],
        scratch_shapes=[pltpu.VMEM((2 * tq, LANES), F32), pltpu.VMEM((2 * tq, LANES), F32),
                        pltpu.VMEM((2 * tq, LANES), F32),
                        pltpu.VMEM((nrow, LANES), F32), pltpu.VMEM((nrow, LANES), F32), pltpu.VMEM((nrow, da), F32),
                        pltpu.VMEM((heads, LANES), F32),
                        pltpu.VMEM((FUSED_RING, pp, da, page), F32), pltpu.VMEM((FUSED_RING, pp, da, page), F32),
                        pltpu.VMEM((FUSED_RING, pp, heads, page), F32),
                        pltpu.SemaphoreType.DMA((FUSED_RING,)), pltpu.SMEM((1,), jnp.int32)])
    return pl.pallas_call(
        kern, grid_spec=grid_spec,
        out_shape=[jax.ShapeDtypeStruct((t, da), BF16), jax.ShapeDtypeStruct((nbs, s_new, da), F32)],
        compiler_params=_cparams(("arbitrary", "arbitrary", "arbitrary")),
        name="fox_fused")(page_table_flat, qa, ka, v, kt_pages, vt_pages, lf_pages, qbd, kn, vn, dtn)


def _post_attn_math(x_ref, gc_ref, at_ref, g_ref, wuc_ref, wua_ref, wmo_ref, gain_ref, wxq_ref):
    dm = x_ref.shape[1]
    y_conv = _dot(gc_ref[...], wuc_ref[...])
    y_attn = _dot(at_ref[...].astype(BF16), wua_ref[...])
    g_conv = g_ref[:, 0:dm].astype(F32)
    g_attn = g_ref[:, dm:2 * dm].astype(F32)
    mixed = jax.nn.sigmoid(g_conv) * y_conv + jax.nn.sigmoid(g_attn) * y_attn
    h = x_ref[...] + _dot(mixed.astype(BF16), wmo_ref[...])
    xn = _rms(h, gain_ref[...]).astype(BF16)
    return h, _dot(xn, wxq_ref[...]).astype(BF16)


def _post_attn_kernel(x_ref, gc_ref, at_ref, g_ref, wuc_ref, wua_ref, wmo_ref, gain_ref, wxq_ref, h_ref, qx_ref):
    h, qx = _post_attn_math(x_ref, gc_ref, at_ref, g_ref, wuc_ref, wua_ref, wmo_ref, gain_ref, wxq_ref)
    h_ref[...] = h
    qx_ref[...] = qx.astype(qx_ref.dtype)


def _post_attn(x, gc, at, g, wuc, wua, wmo, gain, wxq, *, tm, name):
    t, dm = x.shape
    dx = wxq.shape[1]
    rows = lambda w: pl.BlockSpec((tm, w), lambda i: (i, 0))
    return pl.pallas_call(
        _post_attn_kernel, grid=(t // tm,),
        in_specs=[rows(dm), rows(gc.shape[1]), rows(at.shape[1]), rows(g.shape[1]), _const_spec(wuc.shape),
                  _const_spec(wua.shape), _const_spec(wmo.shape), _const_spec((1, dm)), _const_spec(wxq.shape)],
        out_specs=[rows(dm), rows(dx)],
        out_shape=[jax.ShapeDtypeStruct((t, dm), F32), jax.ShapeDtypeStruct((t, dx), BF16)],
        compiler_params=_cparams(("parallel",)), name=name)(x, gc, at, g, wuc, wua, wmo, gain, wxq)


def _memkv_kernel(m_ref, gain_ref, wk_ref, wv_ref, k_ref, v_ref, *, xh):
    mn = _rms(m_ref[...], gain_ref[...]).astype(BF16)
    tm = m_ref.shape[0]
    xd = wk_ref.shape[1] // xh
    k = _dot(mn, wk_ref[...])
    v = _dot(mn, wv_ref[...])
    for h in range(xh):
        k_ref[pl.ds(h, tm, stride=xh), :] = k[:, h * xd:(h + 1) * xd]
        v_ref[pl.ds(h, tm, stride=xh), :] = v[:, h * xd:(h + 1) * xd]


def _memkv(mem, gain, wk, wv, *, xh):
    t, dm = mem.shape
    tm = min(512, t)
    xd = wk.shape[1] // xh
    rows = lambda w: pl.BlockSpec((tm, w), lambda i: (i, 0))
    return pl.pallas_call(
        functools.partial(_memkv_kernel, xh=xh), grid=(t // tm,),
        in_specs=[rows(dm), _const_spec((1, dm)), _const_spec(wk.shape), _const_spec(wv.shape)],
        out_specs=[pl.BlockSpec((tm * xh, xd), lambda i: (i, 0))] * 2,
        out_shape=[jax.ShapeDtypeStruct((t * xh, xd), F32)] * 2,
        compiler_params=_cparams(("parallel",)), name="memkv")(mem, gain, wk, wv)


def _xattn_math(q, mk_ref, mv_ref, xh):
    nm = mk_ref.shape[0] // xh
    xd = mk_ref.shape[1]
    scale = xd ** -0.5
    outs = []
    for h in range(xh):
        mk = mk_ref[pl.ds(h, nm, stride=xh), :].astype(BF16)
        mv = mv_ref[pl.ds(h, nm, stride=xh), :].astype(BF16)
        s = _dot_nt(q[:, h * xd:(h + 1) * xd], mk) * scale
        m = jnp.max(s, axis=1, keepdims=True)
        p = jnp.exp(s - m)
        p = p / jnp.sum(p, axis=1, keepdims=True)
        outs.append(_dot(p.astype(BF16), mv))
    return jnp.concatenate(outs, axis=1)


def _xattn_kernel(q_ref, mk_ref, mv_ref, o_ref, *, xh):
    tq, dx = q_ref.shape[1:]
    xd = dx // xh
    nmh = mk_ref.shape[1]
    scale = xd ** -0.5
    rowh = lax.broadcasted_iota(jnp.int32, (xh * tq, nmh), 0) // tq
    colh = lax.broadcasted_iota(jnp.int32, (xh * tq, nmh), 1) % xh
    same_head = rowh == colh
    for b in range(q_ref.shape[0]):
        q = q_ref[b].astype(BF16)
        qs = jnp.concatenate([q[:, h * xd:(h + 1) * xd] for h in range(xh)], axis=0)
        s = jnp.where(same_head, _dot_nt(qs, mk_ref[b].astype(BF16)) * scale, NEG_BIG)
        m = jnp.max(s, axis=1, keepdims=True)
        p = jnp.exp(s - m)
        p = p / jnp.sum(p, axis=1, keepdims=True)
        o = _dot(p.astype(BF16), mv_ref[b].astype(BF16))
        o_ref[b] = jnp.concatenate([o[h * tq:(h + 1) * tq] for h in range(xh)], axis=1).astype(o_ref.dtype)


def _xattn(q, mk, mv, *, tq, xh, nbb, name):
    nb, s, dx = q.shape
    nm, xd = mk.shape[1:]
    kern = functools.partial(_xattn_kernel, xh=xh)
    return pl.pallas_call(
        kern, grid=(nb // nbb, s // tq),
        in_specs=[pl.BlockSpec((nbb, tq, dx), lambda b, i: (b, i, 0)),
                  pl.BlockSpec((nbb, nm, xd), lambda b, i: (b, 0, 0)),
                  pl.BlockSpec((nbb, nm, xd), lambda b, i: (b, 0, 0))],
        out_specs=pl.BlockSpec((nbb, tq, dx), lambda b, i: (b, i, 0)),
        out_shape=jax.ShapeDtypeStruct((nb, s, dx), q.dtype),
        compiler_params=_cparams(("parallel", "parallel")), name=name)(q, mk, mv)


def _pre_moe_kernel(h_ref, o_ref, wxo_ref, gain_ref, wrh_ref, wrl_ref, cnt_in_ref,
                    h2_ref, xn_ref, rt_ref, rr_ref, cnt_out_ref, base, *, tm, ngroups, epg):
    @pl.when(pl.program_id(0) == 0)
    def _():
        base[...] = cnt_in_ref[...]

    _pre_moe_math(h_ref[...], o_ref[...].astype(BF16), wxo_ref, gain_ref, wrh_ref, wrl_ref,
                  h2_ref, xn_ref, rt_ref, rr_ref, cnt_out_ref, base, tm=tm, ngroups=ngroups, epg=epg)


def _mid_kernel(x_ref, gc_ref, at_ref, g_ref, mk_ref, mv_ref, wuc_ref, wua_ref, wmo_ref, gx_ref, wxq_ref,
                wxo_ref, gf_ref, wrh_ref, wrl_ref, cnt_in_ref,
                h2_ref, xn_ref, rt_ref, rr_ref, cnt_out_ref, base, *, tm, ngroups, epg, xh):
    @pl.when(pl.program_id(0) == 0)
    def _():
        base[...] = cnt_in_ref[...]

    h1, qx = _post_attn_math(x_ref, gc_ref, at_ref, g_ref, wuc_ref, wua_ref, wmo_ref, gx_ref, wxq_ref)
    o = _xattn_math(qx, mk_ref, mv_ref, xh).astype(BF16)
    _pre_moe_math(h1, o, wxo_ref, gf_ref, wrh_ref, wrl_ref,
                  h2_ref, xn_ref, rt_ref, rr_ref, cnt_out_ref, base, tm=tm, ngroups=ngroups, epg=epg)


def _pre_moe_math(h, o, wxo_ref, gain_ref, wrh_ref, wrl_ref, h2_ref, xn_ref, rt_ref, rr_ref, cnt_out_ref, base,
                  *, tm, ngroups, epg):
    h2 = h + _dot(o, wxo_ref[...])
    h2_ref[...] = h2
    xn = _rms(h2, gain_ref[...])
    xw = _pack_bf16_pairs(xn)
    nc = xw.shape[1] // LANES
    for c in range(nc):
        xn_ref[pl.ds(c, tm, stride=nc), :] = xw[:, c * LANES:(c + 1) * LANES]

    xh = xn.astype(BF16)
    xl = (xn - xh.astype(F32)).astype(BF16)
    lt = _dot_nt(wrh_ref[...], xh) + (_dot_nt(wrh_ref[...], xl) + _dot_nt(wrl_ref[...], xh))

    sub = lax.broadcasted_iota(jnp.int32, (SUBLANES, tm), 0)
    gl = jnp.where(sub < ngroups, lt[0:SUBLANES, :], -jnp.inf)
    gmax = jnp.max(gl, axis=0, keepdims=True)
    gidx = jnp.min(jnp.where(gl == gmax, sub, SUBLANES), axis=0, keepdims=True)
    pg = 1.0 / jnp.sum(jnp.exp(gl - gmax), axis=0, keepdims=True)
    el = jnp.zeros((epg, tm), F32)
    for g in range(ngroups):
        el = jnp.where(gidx == g, lt[EXPERT_ROW0 + g * epg:EXPERT_ROW0 + (g + 1) * epg, :], el)
    v1 = jnp.max(el, axis=0, keepdims=True)
    i1 = jnp.min(jnp.where(el == v1, sub, epg), axis=0, keepdims=True)
    el2 = jnp.where(sub == i1, -jnp.inf, el)
    v2 = jnp.max(el2, axis=0, keepdims=True)
    i2 = jnp.min(jnp.where(el2 == v2, sub, epg), axis=0, keepdims=True)
    t2 = jnp.exp(v2 - v1)
    den = 1.0 + t2
    w0 = (1.0 / den) * pg
    w1 = (t2 / den) * pg
    e0 = gidx * epg + i1
    e1 = gidx * epg + i2

    erow = lax.broadcasted_iota(jnp.int32, (LANES, tm), 0)
    oh0 = erow == e0
    oh1 = erow == e1
    rr = lax.broadcasted_iota(jnp.int32, (tm, tm), 0)
    cc = lax.broadcasted_iota(jnp.int32, (tm, tm), 1)
    triu = jnp.where(rr <= cc, 1.0, 0.0).astype(BF16)
    pre0 = _dot(jnp.where(oh0, 1.0, 0.0).astype(BF16), triu)
    pre1 = _dot(jnp.where(oh1, 1.0, 0.0).astype(BF16), triu)
    b0 = base[:, 0:1]
    tot0 = pre0[:, tm - 1:tm]
    tot1 = pre1[:, tm - 1:tm]
    rank0 = jnp.sum(jnp.where(oh0, pre0 - 1.0 + b0, 0.0), axis=0, keepdims=True)
    rank1 = jnp.sum(jnp.where(oh1, pre1 - 1.0 + (b0 + tot0), 0.0), axis=0, keepdims=True)
    newb = b0 + tot0 + tot1
    base[...] = jnp.broadcast_to(newb, base.shape)
    cnt_out_ref[...] = jnp.broadcast_to(newb, cnt_out_ref.shape)

    zero = jnp.zeros((1, tm), F32)
    rt = jnp.concatenate([e0.astype(F32), e1.astype(F32), w0, w1, rank0, rank1, zero, zero], axis=0)
    rt_ref[...] = rt
    rt_pad = jnp.concatenate([rt, jnp.zeros((LANES - ROUTE_ROWS, tm), F32)], axis=0)
    rr_ref[...] = jnp.transpose(rt_pad)


def _pre_moe(h, o, wxo, gain, wrh, wrl, cnt_in, *, tm, ngroups, epg, name):
    t, dm = h.shape
    dx = o.shape[1]
    kern = functools.partial(_pre_moe_kernel, tm=tm, ngroups=ngroups, epg=epg)
    rows = lambda w: pl.BlockSpec((tm, w), lambda i: (i, 0))
    return pl.pallas_call(
        kern, grid=(t // tm,),
        in_specs=[rows(dm), rows(dx), _const_spec(wxo.shape), _const_spec((1, dm)), _const_spec(wrh.shape),
                  _const_spec(wrl.shape), _const_spec((LANES, LANES))],
        out_specs=[rows(dm), pl.BlockSpec((tm * (dm // (2 * LANES)), LANES), lambda i: (i, 0)),
                   pl.BlockSpec((ROUTE_ROWS, tm), lambda i: (0, i)), rows(LANES), _const_spec((LANES, LANES))],
        out_shape=[jax.ShapeDtypeStruct((t, dm), F32), jax.ShapeDtypeStruct((t * (dm // (2 * LANES)), LANES), jnp.uint32),
                   jax.ShapeDtypeStruct((ROUTE_ROWS, t), F32), jax.ShapeDtypeStruct((t, LANES), F32),
                   jax.ShapeDtypeStruct((LANES, LANES), F32)],
        scratch_shapes=[pltpu.VMEM((LANES, LANES), F32)],
        compiler_params=_cparams(("arbitrary",)), name=name)(h, o, wxo, gain, wrh, wrl, cnt_in)


def _mid(x, gc, at, g, mk, mv, wuc, wua, wmo, gx, wxq, wxo, gf, wrh, wrl, cnt_in, *, tm, seq_len, ngroups, epg, xh):
    t, dm = x.shape
    nm, xd = mk.shape[1:]
    seq_blocks = seq_len // tm
    kern = functools.partial(_mid_kernel, tm=tm, ngroups=ngroups, epg=epg, xh=xh)
    rows = lambda w: pl.BlockSpec((tm, w), lambda i: (i, 0))
    mem = pl.BlockSpec((None, nm, xd), lambda i: (i // seq_blocks, 0, 0))
    consts = [wuc, wua, wmo, gx, wxq, wxo, gf, wrh, wrl, cnt_in]
    return pl.pallas_call(
        kern, grid=(t // tm,),
        in_specs=[rows(dm), rows(gc.shape[1]), rows(at.shape[1]), rows(g.shape[1]), mem, mem]
        + [_const_spec(a.shape) for a in consts],
        out_specs=[rows(dm), pl.BlockSpec((tm * (dm // (2 * LANES)), LANES), lambda i: (i, 0)),
                   pl.BlockSpec((ROUTE_ROWS, tm), lambda i: (0, i)), rows(LANES), _const_spec((LANES, LANES))],
        out_shape=[jax.ShapeDtypeStruct((t, dm), F32), jax.ShapeDtypeStruct((t * (dm // (2 * LANES)), LANES), jnp.uint32),
                   jax.ShapeDtypeStruct((ROUTE_ROWS, t), F32), jax.ShapeDtypeStruct((t, LANES), F32),
                   jax.ShapeDtypeStruct((LANES, LANES), F32)],
        scratch_shapes=[pltpu.VMEM((LANES, LANES), F32)],
        compiler_params=_cparams(("arbitrary",)), name="mid_prompt")(x, gc, at, g, mk, mv, *consts)


def _row_copy(src, r_src, dst, r_dst, sem, nc):
    return pltpu.make_async_copy(src.at[pl.ds(pl.multiple_of(r_src * nc, nc), nc)],
                                 dst.at[pl.ds(pl.multiple_of(r_dst * nc, nc), nc)], sem)


def _slab_rows(ref, nc):
    rows = ref.shape[0] // nc
    return jnp.concatenate([ref[pl.ds(c, rows, stride=nc), :] for c in range(nc)], axis=1)


def _pack_bf16_pairs(x):
    half = x.shape[1] // 2
    lo = pltpu.bitcast(x[:, 0:half].astype(BF16).astype(F32), jnp.uint32)
    hi = pltpu.bitcast(x[:, half:2 * half].astype(BF16).astype(F32), jnp.uint32)
    return lax.shift_right_logical(lo, jnp.uint32(16)) | (hi & jnp.uint32(0xFFFF0000))


def _unpack_bf16_pairs(w):
    lo = pltpu.bitcast(lax.shift_left(w, jnp.uint32(16)), F32)
    hi = pltpu.bitcast(w & jnp.uint32(0xFFFF0000), F32)
    return jnp.concatenate([lo, hi], axis=1)


def _dest_kernel(rt_ref, pstart_ref, o_ref):
    rt = rt_ref[...]
    pst = pstart_ref[:, 0:1]
    erow = lax.broadcasted_iota(jnp.int32, (LANES, rt.shape[1]), 0)
    rows = []
    for k in range(TOP_K):
        e = rt[k:k + 1, :].astype(jnp.int32)
        rows.append(jnp.sum(jnp.where(erow == e, pst, 0.0), axis=0, keepdims=True) + rt[4 + k:5 + k, :])
    rows.append(jnp.zeros((ROUTE_ROWS - TOP_K, rt.shape[1]), F32))
    o_ref[...] = jnp.concatenate(rows, axis=0).astype(jnp.int32)


def _dest(rt_all, pstart_col):
    tall = rt_all.shape[1]
    nchunk = 3 if tall % (3 * LANES) == 0 else 1
    w = tall // nchunk
    return pl.pallas_call(
        _dest_kernel, grid=(nchunk,),
        in_specs=[pl.BlockSpec((ROUTE_ROWS, w), lambda i: (0, i)), _const_spec((LANES, LANES))],
        out_specs=pl.BlockSpec((ROUTE_ROWS, w), lambda i: (0, i)),
        out_shape=jax.ShapeDtypeStruct((ROUTE_ROWS, tall), jnp.int32),
        compiler_params=_cparams(("parallel",)), name="dest")(rt_all, pstart_col)


def _scatter_kernel(dest_ref, pstart_ref, pend_ref, xp_ref, xs_ref, out_ref, zeros, sem, zsem,
                    *, tm_p, nblk_p, nexp, nc):
    i = pl.program_id(0)
    tall = dest_ref.shape[0] // TOP_K

    @pl.when(i == 0)
    def _():
        zeros[...] = jnp.zeros_like(zeros)

        def zero_block(blk):
            start = pl.multiple_of(blk * (EXPERT_ROWS * nc), EXPERT_ROWS * nc)
            return pltpu.make_async_copy(zeros, out_ref.at[pl.ds(start, EXPERT_ROWS * nc)], zsem)

        for e in range(nexp):
            @pl.when(pend_ref[e] > pstart_ref[e])
            def _():
                zero_block(pend_ref[e] // EXPERT_ROWS - 1).start()
        for e in range(nexp):
            @pl.when(pend_ref[e] > pstart_ref[e])
            def _():
                zero_block(pend_ref[e] // EXPERT_ROWS - 1).wait()
        nblk = out_ref.shape[0] // (EXPERT_ROWS * nc)
        nused = pend_ref[nexp - 1] // EXPERT_ROWS

        def start_unused(blk, c):
            zero_block(blk).start()
            return c

        def wait_unused(blk, c):
            zero_block(blk).wait()
            return c

        lax.fori_loop(nused, nblk, start_unused, 0)
        lax.fori_loop(nused, nblk, wait_unused, 0)

    def copy_rows(x_ref, base):
        tm = x_ref.shape[0] // nc

        def issue(r, c):
            for k in range(TOP_K):
                _row_copy(x_ref, r, out_ref, dest_ref[k * tall + base + r], sem, nc).start(priority=k % 2)
            return c

        lax.fori_loop(0, tm, issue, 0, unroll=8)
        for k in range(TOP_K):
            pltpu.make_async_copy(x_ref, x_ref, sem).wait()

    @pl.when(i < nblk_p)
    def _():
        copy_rows(xp_ref, i * tm_p)

    @pl.when(i == nblk_p)
    def _():
        copy_rows(xs_ref, nblk_p * tm_p)


def _scatter(dest_flat, pstart, pend, x_p, x_s, *, n_rows, tm_p, nexp, nc):
    tp = x_p.shape[0] // nc
    ts = x_s.shape[0] // nc
    nblk_p = tp // tm_p
    kern = functools.partial(_scatter_kernel, tm_p=tm_p, nblk_p=nblk_p, nexp=nexp, nc=nc)
    grid_spec = pltpu.PrefetchScalarGridSpec(
        num_scalar_prefetch=3, grid=(nblk_p + 1,),
        in_specs=[pl.BlockSpec((tm_p * nc, LANES), lambda i, *_: (jnp.minimum(i, nblk_p - 1), 0)),
                  pl.BlockSpec((ts * nc, LANES), lambda i, *_: (0, 0))],
        out_specs=pl.BlockSpec(memory_space=pl.ANY),
        scratch_shapes=[pltpu.VMEM((EXPERT_ROWS * nc, LANES), jnp.uint32), pltpu.SemaphoreType.DMA(()),
                        pltpu.SemaphoreType.DMA(())])
    return pl.pallas_call(
        kern, grid_spec=grid_spec, out_shape=jax.ShapeDtypeStruct((n_rows * nc, LANES), jnp.uint32),
        compiler_params=_cparams(("arbitrary",)), name="scatter")(dest_flat, pstart, pend, x_p, x_s)


def _experts_kernel(blk_e_ref, nused_ref, xs_ref, wg_ref, wu_ref, wd_ref, y_ref, wgb, wub, wdb, *, nc):
    i = pl.program_id(0)
    prev = blk_e_ref[jnp.maximum(i - 1, 0)]
    fresh = (i == 0) | (blk_e_ref[i] != prev)

    @pl.when(i < nused_ref[0])
    def _():
        @pl.when(fresh)
        def _():
            wgb[...] = wg_ref[...].astype(BF16)
            wub[...] = wu_ref[...].astype(BF16)
            wdb[...] = wd_ref[...].astype(BF16)
        x = _unpack_bf16_pairs(_slab_rows(xs_ref, nc)).astype(BF16)
        a = _dot(x, wgb[...])
        u = _dot(x, wub[...])
        hmid = (a * jax.nn.sigmoid(a)) * u
        y = _pack_bf16_pairs(_dot(hmid.astype(BF16), wdb[...]))
        for c in range(nc):
            y_ref[pl.ds(c, EXPERT_ROWS, stride=nc), :] = y[:, c * LANES:(c + 1) * LANES]

    @pl.when(i >= nused_ref[0])
    def _():
        y_ref[...] = jnp.zeros_like(y_ref)


def _experts(blk_e, nused, xs, wg, wu, wd):
    dm, de = wg.shape[1:]
    nc = dm // (2 * LANES)
    nblk = xs.shape[0] // (EXPERT_ROWS * nc)

    def row_map(i, be, nu):
        return (jnp.minimum(i, nu[0] - 1), 0)

    grid_spec = pltpu.PrefetchScalarGridSpec(
        num_scalar_prefetch=2, grid=(nblk,),
        in_specs=[pl.BlockSpec((EXPERT_ROWS * nc, LANES), row_map),
                  pl.BlockSpec((None, dm, de), lambda i, be, nu: (be[i], 0, 0)),
                  pl.BlockSpec((None, dm, de), lambda i, be, nu: (be[i], 0, 0)),
                  pl.BlockSpec((None, de, dm), lambda i, be, nu: (be[i], 0, 0))],
        out_specs=pl.BlockSpec((EXPERT_ROWS * nc, LANES), lambda i, be, nu: (i, 0)),
        scratch_shapes=[pltpu.VMEM((dm, de), BF16), pltpu.VMEM((dm, de), BF16), pltpu.VMEM((de, dm), BF16)])
    return pl.pallas_call(
        functools.partial(_experts_kernel, nc=nc), grid_spec=grid_spec,
        out_shape=jax.ShapeDtypeStruct(xs.shape, jnp.uint32),
        compiler_params=_cparams(("arbitrary",)), name="experts")(blk_e, nused, xs, wg, wu, wd)


def _combine_kernel(dest_ref, h_ref, rr_ref, gain_ref, y_hbm, o_ref, buf, sem, *, tm, tok0):
    i = pl.program_id(0)
    nsteps = pl.num_programs(0)
    nc = h_ref.shape[1] // (2 * LANES)
    tall = dest_ref.shape[0] // TOP_K

    def request(blk):
        slot = blk % 2
        base = tok0 + blk * tm

        def issue(r, c):
            for k in range(TOP_K):
                _row_copy(y_hbm, dest_ref[k * tall + base + r], buf.at[slot, k], r, sem.at[slot],
                          nc).start(priority=k % 2)
            return c

        lax.fori_loop(0, tm, issue, 0, unroll=8)

    @pl.when(i == 0)
    def _():
        request(0)

    @pl.when(i + 1 < nsteps)
    def _():
        request(i + 1)

    slot = i % 2
    for k in range(TOP_K):
        pltpu.make_async_copy(buf.at[slot, k], buf.at[slot, k], sem.at[slot]).wait()
    rr = rr_ref[...]
    y0 = _unpack_bf16_pairs(_slab_rows(buf.at[slot, 0], nc))
    y1 = _unpack_bf16_pairs(_slab_rows(buf.at[slot, 1], nc))
    h = h_ref[...] + (rr[:, 2:3] * y0 + rr[:, 3:4] * y1)
    o_ref[...] = _rms(h, gain_ref[...])


def _combine(dest_flat, h, rr, gain, y, *, tm, tok0, name):
    t, dm = h.shape
    kern = functools.partial(_combine_kernel, tm=tm, tok0=tok0)
    grid_spec = pltpu.PrefetchScalarGridSpec(
        num_scalar_prefetch=1, grid=(t // tm,),
        in_specs=[pl.BlockSpec((tm, dm), lambda i, *_: (i, 0)), pl.BlockSpec((tm, LANES), lambda i, *_: (i, 0)),
                  pl.BlockSpec((1, dm), lambda i, *_: (0, 0)), pl.BlockSpec(memory_space=pl.ANY)],
        out_specs=pl.BlockSpec((tm, dm), lambda i, *_: (i, 0)),
        scratch_shapes=[pltpu.VMEM((2, TOP_K, tm * (dm // (2 * LANES)), LANES), jnp.uint32),
                        pltpu.SemaphoreType.DMA((2,))])
    return pl.pallas_call(
        kern, grid_spec=grid_spec, out_shape=jax.ShapeDtypeStruct((t, dm), F32),
        compiler_params=_cparams(("arbitrary",)), name=name)(dest_flat, h, rr, gain, y)


def kernel(x_prompt, x_sample, cache_k, cache_v, cache_logf, cache_mem_k, cache_mem_v, state_conv, page_table,
           mem_prompt, norm_mix, w_in, b_forget, conv_w, conv_b, w_up_conv, w_up_attn, w_mix_out, norm_xattn,
           norm_mem, w_xq, w_xk, w_xv, w_xo, norm_ffn, w_router_group, w_router_expert, w_expert_gate,
           w_expert_up, w_expert_down, norm_final):
    depth = w_in.shape[0]
    assert depth == 1, "single-layer trunk"
    nbp, seq, dm = x_prompt.shape
    nbs, s_new, _ = x_sample.shape
    _, n_pool, page, heads, dh = cache_k.shape
    npages = page_table.shape[1]
    nmem, xh, xd = cache_mem_k.shape[2:]
    dc = conv_w.shape[2]
    da = heads * dh
    dx = xh * xd
    ngroups, _, epg = w_router_expert.shape[1:]
    nexp = ngroups * epg
    tp = nbp * seq
    ts = nbs * s_new
    assert conv_w.shape[1] == 3 and s_new >= 2 and dh == 64 and heads == SUBLANES and epg == SUBLANES
    assert page == LANES

    l = 0
    wi = w_in[l]
    wc = wi[:, 0:3 * dc].astype(BF16)
    wqkv = wi[:, 3 * dc:3 * dc + 3 * da].astype(BF16)
    o_f = 3 * dc + 3 * da
    wf = jnp.pad(wi[:, o_f:o_f + heads], ((0, 0), (0, LANES - heads)))
    wfh = wf.astype(BF16)
    wf2 = jnp.concatenate([wfh, (wf - wfh.astype(F32)).astype(BF16)], axis=1)
    wg = wi[:, o_f + heads:].astype(BF16)
    bfp = jnp.pad(b_forget[l][None, :], ((0, 0), (0, LANES - heads)))
    cw = conv_w[l]
    cbias = conv_b[l][None, :]
    g_mix = norm_mix[l][None, :]
    wuc = w_up_conv[l].astype(BF16)
    wua = w_up_attn[l].astype(BF16)
    wmo = w_mix_out[l].astype(BF16)
    g_x = norm_xattn[l][None, :]
    wxq = w_xq[l].astype(BF16)
    wxo = w_xo[l].astype(BF16)
    g_f = norm_ffn[l][None, :]
    wr = jnp.zeros((LANES, dm), F32)
    wr = wr.at[0:ngroups].set(w_router_group[l].T)
    wr = wr.at[EXPERT_ROW0:EXPERT_ROW0 + nexp].set(jnp.transpose(w_router_expert[l], (0, 2, 1)).reshape(nexp, dm))
    wrh = wr.astype(BF16)
    wrl = (wr - wrh.astype(F32)).astype(BF16)

    xp = x_prompt.reshape(tp, dm)
    (gc_p, qa_p, ka_p, vb_p, kt_p, vt_p, g_p, lf_p, ulast_p) = _inproj(
        xp, g_mix, wc, wqkv, wg, wf2, bfp, cw, cbias, seq_len=seq, sample=False, aug=_decay_columns(heads))
    xs_ = x_sample.reshape(ts, dm)
    st = state_conv[l]
    zeros_row = jnp.zeros((nbs, 1, dc), F32)
    fix1 = jnp.concatenate([st[:, 1:2], jnp.tile(zeros_row, (1, s_new - 1, 1))], axis=1).reshape(ts, dc)
    fix2 = jnp.concatenate([st[:, 0:1], st[:, 1:2], jnp.tile(zeros_row, (1, s_new - 2, 1))], axis=1).reshape(ts, dc)
    (gc_s, q_s, kb_s, vb_s, kf_s, vf_s, g_s, lf_s, dt_s, u_s) = _inproj(
        xs_, g_mix, wc, wqkv, wg, wf2, bfp, cw, cbias, seq_len=s_new, sample=True, fix=(fix1, fix2))
    pt_flat = page_table.reshape(-1).astype(jnp.int32)
    head_of_col = jnp.arange(da) // dh
    qbd = jnp.where(head_of_col[None, None, None, :] == jnp.arange(heads)[None, None, :, None],
                    q_s.reshape(nbs, s_new, 1, da), jnp.zeros((), BF16)).reshape(nbs, s_new * heads, da)
    kt_pages = jnp.transpose(cache_k[l], (0, 2, 3, 1)).reshape(n_pool, da, page)
    vt_pages = jnp.transpose(cache_v[l], (0, 2, 3, 1)).reshape(n_pool, da, page)
    lf_pages = jnp.swapaxes(cache_logf[l], 1, 2)
    at_p, at_s = _fox_fused(pt_flat, qa_p, ka_p, vb_p, kt_pages, vt_pages, lf_pages, qbd, kb_s, vb_s, dt_s,
                            nseq=nbp, seq_len=seq, dh=dh, npages=npages, s_new=s_new, heads=heads)

    tm_p = min(512, seq)
    mk_p, mv_p = _memkv(mem_prompt.reshape(nbp * nmem, dm), norm_mem[l][None, :], w_xk[l].astype(BF16),
                        w_xv[l].astype(BF16), xh=xh)
    cnt0 = jnp.zeros((LANES, LANES), F32)
    h2_p, xn_p, rt_p, rr_p, cnt1 = _mid(
        xp, gc_p, at_p, g_p, mk_p.reshape(nbp, nmem * xh, xd), mv_p.reshape(nbp, nmem * xh, xd),
        wuc, wua, wmo, g_x, wxq, wxo, g_f, wrh, wrl, cnt0, tm=tm_p, seq_len=seq, ngroups=ngroups, epg=epg, xh=xh)

    h1_s, qx_s = _post_attn(xs_, gc_s, at_s.reshape(ts, da), g_s, wuc, wua, wmo, g_x, wxq, tm=ts,
                            name="post_attn_sample")
    qx_s8 = jnp.pad(qx_s.astype(F32).reshape(nbs, s_new, dx), ((0, 0), (0, SUBLANES - s_new), (0, 0)))
    o_s = _xattn(qx_s8, cache_mem_k[l].reshape(nbs, nmem * xh, xd), cache_mem_v[l].reshape(nbs, nmem * xh, xd),
                 tq=SUBLANES, xh=xh, nbb=4 if nbs % 4 == 0 else 1, name="xattn_sample")[:, :s_new].reshape(ts, dx)
    h2_s, xn_s, rt_s, rr_s, cnt2 = _pre_moe(h1_s, o_s, wxo, g_f, wrh, wrl, cnt1, tm=ts, ngroups=ngroups, epg=epg,
                                            name="pre_moe_sample")

    tall = tp + ts
    counts = cnt2[0:nexp, 0].astype(jnp.int32)
    padded = (counts + EXPERT_ROWS - 1) // EXPERT_ROWS * EXPERT_ROWS
    pend = jnp.cumsum(padded).astype(jnp.int32)
    pstart = pend - padded
    nblk = (tall * TOP_K + nexp * (EXPERT_ROWS - 1)) // EXPERT_ROWS
    n_rows = nblk * EXPERT_ROWS
    blk_row0 = jnp.arange(nblk, dtype=jnp.int32) * EXPERT_ROWS
    blk_e = jnp.minimum(jnp.sum((pend[None, :] <= blk_row0[:, None]).astype(jnp.int32), axis=1), nexp - 1)
    nused = (pend[nexp - 1:nexp] // EXPERT_ROWS).astype(jnp.int32)
    rt_all = jnp.concatenate([rt_p, rt_s], axis=1)
    pstart_col = jnp.zeros((LANES, LANES), F32).at[0:nexp, :].set(pstart.astype(F32)[:, None])
    dest_flat = _dest(rt_all, pstart_col)[0:TOP_K].reshape(-1)
    xsg = _scatter(dest_flat, pstart, pend, xn_p, xn_s, n_rows=n_rows, tm_p=tm_p, nexp=nexp, nc=dm // (2 * LANES))
    y = _experts(blk_e, nused, xsg, w_expert_gate[l], w_expert_up[l], w_expert_down[l])
    g_fin = norm_final[None, :]
    y_p = _combine(dest_flat, h2_p, rr_p, g_fin, y, tm=tm_p, tok0=0, name="combine_prompt")
    y_s = _combine(dest_flat, h2_s, rr_s, g_fin, y, tm=ts, tok0=tp, name="combine_sample")

    return (y_p.reshape(nbp, seq, dm), y_s.reshape(nbs, s_new, dm),
            ulast_p[None],
            jnp.transpose(kt_p.reshape(nbp, heads, dh, seq), (0, 3, 1, 2))[None],
            jnp.transpose(vt_p.reshape(nbp, heads, dh, seq), (0, 3, 1, 2))[None],
            jnp.transpose(lf_p, (0, 2, 1))[None],
            mk_p.reshape(1, nbp, nmem, xh, xd), mv_p.reshape(1, nbp, nmem, xh, xd),
            u_s.reshape(nbs, s_new, dc)[None, :, s_new - 2:], kf_s.reshape(1, nbs, s_new, heads, dh),
            vf_s.reshape(1, nbs, s_new, heads, dh), lf_s[:, :heads].reshape(1, nbs, s_new, heads))
```
